```python
import math
import jax, jax.numpy as jnp
from jax import lax
import numpy as np

D_MODEL = 1024
BATCH = 2
SEQ = 8192
DEPTH = 1
DEC_BATCH = 128
DEC_SEQ = 1
PAST_LEN = 8192
PAGE_SIZE = 128

MIX_WIDTH = D_MODEL
N_HEADS = 8
NOPE_DIM = 64
ROPE_DIM = 32
V_DIM = 64
Q_RANK = 352
KV_RANK = 128
ROPE_THETA = 10000.0
Q_BLOCK = 128
ATTN_WIDTH = N_HEADS * V_DIM
ATTN_SCALE = 1.0 / math.sqrt(NOPE_DIM + ROPE_DIM)
SSM_WIDTH = MIX_WIDTH - ATTN_WIDTH
SSM_GROUP = 16
N_SSM_GROUPS = SSM_WIDTH // SSM_GROUP
SSM_STATE = 64
DT_MIN = 0.001
DT_MAX = 0.1
IN_WIDTH = Q_RANK + KV_RANK + ROPE_DIM + SSM_WIDTH
N_EXPERTS = 32
TOP_K = 4
D_EXPERT = D_MODEL
SWIGLU_LIMIT = 7.0
SWIGLU_ALPHA = 1.702
RMS_EPS = 1e-6
N_MOD = 6
NEG_INF = -1e30

kernel_name = 'hybrid_mla_s5_moe_adaln_step'


def rms_norm(x, g):
    x = x.astype(jnp.float32)
    return x * lax.rsqrt(jnp.mean(x * x, axis=-1, keepdims=True) + RMS_EPS) * g.astype(jnp.float32)


def rope(x, pos):
    half = ROPE_DIM // 2
    inv = ROPE_THETA ** (-jnp.arange(half, dtype=jnp.float32) / half)
    ang = pos.astype(jnp.float32)[:, None] * inv[None, :]
    cos = jnp.cos(ang)[None, :, None, :]
    sin = jnp.sin(ang)[None, :, None, :]
    x1, x2 = x[..., :half], x[..., half:]
    return jnp.concatenate([x1 * cos - x2 * sin, x1 * sin + x2 * cos], axis=-1)


def mla_attend(q_lat, q_rope, kv_lat, k_rope, q_pos, k_pos):
    s = (jnp.einsum('nqhr,nkr->nhqk', q_lat, kv_lat)
         + jnp.einsum('nqhe,nke->nhqk', q_rope, k_rope)) * ATTN_SCALE
    s = jnp.where(k_pos[None, :] <= q_pos[:, None], s, NEG_INF)
    p = jax.nn.softmax(s, axis=-1)
    return jnp.einsum('nhqk,nkr->nqhr', p, kv_lat)


def _ssm_combine(e1, e2):
    a1, b1 = e1
    a2, b2 = e2
    return a2 * a1, a2 * b1 + b2


def s5_mixer(u, h0, lam_re, lam_im, log_step, b_re, b_im, c_re, c_im, d_skip, w_glu, b_glu):
    n, l, _ = u.shape
    ug = u.astype(jnp.float32).reshape(n, l, N_SSM_GROUPS, SSM_GROUP)
    lam = lax.complex(lam_re.astype(jnp.float32), lam_im.astype(jnp.float32))
    step = jnp.exp(log_step.astype(jnp.float32))[:, None]
    lam_bar = jnp.exp(lam * step)
    b = lax.complex(b_re.astype(jnp.float32), b_im.astype(jnp.float32))
    b_bar = ((lam_bar - 1.0) / lam)[..., None] * b
    bu = jnp.einsum('gpc,nlgc->nlgp', b_bar, ug.astype(jnp.complex64))
    a = jnp.broadcast_to(lam_bar, bu.shape)
    a_cum, h = lax.associative_scan(_ssm_combine, (a, bu), axis=1)
    if h0 is not None:
        h = h + a_cum * h0[:, None]
    c = lax.complex(c_re.astype(jnp.float32), c_im.astype(jnp.float32))
    y = jnp.real(jnp.einsum('gcp,nlgp->nlgc', c, h)) \
        + d_skip.astype(jnp.float32).reshape(N_SSM_GROUPS, SSM_GROUP) * ug
    y = jax.nn.gelu(y.reshape(n, l, SSM_WIDTH))
    y = y * jax.nn.sigmoid(y @ w_glu.astype(jnp.float32) + b_glu.astype(jnp.float32))
    h_last = h[:, -1]
    return y, jnp.stack([jnp.real(h_last), jnp.imag(h_last)], axis=-1)


def moe_ffn(h, w_router, b_router, w1, b1, w2, b2):
    n, l, d = h.shape
    t = h.reshape(n * l, d)
    logits = t @ w_router.astype(jnp.float32) + b_router.astype(jnp.float32)
    top_val, top_idx = lax.top_k(logits, TOP_K)
    gates = jax.nn.softmax(top_val, axis=-1)
    combine = jnp.einsum('tk,tke->te', gates, jax.nn.one_hot(top_idx, N_EXPERTS, dtype=jnp.float32))
    out = jnp.zeros_like(t)
    for e in range(N_EXPERTS):
        hid = t @ w1[e].astype(jnp.float32) + b1[e].astype(jnp.float32)
        glu = jnp.minimum(hid[:, ::2], SWIGLU_LIMIT)
        lin = jnp.clip(hid[:, 1::2], -SWIGLU_LIMIT, SWIGLU_LIMIT)
        act = glu * jax.nn.sigmoid(SWIGLU_ALPHA * glu) * (lin + 1.0)
        out = out + combine[:, e:e + 1] * (act @ w2[e].astype(jnp.float32) + b2[e].astype(jnp.float32))
    return out.reshape(n, l, d)


def decoder_layer(x, c, pos, past_lat, past_rope, h0, p):
    n, l, _ = x.shape
    mod = jax.nn.silu(c.astype(jnp.float32)) @ p['w_ada'].astype(jnp.float32) + p['b_ada'].astype(jnp.float32)
    sh1, sc1, g1, sh2, sc2, g2 = jnp.split(mod[:, None, :], N_MOD, axis=-1)

    h = rms_norm(x, p['g_pre_mix']) * (1.0 + sc1) + sh1
    z = h @ p['w_in'].astype(jnp.float32)
    z_q, z_kv, z_r, u = jnp.split(z, [Q_RANK, Q_RANK + KV_RANK, Q_RANK + KV_RANK + ROPE_DIM], axis=-1)

    q = (rms_norm(z_q, p['g_q']) @ p['w_uq'].astype(jnp.float32)).reshape(n, l, N_HEADS, NOPE_DIM + ROPE_DIM)
    q_lat = jnp.einsum('nlhd,rhd->nlhr', q[..., :NOPE_DIM], p['w_uk'].astype(jnp.float32))
    q_rope = rope(q[..., NOPE_DIM:], pos)
    kv_lat = rms_norm(z_kv, p['g_kv'])
    k_rope = rope(z_r[:, :, None, :], pos)[:, :, 0]
    if past_lat is None:
        nb = l // Q_BLOCK

        def attend_block(blk):
            ql, qr, qp = blk
            return mla_attend(ql, qr, kv_lat, k_rope, qp, pos)

        def to_blocks(t):
            return jnp.swapaxes(t.reshape((n, nb, Q_BLOCK) + t.shape[2:]), 0, 1)

        o_lat = lax.map(attend_block, (to_blocks(q_lat), to_blocks(q_rope), pos.reshape(nb, Q_BLOCK)))
        o_lat = jnp.swapaxes(o_lat, 0, 1).reshape(n, l, N_HEADS, KV_RANK)
    else:
        keys_lat = jnp.concatenate([past_lat.astype(jnp.float32), kv_lat], axis=1)
        keys_rope = jnp.concatenate([past_rope.astype(jnp.float32), k_rope], axis=1)
        k_pos = jnp.arange(keys_lat.shape[1], dtype=jnp.int32)
        o_lat = mla_attend(q_lat, q_rope, keys_lat, keys_rope, pos, k_pos)
    o_attn = jnp.einsum('nlhr,rhv->nlhv', o_lat, p['w_uv'].astype(jnp.float32)).reshape(n, l, ATTN_WIDTH)

    y_ssm, h_last = s5_mixer(u, h0, p['ssm_lam_re'], p['ssm_lam_im'], p['ssm_log_step'],
                             p['ssm_b_re'], p['ssm_b_im'], p['ssm_c_re'], p['ssm_c_im'],
                             p['ssm_d'], p['ssm_w_glu'], p['ssm_b_glu'])

    mix = jnp.concatenate([o_attn, y_ssm], axis=-1) @ p['w_out'].astype(jnp.float32)
    x = x.astype(jnp.float32) + g1 * rms_norm(mix, p['g_post_mix'])

    h2 = rms_norm(x, p['g_pre_ffn']) * (1.0 + sc2) + sh2
    f = moe_ffn(h2, p['w_router'], p['b_router'], p['w1'], p['b1'], p['w2'], p['b2'])
    x = x + g2 * rms_norm(f, p['g_post_ffn'])
    return x, kv_lat, k_rope, h_last


def setup_inputs(seed: int = 0) -> dict:
    key = jax.random.key(seed)
    ks = iter(jax.random.split(key, 48))
    f32 = jnp.float32
    L = DEPTH

    def nrm(shape, scale):
        return jax.random.normal(next(ks), shape, f32) * scale

    def gain(width):
        return 1.0 + nrm((L, width), 0.02)

    n_pages = PAST_LEN // PAGE_SIZE
    n_used = DEC_BATCH * n_pages
    n_phys = n_used + max(1, n_used // 4)
    page_table = jax.random.permutation(next(ks), n_phys)[:n_used].reshape(DEC_BATCH, n_pages).astype(jnp.int32)
    lam_n = jnp.arange(SSM_STATE, dtype=f32)
    return {
        'x_prompt': nrm((BATCH, SEQ, D_MODEL), 1.0),
        'x_sample': nrm((DEC_BATCH, DEC_SEQ, D_MODEL), 1.0),
        'c_prompt': nrm((BATCH, D_MODEL), 1.0),
        'c_sample': nrm((DEC_BATCH, D_MODEL), 1.0),
        'cache_kv_latent': nrm((L, n_phys, PAGE_SIZE, KV_RANK), 1.0),
        'cache_k_rope': nrm((L, n_phys, PAGE_SIZE, ROPE_DIM), 1.0),
        'state_ssm': nrm((L, DEC_BATCH, N_SSM_GROUPS, SSM_STATE, 2), 0.5),
        'page_table': page_table,
        'w_ada': nrm((L, D_MODEL, N_MOD * D_MODEL), D_MODEL ** -0.5),
        'b_ada': nrm((L, N_MOD * D_MODEL), 0.02),
        'g_pre_mix': gain(D_MODEL),
        'w_in': nrm((L, D_MODEL, IN_WIDTH), D_MODEL ** -0.5),
        'g_q': gain(Q_RANK),
        'w_uq': nrm((L, Q_RANK, N_HEADS * (NOPE_DIM + ROPE_DIM)), Q_RANK ** -0.5),
        'w_uk': nrm((L, KV_RANK, N_HEADS, NOPE_DIM), KV_RANK ** -0.5),
        'g_kv': gain(KV_RANK),
        'w_uv': nrm((L, KV_RANK, N_HEADS, V_DIM), KV_RANK ** -0.5),
        'ssm_lam_re': -0.5 + nrm((L, N_SSM_GROUPS, SSM_STATE), 0.01),
        'ssm_lam_im': jnp.pi * lam_n[None, None, :] + nrm((L, N_SSM_GROUPS, SSM_STATE), 0.01),
        'ssm_log_step': jax.random.uniform(next(ks), (L, N_SSM_GROUPS), f32, math.log(DT_MIN), math.log(DT_MAX)),
        'ssm_b_re': nrm((L, N_SSM_GROUPS, SSM_STATE, SSM_GROUP), (2.0 * SSM_GROUP) ** -0.5),
        'ssm_b_im': nrm((L, N_SSM_GROUPS, SSM_STATE, SSM_GROUP), (2.0 * SSM_GROUP) ** -0.5),
        'ssm_c_re': nrm((L, N_SSM_GROUPS, SSM_GROUP, SSM_STATE), (2.0 * SSM_STATE) ** -0.5),
        'ssm_c_im': nrm((L, N_SSM_GROUPS, SSM_GROUP, SSM_STATE), (2.0 * SSM_STATE) ** -0.5),
        'ssm_d': nrm((L, SSM_WIDTH), 1.0),
        'ssm_w_glu': nrm((L, SSM_WIDTH, SSM_WIDTH), SSM_WIDTH ** -0.5),
        'ssm_b_glu': nrm((L, SSM_WIDTH), 0.01),
        'w_out': nrm((L, MIX_WIDTH, D_MODEL), MIX_WIDTH ** -0.5),
        'g_post_mix': gain(D_MODEL),
        'g_pre_ffn': gain(D_MODEL),
        'w_router': nrm((L, D_MODEL, N_EXPERTS), D_MODEL ** -0.5),
        'b_router': nrm((L, N_EXPERTS), 0.01),
        'w1': nrm((L, N_EXPERTS, D_MODEL, 2 * D_EXPERT), D_MODEL ** -0.5),
        'b1': nrm((L, N_EXPERTS, 2 * D_EXPERT), 0.01),
        'w2': nrm((L, N_EXPERTS, D_EXPERT, D_MODEL), D_EXPERT ** -0.5),
        'b2': nrm((L, N_EXPERTS, D_MODEL), 0.01),
        'g_post_ffn': gain(D_MODEL),
    }


def reference(x_prompt, x_sample, c_prompt, c_sample, cache_kv_latent, cache_k_rope, state_ssm, page_table,
              w_ada, b_ada, g_pre_mix, w_in, g_q, w_uq, w_uk, g_kv, w_uv,
              ssm_lam_re, ssm_lam_im, ssm_log_step, ssm_b_re, ssm_b_im, ssm_c_re, ssm_c_im,
              ssm_d, ssm_w_glu, ssm_b_glu, w_out, g_post_mix, g_pre_ffn,
              w_router, b_router, w1, b1, w2, b2, g_post_ffn):
    params = dict(w_ada=w_ada, b_ada=b_ada, g_pre_mix=g_pre_mix, w_in=w_in, g_q=g_q, w_uq=w_uq,
                  w_uk=w_uk, g_kv=g_kv, w_uv=w_uv, ssm_lam_re=ssm_lam_re, ssm_lam_im=ssm_lam_im,
                  ssm_log_step=ssm_log_step, ssm_b_re=ssm_b_re, ssm_b_im=ssm_b_im,
                  ssm_c_re=ssm_c_re, ssm_c_im=ssm_c_im, ssm_d=ssm_d, ssm_w_glu=ssm_w_glu,
                  ssm_b_glu=ssm_b_glu, w_out=w_out, g_post_mix=g_post_mix, g_pre_ffn=g_pre_ffn,
                  w_router=w_router, b_router=b_router, w1=w1, b1=b1, w2=w2, b2=b2,
                  g_post_ffn=g_post_ffn)
    n_dec, n_pages = page_table.shape
    past_len = n_pages * PAGE_SIZE
    pos_prompt = jnp.arange(x_prompt.shape[1], dtype=jnp.int32)
    pos_sample = past_len + jnp.arange(x_sample.shape[1], dtype=jnp.int32)
    y_prompt, y_sample = x_prompt, x_sample
    lat_p, rope_p, ssm_p, lat_s, rope_s, ssm_s = [], [], [], [], [], []
    for layer in range(DEPTH):
        p = {name: arr[layer] for name, arr in params.items()}
        y_prompt, lp, rp, hp = decoder_layer(y_prompt, c_prompt, pos_prompt, None, None, None, p)
        past_lat = cache_kv_latent[layer][page_table].reshape(n_dec, past_len, KV_RANK)
        past_rope = cache_k_rope[layer][page_table].reshape(n_dec, past_len, ROPE_DIM)
        h0 = lax.complex(state_ssm[layer, ..., 0].astype(jnp.float32), state_ssm[layer, ..., 1].astype(jnp.float32))
        y_sample, ls, rs, hs = decoder_layer(y_sample, c_sample, pos_sample, past_lat, past_rope, h0, p)
        lat_p.append(lp)
        rope_p.append(rp)
        ssm_p.append(hp)
        lat_s.append(ls)
        rope_s.append(rs)
        ssm_s.append(hs)
    return (y_prompt, y_sample, jnp.stack(lat_p), jnp.stack(rope_p), jnp.stack(ssm_p),
            jnp.stack(lat_s), jnp.stack(rope_s), jnp.stack(ssm_s))
```

```python
import functools
import math

import jax
import jax.numpy as jnp
from jax import lax
from jax.experimental import pallas as pl
from jax.experimental.pallas import tpu as pltpu

F32 = jnp.float32
BF16 = jnp.bfloat16

D_MODEL = 1024
N_HEADS = 8
NOPE_DIM = 64
ROPE_DIM = 32
V_DIM = 64
Q_RANK = 352
KV_RANK = 128
ROPE_THETA = 10000.0
ATTN_WIDTH = N_HEADS * V_DIM
ATTN_SCALE = 1.0 / math.sqrt(NOPE_DIM + ROPE_DIM)
SSM_WIDTH = D_MODEL - ATTN_WIDTH
SSM_GROUP = 16
N_SSM_GROUPS = SSM_WIDTH // SSM_GROUP
SSM_STATE = 64
N_STATE = N_SSM_GROUPS * SSM_STATE
N_EXPERTS = 32
TOP_K = 4
SWIGLU_LIMIT = 7.0
SWIGLU_ALPHA = 1.702
RMS_EPS = 1e-6
N_MOD = 6
NEG_INF = -1e30
PAGE_SIZE = 128

LANES = 128
SUBLANES = 8
VMEM_LIMIT = 56 * 1024 * 1024

Q_PAD = 384
CAT = 2 * LANES
SSM_BLOCKS = 4
GROUPS_PER_BLOCK = N_SSM_GROUPS // SSM_BLOCKS
BLOCK_STATES = GROUPS_PER_BLOCK * SSM_STATE
IN_PAD = Q_PAD + KV_RANK + 2 * LANES + SSM_WIDTH
UQ_PAD = N_HEADS * NOPE_DIM + 4 * LANES


def _cparams(sem):
    return pltpu.CompilerParams(dimension_semantics=sem, vmem_limit_bytes=VMEM_LIMIT)


def _mm(a, b):
    return jnp.dot(a.astype(BF16), b.astype(BF16), preferred_element_type=F32)


def _mm_nt(a, b):
    return lax.dot_general(a.astype(BF16), b.astype(BF16), (((1,), (1,)), ((), ())),
                           preferred_element_type=F32)


def _split_bf16(x):
    hi = x.astype(BF16)
    lo = (x - hi.astype(F32)).astype(BF16)
    return hi, lo


def _mm3(a, b):
    ah, al = _split_bf16(a)
    bh, bl = _split_bf16(b)
    d = functools.partial(jnp.dot, preferred_element_type=F32)
    return d(ah, bh) + (d(ah, bl) + d(al, bh))


def _sigmoid(x):
    return 1.0 / (1.0 + jnp.exp(-x))


def _rms(x, g, width):
    ms = jnp.sum(x * x, axis=-1, keepdims=True) * (1.0 / width)
    return x * lax.rsqrt(ms + RMS_EPS) * g


def _gelu_tanh(x):
    return 0.5 * x * (1.0 + jnp.tanh(math.sqrt(2.0 / math.pi) * (x + 0.044715 * (x * x * x))))


def _ada_kernel(c_ref, w_ref, b_ref, o_ref):
    c = c_ref[...]
    o_ref[...] = _mm3(c * _sigmoid(c), w_ref[...]) + b_ref[...]


def _ada(c_all, w_ada, b_ada):
    rows = c_all.shape[0]
    tn = 512
    n_out = w_ada.shape[1]
    return pl.pallas_call(
        _ada_kernel,
        grid=(n_out // tn,),
        in_specs=[pl.BlockSpec((rows, D_MODEL), lambda j: (0, 0)),
                  pl.BlockSpec((D_MODEL, tn), lambda j: (0, j)),
                  pl.BlockSpec((1, tn), lambda j: (0, j))],
        out_specs=pl.BlockSpec((rows, tn), lambda j: (0, j)),
        out_shape=jax.ShapeDtypeStruct((rows, n_out), F32),
        compiler_params=_cparams(("parallel",)),
        name="adaln",
    )(c_all, w_ada, b_ada)


def _inproj_kernel(x_ref, sc_ref, sh_ref, gpre_ref, win_ref, gq_ref, wuq_ref, wuk_ref, gkv_ref,
                   cos_ref, sin_ref, q_ref, kcat_ref, kv_ref, kr_ref, u_ref):
    x = x_ref[0]
    h = _rms(x, gpre_ref[...], D_MODEL) * (1.0 + sc_ref[0]) + sh_ref[0]
    z = _mm(h, win_ref[...])
    zq = z[:, :Q_PAD]
    zkv = z[:, Q_PAD:Q_PAD + KV_RANK]
    zr = z[:, Q_PAD + KV_RANK:Q_PAD + KV_RANK + LANES]
    zrs = z[:, Q_PAD + KV_RANK + LANES:Q_PAD + KV_RANK + 2 * LANES]
    u = z[:, Q_PAD + KV_RANK + 2 * LANES:]
    cos4 = cos_ref[0]
    sin4 = sin_ref[0]

    kvl = _rms(zkv, gkv_ref[...], KV_RANK)
    kr4 = zr * cos4 + zrs * sin4
    kv_ref[0] = kvl
    kr_ref[0] = kr4[:, :ROPE_DIM]
    kcat_ref[0] = jnp.concatenate([kvl, kr4], axis=-1).astype(BF16)
    u_ref[0] = u.astype(BF16)

    qn = _rms(zq, gq_ref[...], Q_RANK)
    q = _mm(qn, wuq_ref[...])
    nope_w = N_HEADS * NOPE_DIM
    rope_a = q[:, nope_w:nope_w + LANES] * cos4 + q[:, nope_w + 2 * LANES:nope_w + 3 * LANES] * sin4
    rope_b = q[:, nope_w + LANES:nope_w + 2 * LANES] * cos4 + q[:, nope_w + 3 * LANES:] * sin4
    slot = lax.broadcasted_iota(jnp.int32, (1, LANES), 1) // ROPE_DIM
    for j in range(N_HEADS // 2):
        ql = _mm(q[:, j * LANES:(j + 1) * LANES], wuk_ref[j])
        for hh in range(2):
            head = 2 * j + hh
            rope = rope_a if head < 4 else rope_b
            rope_h = jnp.where(slot == (head % 4), rope, 0.0)
            qc = jnp.concatenate([ql[:, hh * KV_RANK:(hh + 1) * KV_RANK], rope_h], axis=-1)
            q_ref[0, head] = (qc * ATTN_SCALE).astype(BF16)


def _inproj(x, sc, sh, p, cos4, sin4, tm):
    n, l, _ = x.shape
    per_row = sc.shape[1] != 1
    mod_blk = (1, tm, D_MODEL) if per_row else (1, 1, D_MODEL)
    mod_map = (lambda b, i: (b, i, 0)) if per_row else (lambda b, i: (b, 0, 0))
    const2 = lambda b, i: (0, 0)
    const3 = lambda b, i: (0, 0, 0)
    pos_map = lambda b, i: (0, i, 0)
    return pl.pallas_call(
        _inproj_kernel,
        grid=(n, l // tm),
        in_specs=[pl.BlockSpec((1, tm, D_MODEL), lambda b, i: (b, i, 0)),
                  pl.BlockSpec(mod_blk, mod_map),
                  pl.BlockSpec(mod_blk, mod_map),
                  pl.BlockSpec((1, D_MODEL), const2),
                  pl.BlockSpec((D_MODEL, IN_PAD), const2),
                  pl.BlockSpec((1, Q_PAD), const2),
                  pl.BlockSpec((Q_PAD, UQ_PAD), const2),
                  pl.BlockSpec((N_HEADS // 2, LANES, 2 * KV_RANK), const3),
                  pl.BlockSpec((1, KV_RANK), const2),
                  pl.BlockSpec((1, tm, LANES), pos_map),
                  pl.BlockSpec((1, tm, LANES), pos_map)],
        out_specs=[pl.BlockSpec((1, N_HEADS, tm, CAT), lambda b, i: (b, 0, i, 0)),
                   pl.BlockSpec((1, tm, CAT), lambda b, i: (b, i, 0)),
                   pl.BlockSpec((1, tm, KV_RANK), lambda b, i: (b, i, 0)),
                   pl.BlockSpec((1, tm, ROPE_DIM), lambda b, i: (b, i, 0)),
                   pl.BlockSpec((1, tm, SSM_WIDTH), lambda b, i: (b, i, 0))],
        out_shape=[jax.ShapeDtypeStruct((n, N_HEADS, l, CAT), BF16),
                   jax.ShapeDtypeStruct((n, l, CAT), BF16),
                   jax.ShapeDtypeStruct((n, l, KV_RANK), F32),
                   jax.ShapeDtypeStruct((n, l, ROPE_DIM), F32),
                   jax.ShapeDtypeStruct((n, l, SSM_WIDTH), BF16)],
        compiler_params=_cparams(("parallel", "parallel")),
        name="inproj",
    )(x, sc, sh, p["g_pre_mix"], p["w_in"], p["g_q"], p["w_uq"], p["w_uk"], p["g_kv"], cos4, sin4)


def _flash_kernel(q_ref, k_ref, wuv_ref, o_ref, m_ref, l_ref, acc_ref, *, tq, tk):
    i = pl.program_id(1)
    j = pl.program_id(2)
    j_last = ((i + 1) * tq - 1) // tk
    rows = N_HEADS * tq

    @pl.when(j == 0)
    def _():
        m_ref[...] = jnp.full((rows, 1), NEG_INF, F32)
        l_ref[...] = jnp.zeros((rows, 1), F32)
        acc_ref[...] = jnp.zeros((rows, KV_RANK), F32)

    @pl.when(j <= j_last)
    def _():
        q = q_ref[0].reshape(rows, CAT)
        k = k_ref[0]
        s = lax.dot_general(q, k, (((1,), (1,)), ((), ())), preferred_element_type=F32)
        q_pos = i * tq + lax.broadcasted_iota(jnp.int32, (tq, tk), 0)
        k_pos = j * tk + lax.broadcasted_iota(jnp.int32, (tq, tk), 1)
        s = jnp.where((k_pos <= q_pos)[None], s.reshape(N_HEADS, tq, tk), NEG_INF).reshape(rows, tk)
        m_prev = m_ref[...]
        m_new = jnp.maximum(m_prev, jnp.max(s, axis=-1, keepdims=True))
        alpha = jnp.exp(m_prev - m_new)
        p = jnp.exp(s - m_new)
        l_ref[...] = alpha * l_ref[...] + jnp.sum(p, axis=-1, keepdims=True)
        acc_ref[...] = alpha * acc_ref[...] + jnp.dot(p.astype(BF16), k[:, :KV_RANK],
                                                      preferred_element_type=F32)
        m_ref[...] = m_new

    @pl.when(j == j_last)
    def _():
        o = acc_ref[...] / l_ref[...]
        outs = []
        for jj in range(N_HEADS // 2):
            pair = jnp.concatenate([o[(2 * jj) * tq:(2 * jj + 1) * tq],
                                    o[(2 * jj + 1) * tq:(2 * jj + 2) * tq]], axis=-1)
            outs.append(_mm(pair, wuv_ref[jj]))
        o_ref[0] = jnp.concatenate(outs, axis=-1).astype(BF16)


def _flash(q_cat, k_cat, wuv_pair, tq, tk):
    n, _, l, _ = q_cat.shape
    kern = functools.partial(_flash_kernel, tq=tq, tk=tk)

    def k_map(b, i, j):
        return (b, jnp.minimum(j, ((i + 1) * tq - 1) // tk), 0)

    return pl.pallas_call(
        kern,
        grid=(n, l // tq, l // tk),
        in_specs=[pl.BlockSpec((1, N_HEADS, tq, CAT), lambda b, i, j: (b, 0, i, 0)),
                  pl.BlockSpec((1, tk, CAT), k_map),
                  pl.BlockSpec((N_HEADS // 2, 2 * KV_RANK, 2 * V_DIM), lambda b, i, j: (0, 0, 0))],
        out_specs=pl.BlockSpec((1, tq, ATTN_WIDTH), lambda b, i, j: (b, i, 0)),
        out_shape=jax.ShapeDtypeStruct((n, l, ATTN_WIDTH), BF16),
        scratch_shapes=[pltpu.VMEM((N_HEADS * tq, 1), F32),
                        pltpu.VMEM((N_HEADS * tq, 1), F32),
                        pltpu.VMEM((N_HEADS * tq, KV_RANK), F32)],
        compiler_params=_cparams(("parallel", "parallel", "arbitrary")),
        name="flash",
    )(q_cat, k_cat, wuv_pair)


def _ssm_out(y, up, d_ref, wglu_ref, bglu_ref):
    y = _gelu_tanh(y + d_ref[...] * up)
    return y * _sigmoid(_mm(y, wglu_ref[...]) + bglu_ref[...])


def _s5_kernel(u_ref, perm_ref, permt_ref, bmat_ref, cmat_ref, lam_ref, laml_ref, d_ref, wglu_ref,
               bglu_ref, y_ref, hlast_ref, buf_ref, carry_ref, *, lc):
    j = pl.program_id(1)

    @pl.when(j == 0)
    def _():
        carry_ref[...] = jnp.zeros(carry_ref.shape, F32)

    u = u_ref[0]
    up = jnp.dot(perm_ref[...], u, preferred_element_type=F32).astype(BF16)
    blk_w = 2 * BLOCK_STATES
    for b in range(SSM_BLOCKS):
        buf_ref[:, b * blk_w:(b + 1) * blk_w] = jnp.dot(
            up[:, b * LANES:(b + 1) * LANES], bmat_ref[b], preferred_element_type=F32)

    for b in range(SSM_BLOCKS):
        st = slice(b * BLOCK_STATES, (b + 1) * BLOCK_STATES)
        re_cols = slice(b * blk_w, b * blk_w + BLOCK_STATES)
        im_cols = slice(b * blk_w + BLOCK_STATES, (b + 1) * blk_w)
        lr = lam_ref[0:1, st]
        li = lam_ref[1:2, st]

        def run(init, store):
            def body(t, carry):
                re, im = carry
                rows = pl.ds(pl.multiple_of(t * SUBLANES, SUBLANES), SUBLANES)
                n_re = lr * re - li * im + buf_ref[rows, re_cols]
                n_im = lr * im + li * re + buf_ref[rows, im_cols]
                if store:
                    buf_ref[rows, re_cols] = n_re
                    buf_ref[rows, im_cols] = n_im
                return n_re, n_im
            return lax.fori_loop(0, lc, body, init)

        zero = jnp.zeros((SUBLANES, BLOCK_STATES), F32)
        end_re, end_im = run((zero, zero), False)
        ll_r = laml_ref[0:1, st]
        ll_i = laml_ref[1:2, st]
        c_re = carry_ref[0:1, st]
        c_im = carry_ref[1:2, st]
        init_re, init_im = [], []
        for k in range(SUBLANES):
            init_re.append(c_re)
            init_im.append(c_im)
            c_re, c_im = (ll_r * c_re - ll_i * c_im + end_re[k:k + 1],
                          ll_r * c_im + ll_i * c_re + end_im[k:k + 1])
        carry_ref[0:1, st] = c_re
        carry_ref[1:2, st] = c_im
        run((jnp.concatenate(init_re, axis=0), jnp.concatenate(init_im, axis=0)), True)

    hlast_ref[0] = carry_ref[...]
    ys = [jnp.dot(buf_ref[:, b * blk_w:(b + 1) * blk_w].astype(BF16), cmat_ref[b],
                  preferred_element_type=F32) for b in range(SSM_BLOCKS)]
    y = _ssm_out(jnp.concatenate(ys, axis=-1), up.astype(F32), d_ref, wglu_ref, bglu_ref)
    y_ref[0] = jnp.dot(permt_ref[...], y.astype(BF16), preferred_element_type=F32).astype(BF16)


def _s5_prompt(u, sp, lc):
    n, l, _ = u.shape
    s = SUBLANES * lc
    rows = jnp.arange(s)
    src = (rows % SUBLANES) * lc + rows // SUBLANES
    perm = jnp.zeros((s, s), BF16).at[rows, src].set(1.0)
    laml = _complex_pow2(sp["lam"], lc)
    kern = functools.partial(_s5_kernel, lc=lc)
    c2 = lambda b, j: (0, 0)
    c3 = lambda b, j: (0, 0, 0)
    return pl.pallas_call(
        kern,
        grid=(n, l // s),
        in_specs=[pl.BlockSpec((1, s, SSM_WIDTH), lambda b, j: (b, j, 0)),
                  pl.BlockSpec((s, s), c2),
                  pl.BlockSpec((s, s), c2),
                  pl.BlockSpec((SSM_BLOCKS, LANES, 2 * BLOCK_STATES), c3),
                  pl.BlockSpec((SSM_BLOCKS, 2 * BLOCK_STATES, LANES), c3),
                  pl.BlockSpec((2, N_STATE), c2),
                  pl.BlockSpec((2, N_STATE), c2),
                  pl.BlockSpec((1, SSM_WIDTH), c2),
                  pl.BlockSpec((SSM_WIDTH, SSM_WIDTH), c2),
                  pl.BlockSpec((1, SSM_WIDTH), c2)],
        out_specs=[pl.BlockSpec((1, s, SSM_WIDTH), lambda b, j: (b, j, 0)),
                   pl.BlockSpec((1, 2, N_STATE), lambda b, j: (b, 0, 0))],
        out_shape=[jax.ShapeDtypeStruct((n, l, SSM_WIDTH), BF16),
                   jax.ShapeDtypeStruct((n, 2, N_STATE), F32)],
        scratch_shapes=[pltpu.VMEM((s, 2 * N_STATE), F32),
                        pltpu.VMEM((2, N_STATE), F32)],
        compiler_params=_cparams(("parallel", "arbitrary")),
        name="s5_scan",
    )(u, perm, perm.T, sp["bmat"], sp["cmat"], sp["lam"], laml, sp["d"], sp["w_glu"], sp["b_glu"])


def _s5_step_kernel(u_ref, h0_ref, bmat_ref, cmat_ref, lam_ref, d_ref, wglu_ref, bglu_ref,
                    y_ref, h_ref):
    u = u_ref[...]
    ys = []
    for b in range(SSM_BLOCKS):
        st = slice(b * BLOCK_STATES, (b + 1) * BLOCK_STATES)
        bu = jnp.dot(u[:, b * LANES:(b + 1) * LANES], bmat_ref[b], preferred_element_type=F32)
        lr = lam_ref[0:1, st]
        li = lam_ref[1:2, st]
        re = h0_ref[0, :, st]
        im = h0_ref[1, :, st]
        n_re = lr * re - li * im + bu[:, :BLOCK_STATES]
        n_im = lr * im + li * re + bu[:, BLOCK_STATES:]
        h_ref[0, :, st] = n_re
        h_ref[1, :, st] = n_im
        ys.append(_mm(jnp.concatenate([n_re, n_im], axis=-1), cmat_ref[b]))
    y = _ssm_out(jnp.concatenate(ys, axis=-1), u.astype(F32), d_ref, wglu_ref, bglu_ref)
    y_ref[...] = y.astype(BF16)


def _s5_step(u, h0, sp):
    rows = u.shape[0]
    return pl.pallas_call(
        _s5_step_kernel,
        out_shape=[jax.ShapeDtypeStruct((rows, SSM_WIDTH), BF16),
                   jax.ShapeDtypeStruct((2, rows, N_STATE), F32)],
        compiler_params=pltpu.CompilerParams(vmem_limit_bytes=VMEM_LIMIT),
        name="s5_step",
    )(u, h0, sp["bmat"], sp["cmat"], sp["lam"], sp["d"], sp["w_glu"], sp["b_glu"])


def _decode_kernel(pt_ref, ql_ref, qr_ref, kvs_ref, krs_ref, *rest, pg):
    del pt_ref
    lat_refs = rest[:pg]
    rope_refs = rest[pg:2 * pg]
    o_ref, m_ref, l_ref, acc_ref = rest[2 * pg:]
    j = pl.program_id(1)

    @pl.when(j == 0)
    def _():
        m_ref[...] = jnp.full(m_ref.shape, NEG_INF, F32)
        l_ref[...] = jnp.zeros(l_ref.shape, F32)
        acc_ref[...] = jnp.zeros(acc_ref.shape, F32)

    ql = ql_ref[0]
    qr = qr_ref[0]
    lats = [lat_refs[i][0].astype(BF16) for i in range(pg)]
    s = jnp.concatenate(
        [_mm_nt(ql, lats[i]) + _mm_nt(qr, rope_refs[i][0]) for i in range(pg)], axis=-1)
    m_prev = m_ref[...]
    m_new = jnp.maximum(m_prev, jnp.max(s, axis=-1, keepdims=True))
    alpha = jnp.exp(m_prev - m_new)
    p = jnp.exp(s - m_new)
    l_ref[...] = alpha * l_ref[...] + jnp.sum(p, axis=-1, keepdims=True)
    pv = _mm(p[:, :PAGE_SIZE], lats[0])
    for i in range(1, pg):
        pv = pv + _mm(p[:, i * PAGE_SIZE:(i + 1) * PAGE_SIZE], lats[i])
    acc_ref[...] = alpha * acc_ref[...] + pv
    m_ref[...] = m_new

    @pl.when(j == pl.num_programs(1) - 1)
    def _():
        kvs = kvs_ref[0]
        krs = krs_ref[0]
        s_self = (jnp.sum(ql.astype(F32) * kvs, axis=-1, keepdims=True)
                  + jnp.sum(qr.astype(F32) * krs, axis=-1, keepdims=True))
        m_p = m_ref[...]
        m_n = jnp.maximum(m_p, s_self)
        a = jnp.exp(m_p - m_n)
        p_self = jnp.exp(s_self - m_n)
        l_fin = a * l_ref[...] + p_self
        o_ref[0] = (a * acc_ref[...] + p_self * kvs) / l_fin


def _decode_attn(q_lat, q_rope, kv_self, kr_self, cache_lat, cache_rope, page_table, pg):
    b, n_pages = page_table.shape
    kern = functools.partial(_decode_kernel, pg=pg)

    def page_map(i):
        return lambda s, j, pt: (pt[s * n_pages + j * pg + i], 0, 0)

    seq3 = lambda s, j, pt: (s, 0, 0)
    grid_spec = pltpu.PrefetchScalarGridSpec(
        num_scalar_prefetch=1,
        grid=(b, n_pages // pg),
        in_specs=([pl.BlockSpec((1, N_HEADS, KV_RANK), seq3),
                   pl.BlockSpec((1, N_HEADS, ROPE_DIM), seq3),
                   pl.BlockSpec((1, 1, KV_RANK), seq3),
                   pl.BlockSpec((1, 1, ROPE_DIM), seq3)]
                  + [pl.BlockSpec((1, PAGE_SIZE, KV_RANK), page_map(i)) for i in range(pg)]
                  + [pl.BlockSpec((1, PAGE_SIZE, ROPE_DIM), page_map(i)) for i in range(pg)]),
        out_specs=pl.BlockSpec((1, N_HEADS, KV_RANK), seq3),
        scratch_shapes=[pltpu.VMEM((N_HEADS, 1), F32),
                        pltpu.VMEM((N_HEADS, 1), F32),
                        pltpu.VMEM((N_HEADS, KV_RANK), F32)],
    )
    return pl.pallas_call(
        kern,
        grid_spec=grid_spec,
        out_shape=jax.ShapeDtypeStruct((b, N_HEADS, KV_RANK), F32),
        compiler_params=_cparams(("parallel", "arbitrary")),
        name="decode_attn",
    )(page_table.reshape(-1), q_lat, q_rope, kv_self, kr_self,
      *([cache_lat] * pg), *([cache_rope] * pg))


def _uv_kernel(o_ref, w_ref, y_ref):
    y_ref[...] = _mm(o_ref[...], w_ref[...]).astype(BF16)


def _uv_proj(o_lat, wuv_bd):
    rows = o_lat.shape[0]
    return pl.pallas_call(
        _uv_kernel,
        out_shape=jax.ShapeDtypeStruct((rows, ATTN_WIDTH), BF16),
        compiler_params=pltpu.CompilerParams(vmem_limit_bytes=VMEM_LIMIT),
        name="uv_proj",
    )(o_lat, wuv_bd)


def _outproj_kernel(oa_ref, ys_ref, x_ref, g1_ref, sc_ref, sh_ref, wout_ref, gpost_ref, gpre_ref,
                    wr_ref, br_ref, x1_ref, h2_ref, comb_ref):
    mix = (jnp.dot(oa_ref[0], wout_ref[:ATTN_WIDTH].astype(BF16), preferred_element_type=F32)
           + jnp.dot(ys_ref[0], wout_ref[ATTN_WIDTH:].astype(BF16), preferred_element_type=F32))
    x1 = x_ref[0] + g1_ref[0] * _rms(mix, gpost_ref[...], D_MODEL)
    x1_ref[0] = x1
    h2 = _rms(x1, gpre_ref[...], D_MODEL) * (1.0 + sc_ref[0]) + sh_ref[0]
    h2_ref[0] = h2.astype(BF16)
    logits = _mm3(h2, wr_ref[...]) + br_ref[...]
    lane = lax.broadcasted_iota(jnp.int32, logits.shape, 1).astype(F32)
    vals, sels = [], []
    for _ in range(TOP_K):
        v = jnp.max(logits, axis=-1, keepdims=True)
        idx = jnp.min(jnp.where(logits == v, lane, float(LANES)), axis=-1, keepdims=True)
        sel = lane == idx
        vals.append(v)
        sels.append(sel)
        logits = jnp.where(sel, -jnp.inf, logits)
    exps = [jnp.exp(v - vals[0]) for v in vals]
    denom = exps[0] + exps[1] + exps[2] + exps[3]
    comb = jnp.zeros(logits.shape, F32)
    for e, sel in zip(exps, sels):
        comb = jnp.where(sel, e / denom, comb)
    comb_ref[0] = comb


def _outproj(o_attn, y_ssm, x, g1, sc2, sh2, p, tm):
    n, l, _ = x.shape
    per_row = g1.shape[1] != 1
    mod_blk = (1, tm, D_MODEL) if per_row else (1, 1, D_MODEL)
    mod_map = (lambda b, i: (b, i, 0)) if per_row else (lambda b, i: (b, 0, 0))
    row_map = lambda b, i: (b, i, 0)
    const2 = lambda b, i: (0, 0)
    return pl.pallas_call(
        _outproj_kernel,
        grid=(n, l // tm),
        in_specs=[pl.BlockSpec((1, tm, ATTN_WIDTH), row_map),
                  pl.BlockSpec((1, tm, SSM_WIDTH), row_map),
                  pl.BlockSpec((1, tm, D_MODEL), row_map),
                  pl.BlockSpec(mod_blk, mod_map),
                  pl.BlockSpec(mod_blk, mod_map),
                  pl.BlockSpec(mod_blk, mod_map),
                  pl.BlockSpec((D_MODEL, D_MODEL), const2),
                  pl.BlockSpec((1, D_MODEL), const2),
                  pl.BlockSpec((1, D_MODEL), const2),
                  pl.BlockSpec((D_MODEL, LANES), const2),
                  pl.BlockSpec((1, LANES), const2)],
        out_specs=[pl.BlockSpec((1, tm, D_MODEL), row_map),
                   pl.BlockSpec((1, tm, D_MODEL), row_map),
                   pl.BlockSpec((1, tm, LANES), row_map)],
        out_shape=[jax.ShapeDtypeStruct((n, l, D_MODEL), F32),
                   jax.ShapeDtypeStruct((n, l, D_MODEL), BF16),
                   jax.ShapeDtypeStruct((n, l, LANES), F32)],
        compiler_params=_cparams(("parallel", "parallel")),
        name="outproj",
    )(o_attn, y_ssm, x, g1, sc2, sh2, p["w_out"], p["g_post_mix"], p["g_pre_ffn"],
      p["w_router"], p["b_router"])


def _moe_kernel(h_ref, comb_ref, w1g_ref, w1l_ref, b1g_ref, b1l_ref, w2_ref, b2_ref, o_ref, acc_ref):
    e = pl.program_id(1)

    @pl.when(e == 0)
    def _():
        acc_ref[...] = jnp.zeros(acc_ref.shape, F32)

    h = h_ref[...]
    glu = jnp.dot(h, w1g_ref[0], preferred_element_type=F32) + b1g_ref[0]
    lin = jnp.dot(h, w1l_ref[0], preferred_element_type=F32) + b1l_ref[0]
    glu = jnp.minimum(glu, SWIGLU_LIMIT)
    lin = jnp.clip(lin, -SWIGLU_LIMIT, SWIGLU_LIMIT)
    act = glu * _sigmoid(SWIGLU_ALPHA * glu) * (lin + 1.0)
    y = jnp.dot(act.astype(BF16), w2_ref[0], preferred_element_type=F32) + b2_ref[0]
    comb = comb_ref[...]
    lane = lax.broadcasted_iota(jnp.int32, comb.shape, 1)
    gate = jnp.sum(jnp.where(lane == e, comb, 0.0), axis=-1, keepdims=True)
    acc_ref[...] += gate * y

    @pl.when(e == pl.num_programs(1) - 1)
    def _():
        o_ref[...] = acc_ref[...]


def _moe(h2, comb, mp, tm):
    t = h2.shape[0]
    d_exp = mp["w2"].shape[1]
    w_map = lambda i, e: (e, 0, 0)
    row_map = lambda i, e: (i, 0)
    return pl.pallas_call(
        _moe_kernel,
        grid=(t // tm, N_EXPERTS),
        in_specs=[pl.BlockSpec((tm, D_MODEL), row_map),
                  pl.BlockSpec((tm, LANES), row_map),
                  pl.BlockSpec((1, D_MODEL, d_exp), w_map),
                  pl.BlockSpec((1, D_MODEL, d_exp), w_map),
                  pl.BlockSpec((1, 1, d_exp), w_map),
                  pl.BlockSpec((1, 1, d_exp), w_map),
                  pl.BlockSpec((1, d_exp, D_MODEL), w_map),
                  pl.BlockSpec((1, 1, D_MODEL), w_map)],
        out_specs=pl.BlockSpec((tm, D_MODEL), row_map),
        out_shape=jax.ShapeDtypeStruct((t, D_MODEL), F32),
        scratch_shapes=[pltpu.VMEM((tm, D_MODEL), F32)],
        compiler_params=_cparams(("parallel", "arbitrary")),
        name="moe",
    )(h2, comb, mp["w1g"], mp["w1l"], mp["b1g"], mp["b1l"], mp["w2"], mp["b2"])


def _final_kernel(x1_ref, f_ref, g2_ref, gpost_ref, y_ref):
    y_ref[0] = x1_ref[0] + g2_ref[0] * _rms(f_ref[0], gpost_ref[...], D_MODEL)


def _final(x1, f, g2, g_post_ffn, tm):
    n, l, _ = x1.shape
    per_row = g2.shape[1] != 1
    mod_blk = (1, tm, D_MODEL) if per_row else (1, 1, D_MODEL)
    mod_map = (lambda b, i: (b, i, 0)) if per_row else (lambda b, i: (b, 0, 0))
    row_map = lambda b, i: (b, i, 0)
    return pl.pallas_call(
        _final_kernel,
        grid=(n, l // tm),
        in_specs=[pl.BlockSpec((1, tm, D_MODEL), row_map),
                  pl.BlockSpec((1, tm, D_MODEL), row_map),
                  pl.BlockSpec(mod_blk, mod_map),
                  pl.BlockSpec((1, D_MODEL), lambda b, i: (0, 0))],
        out_specs=pl.BlockSpec((1, tm, D_MODEL), row_map),
        out_shape=jax.ShapeDtypeStruct((n, l, D_MODEL), F32),
        compiler_params=_cparams(("parallel", "parallel")),
        name="final",
    )(x1, f, g2, g_post_ffn)


def _complex_pow2(lam, n):
    re, im = lam[0], lam[1]
    steps = int(round(math.log2(n)))
    assert 2 ** steps == n
    for _ in range(steps):
        re, im = re * re - im * im, 2.0 * re * im
    return jnp.stack([re, im])


def _rope_tables(pos):
    half = ROPE_DIM // 2
    inv = ROPE_THETA ** (-jnp.arange(half, dtype=F32) / half)
    ang = pos.astype(F32)[:, None] * inv[None, :]
    cos, sin = jnp.cos(ang), jnp.sin(ang)
    cos4 = jnp.tile(jnp.concatenate([cos, cos], axis=-1), (1, LANES // ROPE_DIM))
    sin4 = jnp.tile(jnp.concatenate([-sin, sin], axis=-1), (1, LANES // ROPE_DIM))
    return cos4[None], sin4[None]


def _swap_halves(w):
    half = ROPE_DIM // 2
    return jnp.concatenate([w[..., half:], w[..., :half]], axis=-1)


def _prep_layer(w, layer):
    g = lambda name: w[name][layer]
    p = {}
    row = lambda v: v.reshape(1, -1)
    p["g_pre_mix"] = row(g("g_pre_mix"))
    w_in = g("w_in")
    w_q, w_kv, w_r, w_u = jnp.split(w_in, [Q_RANK, Q_RANK + KV_RANK, Q_RANK + KV_RANK + ROPE_DIM], axis=-1)
    rep = LANES // ROPE_DIM
    p["w_in"] = jnp.concatenate(
        [w_q, jnp.zeros((D_MODEL, Q_PAD - Q_RANK), F32), w_kv,
         jnp.tile(w_r, (1, rep)), jnp.tile(_swap_halves(w_r), (1, rep)), w_u], axis=-1)
    p["g_q"] = row(jnp.pad(g("g_q"), (0, Q_PAD - Q_RANK)))
    w_uq = g("w_uq").reshape(Q_RANK, N_HEADS, NOPE_DIM + ROPE_DIM)
    nope = w_uq[:, :, :NOPE_DIM].reshape(Q_RANK, N_HEADS * NOPE_DIM)
    rope = w_uq[:, :, NOPE_DIM:]
    rope_sw = _swap_halves(rope)
    w_uq_p = jnp.concatenate([nope, rope.reshape(Q_RANK, -1), rope_sw.reshape(Q_RANK, -1)], axis=-1)
    p["w_uq"] = jnp.pad(w_uq_p, ((0, Q_PAD - Q_RANK), (0, 0)))
    w_uk = jnp.transpose(g("w_uk"), (1, 2, 0))
    z = jnp.zeros((NOPE_DIM, KV_RANK), F32)
    p["w_uk"] = jnp.stack([jnp.block([[w_uk[2 * j], z], [z, w_uk[2 * j + 1]]])
                           for j in range(N_HEADS // 2)])
    p["g_kv"] = row(g("g_kv"))
    w_uv = jnp.transpose(g("w_uv"), (1, 0, 2))
    zv = jnp.zeros((KV_RANK, V_DIM), F32)
    p["w_uv_pair"] = jnp.stack([jnp.block([[w_uv[2 * j], zv], [zv, w_uv[2 * j + 1]]])
                                for j in range(N_HEADS // 2)])
    eye_h = jnp.eye(N_HEADS, dtype=F32)
    p["w_uv_bd"] = jnp.einsum("hrv,hk->hrkv", w_uv, eye_h).reshape(N_HEADS * KV_RANK, ATTN_WIDTH)
    p["w_out"] = g("w_out")
    p["g_post_mix"] = row(g("g_post_mix"))
    p["g_pre_ffn"] = row(g("g_pre_ffn"))
    p["w_router"] = jnp.pad(g("w_router"), ((0, 0), (0, LANES - N_EXPERTS)))
    p["b_router"] = row(jnp.pad(g("b_router"), (0, LANES - N_EXPERTS), constant_values=NEG_INF))
    p["g_post_ffn"] = row(g("g_post_ffn"))

    lam_re, lam_im = g("ssm_lam_re"), g("ssm_lam_im")
    step = jnp.exp(g("ssm_log_step"))[:, None]
    mag = jnp.exp(lam_re * step)
    bar_re, bar_im = mag * jnp.cos(lam_im * step), mag * jnp.sin(lam_im * step)
    den = lam_re * lam_re + lam_im * lam_im
    co_re = ((bar_re - 1.0) * lam_re + bar_im * lam_im) / den
    co_im = (bar_im * lam_re - (bar_re - 1.0) * lam_im) / den
    b_re, b_im = g("ssm_b_re"), g("ssm_b_im")
    bb_re = co_re[..., None] * b_re - co_im[..., None] * b_im
    bb_im = co_re[..., None] * b_im + co_im[..., None] * b_re
    eye_g = jnp.eye(GROUPS_PER_BLOCK, dtype=F32)

    def b_block(m):
        m = m.reshape(SSM_BLOCKS, GROUPS_PER_BLOCK, SSM_STATE, SSM_GROUP)
        return jnp.einsum("bgpc,gh->bgchp", m, eye_g).reshape(SSM_BLOCKS, LANES, BLOCK_STATES)

    def c_block(m):
        m = m.reshape(SSM_BLOCKS, GROUPS_PER_BLOCK, SSM_GROUP, SSM_STATE)
        return jnp.einsum("bgcp,gh->bgphc", m, eye_g).reshape(SSM_BLOCKS, BLOCK_STATES, LANES)

    sp = {
        "lam": jnp.stack([bar_re.reshape(-1), bar_im.reshape(-1)]),
        "bmat": jnp.concatenate([b_block(bb_re), b_block(bb_im)], axis=-1).astype(BF16),
        "cmat": jnp.concatenate([c_block(g("ssm_c_re")), c_block(-g("ssm_c_im"))], axis=1).astype(BF16),
        "d": row(g("ssm_d")),
        "w_glu": g("ssm_w_glu"),
        "b_glu": row(g("ssm_b_glu")),
    }
    w1 = g("w1")
    b1 = g("b1")
    mp = {
        "w1g": w1[:, :, 0::2].astype(BF16),
        "w1l": w1[:, :, 1::2].astype(BF16),
        "b1g": b1[:, None, 0::2],
        "b1l": b1[:, None, 1::2],
        "w2": g("w2").astype(BF16),
        "b2": g("b2")[:, None, :],
    }
    return p, sp, mp


def _mods(mod_rows):
    return [mod_rows[:, k * D_MODEL:(k + 1) * D_MODEL] for k in range(N_MOD)]


def _pick_tile(n, target, mult):
    best = mult
    for c in range(mult, min(n, target) + 1, mult):
        if n % c == 0:
            best = c
    return best


def kernel(x_prompt, x_sample, c_prompt, c_sample, cache_kv_latent, cache_k_rope, state_ssm, page_table, w_ada, b_ada, g_pre_mix, w_in, g_q, w_uq, w_uk, g_kv, w_uv, ssm_lam_re, ssm_lam_im, ssm_log_step, ssm_b_re, ssm_b_im, ssm_c_re, ssm_c_im, ssm_d, ssm_w_glu, ssm_b_glu, w_out, g_post_mix, g_pre_ffn, w_router, b_router, w1, b1, w2, b2, g_post_ffn):
    weights = dict(w_ada=w_ada, b_ada=b_ada, g_pre_mix=g_pre_mix, w_in=w_in, g_q=g_q, w_uq=w_uq,
                   w_uk=w_uk, g_kv=g_kv, w_uv=w_uv, ssm_lam_re=ssm_lam_re, ssm_lam_im=ssm_lam_im,
                   ssm_log_step=ssm_log_step, ssm_b_re=ssm_b_re, ssm_b_im=ssm_b_im,
                   ssm_c_re=ssm_c_re, ssm_c_im=ssm_c_im, ssm_d=ssm_d, ssm_w_glu=ssm_w_glu,
                   ssm_b_glu=ssm_b_glu, w_out=w_out, g_post_mix=g_post_mix, g_pre_ffn=g_pre_ffn,
                   w_router=w_router, b_router=b_router, w1=w1, b1=b1, w2=w2, b2=b2,
                   g_post_ffn=g_post_ffn)
    depth = w_ada.shape[0]
    n_p, l_p, _ = x_prompt.shape
    n_s, l_s, _ = x_sample.shape
    assert l_s == 1
    n_pages = page_table.shape[1]
    past_len = n_pages * PAGE_SIZE

    cos_p, sin_p = _rope_tables(jnp.arange(l_p, dtype=jnp.int32))
    cos_s, sin_s = _rope_tables(jnp.full((1,), past_len, jnp.int32))
    cos_s = jnp.broadcast_to(cos_s, (1, n_s, LANES))
    sin_s = jnp.broadcast_to(sin_s, (1, n_s, LANES))

    tm_p = _pick_tile(l_p, 512, 16)
    tq = _pick_tile(l_p, 256, 16)
    tk = _pick_tile(l_p, 512, 16)
    lc = 64 if l_p % 512 == 0 else l_p // SUBLANES
    pg = _pick_tile(n_pages, 16, 1)
    c_rows = n_p + n_s
    c_pad = -c_rows % SUBLANES
    c_all = jnp.pad(jnp.concatenate([c_prompt, c_sample], axis=0), ((0, c_pad), (0, 0)))

    y_p = x_prompt
    y_s = x_sample.reshape(1, n_s, D_MODEL)
    outs = [[] for _ in range(6)]
    for layer in range(depth):
        p, sp, mp = _prep_layer(weights, layer)
        mod = _ada(c_all, w_ada[layer], b_ada[layer].reshape(1, -1))
        sh1p, sc1p, g1p, sh2p, sc2p, g2p = [m[:, None, :] for m in _mods(mod[:n_p])]
        sh1s, sc1s, g1s, sh2s, sc2s, g2s = [m[None] for m in _mods(mod[n_p:c_rows])]

        q_cat, k_cat, kv_p, kr_p, u_p = _inproj(y_p, sc1p, sh1p, p, cos_p, sin_p, tm_p)
        o_attn_p = _flash(q_cat, k_cat, p["w_uv_pair"], tq, tk)
        y_ssm_p, h_last_p = _s5_prompt(u_p, sp, lc)
        x1_p, h2_p, comb_p = _outproj(o_attn_p, y_ssm_p, y_p, g1p, sc2p, sh2p, p, tm_p)

        q_cat_s, _, kv_s, kr_s, u_s = _inproj(y_s, sc1s, sh1s, p, cos_s, sin_s, n_s)
        q_lat_s = jnp.transpose(q_cat_s[0, :, :, :KV_RANK], (1, 0, 2))
        q_rope_s = jnp.stack(
            [q_cat_s[0, h, :, KV_RANK + ROPE_DIM * (h % 4):KV_RANK + ROPE_DIM * (h % 4 + 1)]
             for h in range(N_HEADS)], axis=1)
        o_lat_s = _decode_attn(q_lat_s, q_rope_s, kv_s.reshape(n_s, 1, KV_RANK),
                               kr_s.reshape(n_s, 1, ROPE_DIM), cache_kv_latent[layer],
                               cache_k_rope[layer], page_table, pg)
        o_attn_s = _uv_proj(o_lat_s.reshape(n_s, N_HEADS * KV_RANK), p["w_uv_bd"])
        h0 = jnp.moveaxis(state_ssm[layer].reshape(n_s, N_STATE, 2), -1, 0)
        y_ssm_s, h_s = _s5_step(u_s[0], h0, sp)
        x1_s, h2_s, comb_s = _outproj(o_attn_s[None], y_ssm_s[None], y_s, g1s, sc2s, sh2s, p, n_s)

        t_p = n_p * l_p
        t_all = t_p + n_s
        tm_moe = _pick_tile(t_all, 1024, 16)
        h2_all = jnp.concatenate([h2_p.reshape(t_p, D_MODEL), h2_s[0]], axis=0)
        comb_all = jnp.concatenate([comb_p.reshape(t_p, LANES), comb_s[0]], axis=0)
        f_all = _moe(h2_all, comb_all, mp, tm_moe)
        y_p = _final(x1_p, f_all[:t_p].reshape(n_p, l_p, D_MODEL), g2p, p["g_post_ffn"], tm_p)
        y_s = _final(x1_s, f_all[t_p:][None], g2s, p["g_post_ffn"], n_s)

        outs[0].append(kv_p)
        outs[1].append(kr_p)
        outs[2].append(jnp.moveaxis(h_last_p, 1, -1).reshape(n_p, N_SSM_GROUPS, SSM_STATE, 2))
        outs[3].append(kv_s.reshape(n_s, 1, KV_RANK))
        outs[4].append(kr_s.reshape(n_s, 1, ROPE_DIM))
        outs[5].append(jnp.moveaxis(h_s, 0, -1).reshape(n_s, N_SSM_GROUPS, SSM_STATE, 2))
    return (y_p, y_s.reshape(n_s, 1, D_MODEL)) + tuple(jnp.stack(o) for o in outs)
```

```python
import functools
import math

import jax
import jax.numpy as jnp
from jax import lax
from jax.experimental import pallas as pl
from jax.experimental.pallas import tpu as pltpu

F32 = jnp.float32
BF16 = jnp.bfloat16

D_MODEL = 1024
N_HEADS = 8
NOPE_DIM = 64
ROPE_DIM = 32
V_DIM = 64
Q_RANK = 352
KV_RANK = 128
ROPE_THETA = 10000.0
ATTN_WIDTH = N_HEADS * V_DIM
ATTN_SCALE = 1.0 / math.sqrt(NOPE_DIM + ROPE_DIM)
SSM_WIDTH = D_MODEL - ATTN_WIDTH
SSM_GROUP = 16
N_SSM_GROUPS = SSM_WIDTH // SSM_GROUP
SSM_STATE = 64
N_STATE = N_SSM_GROUPS * SSM_STATE
N_EXPERTS = 32
TOP_K = 4
SWIGLU_LIMIT = 7.0
SWIGLU_ALPHA = 1.702
RMS_EPS = 1e-6
N_MOD = 6
NEG_INF = -1e30
PAGE_SIZE = 128

LANES = 128
SUBLANES = 8
VMEM_LIMIT = 56 * 1024 * 1024

Q_PAD = 384
CAT = 2 * LANES
SSM_BLOCKS = 4
GROUPS_PER_BLOCK = N_SSM_GROUPS // SSM_BLOCKS
BLOCK_STATES = GROUPS_PER_BLOCK * SSM_STATE
IN_PAD = Q_PAD + KV_RANK + 2 * LANES + SSM_WIDTH
UQ_PAD = N_HEADS * NOPE_DIM + 4 * LANES
GLU_BLOCK = 2 * LANES
MOE_TILE = 192
MOE_COL_PARTS = 4
LOG2E = math.log2(math.e)


def _cparams(sem):
    return pltpu.CompilerParams(dimension_semantics=sem, vmem_limit_bytes=VMEM_LIMIT)


def _mm(a, b):
    return jnp.dot(a.astype(BF16), b.astype(BF16), preferred_element_type=F32)


def _mm_nt(a, b):
    return lax.dot_general(a.astype(BF16), b.astype(BF16), (((1,), (1,)), ((), ())),
                           preferred_element_type=F32)


def _split_bf16(x):
    hi = x.astype(BF16)
    lo = (x - hi.astype(F32)).astype(BF16)
    return hi, lo


def _mm3(a, b):
    ah, al = _split_bf16(a)
    bh, bl = _split_bf16(b)
    d = functools.partial(jnp.dot, preferred_element_type=F32)
    return d(ah, bh) + (d(ah, bl) + d(al, bh))


def _sigmoid(x):
    return 1.0 / (1.0 + jnp.exp(-x))


def _rms(x, g, width):
    ms = jnp.sum(x * x, axis=-1, keepdims=True) * (1.0 / width)
    return x * lax.rsqrt(ms + RMS_EPS) * g


def _gelu_tanh(x):
    return 0.5 * x * (1.0 + jnp.tanh(math.sqrt(2.0 / math.pi) * (x + 0.044715 * (x * x * x))))


def _ada_kernel(c_ref, w_ref, b_ref, o_ref):
    c = c_ref[...]
    o_ref[...] = _mm3(c * _sigmoid(c), w_ref[...]) + b_ref[...]


def _ada(c_all, w_ada, b_ada):
    rows = c_all.shape[0]
    tn = 512
    n_out = w_ada.shape[1]
    return pl.pallas_call(
        _ada_kernel,
        grid=(n_out // tn,),
        in_specs=[pl.BlockSpec((rows, D_MODEL), lambda j: (0, 0)),
                  pl.BlockSpec((D_MODEL, tn), lambda j: (0, j)),
                  pl.BlockSpec((1, tn), lambda j: (0, j))],
        out_specs=pl.BlockSpec((rows, tn), lambda j: (0, j)),
        out_shape=jax.ShapeDtypeStruct((rows, n_out), F32),
        compiler_params=_cparams(("parallel",)),
        name="adaln",
    )(c_all, w_ada, b_ada)


def _inproj_kernel(x_ref, sc_ref, sh_ref, gpre_ref, win_ref, gq_ref, wuq_ref, wuk_ref, gkv_ref,
                   cos_ref, sin_ref, q_ref, kcat_ref, kv_ref, kr_ref, u_ref):
    x = x_ref[0]
    h = _rms(x, gpre_ref[...], D_MODEL) * (1.0 + sc_ref[0]) + sh_ref[0]
    z = _mm(h, win_ref[...])
    zq = z[:, :Q_PAD]
    zkv = z[:, Q_PAD:Q_PAD + KV_RANK]
    zr = z[:, Q_PAD + KV_RANK:Q_PAD + KV_RANK + LANES]
    zrs = z[:, Q_PAD + KV_RANK + LANES:Q_PAD + KV_RANK + 2 * LANES]
    u = z[:, Q_PAD + KV_RANK + 2 * LANES:]
    cos4 = cos_ref[0]
    sin4 = sin_ref[0]

    kvl = _rms(zkv, gkv_ref[...], KV_RANK)
    kr4 = zr * cos4 + zrs * sin4
    kv_ref[0] = kvl
    kr_ref[0] = kr4[:, :ROPE_DIM]
    kcat_ref[0] = jnp.concatenate([kvl, kr4], axis=-1).astype(BF16)
    u_ref[0] = u.astype(BF16)

    qn = _rms(zq, gq_ref[...], Q_RANK)
    q = _mm(qn, wuq_ref[...])
    nope_w = N_HEADS * NOPE_DIM
    rope_a = q[:, nope_w:nope_w + LANES] * cos4 + q[:, nope_w + 2 * LANES:nope_w + 3 * LANES] * sin4
    rope_b = q[:, nope_w + LANES:nope_w + 2 * LANES] * cos4 + q[:, nope_w + 3 * LANES:] * sin4
    slot = lax.broadcasted_iota(jnp.int32, (1, LANES), 1) // ROPE_DIM
    for j in range(N_HEADS // 2):
        ql = _mm(q[:, j * LANES:(j + 1) * LANES], wuk_ref[j])
        for hh in range(2):
            head = 2 * j + hh
            rope = rope_a if head < 4 else rope_b
            rope_h = jnp.where(slot == (head % 4), rope, 0.0)
            qc = jnp.concatenate([ql[:, hh * KV_RANK:(hh + 1) * KV_RANK], rope_h], axis=-1)
            q_ref[0, head] = (qc * (ATTN_SCALE * LOG2E)).astype(BF16)


def _inproj(x, sc, sh, p, cos4, sin4, tm):
    n, l, _ = x.shape
    per_row = sc.shape[1] != 1
    mod_blk = (1, tm, D_MODEL) if per_row else (1, 1, D_MODEL)
    mod_map = (lambda b, i: (b, i, 0)) if per_row else (lambda b, i: (b, 0, 0))
    const2 = lambda b, i: (0, 0)
    const3 = lambda b, i: (0, 0, 0)
    pos_map = lambda b, i: (0, i, 0)
    return pl.pallas_call(
        _inproj_kernel,
        grid=(n, l // tm),
        in_specs=[pl.BlockSpec((1, tm, D_MODEL), lambda b, i: (b, i, 0)),
                  pl.BlockSpec(mod_blk, mod_map),
                  pl.BlockSpec(mod_blk, mod_map),
                  pl.BlockSpec((1, D_MODEL), const2),
                  pl.BlockSpec((D_MODEL, IN_PAD), const2),
                  pl.BlockSpec((1, Q_PAD), const2),
                  pl.BlockSpec((Q_PAD, UQ_PAD), const2),
                  pl.BlockSpec((N_HEADS // 2, LANES, 2 * KV_RANK), const3),
                  pl.BlockSpec((1, KV_RANK), const2),
                  pl.BlockSpec((1, tm, LANES), pos_map),
                  pl.BlockSpec((1, tm, LANES), pos_map)],
        out_specs=[pl.BlockSpec((1, N_HEADS, tm, CAT), lambda b, i: (b, 0, i, 0)),
                   pl.BlockSpec((1, tm, CAT), lambda b, i: (b, i, 0)),
                   pl.BlockSpec((1, tm, KV_RANK), lambda b, i: (b, i, 0)),
                   pl.BlockSpec((1, tm, ROPE_DIM), lambda b, i: (b, i, 0)),
                   pl.BlockSpec((1, tm, SSM_WIDTH), lambda b, i: (b, i, 0))],
        out_shape=[jax.ShapeDtypeStruct((n, N_HEADS, l, CAT), BF16),
                   jax.ShapeDtypeStruct((n, l, CAT), BF16),
                   jax.ShapeDtypeStruct((n, l, KV_RANK), F32),
                   jax.ShapeDtypeStruct((n, l, ROPE_DIM), F32),
                   jax.ShapeDtypeStruct((n, l, SSM_WIDTH), BF16)],
        compiler_params=_cparams(("parallel", "parallel")),
        name="inproj",
    )(x, sc, sh, p["g_pre_mix"], p["w_in"], p["g_q"], p["w_uq"], p["w_uk"], p["g_kv"], cos4, sin4)


def _flash_kernel(q_ref, k_ref, wuv_ref, o_ref, m_ref, l_ref, acc_ref, *, tq, tk):
    i = pl.program_id(1)
    j = pl.program_id(2)
    j_last = ((i + 1) * tq - 1) // tk
    rows = N_HEADS * tq

    @pl.when(j == 0)
    def _():
        m_ref[...] = jnp.full((rows, 1), NEG_INF, F32)
        l_ref[...] = jnp.zeros((rows, 1), F32)
        acc_ref[...] = jnp.zeros((rows, KV_RANK), F32)

    def update(masked):
        q = q_ref[0].reshape(rows, CAT)
        k = k_ref[0]
        s = lax.dot_general(q, k, (((1,), (1,)), ((), ())), preferred_element_type=F32)
        if masked:
            q_pos = i * tq + lax.broadcasted_iota(jnp.int32, (tq, tk), 0)
            k_pos = j * tk + lax.broadcasted_iota(jnp.int32, (tq, tk), 1)
            s = jnp.where((k_pos <= q_pos)[None], s.reshape(N_HEADS, tq, tk), NEG_INF).reshape(rows, tk)
        m_prev = m_ref[...]
        m_new = jnp.maximum(m_prev, jnp.max(s, axis=-1, keepdims=True))
        alpha = jnp.exp2(m_prev - m_new)
        p = jnp.exp2(s - m_new)
        l_ref[...] = alpha * l_ref[...] + jnp.sum(p, axis=-1, keepdims=True)
        acc_ref[...] = alpha * acc_ref[...] + jnp.dot(p.astype(BF16), k[:, :KV_RANK],
                                                      preferred_element_type=F32)
        m_ref[...] = m_new

    crosses = (j + 1) * tk - 1 > i * tq

    @pl.when(jnp.logical_and(j <= j_last, crosses))
    def _():
        update(True)

    @pl.when(jnp.logical_and(j <= j_last, jnp.logical_not(crosses)))
    def _():
        update(False)

    @pl.when(j == j_last)
    def _():
        o = acc_ref[...] / l_ref[...]
        outs = []
        for jj in range(N_HEADS // 2):
            pair = jnp.concatenate([o[(2 * jj) * tq:(2 * jj + 1) * tq],
                                    o[(2 * jj + 1) * tq:(2 * jj + 2) * tq]], axis=-1)
            outs.append(_mm(pair, wuv_ref[jj]))
        o_ref[0] = jnp.concatenate(outs, axis=-1).astype(BF16)


def _flash(q_cat, k_cat, wuv_pair, tq, tk):
    n, _, l, _ = q_cat.shape
    kern = functools.partial(_flash_kernel, tq=tq, tk=tk)

    def k_map(b, i, j):
        return (b, jnp.minimum(j, ((i + 1) * tq - 1) // tk), 0)

    return pl.pallas_call(
        kern,
        grid=(n, l // tq, l // tk),
        in_specs=[pl.BlockSpec((1, N_HEADS, tq, CAT), lambda b, i, j: (b, 0, i, 0)),
                  pl.BlockSpec((1, tk, CAT), k_map),
                  pl.BlockSpec((N_HEADS // 2, 2 * KV_RANK, 2 * V_DIM), lambda b, i, j: (0, 0, 0))],
        out_specs=pl.BlockSpec((1, tq, ATTN_WIDTH), lambda b, i, j: (b, i, 0)),
        out_shape=jax.ShapeDtypeStruct((n, l, ATTN_WIDTH), BF16),
        scratch_shapes=[pltpu.VMEM((N_HEADS * tq, 1), F32),
                        pltpu.VMEM((N_HEADS * tq, 1), F32),
                        pltpu.VMEM((N_HEADS * tq, KV_RANK), F32)],
        compiler_params=_cparams(("parallel", "parallel", "arbitrary")),
        name="flash",
    )(q_cat, k_cat, wuv_pair)


def _ssm_out(y, up, d_ref, wglu_ref, bglu_ref):
    y = _gelu_tanh(y + d_ref[...] * up)
    return y * _sigmoid(_mm(y, wglu_ref[...]) + bglu_ref[...])


def _s5_kernel(u_ref, perm_ref, permt_ref, bmat_ref, cmat_ref, lam_ref, laml_ref, d_ref, wglu_ref,
               bglu_ref, y_ref, hlast_ref, buf_ref, carry_ref, *, lc):
    j = pl.program_id(1)

    @pl.when(j == 0)
    def _():
        carry_ref[...] = jnp.zeros(carry_ref.shape, F32)

    u = u_ref[0]
    up = jnp.dot(perm_ref[...], u, preferred_element_type=F32).astype(BF16)
    blk_w = 2 * BLOCK_STATES
    for b in range(SSM_BLOCKS):
        buf_ref[:, b * blk_w:(b + 1) * blk_w] = jnp.dot(
            up[:, b * LANES:(b + 1) * LANES], bmat_ref[b], preferred_element_type=F32)

    for b in range(SSM_BLOCKS):
        st = slice(b * BLOCK_STATES, (b + 1) * BLOCK_STATES)
        re_cols = slice(b * blk_w, b * blk_w + BLOCK_STATES)
        im_cols = slice(b * blk_w + BLOCK_STATES, (b + 1) * blk_w)
        lr = lam_ref[0:1, st]
        li = lam_ref[1:2, st]

        def run(init, store):
            def body(t, carry):
                re, im = carry
                rows = pl.ds(pl.multiple_of(t * SUBLANES, SUBLANES), SUBLANES)
                n_re = lr * re - li * im + buf_ref[rows, re_cols]
                n_im = lr * im + li * re + buf_ref[rows, im_cols]
                if store:
                    buf_ref[rows, re_cols] = n_re
                    buf_ref[rows, im_cols] = n_im
                return n_re, n_im
            return lax.fori_loop(0, lc, body, init)

        zero = jnp.zeros((SUBLANES, BLOCK_STATES), F32)
        end_re, end_im = run((zero, zero), False)
        ll_r = laml_ref[0:1, st]
        ll_i = laml_ref[1:2, st]
        c_re = carry_ref[0:1, st]
        c_im = carry_ref[1:2, st]
        init_re, init_im = [], []
        for k in range(SUBLANES):
            init_re.append(c_re)
            init_im.append(c_im)
            c_re, c_im = (ll_r * c_re - ll_i * c_im + end_re[k:k + 1],
                          ll_r * c_im + ll_i * c_re + end_im[k:k + 1])
        carry_ref[0:1, st] = c_re
        carry_ref[1:2, st] = c_im
        run((jnp.concatenate(init_re, axis=0), jnp.concatenate(init_im, axis=0)), True)

    hlast_ref[0] = carry_ref[...]
    ys = [jnp.dot(buf_ref[:, b * blk_w:(b + 1) * blk_w].astype(BF16), cmat_ref[b],
                  preferred_element_type=F32) for b in range(SSM_BLOCKS)]
    y = _ssm_out(jnp.concatenate(ys, axis=-1), up.astype(F32), d_ref, wglu_ref, bglu_ref)
    y_ref[0] = jnp.dot(permt_ref[...], y.astype(BF16), preferred_element_type=F32).astype(BF16)


def _s5_prompt(u, sp, lc):
    n, l, _ = u.shape
    s = SUBLANES * lc
    rows = jnp.arange(s)
    src = (rows % SUBLANES) * lc + rows // SUBLANES
    perm = jnp.zeros((s, s), BF16).at[rows, src].set(1.0)
    laml = _complex_pow2(sp["lam"], lc)
    kern = functools.partial(_s5_kernel, lc=lc)
    c2 = lambda b, j: (0, 0)
    c3 = lambda b, j: (0, 0, 0)
    return pl.pallas_call(
        kern,
        grid=(n, l // s),
        in_specs=[pl.BlockSpec((1, s, SSM_WIDTH), lambda b, j: (b, j, 0)),
                  pl.BlockSpec((s, s), c2),
                  pl.BlockSpec((s, s), c2),
                  pl.BlockSpec((SSM_BLOCKS, LANES, 2 * BLOCK_STATES), c3),
                  pl.BlockSpec((SSM_BLOCKS, 2 * BLOCK_STATES, LANES), c3),
                  pl.BlockSpec((2, N_STATE), c2),
                  pl.BlockSpec((2, N_STATE), c2),
                  pl.BlockSpec((1, SSM_WIDTH), c2),
                  pl.BlockSpec((SSM_WIDTH, SSM_WIDTH), c2),
                  pl.BlockSpec((1, SSM_WIDTH), c2)],
        out_specs=[pl.BlockSpec((1, s, SSM_WIDTH), lambda b, j: (b, j, 0)),
                   pl.BlockSpec((1, 2, N_STATE), lambda b, j: (b, 0, 0))],
        out_shape=[jax.ShapeDtypeStruct((n, l, SSM_WIDTH), BF16),
                   jax.ShapeDtypeStruct((n, 2, N_STATE), F32)],
        scratch_shapes=[pltpu.VMEM((s, 2 * N_STATE), F32),
                        pltpu.VMEM((2, N_STATE), F32)],
        compiler_params=_cparams(("parallel", "arbitrary")),
        name="s5_scan",
    )(u, perm, perm.T, sp["bmat"], sp["cmat"], sp["lam"], laml, sp["d"], sp["w_glu"], sp["b_glu"])


def _s5_step_kernel(u_ref, h0_ref, bmat_ref, cmat_ref, lam_ref, d_ref, wglu_ref, bglu_ref,
                    y_ref, h_ref):
    u = u_ref[...]
    ys = []
    for b in range(SSM_BLOCKS):
        st = slice(b * BLOCK_STATES, (b + 1) * BLOCK_STATES)
        bu = jnp.dot(u[:, b * LANES:(b + 1) * LANES], bmat_ref[b], preferred_element_type=F32)
        lr = lam_ref[0:1, st]
        li = lam_ref[1:2, st]
        re = h0_ref[0, :, st]
        im = h0_ref[1, :, st]
        n_re = lr * re - li * im + bu[:, :BLOCK_STATES]
        n_im = lr * im + li * re + bu[:, BLOCK_STATES:]
        h_ref[0, :, st] = n_re
        h_ref[1, :, st] = n_im
        ys.append(_mm(jnp.concatenate([n_re, n_im], axis=-1), cmat_ref[b]))
    y = _ssm_out(jnp.concatenate(ys, axis=-1), u.astype(F32), d_ref, wglu_ref, bglu_ref)
    y_ref[...] = y.astype(BF16)


def _s5_step(u, h0, sp):
    rows = u.shape[0]
    return pl.pallas_call(
        _s5_step_kernel,
        out_shape=[jax.ShapeDtypeStruct((rows, SSM_WIDTH), BF16),
                   jax.ShapeDtypeStruct((2, rows, N_STATE), F32)],
        compiler_params=pltpu.CompilerParams(vmem_limit_bytes=VMEM_LIMIT),
        name="s5_step",
    )(u, h0, sp["bmat"], sp["cmat"], sp["lam"], sp["d"], sp["w_glu"], sp["b_glu"])


def _decode_kernel(pt_ref, ql_ref, qr_ref, kvs_ref, krs_ref, *rest, pg):
    del pt_ref
    lat_refs = rest[:pg]
    rope_refs = rest[pg:2 * pg]
    o_ref, m_ref, l_ref, acc_ref = rest[2 * pg:]
    j = pl.program_id(1)

    @pl.when(j == 0)
    def _():
        m_ref[...] = jnp.full(m_ref.shape, NEG_INF, F32)
        l_ref[...] = jnp.zeros(l_ref.shape, F32)
        acc_ref[...] = jnp.zeros(acc_ref.shape, F32)

    ql = ql_ref[0]
    qr = qr_ref[0]
    lat = jnp.concatenate([lat_refs[i][0] for i in range(pg)], axis=0).astype(BF16)
    rope_t = jnp.concatenate([rope_refs[i][0] for i in range(pg)], axis=1)
    s = _mm_nt(ql, lat) + _mm(qr, rope_t)
    m_prev = m_ref[...]
    m_new = jnp.maximum(m_prev, jnp.max(s, axis=-1, keepdims=True))
    alpha = jnp.exp2(m_prev - m_new)
    p = jnp.exp2(s - m_new)
    l_ref[...] = alpha * l_ref[...] + jnp.sum(p, axis=-1, keepdims=True)
    acc_ref[...] = alpha * acc_ref[...] + _mm(p, lat)
    m_ref[...] = m_new

    @pl.when(j == pl.num_programs(1) - 1)
    def _():
        kvs = kvs_ref[0]
        krs = krs_ref[0]
        s_self = (jnp.sum(ql.astype(F32) * kvs, axis=-1, keepdims=True)
                  + jnp.sum(qr.astype(F32) * krs, axis=-1, keepdims=True))
        m_p = m_ref[...]
        m_n = jnp.maximum(m_p, s_self)
        a = jnp.exp2(m_p - m_n)
        p_self = jnp.exp2(s_self - m_n)
        l_fin = a * l_ref[...] + p_self
        o_ref[0] = (a * acc_ref[...] + p_self * kvs) / l_fin


def _decode_attn(q_lat, q_rope, kv_self, kr_self, cache_lat, cache_rope, page_table, pg):
    b, n_pages = page_table.shape
    kern = functools.partial(_decode_kernel, pg=pg)

    def page_map(i):
        return lambda s, j, pt: (pt[s * n_pages + j * pg + i], 0, 0)

    seq3 = lambda s, j, pt: (s, 0, 0)
    grid_spec = pltpu.PrefetchScalarGridSpec(
        num_scalar_prefetch=1,
        grid=(b, n_pages // pg),
        in_specs=([pl.BlockSpec((1, N_HEADS, KV_RANK), seq3),
                   pl.BlockSpec((1, N_HEADS, ROPE_DIM), seq3),
                   pl.BlockSpec((1, 1, KV_RANK), seq3),
                   pl.BlockSpec((1, 1, ROPE_DIM), seq3)]
                  + [pl.BlockSpec((1, PAGE_SIZE, KV_RANK), page_map(i)) for i in range(pg)]
                  + [pl.BlockSpec((1, ROPE_DIM, PAGE_SIZE), page_map(i)) for i in range(pg)]),
        out_specs=pl.BlockSpec((1, N_HEADS, KV_RANK), seq3),
        scratch_shapes=[pltpu.VMEM((N_HEADS, 1), F32),
                        pltpu.VMEM((N_HEADS, 1), F32),
                        pltpu.VMEM((N_HEADS, KV_RANK), F32)],
    )
    return pl.pallas_call(
        kern,
        grid_spec=grid_spec,
        out_shape=jax.ShapeDtypeStruct((b, N_HEADS, KV_RANK), F32),
        compiler_params=_cparams(("parallel", "arbitrary")),
        name="decode_attn",
    )(page_table.reshape(-1), q_lat, q_rope, kv_self, kr_self,
      *([cache_lat] * pg), *([cache_rope] * pg))


def _uv_kernel(o_ref, w_ref, y_ref):
    y_ref[...] = _mm(o_ref[...], w_ref[...]).astype(BF16)


def _uv_proj(o_lat, wuv_bd):
    rows = o_lat.shape[0]
    return pl.pallas_call(
        _uv_kernel,
        out_shape=jax.ShapeDtypeStruct((rows, ATTN_WIDTH), BF16),
        compiler_params=pltpu.CompilerParams(vmem_limit_bytes=VMEM_LIMIT),
        name="uv_proj",
    )(o_lat, wuv_bd)


def _outproj_kernel(oa_ref, ys_ref, x_ref, g1_ref, sc_ref, sh_ref, wout_ref, gpost_ref, gpre_ref,
                    wr_ref, br_ref, x1_ref, h2_ref, route_ref):
    mix = (jnp.dot(oa_ref[0], wout_ref[:ATTN_WIDTH].astype(BF16), preferred_element_type=F32)
           + jnp.dot(ys_ref[0], wout_ref[ATTN_WIDTH:].astype(BF16), preferred_element_type=F32))
    x1 = x_ref[0] + g1_ref[0] * _rms(mix, gpost_ref[...], D_MODEL)
    x1_ref[0] = x1
    h2 = _rms(x1, gpre_ref[...], D_MODEL) * (1.0 + sc_ref[0]) + sh_ref[0]
    h2_ref[0] = h2
    logits = _mm3(h2, wr_ref[...]) + br_ref[...]
    lane = lax.broadcasted_iota(jnp.int32, logits.shape, 1).astype(F32)
    vals, idxs = [], []
    for _ in range(TOP_K):
        v = jnp.max(logits, axis=-1, keepdims=True)
        idx = jnp.min(jnp.where(logits == v, lane, float(LANES)), axis=-1, keepdims=True)
        vals.append(v)
        idxs.append(idx)
        logits = jnp.where(lane == idx, -jnp.inf, logits)
    exps = [jnp.exp(v - vals[0]) for v in vals]
    denom = exps[0] + exps[1] + exps[2] + exps[3]
    route = jnp.zeros(logits.shape, F32)
    for k in range(TOP_K):
        route = jnp.where(lane == float(k), idxs[k], route)
        route = jnp.where(lane == float(TOP_K + k), exps[k] / denom, route)
    route_ref[0] = route


def _outproj(o_attn, y_ssm, x, g1, sc2, sh2, p, tm):
    n, l, _ = x.shape
    per_row = g1.shape[1] != 1
    mod_blk = (1, tm, D_MODEL) if per_row else (1, 1, D_MODEL)
    mod_map = (lambda b, i: (b, i, 0)) if per_row else (lambda b, i: (b, 0, 0))
    row_map = lambda b, i: (b, i, 0)
    const2 = lambda b, i: (0, 0)
    return pl.pallas_call(
        _outproj_kernel,
        grid=(n, l // tm),
        in_specs=[pl.BlockSpec((1, tm, ATTN_WIDTH), row_map),
                  pl.BlockSpec((1, tm, SSM_WIDTH), row_map),
                  pl.BlockSpec((1, tm, D_MODEL), row_map),
                  pl.BlockSpec(mod_blk, mod_map),
                  pl.BlockSpec(mod_blk, mod_map),
                  pl.BlockSpec(mod_blk, mod_map),
                  pl.BlockSpec((D_MODEL, D_MODEL), const2),
                  pl.BlockSpec((1, D_MODEL), const2),
                  pl.BlockSpec((1, D_MODEL), const2),
                  pl.BlockSpec((D_MODEL, LANES), const2),
                  pl.BlockSpec((1, LANES), const2)],
        out_specs=[pl.BlockSpec((1, tm, D_MODEL), row_map),
                   pl.BlockSpec((1, tm, D_MODEL), row_map),
                   pl.BlockSpec((1, tm, LANES), row_map)],
        out_shape=[jax.ShapeDtypeStruct((n, l, D_MODEL), F32),
                   jax.ShapeDtypeStruct((n, l, D_MODEL), F32),
                   jax.ShapeDtypeStruct((n, l, LANES), F32)],
        compiler_params=_cparams(("parallel", "parallel")),
        name="outproj",
    )(o_attn, y_ssm, x, g1, sc2, sh2, p["w_out"], p["g_post_mix"], p["g_pre_ffn"],
      p["w_router"], p["b_router"])


def _moe_prep_kernel(w1_ref, w2_ref, perm_ref, w1o_ref, w2o_ref):
    for j in range(w1_ref.shape[2] // GLU_BLOCK):
        cols = slice(j * GLU_BLOCK, (j + 1) * GLU_BLOCK)
        w1o_ref[0, :, cols] = jnp.dot(w1_ref[0, :, cols].astype(BF16), perm_ref[...],
                                      preferred_element_type=F32).astype(BF16)
    w2o_ref[0] = w2_ref[0].astype(BF16)


def _moe_prep(w1, w2):
    n_e, d_in, d_hid = w1.shape
    d_exp = w2.shape[1]
    j = jnp.arange(GLU_BLOCK)
    src = jnp.where(j < LANES, 2 * j, 2 * (j - LANES) + 1)
    perm = (j[:, None] == src[None, :]).astype(BF16)
    e_map = lambda e: (e, 0, 0)
    return pl.pallas_call(
        _moe_prep_kernel,
        grid=(n_e,),
        in_specs=[pl.BlockSpec((1, d_in, d_hid), e_map),
                  pl.BlockSpec((1, d_exp, D_MODEL), e_map),
                  pl.BlockSpec((GLU_BLOCK, GLU_BLOCK), lambda e: (0, 0))],
        out_specs=[pl.BlockSpec((1, d_in, d_hid), e_map),
                   pl.BlockSpec((1, d_exp, D_MODEL), e_map)],
        out_shape=[jax.ShapeDtypeStruct(w1.shape, BF16),
                   jax.ShapeDtypeStruct(w2.shape, BF16)],
        compiler_params=_cparams(("parallel",)),
        name="moe_prep",
    )(w1, w2, perm)


def _route_plan(idx, nch, tmx):
    t = idx.shape[0]
    tc = t // nch
    nt = -(-(tc * TOP_K + N_EXPERTS * (tmx - 1)) // tmx)
    i32 = jnp.int32
    hot = (idx[:, :, None] == jnp.arange(N_EXPERTS, dtype=i32)).astype(i32).sum(1)
    hot = hot.reshape(nch, tc, N_EXPERTS)
    csum = jnp.cumsum(hot, axis=1)
    rank = (csum - hot).reshape(t, N_EXPERTS)
    padded = -(-csum[:, -1, :] // tmx) * tmx
    off_end = jnp.cumsum(padded, axis=1)
    off = off_end - padded
    n_tiles = (off_end[:, -1] // tmx).astype(i32)
    chunk = jnp.arange(t, dtype=i32) // tc
    pos = jnp.take_along_axis(off[chunk] + rank, idx, axis=1).astype(i32)
    tok_local = jnp.broadcast_to((jnp.arange(t, dtype=i32) % tc)[:, None], pos.shape)
    row_tok = jnp.zeros((nch, nt * tmx), i32).at[chunk[:, None], pos].set(tok_local, unique_indices=True)
    tile = jnp.arange(nt, dtype=i32)
    texp_raw = (off_end[:, None, :] <= (tile * tmx)[None, :, None]).sum(-1).astype(i32)
    tile_c = jnp.minimum(tile[None, :], n_tiles[:, None] - 1)
    texp = jnp.take_along_axis(texp_raw, tile_c, axis=1)
    tblk = tile_c + (jnp.arange(nch, dtype=i32) * nt)[:, None]
    return dict(nt=nt, n_tiles=n_tiles, texp=texp.reshape(-1), tblk=tblk.reshape(-1),
                row_tok=row_tok.reshape(nch * nt, 1, tmx), pos=pos)


def _moe_kernel(nt_ref, texp_ref, tblk_ref, tok_ref, src_ref, w1_ref, b1_ref, w2_ref, b2_ref,
                ys_ref, xs_ref):
    del texp_ref, tblk_ref
    c = pl.program_id(0)
    i = pl.program_id(1)
    tmx = xs_ref.shape[0]

    @pl.when(i < nt_ref[c])
    def _():
        def gather(r, carry):
            xs_ref[pl.ds(r, 1), :] = src_ref[0, pl.ds(tok_ref[0, 0, r], 1), :]
            return carry
        lax.fori_loop(0, tmx, gather, 0, unroll=8)
        hid = _mm(xs_ref[...], w1_ref[0]) + b1_ref[0]
        acts = []
        for j in range(hid.shape[1] // GLU_BLOCK):
            glu = jnp.minimum(hid[:, j * GLU_BLOCK:j * GLU_BLOCK + LANES], SWIGLU_LIMIT)
            lin = jnp.clip(hid[:, j * GLU_BLOCK + LANES:(j + 1) * GLU_BLOCK], -SWIGLU_LIMIT, SWIGLU_LIMIT)
            acts.append(glu * _sigmoid(SWIGLU_ALPHA * glu) * (lin + 1.0))
        ys_ref[...] = _mm(jnp.concatenate(acts, axis=-1), w2_ref[0]) + b2_ref[0]

    @pl.when(i >= nt_ref[c])
    def _():
        ys_ref[...] = jnp.zeros(ys_ref.shape, F32)


def _moe_experts(h2, plan, w1p, b1p, w2p, b2, tmx):
    nch, tc, _ = h2.shape
    nt = plan["nt"]
    n_e, _, d_hid = w1p.shape
    d_exp = w2p.shape[1]
    flat = lambda c, i: c * nt + i
    grid_spec = pltpu.PrefetchScalarGridSpec(
        num_scalar_prefetch=3,
        grid=(nch, nt),
        in_specs=[pl.BlockSpec((1, 1, tmx), lambda c, i, n, te, tb: (tb[flat(c, i)], 0, 0),
                               memory_space=pltpu.SMEM),
                  pl.BlockSpec((1, tc, D_MODEL), lambda c, i, n, te, tb: (c, 0, 0),
                               pipeline_mode=pl.Buffered(1)),
                  pl.BlockSpec((1, D_MODEL, d_hid), lambda c, i, n, te, tb: (te[flat(c, i)], 0, 0)),
                  pl.BlockSpec((1, 1, d_hid), lambda c, i, n, te, tb: (te[flat(c, i)], 0, 0)),
                  pl.BlockSpec((1, d_exp, D_MODEL), lambda c, i, n, te, tb: (te[flat(c, i)], 0, 0)),
                  pl.BlockSpec((1, 1, D_MODEL), lambda c, i, n, te, tb: (te[flat(c, i)], 0, 0))],
        out_specs=pl.BlockSpec((tmx, D_MODEL), lambda c, i, n, te, tb: (flat(c, i), 0)),
        scratch_shapes=[pltpu.VMEM((tmx, D_MODEL), F32)],
    )
    return pl.pallas_call(
        _moe_kernel,
        grid_spec=grid_spec,
        out_shape=jax.ShapeDtypeStruct((nch * nt * tmx, D_MODEL), F32),
        compiler_params=_cparams(("parallel", "arbitrary")),
        name="moe_experts",
    )(plan["n_tiles"], plan["texp"], plan["tblk"], plan["row_tok"], h2, w1p, b1p, w2p, b2)


def _combine_kernel(pos_ref, gate_ref, ys_ref, o_ref):
    def body(t, carry):
        acc = None
        for k in range(TOP_K):
            term = gate_ref[0, 0, TOP_K * t + k] * ys_ref[pl.ds(pos_ref[0, 0, TOP_K * t + k], 1), :]
            acc = term if acc is None else acc + term
        o_ref[pl.ds(t, 1), :] = acc
        return carry
    lax.fori_loop(0, o_ref.shape[0], body, 0, unroll=4)


def _moe_combine(ys, pos, gates, nch, rc, ts, n_col):
    t = pos.shape[0]
    nsub = t // (nch * ts)
    cw = D_MODEL // n_col
    pos3 = pos.reshape(nch * nsub, 1, ts * TOP_K)
    gate3 = gates.reshape(nch * nsub, 1, ts * TOP_K)
    smem_map = lambda c, q, s: (c * nsub + s, 0, 0)
    return pl.pallas_call(
        _combine_kernel,
        grid=(nch, n_col, nsub),
        in_specs=[pl.BlockSpec((1, 1, ts * TOP_K), smem_map, memory_space=pltpu.SMEM),
                  pl.BlockSpec((1, 1, ts * TOP_K), smem_map, memory_space=pltpu.SMEM),
                  pl.BlockSpec((rc, cw), lambda c, q, s: (c, q), pipeline_mode=pl.Buffered(1))],
        out_specs=pl.BlockSpec((ts, cw), lambda c, q, s: (c * nsub + s, q)),
        out_shape=jax.ShapeDtypeStruct((t, D_MODEL), F32),
        compiler_params=_cparams(("parallel", "parallel", "arbitrary")),
        name="moe_combine",
    )(pos3, gate3, ys)


def _final_kernel(x1_ref, f_ref, g2_ref, gpost_ref, y_ref):
    y_ref[0] = x1_ref[0] + g2_ref[0] * _rms(f_ref[0], gpost_ref[...], D_MODEL)


def _final(x1, f, g2, g_post_ffn, tm):
    n, l, _ = x1.shape
    per_row = g2.shape[1] != 1
    mod_blk = (1, tm, D_MODEL) if per_row else (1, 1, D_MODEL)
    mod_map = (lambda b, i: (b, i, 0)) if per_row else (lambda b, i: (b, 0, 0))
    row_map = lambda b, i: (b, i, 0)
    return pl.pallas_call(
        _final_kernel,
        grid=(n, l // tm),
        in_specs=[pl.BlockSpec((1, tm, D_MODEL), row_map),
                  pl.BlockSpec((1, tm, D_MODEL), row_map),
                  pl.BlockSpec(mod_blk, mod_map),
                  pl.BlockSpec((1, D_MODEL), lambda b, i: (0, 0))],
        out_specs=pl.BlockSpec((1, tm, D_MODEL), row_map),
        out_shape=jax.ShapeDtypeStruct((n, l, D_MODEL), F32),
        compiler_params=_cparams(("parallel", "parallel")),
        name="final",
    )(x1, f, g2, g_post_ffn)


def _complex_pow2(lam, n):
    re, im = lam[0], lam[1]
    steps = int(round(math.log2(n)))
    assert 2 ** steps == n
    for _ in range(steps):
        re, im = re * re - im * im, 2.0 * re * im
    return jnp.stack([re, im])


def _rope_tables(pos):
    half = ROPE_DIM // 2
    inv = ROPE_THETA ** (-jnp.arange(half, dtype=F32) / half)
    ang = pos.astype(F32)[:, None] * inv[None, :]
    cos, sin = jnp.cos(ang), jnp.sin(ang)
    cos4 = jnp.tile(jnp.concatenate([cos, cos], axis=-1), (1, LANES // ROPE_DIM))
    sin4 = jnp.tile(jnp.concatenate([-sin, sin], axis=-1), (1, LANES // ROPE_DIM))
    return cos4[None], sin4[None]


def _swap_halves(w):
    half = ROPE_DIM // 2
    return jnp.concatenate([w[..., half:], w[..., :half]], axis=-1)


def _prep_layer(w, layer):
    g = lambda name: w[name][layer]
    p = {}
    row = lambda v: v.reshape(1, -1)
    p["g_pre_mix"] = row(g("g_pre_mix"))
    w_in = g("w_in")
    w_q, w_kv, w_r, w_u = jnp.split(w_in, [Q_RANK, Q_RANK + KV_RANK, Q_RANK + KV_RANK + ROPE_DIM], axis=-1)
    rep = LANES // ROPE_DIM
    p["w_in"] = jnp.concatenate(
        [w_q, jnp.zeros((D_MODEL, Q_PAD - Q_RANK), F32), w_kv,
         jnp.tile(w_r, (1, rep)), jnp.tile(_swap_halves(w_r), (1, rep)), w_u], axis=-1)
    p["g_q"] = row(jnp.pad(g("g_q"), (0, Q_PAD - Q_RANK)))
    w_uq = g("w_uq").reshape(Q_RANK, N_HEADS, NOPE_DIM + ROPE_DIM)
    nope = w_uq[:, :, :NOPE_DIM].reshape(Q_RANK, N_HEADS * NOPE_DIM)
    rope = w_uq[:, :, NOPE_DIM:]
    rope_sw = _swap_halves(rope)
    w_uq_p = jnp.concatenate([nope, rope.reshape(Q_RANK, -1), rope_sw.reshape(Q_RANK, -1)], axis=-1)
    p["w_uq"] = jnp.pad(w_uq_p, ((0, Q_PAD - Q_RANK), (0, 0)))
    w_uk = jnp.transpose(g("w_uk"), (1, 2, 0))
    z = jnp.zeros((NOPE_DIM, KV_RANK), F32)
    p["w_uk"] = jnp.stack([jnp.block([[w_uk[2 * j], z], [z, w_uk[2 * j + 1]]])
                           for j in range(N_HEADS // 2)])
    p["g_kv"] = row(g("g_kv"))
    w_uv = jnp.transpose(g("w_uv"), (1, 0, 2))
    zv = jnp.zeros((KV_RANK, V_DIM), F32)
    p["w_uv_pair"] = jnp.stack([jnp.block([[w_uv[2 * j], zv], [zv, w_uv[2 * j + 1]]])
                                for j in range(N_HEADS // 2)])
    eye_h = jnp.eye(N_HEADS, dtype=F32)
    p["w_uv_bd"] = jnp.einsum("hrv,hk->hrkv", w_uv, eye_h).reshape(N_HEADS * KV_RANK, ATTN_WIDTH)
    p["w_out"] = g("w_out")
    p["g_post_mix"] = row(g("g_post_mix"))
    p["g_pre_ffn"] = row(g("g_pre_ffn"))
    p["w_router"] = jnp.pad(g("w_router"), ((0, 0), (0, LANES - N_EXPERTS)))
    p["b_router"] = row(jnp.pad(g("b_router"), (0, LANES - N_EXPERTS), constant_values=NEG_INF))
    p["g_post_ffn"] = row(g("g_post_ffn"))

    lam_re, lam_im = g("ssm_lam_re"), g("ssm_lam_im")
    step = jnp.exp(g("ssm_log_step"))[:, None]
    mag = jnp.exp(lam_re * step)
    bar_re, bar_im = mag * jnp.cos(lam_im * step), mag * jnp.sin(lam_im * step)
    den = lam_re * lam_re + lam_im * lam_im
    co_re = ((bar_re - 1.0) * lam_re + bar_im * lam_im) / den
    co_im = (bar_im * lam_re - (bar_re - 1.0) * lam_im) / den
    b_re, b_im = g("ssm_b_re"), g("ssm_b_im")
    bb_re = co_re[..., None] * b_re - co_im[..., None] * b_im
    bb_im = co_re[..., None] * b_im + co_im[..., None] * b_re
    eye_g = jnp.eye(GROUPS_PER_BLOCK, dtype=F32)

    def b_block(m):
        m = m.reshape(SSM_BLOCKS, GROUPS_PER_BLOCK, SSM_STATE, SSM_GROUP)
        return jnp.einsum("bgpc,gh->bgchp", m, eye_g).reshape(SSM_BLOCKS, LANES, BLOCK_STATES)

    def c_block(m):
        m = m.reshape(SSM_BLOCKS, GROUPS_PER_BLOCK, SSM_GROUP, SSM_STATE)
        return jnp.einsum("bgcp,gh->bgphc", m, eye_g).reshape(SSM_BLOCKS, BLOCK_STATES, LANES)

    sp = {
        "lam": jnp.stack([bar_re.reshape(-1), bar_im.reshape(-1)]),
        "bmat": jnp.concatenate([b_block(bb_re), b_block(bb_im)], axis=-1).astype(BF16),
        "cmat": jnp.concatenate([c_block(g("ssm_c_re")), c_block(-g("ssm_c_im"))], axis=1).astype(BF16),
        "d": row(g("ssm_d")),
        "w_glu": g("ssm_w_glu"),
        "b_glu": row(g("ssm_b_glu")),
    }
    b1 = g("b1")
    n_blk = b1.shape[1] // GLU_BLOCK
    mp = {
        "b1": jnp.transpose(b1.reshape(N_EXPERTS, n_blk, LANES, 2), (0, 1, 3, 2)).reshape(N_EXPERTS, 1, -1),
        "b2": g("b2")[:, None, :],
    }
    return p, sp, mp


def _mods(mod_rows):
    return [mod_rows[:, k * D_MODEL:(k + 1) * D_MODEL] for k in range(N_MOD)]


def _pick_tile(n, target, mult):
    best = mult
    for c in range(mult, min(n, target) + 1, mult):
        if n % c == 0:
            best = c
    return best


def kernel(x_prompt, x_sample, c_prompt, c_sample, cache_kv_latent, cache_k_rope, state_ssm, page_table, w_ada, b_ada, g_pre_mix, w_in, g_q, w_uq, w_uk, g_kv, w_uv, ssm_lam_re, ssm_lam_im, ssm_log_step, ssm_b_re, ssm_b_im, ssm_c_re, ssm_c_im, ssm_d, ssm_w_glu, ssm_b_glu, w_out, g_post_mix, g_pre_ffn, w_router, b_router, w1, b1, w2, b2, g_post_ffn):
    weights = dict(w_ada=w_ada, b_ada=b_ada, g_pre_mix=g_pre_mix, w_in=w_in, g_q=g_q, w_uq=w_uq,
                   w_uk=w_uk, g_kv=g_kv, w_uv=w_uv, ssm_lam_re=ssm_lam_re, ssm_lam_im=ssm_lam_im,
                   ssm_log_step=ssm_log_step, ssm_b_re=ssm_b_re, ssm_b_im=ssm_b_im,
                   ssm_c_re=ssm_c_re, ssm_c_im=ssm_c_im, ssm_d=ssm_d, ssm_w_glu=ssm_w_glu,
                   ssm_b_glu=ssm_b_glu, w_out=w_out, g_post_mix=g_post_mix, g_pre_ffn=g_pre_ffn,
                   w_router=w_router, b_router=b_router, w1=w1, b1=b1, w2=w2, b2=b2,
                   g_post_ffn=g_post_ffn)
    depth = w_ada.shape[0]
    n_p, l_p, _ = x_prompt.shape
    n_s, l_s, _ = x_sample.shape
    assert l_s == 1
    n_pages = page_table.shape[1]
    past_len = n_pages * PAGE_SIZE

    cos_p, sin_p = _rope_tables(jnp.arange(l_p, dtype=jnp.int32))
    cos_s, sin_s = _rope_tables(jnp.full((1,), past_len, jnp.int32))
    cos_s = jnp.broadcast_to(cos_s, (1, n_s, LANES))
    sin_s = jnp.broadcast_to(sin_s, (1, n_s, LANES))

    tm_p = _pick_tile(l_p, 512, 16)
    tq = _pick_tile(l_p, 256, 16)
    tk = _pick_tile(l_p, 512, 16)
    lc = 64 if l_p % 512 == 0 else l_p // SUBLANES
    pg = _pick_tile(n_pages, 64, 1)
    c_rows = n_p + n_s
    c_pad = -c_rows % SUBLANES
    c_all = jnp.pad(jnp.concatenate([c_prompt, c_sample], axis=0), ((0, c_pad), (0, 0)))

    y_p = x_prompt
    y_s = x_sample.reshape(1, n_s, D_MODEL)
    outs = [[] for _ in range(6)]
    for layer in range(depth):
        p, sp, mp = _prep_layer(weights, layer)
        mod = _ada(c_all, w_ada[layer], b_ada[layer].reshape(1, -1))
        sh1p, sc1p, g1p, sh2p, sc2p, g2p = [m[:, None, :] for m in _mods(mod[:n_p])]
        sh1s, sc1s, g1s, sh2s, sc2s, g2s = [m[None] for m in _mods(mod[n_p:c_rows])]

        q_cat, k_cat, kv_p, kr_p, u_p = _inproj(y_p, sc1p, sh1p, p, cos_p, sin_p, tm_p)
        o_attn_p = _flash(q_cat, k_cat, p["w_uv_pair"], tq, tk)
        y_ssm_p, h_last_p = _s5_prompt(u_p, sp, lc)
        x1_p, h2_p, comb_p = _outproj(o_attn_p, y_ssm_p, y_p, g1p, sc2p, sh2p, p, tm_p)

        q_cat_s, _, kv_s, kr_s, u_s = _inproj(y_s, sc1s, sh1s, p, cos_s, sin_s, n_s)
        q_lat_s = jnp.transpose(q_cat_s[0, :, :, :KV_RANK], (1, 0, 2))
        q_rope_s = jnp.stack(
            [q_cat_s[0, h, :, KV_RANK + ROPE_DIM * (h % 4):KV_RANK + ROPE_DIM * (h % 4 + 1)]
             for h in range(N_HEADS)], axis=1)
        o_lat_s = _decode_attn(q_lat_s, q_rope_s, kv_s.reshape(n_s, 1, KV_RANK),
                               kr_s.reshape(n_s, 1, ROPE_DIM), cache_kv_latent[layer],
                               jnp.swapaxes(cache_k_rope[layer], 1, 2), page_table, pg)
        o_attn_s = _uv_proj(o_lat_s.reshape(n_s, N_HEADS * KV_RANK), p["w_uv_bd"])
        h0 = jnp.moveaxis(state_ssm[layer].reshape(n_s, N_STATE, 2), -1, 0)
        y_ssm_s, h_s = _s5_step(u_s[0], h0, sp)
        x1_s, h2_s, comb_s = _outproj(o_attn_s[None], y_ssm_s[None], y_s, g1s, sc2s, sh2s, p, n_s)

        t_p = n_p * l_p
        t_all = t_p + n_s
        nch = 4 if t_all % (4 * SUBLANES) == 0 else (2 if t_all % (2 * SUBLANES) == 0 else 1)
        tc = t_all // nch
        h2_all = jnp.concatenate([h2_p.reshape(t_p, D_MODEL), h2_s[0]], axis=0)
        route = jnp.concatenate([comb_p.reshape(t_p, LANES), comb_s[0]], axis=0)
        plan = _route_plan(route[:, :TOP_K].astype(jnp.int32), nch, MOE_TILE)
        w1p, w2p = _moe_prep(w1[layer], w2[layer])
        ys = _moe_experts(h2_all.reshape(nch, tc, D_MODEL), plan, w1p, mp["b1"], w2p, mp["b2"], MOE_TILE)
        f_all = _moe_combine(ys, plan["pos"], route[:, TOP_K:2 * TOP_K], nch, plan["nt"] * MOE_TILE,
                             _pick_tile(tc, 1100, SUBLANES), MOE_COL_PARTS)
        y_p = _final(x1_p, f_all[:t_p].reshape(n_p, l_p, D_MODEL), g2p, p["g_post_ffn"], tm_p)
        y_s = _final(x1_s, f_all[t_p:][None], g2s, p["g_post_ffn"], n_s)

        outs[0].append(kv_p)
        outs[1].append(kr_p)
        outs[2].append(jnp.moveaxis(h_last_p, 1, -1).reshape(n_p, N_SSM_GROUPS, SSM_STATE, 2))
        outs[3].append(kv_s.reshape(n_s, 1, KV_RANK))
        outs[4].append(kr_s.reshape(n_s, 1, ROPE_DIM))
        outs[5].append(jnp.moveaxis(h_s, 0, -1).reshape(n_s, N_SSM_GROUPS, SSM_STATE, 2))
    return (y_p, y_s.reshape(n_s, 1, D_MODEL)) + tuple(jnp.stack(o) for o in outs)
```

```python
import functools
import math

import jax
import jax.numpy as jnp
from jax import lax
from jax.experimental import pallas as pl
from jax.experimental.pallas import tpu as pltpu

F32 = jnp.float32
BF16 = jnp.bfloat16

D_MODEL = 1024
N_HEADS = 8
NOPE_DIM = 64
ROPE_DIM = 32
V_DIM = 64
Q_RANK = 352
KV_RANK = 128
ROPE_THETA = 10000.0
ATTN_WIDTH = N_HEADS * V_DIM
ATTN_SCALE = 1.0 / math.sqrt(NOPE_DIM + ROPE_DIM)
SSM_WIDTH = D_MODEL - ATTN_WIDTH
SSM_GROUP = 16
N_SSM_GROUPS = SSM_WIDTH // SSM_GROUP
SSM_STATE = 64
N_STATE = N_SSM_GROUPS * SSM_STATE
N_EXPERTS = 32
TOP_K = 4
SWIGLU_LIMIT = 7.0
SWIGLU_ALPHA = 1.702
RMS_EPS = 1e-6
N_MOD = 6
NEG_INF = -1e30
PAGE_SIZE = 128

LANES = 128
SUBLANES = 8
VMEM_LIMIT = 56 * 1024 * 1024

Q_PAD = 384
CAT = 2 * LANES
SSM_BLOCKS = 4
GROUPS_PER_BLOCK = N_SSM_GROUPS // SSM_BLOCKS
BLOCK_STATES = GROUPS_PER_BLOCK * SSM_STATE
IN_PAD = Q_PAD + KV_RANK + 2 * LANES + SSM_WIDTH
UQ_PAD = N_HEADS * NOPE_DIM + 4 * LANES
GLU_BLOCK = 2 * LANES
MOE_TILE = 192
MOE_COL_PARTS = 2
LOG2E = math.log2(math.e)
FLASH_LOOKAHEAD = 3


def _cparams(sem):
    return pltpu.CompilerParams(dimension_semantics=sem, vmem_limit_bytes=VMEM_LIMIT)


def _mm(a, b):
    return jnp.dot(a.astype(BF16), b.astype(BF16), preferred_element_type=F32)


def _mm_nt(a, b):
    return lax.dot_general(a.astype(BF16), b.astype(BF16), (((1,), (1,)), ((), ())),
                           preferred_element_type=F32)


def _split_bf16(x):
    hi = x.astype(BF16)
    lo = (x - hi.astype(F32)).astype(BF16)
    return hi, lo


def _mm3(a, b):
    ah, al = _split_bf16(a)
    bh, bl = _split_bf16(b)
    d = functools.partial(jnp.dot, preferred_element_type=F32)
    return d(ah, bh) + (d(ah, bl) + d(al, bh))


def _sigmoid(x):
    return 1.0 / (1.0 + jnp.exp(-x))


def _rms(x, g, width):
    ms = jnp.sum(x * x, axis=-1, keepdims=True) * (1.0 / width)
    return x * lax.rsqrt(ms + RMS_EPS) * g


def _gelu_tanh(x):
    return 0.5 * x * (1.0 + jnp.tanh(math.sqrt(2.0 / math.pi) * (x + 0.044715 * (x * x * x))))


def _ada_kernel(c_ref, w_ref, b_ref, o_ref):
    c = c_ref[...]
    o_ref[...] = _mm3(c * _sigmoid(c), w_ref[...]) + b_ref[...]


def _ada(c_all, w_ada, b_ada):
    rows = c_all.shape[0]
    tn = 512
    n_out = w_ada.shape[1]
    return pl.pallas_call(
        _ada_kernel,
        grid=(n_out // tn,),
        in_specs=[pl.BlockSpec((rows, D_MODEL), lambda j: (0, 0)),
                  pl.BlockSpec((D_MODEL, tn), lambda j: (0, j)),
                  pl.BlockSpec((1, tn), lambda j: (0, j))],
        out_specs=pl.BlockSpec((rows, tn), lambda j: (0, j)),
        out_shape=jax.ShapeDtypeStruct((rows, n_out), F32),
        compiler_params=_cparams(("parallel",)),
        name="adaln",
    )(c_all, w_ada, b_ada)


def _inproj_kernel(x_ref, sc_ref, sh_ref, gpre_ref, win_ref, gq_ref, wuq_ref, wuk_ref, gkv_ref,
                   cos_ref, sin_ref, q_ref, kcat_ref, vt_ref, kv_ref, kr_ref, u_ref):
    x = x_ref[0]
    h = _rms(x, gpre_ref[...], D_MODEL) * (1.0 + sc_ref[0]) + sh_ref[0]
    z = _mm(h, win_ref[...])
    zq = z[:, :Q_PAD]
    zkv = z[:, Q_PAD:Q_PAD + KV_RANK]
    zr = z[:, Q_PAD + KV_RANK:Q_PAD + KV_RANK + LANES]
    zrs = z[:, Q_PAD + KV_RANK + LANES:Q_PAD + KV_RANK + 2 * LANES]
    u = z[:, Q_PAD + KV_RANK + 2 * LANES:]
    cos4 = cos_ref[0]
    sin4 = sin_ref[0]

    kvl = _rms(zkv, gkv_ref[...], KV_RANK)
    kr4 = zr * cos4 + zrs * sin4
    kv_ref[0] = kvl
    kr_ref[0] = kr4[:, :ROPE_DIM]
    kcat_ref[0] = jnp.concatenate([kvl, kr4], axis=-1).astype(BF16)
    vt_ref[0] = kvl.T.astype(BF16)
    u_ref[0] = u.astype(BF16)

    qn = _rms(zq, gq_ref[...], Q_RANK)
    q = _mm(qn, wuq_ref[...])
    nope_w = N_HEADS * NOPE_DIM
    rope_a = q[:, nope_w:nope_w + LANES] * cos4 + q[:, nope_w + 2 * LANES:nope_w + 3 * LANES] * sin4
    rope_b = q[:, nope_w + LANES:nope_w + 2 * LANES] * cos4 + q[:, nope_w + 3 * LANES:] * sin4
    slot = lax.broadcasted_iota(jnp.int32, (1, LANES), 1) // ROPE_DIM
    for j in range(N_HEADS // 2):
        ql = _mm(q[:, j * LANES:(j + 1) * LANES], wuk_ref[j])
        for hh in range(2):
            head = 2 * j + hh
            rope = rope_a if head < 4 else rope_b
            rope_h = jnp.where(slot == (head % 4), rope, 0.0)
            qc = jnp.concatenate([ql[:, hh * KV_RANK:(hh + 1) * KV_RANK], rope_h], axis=-1)
            q_ref[0, head] = (qc * (ATTN_SCALE * LOG2E)).T.astype(BF16)


def _inproj(x, sc, sh, p, cos4, sin4, tm):
    n, l, _ = x.shape
    per_row = sc.shape[1] != 1
    mod_blk = (1, tm, D_MODEL) if per_row else (1, 1, D_MODEL)
    mod_map = (lambda b, i: (b, i, 0)) if per_row else (lambda b, i: (b, 0, 0))
    const2 = lambda b, i: (0, 0)
    const3 = lambda b, i: (0, 0, 0)
    pos_map = lambda b, i: (0, i, 0)
    return pl.pallas_call(
        _inproj_kernel,
        grid=(n, l // tm),
        in_specs=[pl.BlockSpec((1, tm, D_MODEL), lambda b, i: (b, i, 0)),
                  pl.BlockSpec(mod_blk, mod_map),
                  pl.BlockSpec(mod_blk, mod_map),
                  pl.BlockSpec((1, D_MODEL), const2),
                  pl.BlockSpec((D_MODEL, IN_PAD), const2),
                  pl.BlockSpec((1, Q_PAD), const2),
                  pl.BlockSpec((Q_PAD, UQ_PAD), const2),
                  pl.BlockSpec((N_HEADS // 2, LANES, 2 * KV_RANK), const3),
                  pl.BlockSpec((1, KV_RANK), const2),
                  pl.BlockSpec((1, tm, LANES), pos_map),
                  pl.BlockSpec((1, tm, LANES), pos_map)],
        out_specs=[pl.BlockSpec((1, N_HEADS, CAT, tm), lambda b, i: (b, 0, 0, i)),
                   pl.BlockSpec((1, tm, CAT), lambda b, i: (b, i, 0)),
                   pl.BlockSpec((1, KV_RANK, tm), lambda b, i: (b, 0, i)),
                   pl.BlockSpec((1, tm, KV_RANK), lambda b, i: (b, i, 0)),
                   pl.BlockSpec((1, tm, ROPE_DIM), lambda b, i: (b, i, 0)),
                   pl.BlockSpec((1, tm, SSM_WIDTH), lambda b, i: (b, i, 0))],
        out_shape=[jax.ShapeDtypeStruct((n, N_HEADS, CAT, l), BF16),
                   jax.ShapeDtypeStruct((n, l, CAT), BF16),
                   jax.ShapeDtypeStruct((n, KV_RANK, l), BF16),
                   jax.ShapeDtypeStruct((n, l, KV_RANK), F32),
                   jax.ShapeDtypeStruct((n, l, ROPE_DIM), F32),
                   jax.ShapeDtypeStruct((n, l, SSM_WIDTH), BF16)],
        compiler_params=_cparams(("parallel", "parallel")),
        name="inproj",
    )(x, sc, sh, p["g_pre_mix"], p["w_in"], p["g_q"], p["w_uq"], p["w_uk"], p["g_kv"], cos4, sin4)


def _flash_kernel(qt_ref, k_ref, vt_ref, wuvt_ref, o_ref, m_ref, l_ref, acc_ref, *, tq, tk):
    i = pl.program_id(1)
    j = pl.program_id(2)
    j_last = ((i + 1) * tq - 1) // tk

    @pl.when(j == 0)
    def _():
        m_ref[...] = jnp.full(m_ref.shape, NEG_INF, F32)
        l_ref[...] = jnp.zeros(l_ref.shape, F32)
        acc_ref[...] = jnp.zeros(acc_ref.shape, F32)

    def update(masked):
        k = k_ref[0]
        vt = vt_ref[0]
        if masked:
            k_pos = j * tk + lax.broadcasted_iota(jnp.int32, (tk, tq), 0)
            q_pos = i * tq + lax.broadcasted_iota(jnp.int32, (tk, tq), 1)
            keep = k_pos <= q_pos
        m_all = m_ref[...]
        l_all = l_ref[...]
        m_rows, l_rows = [], []
        scores = lambda h: jnp.dot(k, qt_ref[0, h], preferred_element_type=F32)
        ahead = [scores(h) for h in range(min(FLASH_LOOKAHEAD, N_HEADS))]
        pending = None
        for h in range(N_HEADS):
            s = ahead.pop(0)
            if h + FLASH_LOOKAHEAD < N_HEADS:
                ahead.append(scores(h + FLASH_LOOKAHEAD))
            if masked:
                s = jnp.where(keep, s, NEG_INF)
            m_prev = m_all[h:h + 1]
            m_new = jnp.maximum(m_prev, jnp.max(s, axis=0, keepdims=True))
            alpha = jnp.exp2(m_prev - m_new)
            p = jnp.exp2(s - m_new)
            l_rows.append(alpha * l_all[h:h + 1] + jnp.sum(p, axis=0, keepdims=True))
            pv = jnp.dot(vt, p.astype(BF16), preferred_element_type=F32)
            if pending is not None:
                acc_ref[pending[0]] = pending[1] * acc_ref[pending[0]] + pending[2]
            pending = (h, alpha, pv)
            m_rows.append(m_new)
        acc_ref[pending[0]] = pending[1] * acc_ref[pending[0]] + pending[2]
        m_ref[...] = jnp.concatenate(m_rows, axis=0)
        l_ref[...] = jnp.concatenate(l_rows, axis=0)

    crosses = (j + 1) * tk - 1 > i * tq

    @pl.when(jnp.logical_and(j <= j_last, crosses))
    def _():
        update(True)

    @pl.when(jnp.logical_and(j <= j_last, jnp.logical_not(crosses)))
    def _():
        update(False)

    @pl.when(j == j_last)
    def _():
        outs = []
        for h in range(N_HEADS):
            o_t = acc_ref[h] / l_ref[h:h + 1, :]
            outs.append(_mm(wuvt_ref[h], o_t))
        o_ref[0] = jnp.concatenate(outs, axis=0).T.astype(BF16)


def _flash(q_t, k_cat, v_t, wuv_t, tq, tk):
    n, _, _, l = q_t.shape
    kern = functools.partial(_flash_kernel, tq=tq, tk=tk)
    last = lambda i: ((i + 1) * tq - 1) // tk
    return pl.pallas_call(
        kern,
        grid=(n, l // tq, l // tk),
        in_specs=[pl.BlockSpec((1, N_HEADS, CAT, tq), lambda b, i, j: (b, 0, 0, i)),
                  pl.BlockSpec((1, tk, CAT), lambda b, i, j: (b, jnp.minimum(j, last(i)), 0)),
                  pl.BlockSpec((1, KV_RANK, tk), lambda b, i, j: (b, 0, jnp.minimum(j, last(i)))),
                  pl.BlockSpec((N_HEADS, V_DIM, KV_RANK), lambda b, i, j: (0, 0, 0))],
        out_specs=pl.BlockSpec((1, tq, ATTN_WIDTH), lambda b, i, j: (b, i, 0)),
        out_shape=jax.ShapeDtypeStruct((n, l, ATTN_WIDTH), BF16),
        scratch_shapes=[pltpu.VMEM((N_HEADS, tq), F32),
                        pltpu.VMEM((N_HEADS, tq), F32),
                        pltpu.VMEM((N_HEADS, KV_RANK, tq), F32)],
        compiler_params=_cparams(("parallel", "parallel", "arbitrary")),
        name="flash",
    )(q_t, k_cat, v_t, wuv_t)


def _ssm_out(y, up, d_ref, wglu_ref, bglu_ref):
    y = _gelu_tanh(y + d_ref[...] * up)
    return y * _sigmoid(_mm(y, wglu_ref[...]) + bglu_ref[...])


def _s5_kernel(u_ref, perm_ref, permt_ref, bmat_ref, cmat_ref, lam_ref, laml_ref, d_ref, wglu_ref,
               bglu_ref, y_ref, hlast_ref, buf_ref, carry_ref, *, lc):
    j = pl.program_id(1)

    @pl.when(j == 0)
    def _():
        carry_ref[...] = jnp.zeros(carry_ref.shape, F32)

    u = u_ref[0]
    up = jnp.dot(perm_ref[...], u, preferred_element_type=F32).astype(BF16)
    blk_w = 2 * BLOCK_STATES
    for b in range(SSM_BLOCKS):
        buf_ref[:, b * blk_w:(b + 1) * blk_w] = jnp.dot(
            up[:, b * LANES:(b + 1) * LANES], bmat_ref[b], preferred_element_type=F32)

    for b in range(SSM_BLOCKS):
        st = slice(b * BLOCK_STATES, (b + 1) * BLOCK_STATES)
        re_cols = slice(b * blk_w, b * blk_w + BLOCK_STATES)
        im_cols = slice(b * blk_w + BLOCK_STATES, (b + 1) * blk_w)
        lr = lam_ref[0:1, st]
        li = lam_ref[1:2, st]

        def run(init, store):
            def body(t, carry):
                re, im = carry
                rows = pl.ds(pl.multiple_of(t * SUBLANES, SUBLANES), SUBLANES)
                n_re = lr * re - li * im + buf_ref[rows, re_cols]
                n_im = lr * im + li * re + buf_ref[rows, im_cols]
                if store:
                    buf_ref[rows, re_cols] = n_re
                    buf_ref[rows, im_cols] = n_im
                return n_re, n_im
            return lax.fori_loop(0, lc, body, init)

        zero = jnp.zeros((SUBLANES, BLOCK_STATES), F32)
        end_re, end_im = run((zero, zero), False)
        ll_r = laml_ref[0:1, st]
        ll_i = laml_ref[1:2, st]
        c_re = carry_ref[0:1, st]
        c_im = carry_ref[1:2, st]
        init_re, init_im = [], []
        for k in range(SUBLANES):
            init_re.append(c_re)
            init_im.append(c_im)
            c_re, c_im = (ll_r * c_re - ll_i * c_im + end_re[k:k + 1],
                          ll_r * c_im + ll_i * c_re + end_im[k:k + 1])
        carry_ref[0:1, st] = c_re
        carry_ref[1:2, st] = c_im
        run((jnp.concatenate(init_re, axis=0), jnp.concatenate(init_im, axis=0)), True)

    hlast_ref[0] = carry_ref[...]
    ys = [jnp.dot(buf_ref[:, b * blk_w:(b + 1) * blk_w].astype(BF16), cmat_ref[b],
                  preferred_element_type=F32) for b in range(SSM_BLOCKS)]
    y = _ssm_out(jnp.concatenate(ys, axis=-1), up.astype(F32), d_ref, wglu_ref, bglu_ref)
    y_ref[0] = jnp.dot(permt_ref[...], y.astype(BF16), preferred_element_type=F32).astype(BF16)


def _s5_prompt(u, sp, lc):
    n, l, _ = u.shape
    s = SUBLANES * lc
    rows = jnp.arange(s)
    src = (rows % SUBLANES) * lc + rows // SUBLANES
    perm = jnp.zeros((s, s), BF16).at[rows, src].set(1.0)
    laml = _complex_pow2(sp["lam"], lc)
    kern = functools.partial(_s5_kernel, lc=lc)
    c2 = lambda b, j: (0, 0)
    c3 = lambda b, j: (0, 0, 0)
    return pl.pallas_call(
        kern,
        grid=(n, l // s),
        in_specs=[pl.BlockSpec((1, s, SSM_WIDTH), lambda b, j: (b, j, 0)),
                  pl.BlockSpec((s, s), c2),
                  pl.BlockSpec((s, s), c2),
                  pl.BlockSpec((SSM_BLOCKS, LANES, 2 * BLOCK_STATES), c3),
                  pl.BlockSpec((SSM_BLOCKS, 2 * BLOCK_STATES, LANES), c3),
                  pl.BlockSpec((2, N_STATE), c2),
                  pl.BlockSpec((2, N_STATE), c2),
                  pl.BlockSpec((1, SSM_WIDTH), c2),
                  pl.BlockSpec((SSM_WIDTH, SSM_WIDTH), c2),
                  pl.BlockSpec((1, SSM_WIDTH), c2)],
        out_specs=[pl.BlockSpec((1, s, SSM_WIDTH), lambda b, j: (b, j, 0)),
                   pl.BlockSpec((1, 2, N_STATE), lambda b, j: (b, 0, 0))],
        out_shape=[jax.ShapeDtypeStruct((n, l, SSM_WIDTH), BF16),
                   jax.ShapeDtypeStruct((n, 2, N_STATE), F32)],
        scratch_shapes=[pltpu.VMEM((s, 2 * N_STATE), F32),
                        pltpu.VMEM((2, N_STATE), F32)],
        compiler_params=_cparams(("parallel", "arbitrary")),
        name="s5_scan",
    )(u, perm, perm.T, sp["bmat"], sp["cmat"], sp["lam"], laml, sp["d"], sp["w_glu"], sp["b_glu"])


def _s5_step_kernel(u_ref, h0_ref, bmat_ref, cmat_ref, lam_ref, d_ref, wglu_ref, bglu_ref,
                    y_ref, h_ref):
    u = u_ref[...]
    ys = []
    for b in range(SSM_BLOCKS):
        st = slice(b * BLOCK_STATES, (b + 1) * BLOCK_STATES)
        bu = jnp.dot(u[:, b * LANES:(b + 1) * LANES], bmat_ref[b], preferred_element_type=F32)
        lr = lam_ref[0:1, st]
        li = lam_ref[1:2, st]
        re = h0_ref[0, :, st]
        im = h0_ref[1, :, st]
        n_re = lr * re - li * im + bu[:, :BLOCK_STATES]
        n_im = lr * im + li * re + bu[:, BLOCK_STATES:]
        h_ref[0, :, st] = n_re
        h_ref[1, :, st] = n_im
        ys.append(_mm(jnp.concatenate([n_re, n_im], axis=-1), cmat_ref[b]))
    y = _ssm_out(jnp.concatenate(ys, axis=-1), u.astype(F32), d_ref, wglu_ref, bglu_ref)
    y_ref[...] = y.astype(BF16)


def _s5_step(u, h0, sp):
    rows = u.shape[0]
    return pl.pallas_call(
        _s5_step_kernel,
        out_shape=[jax.ShapeDtypeStruct((rows, SSM_WIDTH), BF16),
                   jax.ShapeDtypeStruct((2, rows, N_STATE), F32)],
        compiler_params=pltpu.CompilerParams(vmem_limit_bytes=VMEM_LIMIT),
        name="s5_step",
    )(u, h0, sp["bmat"], sp["cmat"], sp["lam"], sp["d"], sp["w_glu"], sp["b_glu"])


def _decode_kernel(pt_ref, ql_ref, qr_ref, kvs_ref, krs_ref, *rest, pg):
    del pt_ref
    lat_refs = rest[:pg]
    rope_refs = rest[pg:2 * pg]
    o_ref, m_ref, l_ref, acc_ref = rest[2 * pg:]
    j = pl.program_id(1)

    @pl.when(j == 0)
    def _():
        m_ref[...] = jnp.full(m_ref.shape, NEG_INF, F32)
        l_ref[...] = jnp.zeros(l_ref.shape, F32)
        acc_ref[...] = jnp.zeros(acc_ref.shape, F32)

    ql = ql_ref[0]
    qr = qr_ref[0]
    lat = jnp.concatenate([lat_refs[i][0] for i in range(pg)], axis=0).astype(BF16)
    rope_t = jnp.concatenate([rope_refs[i][0] for i in range(pg)], axis=1)
    s = _mm_nt(ql, lat) + _mm(qr, rope_t)
    m_prev = m_ref[...]
    m_new = jnp.maximum(m_prev, jnp.max(s, axis=-1, keepdims=True))
    alpha = jnp.exp2(m_prev - m_new)
    p = jnp.exp2(s - m_new)
    l_ref[...] = alpha * l_ref[...] + jnp.sum(p, axis=-1, keepdims=True)
    acc_ref[...] = alpha * acc_ref[...] + _mm(p, lat)
    m_ref[...] = m_new

    @pl.when(j == pl.num_programs(1) - 1)
    def _():
        kvs = kvs_ref[0]
        krs = krs_ref[0]
        s_self = (jnp.sum(ql.astype(F32) * kvs, axis=-1, keepdims=True)
                  + jnp.sum(qr.astype(F32) * krs, axis=-1, keepdims=True))
        m_p = m_ref[...]
        m_n = jnp.maximum(m_p, s_self)
        a = jnp.exp2(m_p - m_n)
        p_self = jnp.exp2(s_self - m_n)
        l_fin = a * l_ref[...] + p_self
        o_ref[0] = (a * acc_ref[...] + p_self * kvs) / l_fin


def _decode_attn(q_lat, q_rope, kv_self, kr_self, cache_lat, cache_rope, page_table, pg):
    b, n_pages = page_table.shape
    kern = functools.partial(_decode_kernel, pg=pg)

    def page_map(i):
        return lambda s, j, pt: (pt[s * n_pages + j * pg + i], 0, 0)

    seq3 = lambda s, j, pt: (s, 0, 0)
    grid_spec = pltpu.PrefetchScalarGridSpec(
        num_scalar_prefetch=1,
        grid=(b, n_pages // pg),
        in_specs=([pl.BlockSpec((1, N_HEADS, KV_RANK), seq3),
                   pl.BlockSpec((1, N_HEADS, ROPE_DIM), seq3),
                   pl.BlockSpec((1, 1, KV_RANK), seq3),
                   pl.BlockSpec((1, 1, ROPE_DIM), seq3)]
                  + [pl.BlockSpec((1, PAGE_SIZE, KV_RANK), page_map(i)) for i in range(pg)]
                  + [pl.BlockSpec((1, ROPE_DIM, PAGE_SIZE), page_map(i)) for i in range(pg)]),
        out_specs=pl.BlockSpec((1, N_HEADS, KV_RANK), seq3),
        scratch_shapes=[pltpu.VMEM((N_HEADS, 1), F32),
                        pltpu.VMEM((N_HEADS, 1), F32),
                        pltpu.VMEM((N_HEADS, KV_RANK), F32)],
    )
    return pl.pallas_call(
        kern,
        grid_spec=grid_spec,
        out_shape=jax.ShapeDtypeStruct((b, N_HEADS, KV_RANK), F32),
        compiler_params=_cparams(("parallel", "arbitrary")),
        name="decode_attn",
    )(page_table.reshape(-1), q_lat, q_rope, kv_self, kr_self,
      *([cache_lat] * pg), *([cache_rope] * pg))


def _uv_kernel(o_ref, w_ref, y_ref):
    y_ref[...] = _mm(o_ref[...], w_ref[...]).astype(BF16)


def _uv_proj(o_lat, wuv_bd):
    rows = o_lat.shape[0]
    return pl.pallas_call(
        _uv_kernel,
        out_shape=jax.ShapeDtypeStruct((rows, ATTN_WIDTH), BF16),
        compiler_params=pltpu.CompilerParams(vmem_limit_bytes=VMEM_LIMIT),
        name="uv_proj",
    )(o_lat, wuv_bd)


def _outproj_kernel(oa_ref, ys_ref, x_ref, g1_ref, sc_ref, sh_ref, wout_ref, gpost_ref, gpre_ref,
                    wr_ref, br_ref, x1_ref, h2_ref, route_ref):
    mix = (jnp.dot(oa_ref[0], wout_ref[:ATTN_WIDTH].astype(BF16), preferred_element_type=F32)
           + jnp.dot(ys_ref[0], wout_ref[ATTN_WIDTH:].astype(BF16), preferred_element_type=F32))
    x1 = x_ref[0] + g1_ref[0] * _rms(mix, gpost_ref[...], D_MODEL)
    x1_ref[0] = x1
    h2 = _rms(x1, gpre_ref[...], D_MODEL) * (1.0 + sc_ref[0]) + sh_ref[0]
    h2_ref[0] = h2
    logits = _mm3(h2, wr_ref[...]) + br_ref[...]
    lane = lax.broadcasted_iota(jnp.int32, logits.shape, 1).astype(F32)
    vals, idxs = [], []
    for _ in range(TOP_K):
        v = jnp.max(logits, axis=-1, keepdims=True)
        idx = jnp.min(jnp.where(logits == v, lane, float(LANES)), axis=-1, keepdims=True)
        vals.append(v)
        idxs.append(idx)
        logits = jnp.where(lane == idx, -jnp.inf, logits)
    exps = [jnp.exp(v - vals[0]) for v in vals]
    denom = exps[0] + exps[1] + exps[2] + exps[3]
    route = jnp.zeros(logits.shape, F32)
    for k in range(TOP_K):
        route = jnp.where(lane == float(k), idxs[k], route)
        route = jnp.where(lane == float(TOP_K + k), exps[k] / denom, route)
    route_ref[0] = route


def _outproj(o_attn, y_ssm, x, g1, sc2, sh2, p, tm):
    n, l, _ = x.shape
    per_row = g1.shape[1] != 1
    mod_blk = (1, tm, D_MODEL) if per_row else (1, 1, D_MODEL)
    mod_map = (lambda b, i: (b, i, 0)) if per_row else (lambda b, i: (b, 0, 0))
    row_map = lambda b, i: (b, i, 0)
    const2 = lambda b, i: (0, 0)
    return pl.pallas_call(
        _outproj_kernel,
        grid=(n, l // tm),
        in_specs=[pl.BlockSpec((1, tm, ATTN_WIDTH), row_map),
                  pl.BlockSpec((1, tm, SSM_WIDTH), row_map),
                  pl.BlockSpec((1, tm, D_MODEL), row_map),
                  pl.BlockSpec(mod_blk, mod_map),
                  pl.BlockSpec(mod_blk, mod_map),
                  pl.BlockSpec(mod_blk, mod_map),
                  pl.BlockSpec((D_MODEL, D_MODEL), const2),
                  pl.BlockSpec((1, D_MODEL), const2),
                  pl.BlockSpec((1, D_MODEL), const2),
                  pl.BlockSpec((D_MODEL, LANES), const2),
                  pl.BlockSpec((1, LANES), const2)],
        out_specs=[pl.BlockSpec((1, tm, D_MODEL), row_map),
                   pl.BlockSpec((1, tm, D_MODEL), row_map),
                   pl.BlockSpec((1, tm, LANES), row_map)],
        out_shape=[jax.ShapeDtypeStruct((n, l, D_MODEL), F32),
                   jax.ShapeDtypeStruct((n, l, D_MODEL), F32),
                   jax.ShapeDtypeStruct((n, l, LANES), F32)],
        compiler_params=_cparams(("parallel", "parallel")),
        name="outproj",
    )(o_attn, y_ssm, x, g1, sc2, sh2, p["w_out"], p["g_post_mix"], p["g_pre_ffn"],
      p["w_router"], p["b_router"])


def _moe_prep_kernel(w1_ref, w2_ref, perm_ref, w1o_ref, w2o_ref):
    for j in range(w1_ref.shape[2] // GLU_BLOCK):
        cols = slice(j * GLU_BLOCK, (j + 1) * GLU_BLOCK)
        w1o_ref[0, :, cols] = jnp.dot(w1_ref[0, :, cols].astype(BF16), perm_ref[...],
                                      preferred_element_type=F32).astype(BF16)
    w2o_ref[0] = w2_ref[0].astype(BF16)


def _moe_prep(w1, w2):
    n_e, d_in, d_hid = w1.shape
    d_exp = w2.shape[1]
    j = jnp.arange(GLU_BLOCK)
    src = jnp.where(j < LANES, 2 * j, 2 * (j - LANES) + 1)
    perm = (j[:, None] == src[None, :]).astype(BF16)
    e_map = lambda e: (e, 0, 0)
    return pl.pallas_call(
        _moe_prep_kernel,
        grid=(n_e,),
        in_specs=[pl.BlockSpec((1, d_in, d_hid), e_map),
                  pl.BlockSpec((1, d_exp, D_MODEL), e_map),
                  pl.BlockSpec((GLU_BLOCK, GLU_BLOCK), lambda e: (0, 0))],
        out_specs=[pl.BlockSpec((1, d_in, d_hid), e_map),
                   pl.BlockSpec((1, d_exp, D_MODEL), e_map)],
        out_shape=[jax.ShapeDtypeStruct(w1.shape, BF16),
                   jax.ShapeDtypeStruct(w2.shape, BF16)],
        compiler_params=_cparams(("parallel",)),
        name="moe_prep",
    )(w1, w2, perm)


def _route_plan(idx, gates, nch, tmx):
    t = idx.shape[0]
    tc = t // nch
    nt = -(-(tc * TOP_K + N_EXPERTS * (tmx - 1)) // tmx)
    i32 = jnp.int32
    hot = (idx[:, :, None] == jnp.arange(N_EXPERTS, dtype=i32)).astype(i32).sum(1)
    hot = hot.reshape(nch, tc, N_EXPERTS)
    csum = jnp.cumsum(hot, axis=1)
    rank = (csum - hot).reshape(t, N_EXPERTS)
    padded = -(-csum[:, -1, :] // tmx) * tmx
    off_end = jnp.cumsum(padded, axis=1)
    off = off_end - padded
    n_tiles = (off_end[:, -1] // tmx).astype(i32)
    chunk = jnp.arange(t, dtype=i32) // tc
    pos = jnp.take_along_axis(off[chunk] + rank, idx, axis=1).astype(i32)
    tok_local = jnp.broadcast_to((jnp.arange(t, dtype=i32) % tc).astype(F32)[:, None], pos.shape)
    rows = jnp.zeros((nch, nt * tmx, 2), F32).at[chunk[:, None], pos].set(
        jnp.stack([tok_local, gates], axis=-1), unique_indices=True)
    tile = jnp.arange(nt, dtype=i32)
    texp_raw = (off_end[:, None, :] <= (tile * tmx)[None, :, None]).sum(-1).astype(i32)
    tile_c = jnp.minimum(tile[None, :], n_tiles[:, None] - 1)
    tile_n = jnp.minimum(tile[None, :] + 1, n_tiles[:, None] - 1)
    texp = jnp.take_along_axis(texp_raw, tile_c, axis=1)
    base = (jnp.arange(nch, dtype=i32) * nt)[:, None]
    return dict(nt=nt, n_tiles=n_tiles, texp=texp.reshape(-1), tblk=(tile_c + base).reshape(-1),
                tnext=(tile_n + base).reshape(-1),
                row_tok=rows[..., 0].astype(i32).reshape(nch * nt, 1, tmx),
                row_gate=rows[..., 1].reshape(nch * nt, tmx, 1), pos=pos)


def _moe_kernel(nt_ref, texp_ref, tblk_ref, tnext_ref, tok_ref, toknext_ref, gate_ref, src_ref,
                w1_ref, b1_ref, w2_ref, b2_ref, ys_ref, xs_ref):
    del texp_ref, tblk_ref, tnext_ref
    c = pl.program_id(0)
    i = pl.program_id(1)
    tmx = xs_ref.shape[1]
    slot = i % 2

    @pl.when(i == 0)
    def _():
        def gather(r, carry):
            xs_ref[0, pl.ds(r, 1), :] = src_ref[0, pl.ds(tok_ref[0, 0, r], 1), :]
            return carry
        lax.fori_loop(0, tmx, gather, 0, unroll=8)

    @pl.when(i < nt_ref[c])
    def _():
        hid = _mm(xs_ref[slot], w1_ref[0]) + b1_ref[0]
        for r in range(tmx):
            xs_ref[1 - slot, r:r + 1, :] = src_ref[0, pl.ds(toknext_ref[0, 0, r], 1), :]
        acts = []
        for j in range(hid.shape[1] // GLU_BLOCK):
            glu = jnp.minimum(hid[:, j * GLU_BLOCK:j * GLU_BLOCK + LANES], SWIGLU_LIMIT)
            lin = jnp.clip(hid[:, j * GLU_BLOCK + LANES:(j + 1) * GLU_BLOCK], -SWIGLU_LIMIT, SWIGLU_LIMIT)
            acts.append(glu * _sigmoid(SWIGLU_ALPHA * glu) * (lin + 1.0))
        y = _mm(jnp.concatenate(acts, axis=-1), w2_ref[0]) + b2_ref[0]
        ys_ref[...] = gate_ref[0] * y

    @pl.when(i >= nt_ref[c])
    def _():
        ys_ref[...] = jnp.zeros(ys_ref.shape, F32)


def _moe_experts(h2, plan, w1p, b1p, w2p, b2, tmx):
    nch, tc, _ = h2.shape
    nt = plan["nt"]
    n_e, _, d_hid = w1p.shape
    d_exp = w2p.shape[1]
    flat = lambda c, i: c * nt + i
    cur = lambda c, i, n, te, tb, tn: (tb[flat(c, i)], 0, 0)
    nxt = lambda c, i, n, te, tb, tn: (tn[flat(c, i)], 0, 0)
    exp = lambda c, i, n, te, tb, tn: (te[flat(c, i)], 0, 0)
    grid_spec = pltpu.PrefetchScalarGridSpec(
        num_scalar_prefetch=4,
        grid=(nch, nt),
        in_specs=[pl.BlockSpec((1, 1, tmx), cur, memory_space=pltpu.SMEM),
                  pl.BlockSpec((1, 1, tmx), nxt, memory_space=pltpu.SMEM),
                  pl.BlockSpec((1, tmx, 1), cur),
                  pl.BlockSpec((1, tc, D_MODEL), lambda c, i, n, te, tb, tn: (c, 0, 0),
                               pipeline_mode=pl.Buffered(1)),
                  pl.BlockSpec((1, D_MODEL, d_hid), exp),
                  pl.BlockSpec((1, 1, d_hid), exp),
                  pl.BlockSpec((1, d_exp, D_MODEL), exp),
                  pl.BlockSpec((1, 1, D_MODEL), exp)],
        out_specs=pl.BlockSpec((tmx, D_MODEL), lambda c, i, n, te, tb, tn: (flat(c, i), 0)),
        scratch_shapes=[pltpu.VMEM((2, tmx, D_MODEL), F32)],
    )
    return pl.pallas_call(
        _moe_kernel,
        grid_spec=grid_spec,
        out_shape=jax.ShapeDtypeStruct((nch * nt * tmx, D_MODEL), F32),
        compiler_params=_cparams(("parallel", "arbitrary")),
        name="moe_experts",
    )(plan["n_tiles"], plan["texp"], plan["tblk"], plan["tnext"], plan["row_tok"], plan["row_tok"],
      plan["row_gate"], h2, w1p, b1p, w2p, b2)


def _combine_kernel(pos_ref, ys_ref, o_ref):
    def body(t, carry):
        acc = ys_ref[pl.ds(pos_ref[0, 0, TOP_K * t], 1), :]
        for k in range(1, TOP_K):
            acc = acc + ys_ref[pl.ds(pos_ref[0, 0, TOP_K * t + k], 1), :]
        o_ref[pl.ds(t, 1), :] = acc
        return carry
    lax.fori_loop(0, o_ref.shape[0], body, 0, unroll=8)


def _moe_combine(ys, pos, nch, rc, ts, n_col):
    t = pos.shape[0]
    nsub = t // (nch * ts)
    cw = D_MODEL // n_col
    pos3 = pos.reshape(nch * nsub, 1, ts * TOP_K)
    return pl.pallas_call(
        _combine_kernel,
        grid=(nch, n_col, nsub),
        in_specs=[pl.BlockSpec((1, 1, ts * TOP_K), lambda c, q, s: (c * nsub + s, 0, 0),
                               memory_space=pltpu.SMEM),
                  pl.BlockSpec((rc, cw), lambda c, q, s: (c, q), pipeline_mode=pl.Buffered(1))],
        out_specs=pl.BlockSpec((ts, cw), lambda c, q, s: (c * nsub + s, q)),
        out_shape=jax.ShapeDtypeStruct((t, D_MODEL), F32),
        compiler_params=_cparams(("parallel", "parallel", "arbitrary")),
        name="moe_combine",
    )(pos3, ys)


def _final_kernel(x1_ref, f_ref, g2_ref, gpost_ref, y_ref):
    y_ref[0] = x1_ref[0] + g2_ref[0] * _rms(f_ref[0], gpost_ref[...], D_MODEL)


def _final(x1, f, g2, g_post_ffn, tm):
    n, l, _ = x1.shape
    per_row = g2.shape[1] != 1
    mod_blk = (1, tm, D_MODEL) if per_row else (1, 1, D_MODEL)
    mod_map = (lambda b, i: (b, i, 0)) if per_row else (lambda b, i: (b, 0, 0))
    row_map = lambda b, i: (b, i, 0)
    return pl.pallas_call(
        _final_kernel,
        grid=(n, l // tm),
        in_specs=[pl.BlockSpec((1, tm, D_MODEL), row_map),
                  pl.BlockSpec((1, tm, D_MODEL), row_map),
                  pl.BlockSpec(mod_blk, mod_map),
                  pl.BlockSpec((1, D_MODEL), lambda b, i: (0, 0))],
        out_specs=pl.BlockSpec((1, tm, D_MODEL), row_map),
        out_shape=jax.ShapeDtypeStruct((n, l, D_MODEL), F32),
        compiler_params=_cparams(("parallel", "parallel")),
        name="final",
    )(x1, f, g2, g_post_ffn)


def _complex_pow2(lam, n):
    re, im = lam[0], lam[1]
    steps = int(round(math.log2(n)))
    assert 2 ** steps == n
    for _ in range(steps):
        re, im = re * re - im * im, 2.0 * re * im
    return jnp.stack([re, im])


def _rope_tables(pos):
    half = ROPE_DIM // 2
    inv = ROPE_THETA ** (-jnp.arange(half, dtype=F32) / half)
    ang = pos.astype(F32)[:, None] * inv[None, :]
    cos, sin = jnp.cos(ang), jnp.sin(ang)
    cos4 = jnp.tile(jnp.concatenate([cos, cos], axis=-1), (1, LANES // ROPE_DIM))
    sin4 = jnp.tile(jnp.concatenate([-sin, sin], axis=-1), (1, LANES // ROPE_DIM))
    return cos4[None], sin4[None]


def _swap_halves(w):
    half = ROPE_DIM // 2
    return jnp.concatenate([w[..., half:], w[..., :half]], axis=-1)


def _prep_layer(w, layer):
    g = lambda name: w[name][layer]
    p = {}
    row = lambda v: v.reshape(1, -1)
    p["g_pre_mix"] = row(g("g_pre_mix"))
    w_in = g("w_in")
    w_q, w_kv, w_r, w_u = jnp.split(w_in, [Q_RANK, Q_RANK + KV_RANK, Q_RANK + KV_RANK + ROPE_DIM], axis=-1)
    rep = LANES // ROPE_DIM
    p["w_in"] = jnp.concatenate(
        [w_q, jnp.zeros((D_MODEL, Q_PAD - Q_RANK), F32), w_kv,
         jnp.tile(w_r, (1, rep)), jnp.tile(_swap_halves(w_r), (1, rep)), w_u], axis=-1)
    p["g_q"] = row(jnp.pad(g("g_q"), (0, Q_PAD - Q_RANK)))
    w_uq = g("w_uq").reshape(Q_RANK, N_HEADS, NOPE_DIM + ROPE_DIM)
    nope = w_uq[:, :, :NOPE_DIM].reshape(Q_RANK, N_HEADS * NOPE_DIM)
    rope = w_uq[:, :, NOPE_DIM:]
    rope_sw = _swap_halves(rope)
    w_uq_p = jnp.concatenate([nope, rope.reshape(Q_RANK, -1), rope_sw.reshape(Q_RANK, -1)], axis=-1)
    p["w_uq"] = jnp.pad(w_uq_p, ((0, Q_PAD - Q_RANK), (0, 0)))
    w_uk = jnp.transpose(g("w_uk"), (1, 2, 0))
    z = jnp.zeros((NOPE_DIM, KV_RANK), F32)
    p["w_uk"] = jnp.stack([jnp.block([[w_uk[2 * j], z], [z, w_uk[2 * j + 1]]])
                           for j in range(N_HEADS // 2)])
    p["g_kv"] = row(g("g_kv"))
    w_uv = jnp.transpose(g("w_uv"), (1, 0, 2))
    p["w_uv_t"] = jnp.transpose(w_uv, (0, 2, 1))
    eye_h = jnp.eye(N_HEADS, dtype=F32)
    p["w_uv_bd"] = jnp.einsum("hrv,hk->hrkv", w_uv, eye_h).reshape(N_HEADS * KV_RANK, ATTN_WIDTH)
    p["w_out"] = g("w_out")
    p["g_post_mix"] = row(g("g_post_mix"))
    p["g_pre_ffn"] = row(g("g_pre_ffn"))
    p["w_router"] = jnp.pad(g("w_router"), ((0, 0), (0, LANES - N_EXPERTS)))
    p["b_router"] = row(jnp.pad(g("b_router"), (0, LANES - N_EXPERTS), constant_values=NEG_INF))
    p["g_post_ffn"] = row(g("g_post_ffn"))

    lam_re, lam_im = g("ssm_lam_re"), g("ssm_lam_im")
    step = jnp.exp(g("ssm_log_step"))[:, None]
    mag = jnp.exp(lam_re * step)
    bar_re, bar_im = mag * jnp.cos(lam_im * step), mag * jnp.sin(lam_im * step)
    den = lam_re * lam_re + lam_im * lam_im
    co_re = ((bar_re - 1.0) * lam_re + bar_im * lam_im) / den
    co_im = (bar_im * lam_re - (bar_re - 1.0) * lam_im) / den
    b_re, b_im = g("ssm_b_re"), g("ssm_b_im")
    bb_re = co_re[..., None] * b_re - co_im[..., None] * b_im
    bb_im = co_re[..., None] * b_im + co_im[..., None] * b_re
    eye_g = jnp.eye(GROUPS_PER_BLOCK, dtype=F32)

    def b_block(m):
        m = m.reshape(SSM_BLOCKS, GROUPS_PER_BLOCK, SSM_STATE, SSM_GROUP)
        return jnp.einsum("bgpc,gh->bgchp", m, eye_g).reshape(SSM_BLOCKS, LANES, BLOCK_STATES)

    def c_block(m):
        m = m.reshape(SSM_BLOCKS, GROUPS_PER_BLOCK, SSM_GROUP, SSM_STATE)
        return jnp.einsum("bgcp,gh->bgphc", m, eye_g).reshape(SSM_BLOCKS, BLOCK_STATES, LANES)

    sp = {
        "lam": jnp.stack([bar_re.reshape(-1), bar_im.reshape(-1)]),
        "bmat": jnp.concatenate([b_block(bb_re), b_block(bb_im)], axis=-1).astype(BF16),
        "cmat": jnp.concatenate([c_block(g("ssm_c_re")), c_block(-g("ssm_c_im"))], axis=1).astype(BF16),
        "d": row(g("ssm_d")),
        "w_glu": g("ssm_w_glu"),
        "b_glu": row(g("ssm_b_glu")),
    }
    b1 = g("b1")
    n_blk = b1.shape[1] // GLU_BLOCK
    mp = {
        "b1": jnp.transpose(b1.reshape(N_EXPERTS, n_blk, LANES, 2), (0, 1, 3, 2)).reshape(N_EXPERTS, 1, -1),
        "b2": g("b2")[:, None, :],
    }
    return p, sp, mp


def _mods(mod_rows):
    return [mod_rows[:, k * D_MODEL:(k + 1) * D_MODEL] for k in range(N_MOD)]


def _pick_tile(n, target, mult):
    best = mult
    for c in range(mult, min(n, target) + 1, mult):
        if n % c == 0:
            best = c
    return best


def kernel(x_prompt, x_sample, c_prompt, c_sample, cache_kv_latent, cache_k_rope, state_ssm, page_table, w_ada, b_ada, g_pre_mix, w_in, g_q, w_uq, w_uk, g_kv, w_uv, ssm_lam_re, ssm_lam_im, ssm_log_step, ssm_b_re, ssm_b_im, ssm_c_re, ssm_c_im, ssm_d, ssm_w_glu, ssm_b_glu, w_out, g_post_mix, g_pre_ffn, w_router, b_router, w1, b1, w2, b2, g_post_ffn):
    weights = dict(w_ada=w_ada, b_ada=b_ada, g_pre_mix=g_pre_mix, w_in=w_in, g_q=g_q, w_uq=w_uq,
                   w_uk=w_uk, g_kv=g_kv, w_uv=w_uv, ssm_lam_re=ssm_lam_re, ssm_lam_im=ssm_lam_im,
                   ssm_log_step=ssm_log_step, ssm_b_re=ssm_b_re, ssm_b_im=ssm_b_im,
                   ssm_c_re=ssm_c_re, ssm_c_im=ssm_c_im, ssm_d=ssm_d, ssm_w_glu=ssm_w_glu,
                   ssm_b_glu=ssm_b_glu, w_out=w_out, g_post_mix=g_post_mix, g_pre_ffn=g_pre_ffn,
                   w_router=w_router, b_router=b_router, w1=w1, b1=b1, w2=w2, b2=b2,
                   g_post_ffn=g_post_ffn)
    depth = w_ada.shape[0]
    n_p, l_p, _ = x_prompt.shape
    n_s, l_s, _ = x_sample.shape
    assert l_s == 1
    n_pages = page_table.shape[1]
    past_len = n_pages * PAGE_SIZE

    cos_p, sin_p = _rope_tables(jnp.arange(l_p, dtype=jnp.int32))
    cos_s, sin_s = _rope_tables(jnp.full((1,), past_len, jnp.int32))
    cos_s = jnp.broadcast_to(cos_s, (1, n_s, LANES))
    sin_s = jnp.broadcast_to(sin_s, (1, n_s, LANES))

    tm_p = _pick_tile(l_p, 512, 16)
    tq = _pick_tile(l_p, 256, 16)
    tk = _pick_tile(l_p, 1024, 16)
    lc = 64 if l_p % 512 == 0 else l_p // SUBLANES
    pg = _pick_tile(n_pages, 64, 1)
    c_rows = n_p + n_s
    c_pad = -c_rows % SUBLANES
    c_all = jnp.pad(jnp.concatenate([c_prompt, c_sample], axis=0), ((0, c_pad), (0, 0)))

    y_p = x_prompt
    y_s = x_sample.reshape(1, n_s, D_MODEL)
    outs = [[] for _ in range(6)]
    for layer in range(depth):
        p, sp, mp = _prep_layer(weights, layer)
        mod = _ada(c_all, w_ada[layer], b_ada[layer].reshape(1, -1))
        sh1p, sc1p, g1p, sh2p, sc2p, g2p = [m[:, None, :] for m in _mods(mod[:n_p])]
        sh1s, sc1s, g1s, sh2s, sc2s, g2s = [m[None] for m in _mods(mod[n_p:c_rows])]

        q_t, k_cat, v_t, kv_p, kr_p, u_p = _inproj(y_p, sc1p, sh1p, p, cos_p, sin_p, tm_p)
        o_attn_p = _flash(q_t, k_cat, v_t, p["w_uv_t"], tq, tk)
        y_ssm_p, h_last_p = _s5_prompt(u_p, sp, lc)
        x1_p, h2_p, comb_p = _outproj(o_attn_p, y_ssm_p, y_p, g1p, sc2p, sh2p, p, tm_p)

        q_t_s, _, _, kv_s, kr_s, u_s = _inproj(y_s, sc1s, sh1s, p, cos_s, sin_s, n_s)
        q_lat_s = jnp.transpose(q_t_s[0, :, :KV_RANK, :], (2, 0, 1))
        q_rope_s = jnp.stack(
            [q_t_s[0, h, KV_RANK + ROPE_DIM * (h % 4):KV_RANK + ROPE_DIM * (h % 4 + 1), :].T
             for h in range(N_HEADS)], axis=1)
        o_lat_s = _decode_attn(q_lat_s, q_rope_s, kv_s.reshape(n_s, 1, KV_RANK),
                               kr_s.reshape(n_s, 1, ROPE_DIM), cache_kv_latent[layer],
                               jnp.swapaxes(cache_k_rope[layer], 1, 2), page_table, pg)
        o_attn_s = _uv_proj(o_lat_s.reshape(n_s, N_HEADS * KV_RANK), p["w_uv_bd"])
        h0 = jnp.moveaxis(state_ssm[layer].reshape(n_s, N_STATE, 2), -1, 0)
        y_ssm_s, h_s = _s5_step(u_s[0], h0, sp)
        x1_s, h2_s, comb_s = _outproj(o_attn_s[None], y_ssm_s[None], y_s, g1s, sc2s, sh2s, p, n_s)

        t_p = n_p * l_p
        t_all = t_p + n_s
        nch = 4 if t_all % (4 * SUBLANES) == 0 else (2 if t_all % (2 * SUBLANES) == 0 else 1)
        tc = t_all // nch
        h2_all = jnp.concatenate([h2_p.reshape(t_p, D_MODEL), h2_s[0]], axis=0)
        route = jnp.concatenate([comb_p.reshape(t_p, LANES), comb_s[0]], axis=0)
        plan = _route_plan(route[:, :TOP_K].astype(jnp.int32), route[:, TOP_K:2 * TOP_K], nch, MOE_TILE)
        w1p, w2p = _moe_prep(w1[layer], w2[layer])
        ys = _moe_experts(h2_all.reshape(nch, tc, D_MODEL), plan, w1p, mp["b1"], w2p, mp["b2"], MOE_TILE)
        f_all = _moe_combine(ys, plan["pos"], nch, plan["nt"] * MOE_TILE,
                             _pick_tile(tc, 1100, SUBLANES), MOE_COL_PARTS)
        y_p = _final(x1_p, f_all[:t_p].reshape(n_p, l_p, D_MODEL), g2p, p["g_post_ffn"], tm_p)
        y_s = _final(x1_s, f_all[t_p:][None], g2s, p["g_post_ffn"], n_s)

        outs[0].append(kv_p)
        outs[1].append(kr_p)
        outs[2].append(jnp.moveaxis(h_last_p, 1, -1).reshape(n_p, N_SSM_GROUPS, SSM_STATE, 2))
        outs[3].append(kv_s.reshape(n_s, 1, KV_RANK))
        outs[4].append(kr_s.reshape(n_s, 1, ROPE_DIM))
        outs[5].append(jnp.moveaxis(h_s, 0, -1).reshape(n_s, N_SSM_GROUPS, SSM_STATE, 2))
    return (y_p, y_s.reshape(n_s, 1, D_MODEL)) + tuple(jnp.stack(o) for o in outs)
```

```python
import functools
import math

import jax
import jax.numpy as jnp
from jax import lax
from jax.experimental import pallas as pl
from jax.experimental.pallas import tpu as pltpu

F32 = jnp.float32
BF16 = jnp.bfloat16

D_MODEL = 1024
N_HEADS = 8
NOPE_DIM = 64
ROPE_DIM = 32
V_DIM = 64
Q_RANK = 352
KV_RANK = 128
ROPE_THETA = 10000.0
ATTN_WIDTH = N_HEADS * V_DIM
ATTN_SCALE = 1.0 / math.sqrt(NOPE_DIM + ROPE_DIM)
SSM_WIDTH = D_MODEL - ATTN_WIDTH
SSM_GROUP = 16
N_SSM_GROUPS = SSM_WIDTH // SSM_GROUP
SSM_STATE = 64
N_STATE = N_SSM_GROUPS * SSM_STATE
N_EXPERTS = 32
TOP_K = 4
SWIGLU_LIMIT = 7.0
SWIGLU_ALPHA = 1.702
RMS_EPS = 1e-6
N_MOD = 6
NEG_INF = -1e30
PAGE_SIZE = 128

LANES = 128
SUBLANES = 8
VMEM_LIMIT = 56 * 1024 * 1024

Q_PAD = 384
CAT = 2 * LANES
SSM_BLOCKS = 4
GROUPS_PER_BLOCK = N_SSM_GROUPS // SSM_BLOCKS
BLOCK_STATES = GROUPS_PER_BLOCK * SSM_STATE
IN_PAD = Q_PAD + KV_RANK + 2 * LANES + SSM_WIDTH
UQ_PAD = N_HEADS * NOPE_DIM + 4 * LANES
GLU_BLOCK = 2 * LANES
MOE_TILE = 192
MOE_COL_PARTS = 2
LOG2E = math.log2(math.e)
FLASH_LOOKAHEAD = 3


def _cparams(sem):
    return pltpu.CompilerParams(dimension_semantics=sem, vmem_limit_bytes=VMEM_LIMIT)


def _mm(a, b):
    return jnp.dot(a.astype(BF16), b.astype(BF16), preferred_element_type=F32)


def _mm_nt(a, b):
    return lax.dot_general(a.astype(BF16), b.astype(BF16), (((1,), (1,)), ((), ())),
                           preferred_element_type=F32)


def _split_bf16(x):
    hi = x.astype(BF16)
    lo = (x - hi.astype(F32)).astype(BF16)
    return hi, lo


def _mm3(a, b):
    ah, al = _split_bf16(a)
    bh, bl = _split_bf16(b)
    d = functools.partial(jnp.dot, preferred_element_type=F32)
    return d(ah, bh) + (d(ah, bl) + d(al, bh))


def _sigmoid(x):
    return 1.0 / (1.0 + jnp.exp(-x))


def _rms(x, g, width):
    ms = jnp.sum(x * x, axis=-1, keepdims=True) * (1.0 / width)
    return x * lax.rsqrt(ms + RMS_EPS) * g


def _gelu_tanh(x):
    return 0.5 * x * (1.0 + jnp.tanh(math.sqrt(2.0 / math.pi) * (x + 0.044715 * (x * x * x))))


def _ada_kernel(c_ref, w_ref, b_ref, o_ref):
    c = c_ref[...]
    o_ref[...] = _mm3(c * _sigmoid(c), w_ref[...]) + b_ref[...]


def _ada(c_all, w_ada, b_ada):
    rows = c_all.shape[0]
    tn = 512
    n_out = w_ada.shape[1]
    return pl.pallas_call(
        _ada_kernel,
        grid=(n_out // tn,),
        in_specs=[pl.BlockSpec((rows, D_MODEL), lambda j: (0, 0)),
                  pl.BlockSpec((D_MODEL, tn), lambda j: (0, j)),
                  pl.BlockSpec((1, tn), lambda j: (0, j))],
        out_specs=pl.BlockSpec((rows, tn), lambda j: (0, j)),
        out_shape=jax.ShapeDtypeStruct((rows, n_out), F32),
        compiler_params=_cparams(("parallel",)),
        name="adaln",
    )(c_all, w_ada, b_ada)


def _inproj_kernel(x_ref, sc_ref, sh_ref, gpre_ref, win_ref, gq_ref, wuq_ref, wuk_ref, gkv_ref,
                   cos_ref, sin_ref, q_ref, kcat_ref, vt_ref, kv_ref, kr_ref, u_ref):
    x = x_ref[0]
    h = _rms(x, gpre_ref[...], D_MODEL) * (1.0 + sc_ref[0]) + sh_ref[0]
    z = _mm(h, win_ref[...])
    zq = z[:, :Q_PAD]
    zkv = z[:, Q_PAD:Q_PAD + KV_RANK]
    zr = z[:, Q_PAD + KV_RANK:Q_PAD + KV_RANK + LANES]
    zrs = z[:, Q_PAD + KV_RANK + LANES:Q_PAD + KV_RANK + 2 * LANES]
    u = z[:, Q_PAD + KV_RANK + 2 * LANES:]
    cos4 = cos_ref[0]
    sin4 = sin_ref[0]

    kvl = _rms(zkv, gkv_ref[...], KV_RANK)
    kr4 = zr * cos4 + zrs * sin4
    kv_ref[0] = kvl
    kr_ref[0] = kr4[:, :ROPE_DIM]
    kcat_ref[0] = jnp.concatenate([kvl, kr4], axis=-1).astype(BF16)
    vt_ref[0] = kvl.T.astype(BF16)
    u_ref[0] = u.astype(BF16)

    qn = _rms(zq, gq_ref[...], Q_RANK)
    q = _mm(qn, wuq_ref[...])
    nope_w = N_HEADS * NOPE_DIM
    rope_a = q[:, nope_w:nope_w + LANES] * cos4 + q[:, nope_w + 2 * LANES:nope_w + 3 * LANES] * sin4
    rope_b = q[:, nope_w + LANES:nope_w + 2 * LANES] * cos4 + q[:, nope_w + 3 * LANES:] * sin4
    slot = lax.broadcasted_iota(jnp.int32, (1, LANES), 1) // ROPE_DIM
    for j in range(N_HEADS // 2):
        ql = _mm(q[:, j * LANES:(j + 1) * LANES], wuk_ref[j])
        for hh in range(2):
            head = 2 * j + hh
            rope = rope_a if head < 4 else rope_b
            rope_h = jnp.where(slot == (head % 4), rope, 0.0)
            qc = jnp.concatenate([ql[:, hh * KV_RANK:(hh + 1) * KV_RANK], rope_h], axis=-1)
            q_ref[0, head] = (qc * (ATTN_SCALE * LOG2E)).T.astype(BF16)


def _inproj(x, sc, sh, p, cos4, sin4, tm):
    n, l, _ = x.shape
    per_row = sc.shape[1] != 1
    mod_blk = (1, tm, D_MODEL) if per_row else (1, 1, D_MODEL)
    mod_map = (lambda b, i: (b, i, 0)) if per_row else (lambda b, i: (b, 0, 0))
    const2 = lambda b, i: (0, 0)
    const3 = lambda b, i: (0, 0, 0)
    pos_map = lambda b, i: (0, i, 0)
    return pl.pallas_call(
        _inproj_kernel,
        grid=(n, l // tm),
        in_specs=[pl.BlockSpec((1, tm, D_MODEL), lambda b, i: (b, i, 0)),
                  pl.BlockSpec(mod_blk, mod_map),
                  pl.BlockSpec(mod_blk, mod_map),
                  pl.BlockSpec((1, D_MODEL), const2),
                  pl.BlockSpec((D_MODEL, IN_PAD), const2),
                  pl.BlockSpec((1, Q_PAD), const2),
                  pl.BlockSpec((Q_PAD, UQ_PAD), const2),
                  pl.BlockSpec((N_HEADS // 2, LANES, 2 * KV_RANK), const3),
                  pl.BlockSpec((1, KV_RANK), const2),
                  pl.BlockSpec((1, tm, LANES), pos_map),
                  pl.BlockSpec((1, tm, LANES), pos_map)],
        out_specs=[pl.BlockSpec((1, N_HEADS, CAT, tm), lambda b, i: (b, 0, 0, i)),
                   pl.BlockSpec((1, tm, CAT), lambda b, i: (b, i, 0)),
                   pl.BlockSpec((1, KV_RANK, tm), lambda b, i: (b, 0, i)),
                   pl.BlockSpec((1, tm, KV_RANK), lambda b, i: (b, i, 0)),
                   pl.BlockSpec((1, tm, ROPE_DIM), lambda b, i: (b, i, 0)),
                   pl.BlockSpec((1, tm, SSM_WIDTH), lambda b, i: (b, i, 0))],
        out_shape=[jax.ShapeDtypeStruct((n, N_HEADS, CAT, l), BF16),
                   jax.ShapeDtypeStruct((n, l, CAT), BF16),
                   jax.ShapeDtypeStruct((n, KV_RANK, l), BF16),
                   jax.ShapeDtypeStruct((n, l, KV_RANK), F32),
                   jax.ShapeDtypeStruct((n, l, ROPE_DIM), F32),
                   jax.ShapeDtypeStruct((n, l, SSM_WIDTH), BF16)],
        compiler_params=_cparams(("parallel", "parallel")),
        name="inproj",
    )(x, sc, sh, p["g_pre_mix"], p["w_in"], p["g_q"], p["w_uq"], p["w_uk"], p["g_kv"], cos4, sin4)


def _flash_kernel(qt_ref, k_ref, vt_ref, wuvt_ref, o_ref, m_ref, l_ref, acc_ref, *, tq, tk):
    i = pl.program_id(1)
    j = pl.program_id(2)
    j_last = ((i + 1) * tq - 1) // tk

    @pl.when(j == 0)
    def _():
        m_ref[...] = jnp.full(m_ref.shape, NEG_INF, F32)
        l_ref[...] = jnp.zeros(l_ref.shape, F32)
        acc_ref[...] = jnp.zeros(acc_ref.shape, F32)

    def update(masked):
        k = k_ref[0]
        vt = vt_ref[0]
        if masked:
            k_pos = j * tk + lax.broadcasted_iota(jnp.int32, (tk, tq), 0)
            q_pos = i * tq + lax.broadcasted_iota(jnp.int32, (tk, tq), 1)
            keep = k_pos <= q_pos
        m_all = m_ref[...]
        l_all = l_ref[...]
        m_rows, l_rows = [], []
        scores = lambda h: jnp.dot(k, qt_ref[0, h], preferred_element_type=F32)
        ahead = [scores(h) for h in range(min(FLASH_LOOKAHEAD, N_HEADS))]
        pending = None
        for h in range(N_HEADS):
            s = ahead.pop(0)
            if h + FLASH_LOOKAHEAD < N_HEADS:
                ahead.append(scores(h + FLASH_LOOKAHEAD))
            if masked:
                s = jnp.where(keep, s, NEG_INF)
            m_prev = m_all[h:h + 1]
            m_new = jnp.maximum(m_prev, jnp.max(s, axis=0, keepdims=True))
            alpha = jnp.exp2(m_prev - m_new)
            p = jnp.exp2(s - m_new)
            l_rows.append(alpha * l_all[h:h + 1] + jnp.sum(p, axis=0, keepdims=True))
            pv = jnp.dot(vt, p.astype(BF16), preferred_element_type=F32)
            if pending is not None:
                acc_ref[pending[0]] = pending[1] * acc_ref[pending[0]] + pending[2]
            pending = (h, alpha, pv)
            m_rows.append(m_new)
        acc_ref[pending[0]] = pending[1] * acc_ref[pending[0]] + pending[2]
        m_ref[...] = jnp.concatenate(m_rows, axis=0)
        l_ref[...] = jnp.concatenate(l_rows, axis=0)

    crosses = (j + 1) * tk - 1 > i * tq

    @pl.when(jnp.logical_and(j <= j_last, crosses))
    def _():
        update(True)

    @pl.when(jnp.logical_and(j <= j_last, jnp.logical_not(crosses)))
    def _():
        update(False)

    @pl.when(j == j_last)
    def _():
        outs = []
        for h in range(N_HEADS):
            o_t = acc_ref[h] / l_ref[h:h + 1, :]
            outs.append(_mm(wuvt_ref[h], o_t))
        o_ref[0] = jnp.concatenate(outs, axis=0).T.astype(BF16)


def _flash(q_t, k_cat, v_t, wuv_t, tq, tk):
    n, _, _, l = q_t.shape
    kern = functools.partial(_flash_kernel, tq=tq, tk=tk)
    last = lambda i: ((i + 1) * tq - 1) // tk
    return pl.pallas_call(
        kern,
        grid=(n, l // tq, l // tk),
        in_specs=[pl.BlockSpec((1, N_HEADS, CAT, tq), lambda b, i, j: (b, 0, 0, i)),
                  pl.BlockSpec((1, tk, CAT), lambda b, i, j: (b, jnp.minimum(j, last(i)), 0)),
                  pl.BlockSpec((1, KV_RANK, tk), lambda b, i, j: (b, 0, jnp.minimum(j, last(i)))),
                  pl.BlockSpec((N_HEADS, V_DIM, KV_RANK), lambda b, i, j: (0, 0, 0))],
        out_specs=pl.BlockSpec((1, tq, ATTN_WIDTH), lambda b, i, j: (b, i, 0)),
        out_shape=jax.ShapeDtypeStruct((n, l, ATTN_WIDTH), BF16),
        scratch_shapes=[pltpu.VMEM((N_HEADS, tq), F32),
                        pltpu.VMEM((N_HEADS, tq), F32),
                        pltpu.VMEM((N_HEADS, KV_RANK, tq), F32)],
        compiler_params=_cparams(("parallel", "parallel", "arbitrary")),
        name="flash",
    )(q_t, k_cat, v_t, wuv_t)


def _ssm_out(y, up, d_ref, wglu_ref, bglu_ref):
    y = _gelu_tanh(y + d_ref[...] * up)
    return y * _sigmoid(_mm(y, wglu_ref[...]) + bglu_ref[...])


def _s5_kernel(u_ref, perm_ref, permt_ref, bmat_ref, cmat_ref, lam_ref, laml_ref, d_ref, wglu_ref,
               bglu_ref, y_ref, hlast_ref, buf_ref, carry_ref, *, lc):
    j = pl.program_id(1)

    @pl.when(j == 0)
    def _():
        carry_ref[...] = jnp.zeros(carry_ref.shape, F32)

    u = u_ref[0]
    up = jnp.dot(perm_ref[...], u, preferred_element_type=F32).astype(BF16)
    blk_w = 2 * BLOCK_STATES
    for b in range(SSM_BLOCKS):
        buf_ref[:, b * blk_w:(b + 1) * blk_w] = jnp.dot(
            up[:, b * LANES:(b + 1) * LANES], bmat_ref[b], preferred_element_type=F32)

    for b in range(SSM_BLOCKS):
        st = slice(b * BLOCK_STATES, (b + 1) * BLOCK_STATES)
        re_cols = slice(b * blk_w, b * blk_w + BLOCK_STATES)
        im_cols = slice(b * blk_w + BLOCK_STATES, (b + 1) * blk_w)
        lr = lam_ref[0:1, st]
        li = lam_ref[1:2, st]

        def run(init, store):
            def body(t, carry):
                re, im = carry
                rows = pl.ds(pl.multiple_of(t * SUBLANES, SUBLANES), SUBLANES)
                n_re = lr * re - li * im + buf_ref[rows, re_cols]
                n_im = lr * im + li * re + buf_ref[rows, im_cols]
                if store:
                    buf_ref[rows, re_cols] = n_re
                    buf_ref[rows, im_cols] = n_im
                return n_re, n_im
            return lax.fori_loop(0, lc, body, init)

        zero = jnp.zeros((SUBLANES, BLOCK_STATES), F32)
        end_re, end_im = run((zero, zero), False)
        ll_r = laml_ref[0:1, st]
        ll_i = laml_ref[1:2, st]
        c_re = carry_ref[0:1, st]
        c_im = carry_ref[1:2, st]
        init_re, init_im = [], []
        for k in range(SUBLANES):
            init_re.append(c_re)
            init_im.append(c_im)
            c_re, c_im = (ll_r * c_re - ll_i * c_im + end_re[k:k + 1],
                          ll_r * c_im + ll_i * c_re + end_im[k:k + 1])
        carry_ref[0:1, st] = c_re
        carry_ref[1:2, st] = c_im
        run((jnp.concatenate(init_re, axis=0), jnp.concatenate(init_im, axis=0)), True)

    hlast_ref[0] = carry_ref[...]
    ys = [jnp.dot(buf_ref[:, b * blk_w:(b + 1) * blk_w].astype(BF16), cmat_ref[b],
                  preferred_element_type=F32) for b in range(SSM_BLOCKS)]
    y = _ssm_out(jnp.concatenate(ys, axis=-1), up.astype(F32), d_ref, wglu_ref, bglu_ref)
    y_ref[0] = jnp.dot(permt_ref[...], y.astype(BF16), preferred_element_type=F32).astype(BF16)


def _s5_prompt(u, sp, lc):
    n, l, _ = u.shape
    s = SUBLANES * lc
    rows = jnp.arange(s)
    src = (rows % SUBLANES) * lc + rows // SUBLANES
    perm = jnp.zeros((s, s), BF16).at[rows, src].set(1.0)
    laml = _complex_pow2(sp["lam"], lc)
    kern = functools.partial(_s5_kernel, lc=lc)
    c2 = lambda b, j: (0, 0)
    c3 = lambda b, j: (0, 0, 0)
    return pl.pallas_call(
        kern,
        grid=(n, l // s),
        in_specs=[pl.BlockSpec((1, s, SSM_WIDTH), lambda b, j: (b, j, 0)),
                  pl.BlockSpec((s, s), c2),
                  pl.BlockSpec((s, s), c2),
                  pl.BlockSpec((SSM_BLOCKS, LANES, 2 * BLOCK_STATES), c3),
                  pl.BlockSpec((SSM_BLOCKS, 2 * BLOCK_STATES, LANES), c3),
                  pl.BlockSpec((2, N_STATE), c2),
                  pl.BlockSpec((2, N_STATE), c2),
                  pl.BlockSpec((1, SSM_WIDTH), c2),
                  pl.BlockSpec((SSM_WIDTH, SSM_WIDTH), c2),
                  pl.BlockSpec((1, SSM_WIDTH), c2)],
        out_specs=[pl.BlockSpec((1, s, SSM_WIDTH), lambda b, j: (b, j, 0)),
                   pl.BlockSpec((1, 2, N_STATE), lambda b, j: (b, 0, 0))],
        out_shape=[jax.ShapeDtypeStruct((n, l, SSM_WIDTH), BF16),
                   jax.ShapeDtypeStruct((n, 2, N_STATE), F32)],
        scratch_shapes=[pltpu.VMEM((s, 2 * N_STATE), F32),
                        pltpu.VMEM((2, N_STATE), F32)],
        compiler_params=_cparams(("parallel", "arbitrary")),
        name="s5_scan",
    )(u, perm, perm.T, sp["bmat"], sp["cmat"], sp["lam"], laml, sp["d"], sp["w_glu"], sp["b_glu"])


def _s5_step_kernel(u_ref, h0_ref, bmat_ref, cmat_ref, lam_ref, d_ref, wglu_ref, bglu_ref,
                    y_ref, h_ref):
    u = u_ref[...]
    ys = []
    for b in range(SSM_BLOCKS):
        st = slice(b * BLOCK_STATES, (b + 1) * BLOCK_STATES)
        bu = jnp.dot(u[:, b * LANES:(b + 1) * LANES], bmat_ref[b], preferred_element_type=F32)
        lr = lam_ref[0:1, st]
        li = lam_ref[1:2, st]
        re = h0_ref[0, :, st]
        im = h0_ref[1, :, st]
        n_re = lr * re - li * im + bu[:, :BLOCK_STATES]
        n_im = lr * im + li * re + bu[:, BLOCK_STATES:]
        h_ref[0, :, st] = n_re
        h_ref[1, :, st] = n_im
        ys.append(_mm(jnp.concatenate([n_re, n_im], axis=-1), cmat_ref[b]))
    y = _ssm_out(jnp.concatenate(ys, axis=-1), u.astype(F32), d_ref, wglu_ref, bglu_ref)
    y_ref[...] = y.astype(BF16)


def _s5_step(u, h0, sp):
    rows = u.shape[0]
    return pl.pallas_call(
        _s5_step_kernel,
        out_shape=[jax.ShapeDtypeStruct((rows, SSM_WIDTH), BF16),
                   jax.ShapeDtypeStruct((2, rows, N_STATE), F32)],
        compiler_params=pltpu.CompilerParams(vmem_limit_bytes=VMEM_LIMIT),
        name="s5_step",
    )(u, h0, sp["bmat"], sp["cmat"], sp["lam"], sp["d"], sp["w_glu"], sp["b_glu"])


def _decode_kernel(pt_ref, ql_ref, qr_ref, kvs_ref, krs_ref, lat_hbm, rope_hbm, o_ref,
                   lat_buf, rope_buf, sems, *, n_pages):
    b = pl.program_id(0)
    slot = b % 2

    def page_copies(seq, dst_slot):
        out = []
        for i in range(n_pages):
            page = pt_ref[seq * n_pages + i]
            keys = pl.ds(i * PAGE_SIZE, PAGE_SIZE)
            out.append(pltpu.make_async_copy(lat_hbm.at[page], lat_buf.at[dst_slot, keys, :],
                                             sems.at[dst_slot, 0]))
            out.append(pltpu.make_async_copy(rope_hbm.at[page], rope_buf.at[dst_slot, :, keys],
                                             sems.at[dst_slot, 1]))
        return out

    @pl.when(b == 0)
    def _():
        for cp in page_copies(0, 0):
            cp.start()

    @pl.when(b + 1 < pl.num_programs(0))
    def _():
        for cp in page_copies(b + 1, 1 - slot):
            cp.start()

    for cp in page_copies(b, slot):
        cp.wait()

    ql = ql_ref[0]
    qr = qr_ref[0]
    kvs = kvs_ref[0]
    krs = krs_ref[0]
    lat = lat_buf[slot].astype(BF16)
    s = _mm_nt(ql, lat) + _mm(qr, rope_buf[slot])
    s_self = (jnp.sum(ql.astype(F32) * kvs, axis=-1, keepdims=True)
              + jnp.sum(qr.astype(F32) * krs, axis=-1, keepdims=True))
    m = jnp.maximum(jnp.max(s, axis=-1, keepdims=True), s_self)
    p = jnp.exp2(s - m)
    p_self = jnp.exp2(s_self - m)
    denom = jnp.sum(p, axis=-1, keepdims=True) + p_self
    o_ref[0] = (_mm(p, lat) + p_self * kvs) / denom


def _decode_attn(q_lat, q_rope, kv_self, kr_self, cache_lat, cache_rope_t, page_table):
    b, n_pages = page_table.shape
    kern = functools.partial(_decode_kernel, n_pages=n_pages)
    seq3 = lambda s, pt: (s, 0, 0)
    keys = n_pages * PAGE_SIZE
    grid_spec = pltpu.PrefetchScalarGridSpec(
        num_scalar_prefetch=1,
        grid=(b,),
        in_specs=[pl.BlockSpec((1, N_HEADS, KV_RANK), seq3),
                  pl.BlockSpec((1, N_HEADS, ROPE_DIM), seq3),
                  pl.BlockSpec((1, 1, KV_RANK), seq3),
                  pl.BlockSpec((1, 1, ROPE_DIM), seq3),
                  pl.BlockSpec(memory_space=pl.ANY),
                  pl.BlockSpec(memory_space=pl.ANY)],
        out_specs=pl.BlockSpec((1, N_HEADS, KV_RANK), seq3),
        scratch_shapes=[pltpu.VMEM((2, keys, KV_RANK), F32),
                        pltpu.VMEM((2, ROPE_DIM, keys), F32),
                        pltpu.SemaphoreType.DMA((2, 2))],
    )
    return pl.pallas_call(
        kern,
        grid_spec=grid_spec,
        out_shape=jax.ShapeDtypeStruct((b, N_HEADS, KV_RANK), F32),
        compiler_params=_cparams(("arbitrary",)),
        name="decode_attn",
    )(page_table.reshape(-1), q_lat, q_rope, kv_self, kr_self, cache_lat, cache_rope_t)


def _uv_kernel(o_ref, w_ref, y_ref):
    y_ref[...] = _mm(o_ref[...], w_ref[...]).astype(BF16)


def _uv_proj(o_lat, wuv_bd):
    rows = o_lat.shape[0]
    return pl.pallas_call(
        _uv_kernel,
        out_shape=jax.ShapeDtypeStruct((rows, ATTN_WIDTH), BF16),
        compiler_params=pltpu.CompilerParams(vmem_limit_bytes=VMEM_LIMIT),
        name="uv_proj",
    )(o_lat, wuv_bd)


def _outproj_kernel(oa_ref, ys_ref, x_ref, g1_ref, sc_ref, sh_ref, wout_ref, gpost_ref, gpre_ref,
                    wr_ref, br_ref, *rest):
    x1_ref, h2_ref, route_ref = rest[-3:]
    mix = (jnp.dot(oa_ref[0], wout_ref[:ATTN_WIDTH].astype(BF16), preferred_element_type=F32)
           + jnp.dot(ys_ref[0], wout_ref[ATTN_WIDTH:].astype(BF16), preferred_element_type=F32))
    x1 = x_ref[0] + g1_ref[0] * _rms(mix, gpost_ref[...], D_MODEL)
    x1_ref[0] = x1
    h2 = _rms(x1, gpre_ref[...], D_MODEL) * (1.0 + sc_ref[0]) + sh_ref[0]
    for s in range(D_MODEL // LANES):
        h2_ref[:, s, :] = h2[:, s * LANES:(s + 1) * LANES]
    logits = _mm3(h2, wr_ref[...]) + br_ref[...]
    lane = lax.broadcasted_iota(jnp.int32, logits.shape, 1).astype(F32)
    vals, idxs = [], []
    for _ in range(TOP_K):
        v = jnp.max(logits, axis=-1, keepdims=True)
        idx = jnp.min(jnp.where(logits == v, lane, float(LANES)), axis=-1, keepdims=True)
        vals.append(v)
        idxs.append(idx)
        logits = jnp.where(lane == idx, -jnp.inf, logits)
    exps = [jnp.exp(v - vals[0]) for v in vals]
    denom = exps[0] + exps[1] + exps[2] + exps[3]
    route = jnp.zeros(logits.shape, F32)
    for k in range(TOP_K):
        route = jnp.where(lane == float(k), idxs[k], route)
        route = jnp.where(lane == float(TOP_K + k), exps[k] / denom, route)
    route_ref[...] = route


def _outproj(o_attn, y_ssm, x, g1, sc2, sh2, p, tm, t_all, row0, dest=None):
    n, l, _ = x.shape
    per_row = g1.shape[1] != 1
    mod_blk = (1, tm, D_MODEL) if per_row else (1, 1, D_MODEL)
    mod_map = (lambda b, i: (b, i, 0)) if per_row else (lambda b, i: (b, 0, 0))
    row_map = lambda b, i: (b, i, 0)
    const2 = lambda b, i: (0, 0)
    blk0, per_seq = row0 // tm, l // tm
    n_tile = D_MODEL // LANES
    in_specs = [pl.BlockSpec((1, tm, ATTN_WIDTH), row_map),
                pl.BlockSpec((1, tm, SSM_WIDTH), row_map),
                pl.BlockSpec((1, tm, D_MODEL), row_map),
                pl.BlockSpec(mod_blk, mod_map),
                pl.BlockSpec(mod_blk, mod_map),
                pl.BlockSpec(mod_blk, mod_map),
                pl.BlockSpec((D_MODEL, D_MODEL), const2),
                pl.BlockSpec((1, D_MODEL), const2),
                pl.BlockSpec((1, D_MODEL), const2),
                pl.BlockSpec((D_MODEL, LANES), const2),
                pl.BlockSpec((1, LANES), const2)]
    args = [o_attn, y_ssm, x, g1, sc2, sh2, p["w_out"], p["g_post_mix"], p["g_pre_ffn"],
            p["w_router"], p["b_router"]]
    aliases = {}
    if dest is not None:
        aliases = {len(args): 1, len(args) + 1: 2}
        in_specs += [pl.BlockSpec(memory_space=pl.ANY), pl.BlockSpec(memory_space=pl.ANY)]
        args += list(dest)
    return pl.pallas_call(
        _outproj_kernel,
        grid=(n, per_seq),
        in_specs=in_specs,
        out_specs=[pl.BlockSpec((1, tm, D_MODEL), row_map),
                   pl.BlockSpec((tm, n_tile, LANES), lambda b, i: (blk0 + b * per_seq + i, 0, 0)),
                   pl.BlockSpec((tm, LANES), lambda b, i: (blk0 + b * per_seq + i, 0))],
        out_shape=[jax.ShapeDtypeStruct((n, l, D_MODEL), F32),
                   jax.ShapeDtypeStruct((t_all, n_tile, LANES), F32),
                   jax.ShapeDtypeStruct((t_all, LANES), F32)],
        input_output_aliases=aliases,
        compiler_params=_cparams(("parallel", "parallel")),
        name="outproj",
    )(*args)


def _moe_prep_kernel(w1_ref, w2_ref, perm_ref, w1o_ref, w2o_ref):
    for j in range(w1_ref.shape[2] // GLU_BLOCK):
        cols = slice(j * GLU_BLOCK, (j + 1) * GLU_BLOCK)
        w1o_ref[0, :, cols] = jnp.dot(w1_ref[0, :, cols].astype(BF16), perm_ref[...],
                                      preferred_element_type=F32).astype(BF16)
    w2o_ref[0] = w2_ref[0].astype(BF16)


def _moe_prep(w1, w2):
    n_e, d_in, d_hid = w1.shape
    d_exp = w2.shape[1]
    j = jnp.arange(GLU_BLOCK)
    src = jnp.where(j < LANES, 2 * j, 2 * (j - LANES) + 1)
    perm = (j[:, None] == src[None, :]).astype(BF16)
    e_map = lambda e: (e, 0, 0)
    return pl.pallas_call(
        _moe_prep_kernel,
        grid=(n_e,),
        in_specs=[pl.BlockSpec((1, d_in, d_hid), e_map),
                  pl.BlockSpec((1, d_exp, D_MODEL), e_map),
                  pl.BlockSpec((GLU_BLOCK, GLU_BLOCK), lambda e: (0, 0))],
        out_specs=[pl.BlockSpec((1, d_in, d_hid), e_map),
                   pl.BlockSpec((1, d_exp, D_MODEL), e_map)],
        out_shape=[jax.ShapeDtypeStruct(w1.shape, BF16),
                   jax.ShapeDtypeStruct(w2.shape, BF16)],
        compiler_params=_cparams(("parallel",)),
        name="moe_prep",
    )(w1, w2, perm)


def _route_plan(idx, gates, nch, tmx):
    t = idx.shape[0]
    tc = t // nch
    nt = -(-(tc * TOP_K + N_EXPERTS * (tmx - 1)) // tmx)
    i32 = jnp.int32
    hot = (idx[:, :, None] == jnp.arange(N_EXPERTS, dtype=i32)).astype(i32).sum(1)
    hot = hot.reshape(nch, tc, N_EXPERTS)
    csum = jnp.cumsum(hot, axis=1)
    rank = (csum - hot).reshape(t, N_EXPERTS)
    padded = -(-csum[:, -1, :] // tmx) * tmx
    off_end = jnp.cumsum(padded, axis=1)
    off = off_end - padded
    n_tiles = (off_end[:, -1] // tmx).astype(i32)
    chunk = jnp.arange(t, dtype=i32) // tc
    pos = jnp.take_along_axis(off[chunk] + rank, idx, axis=1).astype(i32)
    rc = nt * tmx
    row_pair = jnp.full((nch * rc,), -1, i32).at[(chunk[:, None] * rc + pos).reshape(-1)].set(
        jnp.arange(t * TOP_K, dtype=i32), unique_indices=True)
    valid = row_pair >= 0
    pair = jnp.maximum(row_pair, 0)
    row_tok = jnp.where(valid, (pair // TOP_K) % tc, 0)
    row_gate = jnp.where(valid, gates.reshape(-1)[pair], 0.0)
    tile = jnp.arange(nt, dtype=i32)
    texp_raw = (off_end[:, None, :] <= (tile * tmx)[None, :, None]).sum(-1).astype(i32)
    tile_c = jnp.minimum(tile[None, :], n_tiles[:, None] - 1)
    tile_n = jnp.minimum(tile[None, :] + 1, n_tiles[:, None] - 1)
    texp = jnp.take_along_axis(texp_raw, tile_c, axis=1)
    base = (jnp.arange(nch, dtype=i32) * nt)[:, None]
    return dict(nt=nt, n_tiles=n_tiles, texp=texp.reshape(-1), tblk=(tile_c + base).reshape(-1),
                tnext=(tile_n + base).reshape(-1),
                row_tok=row_tok.reshape(nch * nt, 1, tmx),
                row_gate=row_gate.reshape(nch * nt, tmx, 1), pos=pos)


def _moe_kernel(nt_ref, texp_ref, tblk_ref, tnext_ref, tok_ref, toknext_ref, gate_ref, src_ref,
                w1_ref, b1_ref, w2_ref, b2_ref, ys_ref, xs_ref):
    del texp_ref, tblk_ref, tnext_ref
    c = pl.program_id(0)
    i = pl.program_id(1)
    tmx = ys_ref.shape[0]
    slot = i % 2

    @pl.when(i == 0)
    def _():
        def gather(r, carry):
            xs_ref[0, pl.ds(pl.multiple_of(r * SUBLANES, SUBLANES), SUBLANES), :] = src_ref[0, tok_ref[0, 0, r]]
            return carry
        lax.fori_loop(0, tmx, gather, 0, unroll=8)

    @pl.when(i < nt_ref[c])
    def _():
        x = jnp.concatenate([xs_ref[slot, pl.ds(s, tmx, stride=SUBLANES), :]
                             for s in range(D_MODEL // LANES)], axis=-1).astype(BF16)
        n_blk = w1_ref.shape[2] // (2 * GLU_BLOCK)
        wide = 2 * GLU_BLOCK

        def hidden(j):
            cols = slice(j * wide, (j + 1) * wide)
            return jnp.dot(x, w1_ref[0, :, cols], preferred_element_type=F32) + b1_ref[0, :, cols]

        hid = hidden(0)
        y = None
        for j in range(n_blk):
            nxt = hidden(j + 1) if j + 1 < n_blk else None
            for r in range(j * tmx // n_blk, (j + 1) * tmx // n_blk):
                xs_ref[1 - slot, r * SUBLANES:(r + 1) * SUBLANES, :] = src_ref[0, toknext_ref[0, 0, r]]
            acts = []
            for b in range(2):
                glu = jnp.minimum(hid[:, b * GLU_BLOCK:b * GLU_BLOCK + LANES], SWIGLU_LIMIT)
                lin = jnp.clip(hid[:, b * GLU_BLOCK + LANES:(b + 1) * GLU_BLOCK], -SWIGLU_LIMIT, SWIGLU_LIMIT)
                acts.append(glu * _sigmoid(SWIGLU_ALPHA * glu) * (lin + 1.0))
            part = _mm(jnp.concatenate(acts, axis=-1), w2_ref[0, j * GLU_BLOCK:(j + 1) * GLU_BLOCK, :])
            y = part if y is None else y + part
            hid = nxt
        ys_ref[...] = gate_ref[0] * (y + b2_ref[0])

    @pl.when(i >= nt_ref[c])
    def _():
        ys_ref[...] = jnp.zeros(ys_ref.shape, F32)


def _moe_experts(h2, plan, w1p, b1p, w2p, b2, tmx):
    nch, tc = h2.shape[:2]
    nt = plan["nt"]
    n_e, _, d_hid = w1p.shape
    d_exp = w2p.shape[1]
    flat = lambda c, i: c * nt + i
    cur = lambda c, i, n, te, tb, tn: (tb[flat(c, i)], 0, 0)
    nxt = lambda c, i, n, te, tb, tn: (tn[flat(c, i)], 0, 0)
    exp = lambda c, i, n, te, tb, tn: (te[flat(c, i)], 0, 0)
    grid_spec = pltpu.PrefetchScalarGridSpec(
        num_scalar_prefetch=4,
        grid=(nch, nt),
        in_specs=[pl.BlockSpec((1, 1, tmx), cur, memory_space=pltpu.SMEM),
                  pl.BlockSpec((1, 1, tmx), nxt, memory_space=pltpu.SMEM),
                  pl.BlockSpec((1, tmx, 1), cur),
                  pl.BlockSpec((1, tc, D_MODEL // LANES, LANES), lambda c, i, n, te, tb, tn: (c, 0, 0, 0),
                               pipeline_mode=pl.Buffered(1)),
                  pl.BlockSpec((1, D_MODEL, d_hid), exp),
                  pl.BlockSpec((1, 1, d_hid), exp),
                  pl.BlockSpec((1, d_exp, D_MODEL), exp),
                  pl.BlockSpec((1, 1, D_MODEL), exp)],
        out_specs=pl.BlockSpec((tmx, D_MODEL), lambda c, i, n, te, tb, tn: (flat(c, i), 0)),
        scratch_shapes=[pltpu.VMEM((2, tmx * (D_MODEL // LANES), LANES), F32)],
    )
    return pl.pallas_call(
        _moe_kernel,
        grid_spec=grid_spec,
        out_shape=jax.ShapeDtypeStruct((nch * nt * tmx, D_MODEL), F32),
        compiler_params=_cparams(("parallel", "arbitrary")),
        name="moe_experts",
    )(plan["n_tiles"], plan["texp"], plan["tblk"], plan["tnext"], plan["row_tok"], plan["row_tok"],
      plan["row_gate"], h2, w1p, b1p, w2p, b2)


def _combine_kernel(pos_ref, ys_ref, o_ref):
    def body(t, carry):
        acc = ys_ref[pl.ds(pos_ref[0, 0, TOP_K * t], 1), :]
        for k in range(1, TOP_K):
            acc = acc + ys_ref[pl.ds(pos_ref[0, 0, TOP_K * t + k], 1), :]
        o_ref[pl.ds(t, 1), :] = acc
        return carry
    lax.fori_loop(0, o_ref.shape[0], body, 0, unroll=8)


def _moe_combine(ys, pos, nch, rc, ts, n_col):
    t = pos.shape[0]
    nsub = t // (nch * ts)
    cw = D_MODEL // n_col
    pos3 = pos.reshape(nch * nsub, 1, ts * TOP_K)
    return pl.pallas_call(
        _combine_kernel,
        grid=(nch, n_col, nsub),
        in_specs=[pl.BlockSpec((1, 1, ts * TOP_K), lambda c, q, s: (c * nsub + s, 0, 0),
                               memory_space=pltpu.SMEM),
                  pl.BlockSpec((rc, cw), lambda c, q, s: (c, q), pipeline_mode=pl.Buffered(1))],
        out_specs=pl.BlockSpec((ts, cw), lambda c, q, s: (c * nsub + s, q)),
        out_shape=jax.ShapeDtypeStruct((t, D_MODEL), F32),
        compiler_params=_cparams(("parallel", "parallel", "arbitrary")),
        name="moe_combine",
    )(pos3, ys)


def _final_kernel(x1_ref, f_ref, g2_ref, gpost_ref, y_ref):
    y_ref[0] = x1_ref[0] + g2_ref[0] * _rms(f_ref[...], gpost_ref[...], D_MODEL)


def _final(x1, f_all, row0, g2, g_post_ffn, tm):
    n, l, _ = x1.shape
    per_row = g2.shape[1] != 1
    mod_blk = (1, tm, D_MODEL) if per_row else (1, 1, D_MODEL)
    mod_map = (lambda b, i: (b, i, 0)) if per_row else (lambda b, i: (b, 0, 0))
    row_map = lambda b, i: (b, i, 0)
    blk0, per_seq = row0 // tm, l // tm
    return pl.pallas_call(
        _final_kernel,
        grid=(n, l // tm),
        in_specs=[pl.BlockSpec((1, tm, D_MODEL), row_map),
                  pl.BlockSpec((tm, D_MODEL), lambda b, i: (blk0 + b * per_seq + i, 0)),
                  pl.BlockSpec(mod_blk, mod_map),
                  pl.BlockSpec((1, D_MODEL), lambda b, i: (0, 0))],
        out_specs=pl.BlockSpec((1, tm, D_MODEL), row_map),
        out_shape=jax.ShapeDtypeStruct((n, l, D_MODEL), F32),
        compiler_params=_cparams(("parallel", "parallel")),
        name="final",
    )(x1, f_all, g2, g_post_ffn)


def _complex_pow2(lam, n):
    re, im = lam[0], lam[1]
    steps = int(round(math.log2(n)))
    assert 2 ** steps == n
    for _ in range(steps):
        re, im = re * re - im * im, 2.0 * re * im
    return jnp.stack([re, im])


def _rope_tables(pos):
    half = ROPE_DIM // 2
    inv = ROPE_THETA ** (-jnp.arange(half, dtype=F32) / half)
    ang = pos.astype(F32)[:, None] * inv[None, :]
    cos, sin = jnp.cos(ang), jnp.sin(ang)
    cos4 = jnp.tile(jnp.concatenate([cos, cos], axis=-1), (1, LANES // ROPE_DIM))
    sin4 = jnp.tile(jnp.concatenate([-sin, sin], axis=-1), (1, LANES // ROPE_DIM))
    return cos4[None], sin4[None]


def _swap_halves(w):
    half = ROPE_DIM // 2
    return jnp.concatenate([w[..., half:], w[..., :half]], axis=-1)


def _prep_layer(w, layer):
    g = lambda name: w[name][layer]
    p = {}
    row = lambda v: v.reshape(1, -1)
    p["g_pre_mix"] = row(g("g_pre_mix"))
    w_in = g("w_in")
    w_q, w_kv, w_r, w_u = jnp.split(w_in, [Q_RANK, Q_RANK + KV_RANK, Q_RANK + KV_RANK + ROPE_DIM], axis=-1)
    rep = LANES // ROPE_DIM
    p["w_in"] = jnp.concatenate(
        [w_q, jnp.zeros((D_MODEL, Q_PAD - Q_RANK), F32), w_kv,
         jnp.tile(w_r, (1, rep)), jnp.tile(_swap_halves(w_r), (1, rep)), w_u], axis=-1)
    p["g_q"] = row(jnp.pad(g("g_q"), (0, Q_PAD - Q_RANK)))
    w_uq = g("w_uq").reshape(Q_RANK, N_HEADS, NOPE_DIM + ROPE_DIM)
    nope = w_uq[:, :, :NOPE_DIM].reshape(Q_RANK, N_HEADS * NOPE_DIM)
    rope = w_uq[:, :, NOPE_DIM:]
    rope_sw = _swap_halves(rope)
    w_uq_p = jnp.concatenate([nope, rope.reshape(Q_RANK, -1), rope_sw.reshape(Q_RANK, -1)], axis=-1)
    p["w_uq"] = jnp.pad(w_uq_p, ((0, Q_PAD - Q_RANK), (0, 0)))
    w_uk = jnp.transpose(g("w_uk"), (1, 2, 0))
    z = jnp.zeros((NOPE_DIM, KV_RANK), F32)
    p["w_uk"] = jnp.stack([jnp.block([[w_uk[2 * j], z], [z, w_uk[2 * j + 1]]])
                           for j in range(N_HEADS // 2)])
    p["g_kv"] = row(g("g_kv"))
    w_uv = jnp.transpose(g("w_uv"), (1, 0, 2))
    p["w_uv_t"] = jnp.transpose(w_uv, (0, 2, 1))
    eye_h = jnp.eye(N_HEADS, dtype=F32)
    p["w_uv_bd"] = jnp.einsum("hrv,hk->hrkv", w_uv, eye_h).reshape(N_HEADS * KV_RANK, ATTN_WIDTH)
    p["w_out"] = g("w_out")
    p["g_post_mix"] = row(g("g_post_mix"))
    p["g_pre_ffn"] = row(g("g_pre_ffn"))
    p["w_router"] = jnp.pad(g("w_router"), ((0, 0), (0, LANES - N_EXPERTS)))
    p["b_router"] = row(jnp.pad(g("b_router"), (0, LANES - N_EXPERTS), constant_values=NEG_INF))
    p["g_post_ffn"] = row(g("g_post_ffn"))

    lam_re, lam_im = g("ssm_lam_re"), g("ssm_lam_im")
    step = jnp.exp(g("ssm_log_step"))[:, None]
    mag = jnp.exp(lam_re * step)
    bar_re, bar_im = mag * jnp.cos(lam_im * step), mag * jnp.sin(lam_im * step)
    den = lam_re * lam_re + lam_im * lam_im
    co_re = ((bar_re - 1.0) * lam_re + bar_im * lam_im) / den
    co_im = (bar_im * lam_re - (bar_re - 1.0) * lam_im) / den
    b_re, b_im = g("ssm_b_re"), g("ssm_b_im")
    bb_re = co_re[..., None] * b_re - co_im[..., None] * b_im
    bb_im = co_re[..., None] * b_im + co_im[..., None] * b_re
    eye_g = jnp.eye(GROUPS_PER_BLOCK, dtype=F32)

    def b_block(m):
        m = m.reshape(SSM_BLOCKS, GROUPS_PER_BLOCK, SSM_STATE, SSM_GROUP)
        return jnp.einsum("bgpc,gh->bgchp", m, eye_g).reshape(SSM_BLOCKS, LANES, BLOCK_STATES)

    def c_block(m):
        m = m.reshape(SSM_BLOCKS, GROUPS_PER_BLOCK, SSM_GROUP, SSM_STATE)
        return jnp.einsum("bgcp,gh->bgphc", m, eye_g).reshape(SSM_BLOCKS, BLOCK_STATES, LANES)

    sp = {
        "lam": jnp.stack([bar_re.reshape(-1), bar_im.reshape(-1)]),
        "bmat": jnp.concatenate([b_block(bb_re), b_block(bb_im)], axis=-1).astype(BF16),
        "cmat": jnp.concatenate([c_block(g("ssm_c_re")), c_block(-g("ssm_c_im"))], axis=1).astype(BF16),
        "d": row(g("ssm_d")),
        "w_glu": g("ssm_w_glu"),
        "b_glu": row(g("ssm_b_glu")),
    }
    b1 = g("b1")
    n_blk = b1.shape[1] // GLU_BLOCK
    mp = {
        "b1": jnp.transpose(b1.reshape(N_EXPERTS, n_blk, LANES, 2), (0, 1, 3, 2)).reshape(N_EXPERTS, 1, -1),
        "b2": g("b2")[:, None, :],
    }
    return p, sp, mp


def _mods(mod_rows):
    return [mod_rows[:, k * D_MODEL:(k + 1) * D_MODEL] for k in range(N_MOD)]


def _pick_tile(n, target, mult):
    best = mult
    for c in range(mult, min(n, target) + 1, mult):
        if n % c == 0:
            best = c
    return best


def kernel(x_prompt, x_sample, c_prompt, c_sample, cache_kv_latent, cache_k_rope, state_ssm, page_table, w_ada, b_ada, g_pre_mix, w_in, g_q, w_uq, w_uk, g_kv, w_uv, ssm_lam_re, ssm_lam_im, ssm_log_step, ssm_b_re, ssm_b_im, ssm_c_re, ssm_c_im, ssm_d, ssm_w_glu, ssm_b_glu, w_out, g_post_mix, g_pre_ffn, w_router, b_router, w1, b1, w2, b2, g_post_ffn):
    weights = dict(w_ada=w_ada, b_ada=b_ada, g_pre_mix=g_pre_mix, w_in=w_in, g_q=g_q, w_uq=w_uq,
                   w_uk=w_uk, g_kv=g_kv, w_uv=w_uv, ssm_lam_re=ssm_lam_re, ssm_lam_im=ssm_lam_im,
                   ssm_log_step=ssm_log_step, ssm_b_re=ssm_b_re, ssm_b_im=ssm_b_im,
                   ssm_c_re=ssm_c_re, ssm_c_im=ssm_c_im, ssm_d=ssm_d, ssm_w_glu=ssm_w_glu,
                   ssm_b_glu=ssm_b_glu, w_out=w_out, g_post_mix=g_post_mix, g_pre_ffn=g_pre_ffn,
                   w_router=w_router, b_router=b_router, w1=w1, b1=b1, w2=w2, b2=b2,
                   g_post_ffn=g_post_ffn)
    depth = w_ada.shape[0]
    n_p, l_p, _ = x_prompt.shape
    n_s, l_s, _ = x_sample.shape
    assert l_s == 1
    n_pages = page_table.shape[1]
    past_len = n_pages * PAGE_SIZE

    cos_p, sin_p = _rope_tables(jnp.arange(l_p, dtype=jnp.int32))
    cos_s, sin_s = _rope_tables(jnp.full((1,), past_len, jnp.int32))
    cos_s = jnp.broadcast_to(cos_s, (1, n_s, LANES))
    sin_s = jnp.broadcast_to(sin_s, (1, n_s, LANES))

    tm_p = _pick_tile(l_p, 512, 16)
    tq = _pick_tile(l_p, 256, 16)
    tk = _pick_tile(l_p, 1024, 16)
    lc = 64 if l_p % 512 == 0 else l_p // SUBLANES
    c_rows = n_p + n_s
    c_pad = -c_rows % SUBLANES
    c_all = jnp.pad(jnp.concatenate([c_prompt, c_sample], axis=0), ((0, c_pad), (0, 0)))

    y_p = x_prompt
    y_s = x_sample.reshape(1, n_s, D_MODEL)
    outs = [[] for _ in range(6)]
    for layer in range(depth):
        p, sp, mp = _prep_layer(weights, layer)
        mod = _ada(c_all, w_ada[layer], b_ada[layer].reshape(1, -1))
        sh1p, sc1p, g1p, sh2p, sc2p, g2p = [m[:, None, :] for m in _mods(mod[:n_p])]
        sh1s, sc1s, g1s, sh2s, sc2s, g2s = [m[None] for m in _mods(mod[n_p:c_rows])]

        q_t, k_cat, v_t, kv_p, kr_p, u_p = _inproj(y_p, sc1p, sh1p, p, cos_p, sin_p, tm_p)
        o_attn_p = _flash(q_t, k_cat, v_t, p["w_uv_t"], tq, tk)
        y_ssm_p, h_last_p = _s5_prompt(u_p, sp, lc)
        t_p = n_p * l_p
        t_all = t_p + n_s
        blank = (jnp.zeros((t_all, D_MODEL // LANES, LANES), F32), jnp.zeros((t_all, LANES), F32))
        x1_p, h2_all, route = _outproj(o_attn_p, y_ssm_p, y_p, g1p, sc2p, sh2p, p, tm_p, t_all, 0,
                                       dest=blank)

        q_t_s, _, _, kv_s, kr_s, u_s = _inproj(y_s, sc1s, sh1s, p, cos_s, sin_s, n_s)
        q_lat_s = jnp.transpose(q_t_s[0, :, :KV_RANK, :], (2, 0, 1))
        q_rope_s = jnp.stack(
            [q_t_s[0, h, KV_RANK + ROPE_DIM * (h % 4):KV_RANK + ROPE_DIM * (h % 4 + 1), :].T
             for h in range(N_HEADS)], axis=1)
        o_lat_s = _decode_attn(q_lat_s, q_rope_s, kv_s.reshape(n_s, 1, KV_RANK),
                               kr_s.reshape(n_s, 1, ROPE_DIM), cache_kv_latent[layer],
                               jnp.swapaxes(cache_k_rope[layer], 1, 2), page_table)
        o_attn_s = _uv_proj(o_lat_s.reshape(n_s, N_HEADS * KV_RANK), p["w_uv_bd"])
        h0 = jnp.moveaxis(state_ssm[layer].reshape(n_s, N_STATE, 2), -1, 0)
        y_ssm_s, h_s = _s5_step(u_s[0], h0, sp)
        x1_s, h2_all, route = _outproj(o_attn_s[None], y_ssm_s[None], y_s, g1s, sc2s, sh2s, p, n_s,
                                       t_all, t_p, dest=(h2_all, route))

        nch = 4 if t_all % (4 * SUBLANES) == 0 else (2 if t_all % (2 * SUBLANES) == 0 else 1)
        tc = t_all // nch
        plan = _route_plan(route[:, :TOP_K].astype(jnp.int32), route[:, TOP_K:2 * TOP_K], nch, MOE_TILE)
        w1p, w2p = _moe_prep(w1[layer], w2[layer])
        ys = _moe_experts(h2_all.reshape(nch, tc, D_MODEL // LANES, LANES), plan, w1p, mp["b1"], w2p,
                          mp["b2"], MOE_TILE)
        f_all = _moe_combine(ys, plan["pos"], nch, plan["nt"] * MOE_TILE,
                             _pick_tile(tc, 1100, SUBLANES), MOE_COL_PARTS)
        y_p = _final(x1_p, f_all, 0, g2p, p["g_post_ffn"], tm_p)
        y_s = _final(x1_s, f_all, t_p, g2s, p["g_post_ffn"], n_s)

        outs[0].append(kv_p)
        outs[1].append(kr_p)
        outs[2].append(jnp.moveaxis(h_last_p, 1, -1).reshape(n_p, N_SSM_GROUPS, SSM_STATE, 2))
        outs[3].append(kv_s.reshape(n_s, 1, KV_RANK))
        outs[4].append(kr_s.reshape(n_s, 1, ROPE_DIM))
        outs[5].append(jnp.moveaxis(h_s, 0, -1).reshape(n_s, N_SSM_GROUPS, SSM_STATE, 2))
    return (y_p, y_s.reshape(n_s, 1, D_MODEL)) + tuple(jnp.stack(o) for o in outs)
```

```python
import functools
import math

import jax
import jax.numpy as jnp
from jax import lax
from jax.experimental import pallas as pl
from jax.experimental.pallas import tpu as pltpu

F32 = jnp.float32
BF16 = jnp.bfloat16

D_MODEL = 1024
N_HEADS = 8
NOPE_DIM = 64
ROPE_DIM = 32
V_DIM = 64
Q_RANK = 352
KV_RANK = 128
ROPE_THETA = 10000.0
ATTN_WIDTH = N_HEADS * V_DIM
ATTN_SCALE = 1.0 / math.sqrt(NOPE_DIM + ROPE_DIM)
SSM_WIDTH = D_MODEL - ATTN_WIDTH
SSM_GROUP = 16
N_SSM_GROUPS = SSM_WIDTH // SSM_GROUP
SSM_STATE = 64
N_STATE = N_SSM_GROUPS * SSM_STATE
N_EXPERTS = 32
TOP_K = 4
SWIGLU_LIMIT = 7.0
SWIGLU_ALPHA = 1.702
RMS_EPS = 1e-6
N_MOD = 6
NEG_INF = -1e30
PAGE_SIZE = 128

LANES = 128
SUBLANES = 8
VMEM_LIMIT = 56 * 1024 * 1024

Q_PAD = 384
CAT = 2 * LANES
SSM_BLOCKS = 4
GROUPS_PER_BLOCK = N_SSM_GROUPS // SSM_BLOCKS
BLOCK_STATES = GROUPS_PER_BLOCK * SSM_STATE
IN_PAD = Q_PAD + KV_RANK + 2 * LANES + SSM_WIDTH
UQ_PAD = N_HEADS * NOPE_DIM + 4 * LANES
GLU_BLOCK = 2 * LANES
MOE_TILE = 192
MOE_COL_PARTS = 2
LOG2E = math.log2(math.e)
FLASH_LOOKAHEAD = 3
FLASH_DIAG_SUB = 256
VT_ROWS = KV_RANK + 16


def _cparams(sem):
    return pltpu.CompilerParams(dimension_semantics=sem, vmem_limit_bytes=VMEM_LIMIT)


def _mm(a, b):
    return jnp.dot(a.astype(BF16), b.astype(BF16), preferred_element_type=F32)


def _mm_nt(a, b):
    return lax.dot_general(a.astype(BF16), b.astype(BF16), (((1,), (1,)), ((), ())),
                           preferred_element_type=F32)


def _split_bf16(x):
    hi = x.astype(BF16)
    lo = (x - hi.astype(F32)).astype(BF16)
    return hi, lo


def _mm3(a, b):
    ah, al = _split_bf16(a)
    bh, bl = _split_bf16(b)
    d = functools.partial(jnp.dot, preferred_element_type=F32)
    return d(ah, bh) + (d(ah, bl) + d(al, bh))


def _sigmoid(x):
    return 1.0 / (1.0 + jnp.exp(-x))


def _rms(x, g, width):
    ms = jnp.sum(x * x, axis=-1, keepdims=True) * (1.0 / width)
    return x * lax.rsqrt(ms + RMS_EPS) * g


def _gelu_tanh(x):
    return 0.5 * x * (1.0 + jnp.tanh(math.sqrt(2.0 / math.pi) * (x + 0.044715 * (x * x * x))))


def _ada_kernel(c_ref, w_ref, b_ref, o_ref):
    c = c_ref[...]
    o_ref[...] = _mm3(c * _sigmoid(c), w_ref[...]) + b_ref[...]


def _ada(c_all, w_ada, b_ada):
    rows = c_all.shape[0]
    tn = 512
    n_out = w_ada.shape[1]
    return pl.pallas_call(
        _ada_kernel,
        grid=(n_out // tn,),
        in_specs=[pl.BlockSpec((rows, D_MODEL), lambda j: (0, 0)),
                  pl.BlockSpec((D_MODEL, tn), lambda j: (0, j)),
                  pl.BlockSpec((1, tn), lambda j: (0, j))],
        out_specs=pl.BlockSpec((rows, tn), lambda j: (0, j)),
        out_shape=jax.ShapeDtypeStruct((rows, n_out), F32),
        compiler_params=_cparams(("parallel",)),
        name="adaln",
    )(c_all, w_ada, b_ada)


def _inproj_kernel(x_ref, sc_ref, sh_ref, gpre_ref, win_ref, gq_ref, wuq_ref, wuk_ref, gkv_ref,
                   cos_ref, sin_ref, q_ref, kcat_ref, vt_ref, kv_ref, kr_ref, u_ref):
    x = x_ref[0]
    h = _rms(x, gpre_ref[...], D_MODEL) * (1.0 + sc_ref[0]) + sh_ref[0]
    z = _mm(h, win_ref[...])
    zq = z[:, :Q_PAD]
    zkv = z[:, Q_PAD:Q_PAD + KV_RANK]
    zr = z[:, Q_PAD + KV_RANK:Q_PAD + KV_RANK + LANES]
    zrs = z[:, Q_PAD + KV_RANK + LANES:Q_PAD + KV_RANK + 2 * LANES]
    u = z[:, Q_PAD + KV_RANK + 2 * LANES:]
    cos4 = cos_ref[0]
    sin4 = sin_ref[0]

    kvl = _rms(zkv, gkv_ref[...], KV_RANK)
    kr4 = zr * cos4 + zrs * sin4
    kv_ref[0] = kvl
    kr_ref[0] = kr4[:, :ROPE_DIM]
    kcat_ref[0] = jnp.concatenate([kvl, kr4], axis=-1).astype(BF16)
    extra = (lax.broadcasted_iota(jnp.int32, (VT_ROWS - KV_RANK, kvl.shape[0]), 0) == 0).astype(F32)
    vt_ref[0] = jnp.concatenate([kvl.T, extra], axis=0).astype(BF16)
    u_ref[0] = u.astype(BF16)

    qn = _rms(zq, gq_ref[...], Q_RANK)
    q = _mm(qn, wuq_ref[...])
    nope_w = N_HEADS * NOPE_DIM
    rope_a = q[:, nope_w:nope_w + LANES] * cos4 + q[:, nope_w + 2 * LANES:nope_w + 3 * LANES] * sin4
    rope_b = q[:, nope_w + LANES:nope_w + 2 * LANES] * cos4 + q[:, nope_w + 3 * LANES:] * sin4
    slot = lax.broadcasted_iota(jnp.int32, (1, LANES), 1) // ROPE_DIM
    for j in range(N_HEADS // 2):
        ql = _mm(q[:, j * LANES:(j + 1) * LANES], wuk_ref[j])
        for hh in range(2):
            head = 2 * j + hh
            rope = rope_a if head < 4 else rope_b
            rope_h = jnp.where(slot == (head % 4), rope, 0.0)
            qc = jnp.concatenate([ql[:, hh * KV_RANK:(hh + 1) * KV_RANK], rope_h], axis=-1)
            q_ref[0, head] = (qc * (ATTN_SCALE * LOG2E)).T.astype(BF16)


def _inproj(x, sc, sh, p, cos4, sin4, tm):
    n, l, _ = x.shape
    per_row = sc.shape[1] != 1
    mod_blk = (1, tm, D_MODEL) if per_row else (1, 1, D_MODEL)
    mod_map = (lambda b, i: (b, i, 0)) if per_row else (lambda b, i: (b, 0, 0))
    const2 = lambda b, i: (0, 0)
    const3 = lambda b, i: (0, 0, 0)
    pos_map = lambda b, i: (0, i, 0)
    return pl.pallas_call(
        _inproj_kernel,
        grid=(n, l // tm),
        in_specs=[pl.BlockSpec((1, tm, D_MODEL), lambda b, i: (b, i, 0)),
                  pl.BlockSpec(mod_blk, mod_map),
                  pl.BlockSpec(mod_blk, mod_map),
                  pl.BlockSpec((1, D_MODEL), const2),
                  pl.BlockSpec((D_MODEL, IN_PAD), const2),
                  pl.BlockSpec((1, Q_PAD), const2),
                  pl.BlockSpec((Q_PAD, UQ_PAD), const2),
                  pl.BlockSpec((N_HEADS // 2, LANES, 2 * KV_RANK), const3),
                  pl.BlockSpec((1, KV_RANK), const2),
                  pl.BlockSpec((1, tm, LANES), pos_map),
                  pl.BlockSpec((1, tm, LANES), pos_map)],
        out_specs=[pl.BlockSpec((1, N_HEADS, CAT, tm), lambda b, i: (b, 0, 0, i)),
                   pl.BlockSpec((1, tm, CAT), lambda b, i: (b, i, 0)),
                   pl.BlockSpec((1, VT_ROWS, tm), lambda b, i: (b, 0, i)),
                   pl.BlockSpec((1, tm, KV_RANK), lambda b, i: (b, i, 0)),
                   pl.BlockSpec((1, tm, ROPE_DIM), lambda b, i: (b, i, 0)),
                   pl.BlockSpec((1, tm, SSM_WIDTH), lambda b, i: (b, i, 0))],
        out_shape=[jax.ShapeDtypeStruct((n, N_HEADS, CAT, l), BF16),
                   jax.ShapeDtypeStruct((n, l, CAT), BF16),
                   jax.ShapeDtypeStruct((n, VT_ROWS, l), BF16),
                   jax.ShapeDtypeStruct((n, l, KV_RANK), F32),
                   jax.ShapeDtypeStruct((n, l, ROPE_DIM), F32),
                   jax.ShapeDtypeStruct((n, l, SSM_WIDTH), BF16)],
        compiler_params=_cparams(("parallel", "parallel")),
        name="inproj",
    )(x, sc, sh, p["g_pre_mix"], p["w_in"], p["g_q"], p["w_uq"], p["w_uk"], p["g_kv"], cos4, sin4)


def _flash_kernel(qt_ref, k_ref, vt_ref, wuvt_ref, o_ref, m_ref, acc_ref, *, tq, tk):
    i = pl.program_id(1)
    j = pl.program_id(2)
    j_last = ((i + 1) * tq - 1) // tk

    @pl.when(j == 0)
    def _():
        m_ref[...] = jnp.full(m_ref.shape, NEG_INF, F32)
        acc_ref[...] = jnp.zeros(acc_ref.shape, F32)

    def update(masked, key0, size):
        k = k_ref[0, pl.ds(key0, size), :]
        vt = vt_ref[0, :, pl.ds(key0, size)]
        if masked:
            k_pos = j * tk + key0 + lax.broadcasted_iota(jnp.int32, (size, tq), 0)
            q_pos = i * tq + lax.broadcasted_iota(jnp.int32, (size, tq), 1)
            keep = k_pos <= q_pos
        m_all = m_ref[...]
        m_rows = []
        scores = lambda h: jnp.dot(k, qt_ref[0, h], preferred_element_type=F32)
        ahead = [scores(h) for h in range(min(FLASH_LOOKAHEAD, N_HEADS))]
        pending = None
        for h in range(N_HEADS):
            s = ahead.pop(0)
            if h + FLASH_LOOKAHEAD < N_HEADS:
                ahead.append(scores(h + FLASH_LOOKAHEAD))
            if masked:
                s = jnp.where(keep, s, NEG_INF)
            m_prev = m_all[h:h + 1]
            m_new = jnp.maximum(m_prev, jnp.max(s, axis=0, keepdims=True))
            alpha = jnp.exp2(m_prev - m_new)
            p = jnp.exp2(s - m_new)
            pv = jnp.dot(vt, p.astype(BF16), preferred_element_type=F32)
            if pending is not None:
                acc_ref[pending[0]] = pending[1] * acc_ref[pending[0]] + pending[2]
            pending = (h, alpha, pv)
            m_rows.append(m_new)
        acc_ref[pending[0]] = pending[1] * acc_ref[pending[0]] + pending[2]
        m_ref[...] = jnp.concatenate(m_rows, axis=0)

    crosses = (j + 1) * tk - 1 > i * tq
    sub = min(tk, FLASH_DIAG_SUB)

    @pl.when(jnp.logical_and(j <= j_last, crosses))
    def _():
        n_sub = jnp.minimum(tk // sub, ((i + 1) * tq - 1 - j * tk) // sub + 1)

        def body(sb, carry):
            update(True, pl.multiple_of(sb * sub, sub), sub)
            return carry
        lax.fori_loop(0, n_sub, body, 0)

    @pl.when(jnp.logical_and(j <= j_last, jnp.logical_not(crosses)))
    def _():
        update(False, 0, tk)

    @pl.when(j == j_last)
    def _():
        outs = []
        for h in range(N_HEADS):
            o_t = acc_ref[h, :KV_RANK, :] / acc_ref[h, KV_RANK:KV_RANK + 1, :]
            outs.append(_mm(wuvt_ref[h], o_t))
        o_ref[0] = jnp.concatenate(outs, axis=0).T.astype(BF16)


def _flash(q_t, k_cat, v_t, wuv_t, tq, tk):
    n, _, _, l = q_t.shape
    kern = functools.partial(_flash_kernel, tq=tq, tk=tk)
    last = lambda i: ((i + 1) * tq - 1) // tk
    return pl.pallas_call(
        kern,
        grid=(n, l // tq, l // tk),
        in_specs=[pl.BlockSpec((1, N_HEADS, CAT, tq), lambda b, i, j: (b, 0, 0, i)),
                  pl.BlockSpec((1, tk, CAT), lambda b, i, j: (b, jnp.minimum(j, last(i)), 0)),
                  pl.BlockSpec((1, VT_ROWS, tk), lambda b, i, j: (b, 0, jnp.minimum(j, last(i)))),
                  pl.BlockSpec((N_HEADS, V_DIM, KV_RANK), lambda b, i, j: (0, 0, 0))],
        out_specs=pl.BlockSpec((1, tq, ATTN_WIDTH), lambda b, i, j: (b, i, 0)),
        out_shape=jax.ShapeDtypeStruct((n, l, ATTN_WIDTH), BF16),
        scratch_shapes=[pltpu.VMEM((N_HEADS, tq), F32),
                        pltpu.VMEM((N_HEADS, VT_ROWS, tq), F32)],
        compiler_params=_cparams(("parallel", "parallel", "arbitrary")),
        name="flash",
    )(q_t, k_cat, v_t, wuv_t)


def _ssm_out(y, up, d_ref, wglu_ref, bglu_ref):
    y = _gelu_tanh(y + d_ref[...] * up)
    return y * _sigmoid(_mm(y, wglu_ref[...]) + bglu_ref[...])


def _s5_kernel(u_ref, perm_ref, permt_ref, bmat_ref, cmat_ref, lam_ref, laml_ref, d_ref, wglu_ref,
               bglu_ref, y_ref, hlast_ref, buf_ref, carry_ref, *, lc):
    j = pl.program_id(1)

    @pl.when(j == 0)
    def _():
        carry_ref[...] = jnp.zeros(carry_ref.shape, F32)

    u = u_ref[0]
    up = jnp.dot(perm_ref[...], u, preferred_element_type=F32).astype(BF16)
    blk_w = 2 * BLOCK_STATES
    for b in range(SSM_BLOCKS):
        buf_ref[:, b * blk_w:(b + 1) * blk_w] = jnp.dot(
            up[:, b * LANES:(b + 1) * LANES], bmat_ref[b], preferred_element_type=F32)

    for b in range(SSM_BLOCKS):
        st = slice(b * BLOCK_STATES, (b + 1) * BLOCK_STATES)
        re_cols = slice(b * blk_w, b * blk_w + BLOCK_STATES)
        im_cols = slice(b * blk_w + BLOCK_STATES, (b + 1) * blk_w)
        lr = lam_ref[0:1, st]
        li = lam_ref[1:2, st]

        def run(init, store):
            def body(t, carry):
                re, im = carry
                rows = pl.ds(pl.multiple_of(t * SUBLANES, SUBLANES), SUBLANES)
                n_re = lr * re - li * im + buf_ref[rows, re_cols]
                n_im = lr * im + li * re + buf_ref[rows, im_cols]
                if store:
                    buf_ref[rows, re_cols] = n_re
                    buf_ref[rows, im_cols] = n_im
                return n_re, n_im
            return lax.fori_loop(0, lc, body, init)

        zero = jnp.zeros((SUBLANES, BLOCK_STATES), F32)
        end_re, end_im = run((zero, zero), False)
        ll_r = laml_ref[0:1, st]
        ll_i = laml_ref[1:2, st]
        c_re = carry_ref[0:1, st]
        c_im = carry_ref[1:2, st]
        init_re, init_im = [], []
        for k in range(SUBLANES):
            init_re.append(c_re)
            init_im.append(c_im)
            c_re, c_im = (ll_r * c_re - ll_i * c_im + end_re[k:k + 1],
                          ll_r * c_im + ll_i * c_re + end_im[k:k + 1])
        carry_ref[0:1, st] = c_re
        carry_ref[1:2, st] = c_im
        run((jnp.concatenate(init_re, axis=0), jnp.concatenate(init_im, axis=0)), True)

    hlast_ref[0] = carry_ref[...]
    ys = [jnp.dot(buf_ref[:, b * blk_w:(b + 1) * blk_w].astype(BF16), cmat_ref[b],
                  preferred_element_type=F32) for b in range(SSM_BLOCKS)]
    y = _ssm_out(jnp.concatenate(ys, axis=-1), up.astype(F32), d_ref, wglu_ref, bglu_ref)
    y_ref[0] = jnp.dot(permt_ref[...], y.astype(BF16), preferred_element_type=F32).astype(BF16)


def _s5_prompt(u, sp, lc):
    n, l, _ = u.shape
    s = SUBLANES * lc
    rows = jnp.arange(s)
    src = (rows % SUBLANES) * lc + rows // SUBLANES
    perm = jnp.zeros((s, s), BF16).at[rows, src].set(1.0)
    laml = _complex_pow2(sp["lam"], lc)
    kern = functools.partial(_s5_kernel, lc=lc)
    c2 = lambda b, j: (0, 0)
    c3 = lambda b, j: (0, 0, 0)
    return pl.pallas_call(
        kern,
        grid=(n, l // s),
        in_specs=[pl.BlockSpec((1, s, SSM_WIDTH), lambda b, j: (b, j, 0)),
                  pl.BlockSpec((s, s), c2),
                  pl.BlockSpec((s, s), c2),
                  pl.BlockSpec((SSM_BLOCKS, LANES, 2 * BLOCK_STATES), c3),
                  pl.BlockSpec((SSM_BLOCKS, 2 * BLOCK_STATES, LANES), c3),
                  pl.BlockSpec((2, N_STATE), c2),
                  pl.BlockSpec((2, N_STATE), c2),
                  pl.BlockSpec((1, SSM_WIDTH), c2),
                  pl.BlockSpec((SSM_WIDTH, SSM_WIDTH), c2),
                  pl.BlockSpec((1, SSM_WIDTH), c2)],
        out_specs=[pl.BlockSpec((1, s, SSM_WIDTH), lambda b, j: (b, j, 0)),
                   pl.BlockSpec((1, 2, N_STATE), lambda b, j: (b, 0, 0))],
        out_shape=[jax.ShapeDtypeStruct((n, l, SSM_WIDTH), BF16),
                   jax.ShapeDtypeStruct((n, 2, N_STATE), F32)],
        scratch_shapes=[pltpu.VMEM((s, 2 * N_STATE), F32),
                        pltpu.VMEM((2, N_STATE), F32)],
        compiler_params=_cparams(("parallel", "arbitrary")),
        name="s5_scan",
    )(u, perm, perm.T, sp["bmat"], sp["cmat"], sp["lam"], laml, sp["d"], sp["w_glu"], sp["b_glu"])


def _s5_step_kernel(u_ref, h0_ref, bmat_ref, cmat_ref, lam_ref, d_ref, wglu_ref, bglu_ref,
                    y_ref, h_ref):
    u = u_ref[...]
    ys = []
    for b in range(SSM_BLOCKS):
        st = slice(b * BLOCK_STATES, (b + 1) * BLOCK_STATES)
        bu = jnp.dot(u[:, b * LANES:(b + 1) * LANES], bmat_ref[b], preferred_element_type=F32)
        lr = lam_ref[0:1, st]
        li = lam_ref[1:2, st]
        re = h0_ref[0, :, st]
        im = h0_ref[1, :, st]
        n_re = lr * re - li * im + bu[:, :BLOCK_STATES]
        n_im = lr * im + li * re + bu[:, BLOCK_STATES:]
        h_ref[0, :, st] = n_re
        h_ref[1, :, st] = n_im
        ys.append(_mm(jnp.concatenate([n_re, n_im], axis=-1), cmat_ref[b]))
    y = _ssm_out(jnp.concatenate(ys, axis=-1), u.astype(F32), d_ref, wglu_ref, bglu_ref)
    y_ref[...] = y.astype(BF16)


def _s5_step(u, h0, sp):
    rows = u.shape[0]
    return pl.pallas_call(
        _s5_step_kernel,
        out_shape=[jax.ShapeDtypeStruct((rows, SSM_WIDTH), BF16),
                   jax.ShapeDtypeStruct((2, rows, N_STATE), F32)],
        compiler_params=pltpu.CompilerParams(vmem_limit_bytes=VMEM_LIMIT),
        name="s5_step",
    )(u, h0, sp["bmat"], sp["cmat"], sp["lam"], sp["d"], sp["w_glu"], sp["b_glu"])


def _decode_kernel(pt_ref, ql_ref, qr_ref, kvs_ref, krs_ref, lat_hbm, rope_hbm, o_ref,
                   lat_buf, rope_buf, sems, *, n_pages):
    b = pl.program_id(0)
    slot = b % 2

    def page_copies(seq, dst_slot):
        out = []
        for i in range(n_pages):
            page = pt_ref[seq * n_pages + i]
            keys = pl.ds(i * PAGE_SIZE, PAGE_SIZE)
            out.append(pltpu.make_async_copy(lat_hbm.at[page], lat_buf.at[dst_slot, keys, :],
                                             sems.at[dst_slot, 0]))
            out.append(pltpu.make_async_copy(rope_hbm.at[page], rope_buf.at[dst_slot, :, keys],
                                             sems.at[dst_slot, 1]))
        return out

    @pl.when(b == 0)
    def _():
        for cp in page_copies(0, 0):
            cp.start()

    @pl.when(b + 1 < pl.num_programs(0))
    def _():
        for cp in page_copies(b + 1, 1 - slot):
            cp.start()

    for cp in page_copies(b, slot):
        cp.wait()

    ql = ql_ref[0]
    qr = qr_ref[0]
    kvs = kvs_ref[0]
    krs = krs_ref[0]
    lat = lat_buf[slot].astype(BF16)
    s = _mm_nt(ql, lat) + _mm(qr, rope_buf[slot])
    s_self = (jnp.sum(ql.astype(F32) * kvs, axis=-1, keepdims=True)
              + jnp.sum(qr.astype(F32) * krs, axis=-1, keepdims=True))
    m = jnp.maximum(jnp.max(s, axis=-1, keepdims=True), s_self)
    p = jnp.exp2(s - m)
    p_self = jnp.exp2(s_self - m)
    denom = jnp.sum(p, axis=-1, keepdims=True) + p_self
    o_ref[0] = (_mm(p, lat) + p_self * kvs) / denom


def _decode_attn(q_lat, q_rope, kv_self, kr_self, cache_lat, cache_rope_t, page_table):
    b, n_pages = page_table.shape
    kern = functools.partial(_decode_kernel, n_pages=n_pages)
    seq3 = lambda s, pt: (s, 0, 0)
    keys = n_pages * PAGE_SIZE
    grid_spec = pltpu.PrefetchScalarGridSpec(
        num_scalar_prefetch=1,
        grid=(b,),
        in_specs=[pl.BlockSpec((1, N_HEADS, KV_RANK), seq3),
                  pl.BlockSpec((1, N_HEADS, ROPE_DIM), seq3),
                  pl.BlockSpec((1, 1, KV_RANK), seq3),
                  pl.BlockSpec((1, 1, ROPE_DIM), seq3),
                  pl.BlockSpec(memory_space=pl.ANY),
                  pl.BlockSpec(memory_space=pl.ANY)],
        out_specs=pl.BlockSpec((1, N_HEADS, KV_RANK), seq3),
        scratch_shapes=[pltpu.VMEM((2, keys, KV_RANK), F32),
                        pltpu.VMEM((2, ROPE_DIM, keys), F32),
                        pltpu.SemaphoreType.DMA((2, 2))],
    )
    return pl.pallas_call(
        kern,
        grid_spec=grid_spec,
        out_shape=jax.ShapeDtypeStruct((b, N_HEADS, KV_RANK), F32),
        compiler_params=_cparams(("arbitrary",)),
        name="decode_attn",
    )(page_table.reshape(-1), q_lat, q_rope, kv_self, kr_self, cache_lat, cache_rope_t)


def _uv_kernel(o_ref, w_ref, y_ref):
    y_ref[...] = _mm(o_ref[...], w_ref[...]).astype(BF16)


def _uv_proj(o_lat, wuv_bd):
    rows = o_lat.shape[0]
    return pl.pallas_call(
        _uv_kernel,
        out_shape=jax.ShapeDtypeStruct((rows, ATTN_WIDTH), BF16),
        compiler_params=pltpu.CompilerParams(vmem_limit_bytes=VMEM_LIMIT),
        name="uv_proj",
    )(o_lat, wuv_bd)


def _outproj_kernel(oa_ref, ys_ref, x_ref, g1_ref, sc_ref, sh_ref, wout_ref, gpost_ref, gpre_ref,
                    wr_ref, br_ref, *rest):
    x1_ref, h2_ref, route_ref = rest[-3:]
    mix = (jnp.dot(oa_ref[0], wout_ref[:ATTN_WIDTH].astype(BF16), preferred_element_type=F32)
           + jnp.dot(ys_ref[0], wout_ref[ATTN_WIDTH:].astype(BF16), preferred_element_type=F32))
    x1 = x_ref[0] + g1_ref[0] * _rms(mix, gpost_ref[...], D_MODEL)
    x1_ref[0] = x1
    h2 = _rms(x1, gpre_ref[...], D_MODEL) * (1.0 + sc_ref[0]) + sh_ref[0]
    for s in range(D_MODEL // LANES):
        h2_ref[:, s, :] = h2[:, s * LANES:(s + 1) * LANES]
    logits = _mm3(h2, wr_ref[...]) + br_ref[...]
    lane = lax.broadcasted_iota(jnp.int32, logits.shape, 1).astype(F32)
    vals, idxs = [], []
    for _ in range(TOP_K):
        v = jnp.max(logits, axis=-1, keepdims=True)
        idx = jnp.min(jnp.where(logits == v, lane, float(LANES)), axis=-1, keepdims=True)
        vals.append(v)
        idxs.append(idx)
        logits = jnp.where(lane == idx, -jnp.inf, logits)
    exps = [jnp.exp(v - vals[0]) for v in vals]
    denom = exps[0] + exps[1] + exps[2] + exps[3]
    route = jnp.zeros(logits.shape, F32)
    for k in range(TOP_K):
        route = jnp.where(lane == float(k), idxs[k], route)
        route = jnp.where(lane == float(TOP_K + k), exps[k] / denom, route)
    route_ref[...] = route


def _outproj(o_attn, y_ssm, x, g1, sc2, sh2, p, tm, t_all, row0, dest=None):
    n, l, _ = x.shape
    per_row = g1.shape[1] != 1
    mod_blk = (1, tm, D_MODEL) if per_row else (1, 1, D_MODEL)
    mod_map = (lambda b, i: (b, i, 0)) if per_row else (lambda b, i: (b, 0, 0))
    row_map = lambda b, i: (b, i, 0)
    const2 = lambda b, i: (0, 0)
    blk0, per_seq = row0 // tm, l // tm
    n_tile = D_MODEL // LANES
    in_specs = [pl.BlockSpec((1, tm, ATTN_WIDTH), row_map),
                pl.BlockSpec((1, tm, SSM_WIDTH), row_map),
                pl.BlockSpec((1, tm, D_MODEL), row_map),
                pl.BlockSpec(mod_blk, mod_map),
                pl.BlockSpec(mod_blk, mod_map),
                pl.BlockSpec(mod_blk, mod_map),
                pl.BlockSpec((D_MODEL, D_MODEL), const2),
                pl.BlockSpec((1, D_MODEL), const2),
                pl.BlockSpec((1, D_MODEL), const2),
                pl.BlockSpec((D_MODEL, LANES), const2),
                pl.BlockSpec((1, LANES), const2)]
    args = [o_attn, y_ssm, x, g1, sc2, sh2, p["w_out"], p["g_post_mix"], p["g_pre_ffn"],
            p["w_router"], p["b_router"]]
    aliases = {}
    if dest is not None:
        aliases = {len(args): 1, len(args) + 1: 2}
        in_specs += [pl.BlockSpec(memory_space=pl.ANY), pl.BlockSpec(memory_space=pl.ANY)]
        args += list(dest)
    return pl.pallas_call(
        _outproj_kernel,
        grid=(n, per_seq),
        in_specs=in_specs,
        out_specs=[pl.BlockSpec((1, tm, D_MODEL), row_map),
                   pl.BlockSpec((tm, n_tile, LANES), lambda b, i: (blk0 + b * per_seq + i, 0, 0)),
                   pl.BlockSpec((tm, LANES), lambda b, i: (blk0 + b * per_seq + i, 0))],
        out_shape=[jax.ShapeDtypeStruct((n, l, D_MODEL), F32),
                   jax.ShapeDtypeStruct((t_all, n_tile, LANES), F32),
                   jax.ShapeDtypeStruct((t_all, LANES), F32)],
        input_output_aliases=aliases,
        compiler_params=_cparams(("parallel", "parallel")),
        name="outproj",
    )(*args)


def _moe_prep_kernel(w1_ref, w2_ref, perm_ref, after_ref, w1o_ref, w2o_ref):
    del after_ref
    for j in range(w1_ref.shape[2] // GLU_BLOCK):
        cols = slice(j * GLU_BLOCK, (j + 1) * GLU_BLOCK)
        w1o_ref[0, :, cols] = jnp.dot(w1_ref[0, :, cols].astype(BF16), perm_ref[...],
                                      preferred_element_type=F32).astype(BF16)
    w2o_ref[0] = w2_ref[0].astype(BF16)


def _moe_prep(w1, w2, after):
    n_e, d_in, d_hid = w1.shape
    d_exp = w2.shape[1]
    j = jnp.arange(GLU_BLOCK)
    src = jnp.where(j < LANES, 2 * j, 2 * (j - LANES) + 1)
    perm = (j[:, None] == src[None, :]).astype(BF16)
    e_map = lambda e: (e, 0, 0)
    return pl.pallas_call(
        _moe_prep_kernel,
        grid=(n_e,),
        in_specs=[pl.BlockSpec((1, d_in, d_hid), e_map),
                  pl.BlockSpec((1, d_exp, D_MODEL), e_map),
                  pl.BlockSpec((GLU_BLOCK, GLU_BLOCK), lambda e: (0, 0)),
                  pl.BlockSpec(memory_space=pltpu.SMEM)],
        out_specs=[pl.BlockSpec((1, d_in, d_hid), e_map),
                   pl.BlockSpec((1, d_exp, D_MODEL), e_map)],
        out_shape=[jax.ShapeDtypeStruct(w1.shape, BF16),
                   jax.ShapeDtypeStruct(w2.shape, BF16)],
        compiler_params=_cparams(("parallel",)),
        name="moe_prep",
    )(w1, w2, perm, after)


def _route_plan(idx, gates, nch, tmx):
    t = idx.shape[0]
    tc = t // nch
    nt = -(-(tc * TOP_K + N_EXPERTS * (tmx - 1)) // tmx)
    i32 = jnp.int32
    hot = (idx[:, :, None] == jnp.arange(N_EXPERTS, dtype=i32)).astype(i32).sum(1)
    hot = hot.reshape(nch, tc, N_EXPERTS)
    csum = jnp.cumsum(hot, axis=1)
    rank = (csum - hot).reshape(t, N_EXPERTS)
    padded = -(-csum[:, -1, :] // tmx) * tmx
    off_end = jnp.cumsum(padded, axis=1)
    off = off_end - padded
    n_tiles = (off_end[:, -1] // tmx).astype(i32)
    chunk = jnp.arange(t, dtype=i32) // tc
    pos = jnp.take_along_axis(off[chunk] + rank, idx, axis=1).astype(i32)
    rc = nt * tmx
    row_pair = jnp.full((nch * rc,), -1, i32).at[(chunk[:, None] * rc + pos).reshape(-1)].set(
        jnp.arange(t * TOP_K, dtype=i32), unique_indices=True)
    valid = row_pair >= 0
    pair = jnp.maximum(row_pair, 0)
    row_tok = jnp.where(valid, (pair // TOP_K) % tc, 0)
    row_dst = jnp.where(valid, row_tok, tc + jnp.arange(nch * rc, dtype=i32) % SUBLANES)
    row_gate = jnp.where(valid, gates.reshape(-1)[pair], 0.0)
    tile = jnp.arange(nt, dtype=i32)
    texp_raw = (off_end[:, None, :] <= (tile * tmx)[None, :, None]).sum(-1).astype(i32)
    tile_c = jnp.minimum(tile[None, :], n_tiles[:, None] - 1)
    tile_n = jnp.minimum(tile[None, :] + 1, n_tiles[:, None] - 1)
    texp = jnp.take_along_axis(texp_raw, tile_c, axis=1)
    base = (jnp.arange(nch, dtype=i32) * nt)[:, None]
    return dict(nt=nt, n_tiles=n_tiles, texp=texp.reshape(-1), tblk=(tile_c + base).reshape(-1),
                tnext=(tile_n + base).reshape(-1),
                row_tok=row_tok.reshape(nch * nt, 1, tmx), row_dst=row_dst.reshape(nch * nt, 1, tmx),
                row_gate=row_gate.reshape(nch * nt, tmx, 1), pos=pos)


def _moe_kernel(nt_ref, texp_ref, tblk_ref, tnext_ref, tok_ref, toknext_ref, dst_ref, gate_ref, src_ref,
                w1_ref, b1_ref, w2_ref, b2_ref, out_ref, xs_ref, yt_ref):
    del texp_ref, tblk_ref, tnext_ref
    c = pl.program_id(0)
    i = pl.program_id(1)
    tmx = gate_ref.shape[1]
    slot = i % 2

    @pl.when(i == 0)
    def _():
        out_ref[...] = jnp.zeros(out_ref.shape, F32)

        def gather(r, carry):
            xs_ref[0, pl.ds(pl.multiple_of(r * SUBLANES, SUBLANES), SUBLANES), :] = src_ref[0, tok_ref[0, 0, r]]
            return carry
        lax.fori_loop(0, tmx, gather, 0, unroll=8)

    @pl.when(i < nt_ref[c])
    def _():
        x = jnp.concatenate([xs_ref[slot, pl.ds(s, tmx, stride=SUBLANES), :]
                             for s in range(D_MODEL // LANES)], axis=-1).astype(BF16)
        n_blk = w1_ref.shape[2] // (2 * GLU_BLOCK)
        wide = 2 * GLU_BLOCK

        def hidden(j):
            cols = slice(j * wide, (j + 1) * wide)
            return jnp.dot(x, w1_ref[0, :, cols], preferred_element_type=F32) + b1_ref[0, :, cols]

        hid = hidden(0)
        y = None
        for j in range(n_blk):
            nxt = hidden(j + 1) if j + 1 < n_blk else None
            for r in range(j * tmx // n_blk, (j + 1) * tmx // n_blk):
                xs_ref[1 - slot, r * SUBLANES:(r + 1) * SUBLANES, :] = src_ref[0, toknext_ref[0, 0, r]]
            acts = []
            for b in range(2):
                glu = jnp.minimum(hid[:, b * GLU_BLOCK:b * GLU_BLOCK + LANES], SWIGLU_LIMIT)
                lin = jnp.clip(hid[:, b * GLU_BLOCK + LANES:(b + 1) * GLU_BLOCK], -SWIGLU_LIMIT, SWIGLU_LIMIT)
                acts.append(glu * _sigmoid(SWIGLU_ALPHA * glu) * (lin + 1.0))
            part = _mm(jnp.concatenate(acts, axis=-1), w2_ref[0, j * GLU_BLOCK:(j + 1) * GLU_BLOCK, :])
            y = part if y is None else y + part
            hid = nxt
        y = gate_ref[0] * (y + b2_ref[0])
        for s in range(D_MODEL // LANES):
            yt_ref[pl.ds(s, tmx, stride=SUBLANES), :] = y[:, s * LANES:(s + 1) * LANES]
        for g in range(tmx // SUBLANES):
            toks = [dst_ref[0, 0, g * SUBLANES + u] for u in range(SUBLANES)]
            sums = [out_ref[0, toks[u]]
                    + yt_ref[(g * SUBLANES + u) * SUBLANES:(g * SUBLANES + u + 1) * SUBLANES, :]
                    for u in range(SUBLANES)]
            for u in range(SUBLANES):
                out_ref[0, toks[u]] = sums[u]


def _moe_experts(h2, plan, w1p, b1p, w2p, b2, tmx):
    nch, tc = h2.shape[:2]
    nt = plan["nt"]
    n_e, _, d_hid = w1p.shape
    d_exp = w2p.shape[1]
    flat = lambda c, i: c * nt + i
    cur = lambda c, i, n, te, tb, tn: (tb[flat(c, i)], 0, 0)
    nxt = lambda c, i, n, te, tb, tn: (tn[flat(c, i)], 0, 0)
    exp = lambda c, i, n, te, tb, tn: (te[flat(c, i)], 0, 0)
    grid_spec = pltpu.PrefetchScalarGridSpec(
        num_scalar_prefetch=4,
        grid=(nch, nt),
        in_specs=[pl.BlockSpec((1, 1, tmx), cur, memory_space=pltpu.SMEM),
                  pl.BlockSpec((1, 1, tmx), nxt, memory_space=pltpu.SMEM),
                  pl.BlockSpec((1, 1, tmx), cur, memory_space=pltpu.SMEM),
                  pl.BlockSpec((1, tmx, 1), cur),
                  pl.BlockSpec((1, tc, D_MODEL // LANES, LANES), lambda c, i, n, te, tb, tn: (c, 0, 0, 0),
                               pipeline_mode=pl.Buffered(1)),
                  pl.BlockSpec((1, D_MODEL, d_hid), exp),
                  pl.BlockSpec((1, 1, d_hid), exp),
                  pl.BlockSpec((1, d_exp, D_MODEL), exp),
                  pl.BlockSpec((1, 1, D_MODEL), exp)],
        out_specs=pl.BlockSpec((1, tc + SUBLANES, D_MODEL // LANES, LANES),
                               lambda c, i, n, te, tb, tn: (c, 0, 0, 0), pipeline_mode=pl.Buffered(1)),
        scratch_shapes=[pltpu.VMEM((2, tmx * (D_MODEL // LANES), LANES), F32),
                        pltpu.VMEM((tmx * (D_MODEL // LANES), LANES), F32)],
    )
    return pl.pallas_call(
        _moe_kernel,
        grid_spec=grid_spec,
        out_shape=jax.ShapeDtypeStruct((nch, tc + SUBLANES, D_MODEL // LANES, LANES), F32),
        compiler_params=_cparams(("parallel", "arbitrary")),
        name="moe_experts",
    )(plan["n_tiles"], plan["texp"], plan["tblk"], plan["tnext"], plan["row_tok"], plan["row_tok"],
      plan["row_dst"], plan["row_gate"], h2, w1p, b1p, w2p, b2)


def _final_kernel(x1_ref, f_ref, g2_ref, gpost_ref, y_ref):
    tm = x1_ref.shape[1]
    f = jnp.concatenate([f_ref[pl.ds(s, tm, stride=SUBLANES), :] for s in range(D_MODEL // LANES)],
                        axis=-1)
    y_ref[0] = x1_ref[0] + g2_ref[0] * _rms(f, gpost_ref[...], D_MODEL)


def _final(x1, f_all, row0, g2, g_post_ffn, tm):
    n, l, _ = x1.shape
    per_row = g2.shape[1] != 1
    mod_blk = (1, tm, D_MODEL) if per_row else (1, 1, D_MODEL)
    mod_map = (lambda b, i: (b, i, 0)) if per_row else (lambda b, i: (b, 0, 0))
    row_map = lambda b, i: (b, i, 0)
    blk0, per_seq = row0 // tm, l // tm
    return pl.pallas_call(
        _final_kernel,
        grid=(n, l // tm),
        in_specs=[pl.BlockSpec((1, tm, D_MODEL), row_map),
                  pl.BlockSpec((tm * (D_MODEL // LANES), LANES), lambda b, i: (blk0 + b * per_seq + i, 0)),
                  pl.BlockSpec(mod_blk, mod_map),
                  pl.BlockSpec((1, D_MODEL), lambda b, i: (0, 0))],
        out_specs=pl.BlockSpec((1, tm, D_MODEL), row_map),
        out_shape=jax.ShapeDtypeStruct((n, l, D_MODEL), F32),
        compiler_params=_cparams(("parallel", "parallel")),
        name="final",
    )(x1, f_all, g2, g_post_ffn)


def _complex_pow2(lam, n):
    re, im = lam[0], lam[1]
    steps = int(round(math.log2(n)))
    assert 2 ** steps == n
    for _ in range(steps):
        re, im = re * re - im * im, 2.0 * re * im
    return jnp.stack([re, im])


def _rope_tables(pos):
    half = ROPE_DIM // 2
    inv = ROPE_THETA ** (-jnp.arange(half, dtype=F32) / half)
    ang = pos.astype(F32)[:, None] * inv[None, :]
    cos, sin = jnp.cos(ang), jnp.sin(ang)
    cos4 = jnp.tile(jnp.concatenate([cos, cos], axis=-1), (1, LANES // ROPE_DIM))
    sin4 = jnp.tile(jnp.concatenate([-sin, sin], axis=-1), (1, LANES // ROPE_DIM))
    return cos4[None], sin4[None]


def _swap_halves(w):
    half = ROPE_DIM // 2
    return jnp.concatenate([w[..., half:], w[..., :half]], axis=-1)


def _prep_layer(w, layer):
    g = lambda name: w[name][layer]
    p = {}
    row = lambda v: v.reshape(1, -1)
    p["g_pre_mix"] = row(g("g_pre_mix"))
    w_in = g("w_in")
    w_q, w_kv, w_r, w_u = jnp.split(w_in, [Q_RANK, Q_RANK + KV_RANK, Q_RANK + KV_RANK + ROPE_DIM], axis=-1)
    rep = LANES // ROPE_DIM
    p["w_in"] = jnp.concatenate(
        [w_q, jnp.zeros((D_MODEL, Q_PAD - Q_RANK), F32), w_kv,
         jnp.tile(w_r, (1, rep)), jnp.tile(_swap_halves(w_r), (1, rep)), w_u], axis=-1)
    p["g_q"] = row(jnp.pad(g("g_q"), (0, Q_PAD - Q_RANK)))
    w_uq = g("w_uq").reshape(Q_RANK, N_HEADS, NOPE_DIM + ROPE_DIM)
    nope = w_uq[:, :, :NOPE_DIM].reshape(Q_RANK, N_HEADS * NOPE_DIM)
    rope = w_uq[:, :, NOPE_DIM:]
    rope_sw = _swap_halves(rope)
    w_uq_p = jnp.concatenate([nope, rope.reshape(Q_RANK, -1), rope_sw.reshape(Q_RANK, -1)], axis=-1)
    p["w_uq"] = jnp.pad(w_uq_p, ((0, Q_PAD - Q_RANK), (0, 0)))
    w_uk = jnp.transpose(g("w_uk"), (1, 2, 0))
    z = jnp.zeros((NOPE_DIM, KV_RANK), F32)
    p["w_uk"] = jnp.stack([jnp.block([[w_uk[2 * j], z], [z, w_uk[2 * j + 1]]])
                           for j in range(N_HEADS // 2)])
    p["g_kv"] = row(g("g_kv"))
    w_uv = jnp.transpose(g("w_uv"), (1, 0, 2))
    p["w_uv_t"] = jnp.transpose(w_uv, (0, 2, 1))
    eye_h = jnp.eye(N_HEADS, dtype=F32)
    p["w_uv_bd"] = jnp.einsum("hrv,hk->hrkv", w_uv, eye_h).reshape(N_HEADS * KV_RANK, ATTN_WIDTH)
    p["w_out"] = g("w_out")
    p["g_post_mix"] = row(g("g_post_mix"))
    p["g_pre_ffn"] = row(g("g_pre_ffn"))
    p["w_router"] = jnp.pad(g("w_router"), ((0, 0), (0, LANES - N_EXPERTS)))
    p["b_router"] = row(jnp.pad(g("b_router"), (0, LANES - N_EXPERTS), constant_values=NEG_INF))
    p["g_post_ffn"] = row(g("g_post_ffn"))

    lam_re, lam_im = g("ssm_lam_re"), g("ssm_lam_im")
    step = jnp.exp(g("ssm_log_step"))[:, None]
    mag = jnp.exp(lam_re * step)
    bar_re, bar_im = mag * jnp.cos(lam_im * step), mag * jnp.sin(lam_im * step)
    den = lam_re * lam_re + lam_im * lam_im
    co_re = ((bar_re - 1.0) * lam_re + bar_im * lam_im) / den
    co_im = (bar_im * lam_re - (bar_re - 1.0) * lam_im) / den
    b_re, b_im = g("ssm_b_re"), g("ssm_b_im")
    bb_re = co_re[..., None] * b_re - co_im[..., None] * b_im
    bb_im = co_re[..., None] * b_im + co_im[..., None] * b_re
    eye_g = jnp.eye(GROUPS_PER_BLOCK, dtype=F32)

    def b_block(m):
        m = m.reshape(SSM_BLOCKS, GROUPS_PER_BLOCK, SSM_STATE, SSM_GROUP)
        return jnp.einsum("bgpc,gh->bgchp", m, eye_g).reshape(SSM_BLOCKS, LANES, BLOCK_STATES)

    def c_block(m):
        m = m.reshape(SSM_BLOCKS, GROUPS_PER_BLOCK, SSM_GROUP, SSM_STATE)
        return jnp.einsum("bgcp,gh->bgphc", m, eye_g).reshape(SSM_BLOCKS, BLOCK_STATES, LANES)

    sp = {
        "lam": jnp.stack([bar_re.reshape(-1), bar_im.reshape(-1)]),
        "bmat": jnp.concatenate([b_block(bb_re), b_block(bb_im)], axis=-1).astype(BF16),
        "cmat": jnp.concatenate([c_block(g("ssm_c_re")), c_block(-g("ssm_c_im"))], axis=1).astype(BF16),
        "d": row(g("ssm_d")),
        "w_glu": g("ssm_w_glu"),
        "b_glu": row(g("ssm_b_glu")),
    }
    b1 = g("b1")
    n_blk = b1.shape[1] // GLU_BLOCK
    mp = {
        "b1": jnp.transpose(b1.reshape(N_EXPERTS, n_blk, LANES, 2), (0, 1, 3, 2)).reshape(N_EXPERTS, 1, -1),
        "b2": g("b2")[:, None, :],
    }
    return p, sp, mp


def _mods(mod_rows):
    return [mod_rows[:, k * D_MODEL:(k + 1) * D_MODEL] for k in range(N_MOD)]


def _pick_tile(n, target, mult):
    best = mult
    for c in range(mult, min(n, target) + 1, mult):
        if n % c == 0:
            best = c
    return best


def kernel(x_prompt, x_sample, c_prompt, c_sample, cache_kv_latent, cache_k_rope, state_ssm, page_table, w_ada, b_ada, g_pre_mix, w_in, g_q, w_uq, w_uk, g_kv, w_uv, ssm_lam_re, ssm_lam_im, ssm_log_step, ssm_b_re, ssm_b_im, ssm_c_re, ssm_c_im, ssm_d, ssm_w_glu, ssm_b_glu, w_out, g_post_mix, g_pre_ffn, w_router, b_router, w1, b1, w2, b2, g_post_ffn):
    weights = dict(w_ada=w_ada, b_ada=b_ada, g_pre_mix=g_pre_mix, w_in=w_in, g_q=g_q, w_uq=w_uq,
                   w_uk=w_uk, g_kv=g_kv, w_uv=w_uv, ssm_lam_re=ssm_lam_re, ssm_lam_im=ssm_lam_im,
                   ssm_log_step=ssm_log_step, ssm_b_re=ssm_b_re, ssm_b_im=ssm_b_im,
                   ssm_c_re=ssm_c_re, ssm_c_im=ssm_c_im, ssm_d=ssm_d, ssm_w_glu=ssm_w_glu,
                   ssm_b_glu=ssm_b_glu, w_out=w_out, g_post_mix=g_post_mix, g_pre_ffn=g_pre_ffn,
                   w_router=w_router, b_router=b_router, w1=w1, b1=b1, w2=w2, b2=b2,
                   g_post_ffn=g_post_ffn)
    depth = w_ada.shape[0]
    n_p, l_p, _ = x_prompt.shape
    n_s, l_s, _ = x_sample.shape
    assert l_s == 1
    n_pages = page_table.shape[1]
    past_len = n_pages * PAGE_SIZE

    cos_p, sin_p = _rope_tables(jnp.arange(l_p, dtype=jnp.int32))
    cos_s, sin_s = _rope_tables(jnp.full((1,), past_len, jnp.int32))
    cos_s = jnp.broadcast_to(cos_s, (1, n_s, LANES))
    sin_s = jnp.broadcast_to(sin_s, (1, n_s, LANES))

    tm_p = _pick_tile(l_p, 512, 16)
    tq = _pick_tile(l_p, 256, 16)
    tk = _pick_tile(l_p, 1024, 16)
    lc = 64 if l_p % 512 == 0 else l_p // SUBLANES
    c_rows = n_p + n_s
    c_pad = -c_rows % SUBLANES
    c_all = jnp.pad(jnp.concatenate([c_prompt, c_sample], axis=0), ((0, c_pad), (0, 0)))

    y_p = x_prompt
    y_s = x_sample.reshape(1, n_s, D_MODEL)
    outs = [[] for _ in range(6)]
    for layer in range(depth):
        p, sp, mp = _prep_layer(weights, layer)
        mod = _ada(c_all, w_ada[layer], b_ada[layer].reshape(1, -1))
        sh1p, sc1p, g1p, sh2p, sc2p, g2p = [m[:, None, :] for m in _mods(mod[:n_p])]
        sh1s, sc1s, g1s, sh2s, sc2s, g2s = [m[None] for m in _mods(mod[n_p:c_rows])]

        q_t, k_cat, v_t, kv_p, kr_p, u_p = _inproj(y_p, sc1p, sh1p, p, cos_p, sin_p, tm_p)
        o_attn_p = _flash(q_t, k_cat, v_t, p["w_uv_t"], tq, tk)
        y_ssm_p, h_last_p = _s5_prompt(u_p, sp, lc)
        t_p = n_p * l_p
        t_all = t_p + n_s
        blank = (jnp.zeros((t_all, D_MODEL // LANES, LANES), F32), jnp.zeros((t_all, LANES), F32))
        x1_p, h2_all, route = _outproj(o_attn_p, y_ssm_p, y_p, g1p, sc2p, sh2p, p, tm_p, t_all, 0,
                                       dest=blank)

        q_t_s, _, _, kv_s, kr_s, u_s = _inproj(y_s, sc1s, sh1s, p, cos_s, sin_s, n_s)
        q_lat_s = jnp.transpose(q_t_s[0, :, :KV_RANK, :], (2, 0, 1))
        q_rope_s = jnp.stack(
            [q_t_s[0, h, KV_RANK + ROPE_DIM * (h % 4):KV_RANK + ROPE_DIM * (h % 4 + 1), :].T
             for h in range(N_HEADS)], axis=1)
        o_lat_s = _decode_attn(q_lat_s, q_rope_s, kv_s.reshape(n_s, 1, KV_RANK),
                               kr_s.reshape(n_s, 1, ROPE_DIM), cache_kv_latent[layer],
                               jnp.swapaxes(cache_k_rope[layer], 1, 2), page_table)
        o_attn_s = _uv_proj(o_lat_s.reshape(n_s, N_HEADS * KV_RANK), p["w_uv_bd"])
        h0 = jnp.moveaxis(state_ssm[layer].reshape(n_s, N_STATE, 2), -1, 0)
        y_ssm_s, h_s = _s5_step(u_s[0], h0, sp)
        x1_s, h2_all, route = _outproj(o_attn_s[None], y_ssm_s[None], y_s, g1s, sc2s, sh2s, p, n_s,
                                       t_all, t_p, dest=(h2_all, route))

        nch = 4 if t_all % (4 * SUBLANES) == 0 else (2 if t_all % (2 * SUBLANES) == 0 else 1)
        tc = t_all // nch
        plan = _route_plan(route[:, :TOP_K].astype(jnp.int32), route[:, TOP_K:2 * TOP_K], nch, MOE_TILE)
        w1p, w2p = _moe_prep(w1[layer], w2[layer], plan["row_tok"][0])
        f_chunks = _moe_experts(h2_all.reshape(nch, tc, D_MODEL // LANES, LANES), plan, w1p, mp["b1"],
                                w2p, mp["b2"], MOE_TILE)
        f_all = f_chunks[:, :tc].reshape(t_all * (D_MODEL // LANES), LANES)
        y_p = _final(x1_p, f_all, 0, g2p, p["g_post_ffn"], tm_p)
        y_s = _final(x1_s, f_all, t_p, g2s, p["g_post_ffn"], n_s)

        outs[0].append(kv_p)
        outs[1].append(kr_p)
        outs[2].append(jnp.moveaxis(h_last_p, 1, -1).reshape(n_p, N_SSM_GROUPS, SSM_STATE, 2))
        outs[3].append(kv_s.reshape(n_s, 1, KV_RANK))
        outs[4].append(kr_s.reshape(n_s, 1, ROPE_DIM))
        outs[5].append(jnp.moveaxis(h_s, 0, -1).reshape(n_s, N_SSM_GROUPS, SSM_STATE, 2))
    return (y_p, y_s.reshape(n_s, 1, D_MODEL)) + tuple(jnp.stack(o) for o in outs)
```

```python
import functools
import math

import jax
import jax.numpy as jnp
from jax import lax
from jax.experimental import pallas as pl
from jax.experimental.pallas import tpu as pltpu

F32 = jnp.float32
BF16 = jnp.bfloat16

D_MODEL = 1024
N_HEADS = 8
NOPE_DIM = 64
ROPE_DIM = 32
V_DIM = 64
Q_RANK = 352
KV_RANK = 128
ROPE_THETA = 10000.0
ATTN_WIDTH = N_HEADS * V_DIM
ATTN_SCALE = 1.0 / math.sqrt(NOPE_DIM + ROPE_DIM)
SSM_WIDTH = D_MODEL - ATTN_WIDTH
SSM_GROUP = 16
N_SSM_GROUPS = SSM_WIDTH // SSM_GROUP
SSM_STATE = 64
N_STATE = N_SSM_GROUPS * SSM_STATE
N_EXPERTS = 32
TOP_K = 4
SWIGLU_LIMIT = 7.0
SWIGLU_ALPHA = 1.702
RMS_EPS = 1e-6
N_MOD = 6
NEG_INF = -1e30
PAGE_SIZE = 128

LANES = 128
SUBLANES = 8
VMEM_LIMIT = 56 * 1024 * 1024

Q_PAD = 384
CAT = 2 * LANES
SSM_BLOCKS = 4
GROUPS_PER_BLOCK = N_SSM_GROUPS // SSM_BLOCKS
BLOCK_STATES = GROUPS_PER_BLOCK * SSM_STATE
IN_PAD = Q_PAD + KV_RANK + 2 * LANES + SSM_WIDTH
UQ_PAD = N_HEADS * NOPE_DIM + 4 * LANES
GLU_BLOCK = 2 * LANES
MOE_TILE = 192
MOE_COL_PARTS = 2
LOG2E = math.log2(math.e)
FLASH_LOOKAHEAD = 3
FLASH_DIAG_SUB = 256
VT_ROWS = KV_RANK + 16
ROUTE_TOKEN = 2 * TOP_K
ROUTE_VALID = 2 * TOP_K + 1


def _cparams(sem):
    return pltpu.CompilerParams(dimension_semantics=sem, vmem_limit_bytes=VMEM_LIMIT)


def _mm(a, b):
    return jnp.dot(a.astype(BF16), b.astype(BF16), preferred_element_type=F32)


def _mm_nt(a, b):
    return lax.dot_general(a.astype(BF16), b.astype(BF16), (((1,), (1,)), ((), ())),
                           preferred_element_type=F32)


def _split_bf16(x):
    hi = x.astype(BF16)
    lo = (x - hi.astype(F32)).astype(BF16)
    return hi, lo


def _mm3(a, b):
    ah, al = _split_bf16(a)
    bh, bl = _split_bf16(b)
    d = functools.partial(jnp.dot, preferred_element_type=F32)
    return d(ah, bh) + (d(ah, bl) + d(al, bh))


def _sigmoid(x):
    return 1.0 / (1.0 + jnp.exp(-x))


def _rms(x, g, width):
    ms = jnp.sum(x * x, axis=-1, keepdims=True) * (1.0 / width)
    return x * lax.rsqrt(ms + RMS_EPS) * g


def _gelu_tanh(x):
    return 0.5 * x * (1.0 + jnp.tanh(math.sqrt(2.0 / math.pi) * (x + 0.044715 * (x * x * x))))


def _ada_kernel(c_ref, w_ref, b_ref, o_ref):
    c = c_ref[...]
    o_ref[...] = _mm3(c * _sigmoid(c), w_ref[...]) + b_ref[...]


def _ada(c_all, w_ada, b_ada):
    rows = c_all.shape[0]
    tn = 512
    n_out = w_ada.shape[1]
    return pl.pallas_call(
        _ada_kernel,
        grid=(n_out // tn,),
        in_specs=[pl.BlockSpec((rows, D_MODEL), lambda j: (0, 0)),
                  pl.BlockSpec((D_MODEL, tn), lambda j: (0, j)),
                  pl.BlockSpec((1, tn), lambda j: (0, j))],
        out_specs=pl.BlockSpec((rows, tn), lambda j: (0, j)),
        out_shape=jax.ShapeDtypeStruct((rows, n_out), F32),
        compiler_params=_cparams(("parallel",)),
        name="adaln",
    )(c_all, w_ada, b_ada)


def _inproj_kernel(x_ref, sc_ref, sh_ref, gpre_ref, win_ref, gq_ref, wuq_ref, wuk_ref, gkv_ref,
                   cos_ref, sin_ref, q_ref, kcat_ref, vt_ref, kv_ref, kr_ref, u_ref):
    x = x_ref[0]
    h = _rms(x, gpre_ref[...], D_MODEL) * (1.0 + sc_ref[0]) + sh_ref[0]
    z = _mm(h, win_ref[...])
    zq = z[:, :Q_PAD]
    zkv = z[:, Q_PAD:Q_PAD + KV_RANK]
    zr = z[:, Q_PAD + KV_RANK:Q_PAD + KV_RANK + LANES]
    zrs = z[:, Q_PAD + KV_RANK + LANES:Q_PAD + KV_RANK + 2 * LANES]
    u = z[:, Q_PAD + KV_RANK + 2 * LANES:]
    cos4 = cos_ref[0]
    sin4 = sin_ref[0]

    kvl = _rms(zkv, gkv_ref[...], KV_RANK)
    kr4 = zr * cos4 + zrs * sin4
    kv_ref[0] = kvl
    kr_ref[0] = kr4[:, :ROPE_DIM]
    kcat_ref[0] = jnp.concatenate([kvl, kr4], axis=-1).astype(BF16)
    extra = (lax.broadcasted_iota(jnp.int32, (VT_ROWS - KV_RANK, kvl.shape[0]), 0) == 0).astype(F32)
    vt_ref[0] = jnp.concatenate([kvl.T, extra], axis=0).astype(BF16)
    u_ref[0] = u.astype(BF16)

    qn = _rms(zq, gq_ref[...], Q_RANK)
    q = _mm(qn, wuq_ref[...])
    nope_w = N_HEADS * NOPE_DIM
    rope_a = q[:, nope_w:nope_w + LANES] * cos4 + q[:, nope_w + 2 * LANES:nope_w + 3 * LANES] * sin4
    rope_b = q[:, nope_w + LANES:nope_w + 2 * LANES] * cos4 + q[:, nope_w + 3 * LANES:] * sin4
    slot = lax.broadcasted_iota(jnp.int32, (1, LANES), 1) // ROPE_DIM
    for j in range(N_HEADS // 2):
        ql = _mm(q[:, j * LANES:(j + 1) * LANES], wuk_ref[j])
        for hh in range(2):
            head = 2 * j + hh
            rope = rope_a if head < 4 else rope_b
            rope_h = jnp.where(slot == (head % 4), rope, 0.0)
            qc = jnp.concatenate([ql[:, hh * KV_RANK:(hh + 1) * KV_RANK], rope_h], axis=-1)
            q_ref[0, head] = (qc * (ATTN_SCALE * LOG2E)).T.astype(BF16)


def _inproj(x, sc, sh, p, cos4, sin4, tm):
    n, l, _ = x.shape
    per_row = sc.shape[1] != 1
    mod_blk = (1, tm, D_MODEL) if per_row else (1, 1, D_MODEL)
    mod_map = (lambda b, i: (b, i, 0)) if per_row else (lambda b, i: (b, 0, 0))
    const2 = lambda b, i: (0, 0)
    const3 = lambda b, i: (0, 0, 0)
    pos_map = lambda b, i: (0, i, 0)
    return pl.pallas_call(
        _inproj_kernel,
        grid=(n, l // tm),
        in_specs=[pl.BlockSpec((1, tm, D_MODEL), lambda b, i: (b, i, 0)),
                  pl.BlockSpec(mod_blk, mod_map),
                  pl.BlockSpec(mod_blk, mod_map),
                  pl.BlockSpec((1, D_MODEL), const2),
                  pl.BlockSpec((D_MODEL, IN_PAD), const2),
                  pl.BlockSpec((1, Q_PAD), const2),
                  pl.BlockSpec((Q_PAD, UQ_PAD), const2),
                  pl.BlockSpec((N_HEADS // 2, LANES, 2 * KV_RANK), const3),
                  pl.BlockSpec((1, KV_RANK), const2),
                  pl.BlockSpec((1, tm, LANES), pos_map),
                  pl.BlockSpec((1, tm, LANES), pos_map)],
        out_specs=[pl.BlockSpec((1, N_HEADS, CAT, tm), lambda b, i: (b, 0, 0, i)),
                   pl.BlockSpec((1, tm, CAT), lambda b, i: (b, i, 0)),
                   pl.BlockSpec((1, VT_ROWS, tm), lambda b, i: (b, 0, i)),
                   pl.BlockSpec((1, tm, KV_RANK), lambda b, i: (b, i, 0)),
                   pl.BlockSpec((1, tm, ROPE_DIM), lambda b, i: (b, i, 0)),
                   pl.BlockSpec((1, tm, SSM_WIDTH), lambda b, i: (b, i, 0))],
        out_shape=[jax.ShapeDtypeStruct((n, N_HEADS, CAT, l), BF16),
                   jax.ShapeDtypeStruct((n, l, CAT), BF16),
                   jax.ShapeDtypeStruct((n, VT_ROWS, l), BF16),
                   jax.ShapeDtypeStruct((n, l, KV_RANK), F32),
                   jax.ShapeDtypeStruct((n, l, ROPE_DIM), F32),
                   jax.ShapeDtypeStruct((n, l, SSM_WIDTH), BF16)],
        compiler_params=_cparams(("parallel", "parallel")),
        name="inproj",
    )(x, sc, sh, p["g_pre_mix"], p["w_in"], p["g_q"], p["w_uq"], p["w_uk"], p["g_kv"], cos4, sin4)


def _flash_kernel(qt_ref, k_ref, vt_ref, wuvt_ref, o_ref, m_ref, acc_ref, *, tq, tk):
    i = pl.program_id(1)
    j = pl.program_id(2)
    j_last = ((i + 1) * tq - 1) // tk

    @pl.when(j == 0)
    def _():
        m_ref[...] = jnp.full(m_ref.shape, NEG_INF, F32)
        acc_ref[...] = jnp.zeros(acc_ref.shape, F32)

    def update(masked, key0, size):
        k = k_ref[0, pl.ds(key0, size), :]
        vt = vt_ref[0, :, pl.ds(key0, size)]
        if masked:
            k_pos = j * tk + key0 + lax.broadcasted_iota(jnp.int32, (size, tq), 0)
            q_pos = i * tq + lax.broadcasted_iota(jnp.int32, (size, tq), 1)
            keep = k_pos <= q_pos
        m_all = m_ref[...]
        m_rows = []
        scores = lambda h: jnp.dot(k, qt_ref[0, h], preferred_element_type=F32)
        ahead = [scores(h) for h in range(min(FLASH_LOOKAHEAD, N_HEADS))]
        pending = None
        for h in range(N_HEADS):
            s = ahead.pop(0)
            if h + FLASH_LOOKAHEAD < N_HEADS:
                ahead.append(scores(h + FLASH_LOOKAHEAD))
            if masked:
                s = jnp.where(keep, s, NEG_INF)
            m_prev = m_all[h:h + 1]
            m_new = jnp.maximum(m_prev, jnp.max(s, axis=0, keepdims=True))
            alpha = jnp.exp2(m_prev - m_new)
            p = jnp.exp2(s - m_new)
            pv = jnp.dot(vt, p.astype(BF16), preferred_element_type=F32)
            if pending is not None:
                acc_ref[pending[0]] = pending[1] * acc_ref[pending[0]] + pending[2]
            pending = (h, alpha, pv)
            m_rows.append(m_new)
        acc_ref[pending[0]] = pending[1] * acc_ref[pending[0]] + pending[2]
        m_ref[...] = jnp.concatenate(m_rows, axis=0)

    crosses = (j + 1) * tk - 1 > i * tq
    sub = min(tk, FLASH_DIAG_SUB)

    @pl.when(jnp.logical_and(j <= j_last, crosses))
    def _():
        n_sub = jnp.minimum(tk // sub, ((i + 1) * tq - 1 - j * tk) // sub + 1)

        def body(sb, carry):
            update(True, pl.multiple_of(sb * sub, sub), sub)
            return carry
        lax.fori_loop(0, n_sub, body, 0)

    @pl.when(jnp.logical_and(j <= j_last, jnp.logical_not(crosses)))
    def _():
        update(False, 0, tk)

    @pl.when(j == j_last)
    def _():
        outs = []
        for h in range(N_HEADS):
            o_t = acc_ref[h, :KV_RANK, :] / acc_ref[h, KV_RANK:KV_RANK + 1, :]
            outs.append(_mm(wuvt_ref[h], o_t))
        o_ref[0] = jnp.concatenate(outs, axis=0).T.astype(BF16)


def _flash(q_t, k_cat, v_t, wuv_t, tq, tk):
    n, _, _, l = q_t.shape
    kern = functools.partial(_flash_kernel, tq=tq, tk=tk)
    last = lambda i: ((i + 1) * tq - 1) // tk
    return pl.pallas_call(
        kern,
        grid=(n, l // tq, l // tk),
        in_specs=[pl.BlockSpec((1, N_HEADS, CAT, tq), lambda b, i, j: (b, 0, 0, i)),
                  pl.BlockSpec((1, tk, CAT), lambda b, i, j: (b, jnp.minimum(j, last(i)), 0)),
                  pl.BlockSpec((1, VT_ROWS, tk), lambda b, i, j: (b, 0, jnp.minimum(j, last(i)))),
                  pl.BlockSpec((N_HEADS, V_DIM, KV_RANK), lambda b, i, j: (0, 0, 0))],
        out_specs=pl.BlockSpec((1, tq, ATTN_WIDTH), lambda b, i, j: (b, i, 0)),
        out_shape=jax.ShapeDtypeStruct((n, l, ATTN_WIDTH), BF16),
        scratch_shapes=[pltpu.VMEM((N_HEADS, tq), F32),
                        pltpu.VMEM((N_HEADS, VT_ROWS, tq), F32)],
        compiler_params=_cparams(("parallel", "parallel", "arbitrary")),
        name="flash",
    )(q_t, k_cat, v_t, wuv_t)


def _ssm_out(y, up, d_ref, wglu_ref, bglu_ref):
    y = _gelu_tanh(y + d_ref[...] * up)
    return y * _sigmoid(_mm(y, wglu_ref[...]) + bglu_ref[...])


def _s5_kernel(u_ref, perm_ref, permt_ref, bmat_ref, cmat_ref, lam_ref, laml_ref, d_ref, wglu_ref,
               bglu_ref, y_ref, hlast_ref, buf_ref, carry_ref, *, lc):
    j = pl.program_id(1)

    @pl.when(j == 0)
    def _():
        carry_ref[...] = jnp.zeros(carry_ref.shape, F32)

    u = u_ref[0]
    up = jnp.dot(perm_ref[...], u, preferred_element_type=F32).astype(BF16)
    blk_w = 2 * BLOCK_STATES
    for b in range(SSM_BLOCKS):
        buf_ref[:, b * blk_w:(b + 1) * blk_w] = jnp.dot(
            up[:, b * LANES:(b + 1) * LANES], bmat_ref[b], preferred_element_type=F32)

    for b in range(SSM_BLOCKS):
        st = slice(b * BLOCK_STATES, (b + 1) * BLOCK_STATES)
        re_cols = slice(b * blk_w, b * blk_w + BLOCK_STATES)
        im_cols = slice(b * blk_w + BLOCK_STATES, (b + 1) * blk_w)
        lr = lam_ref[0:1, st]
        li = lam_ref[1:2, st]

        def run(init, store):
            def body(t, carry):
                re, im = carry
                rows = pl.ds(pl.multiple_of(t * SUBLANES, SUBLANES), SUBLANES)
                n_re = lr * re - li * im + buf_ref[rows, re_cols]
                n_im = lr * im + li * re + buf_ref[rows, im_cols]
                if store:
                    buf_ref[rows, re_cols] = n_re
                    buf_ref[rows, im_cols] = n_im
                return n_re, n_im
            return lax.fori_loop(0, lc, body, init)

        zero = jnp.zeros((SUBLANES, BLOCK_STATES), F32)
        end_re, end_im = run((zero, zero), False)
        ll_r = laml_ref[0:1, st]
        ll_i = laml_ref[1:2, st]
        c_re = carry_ref[0:1, st]
        c_im = carry_ref[1:2, st]
        init_re, init_im = [], []
        for k in range(SUBLANES):
            init_re.append(c_re)
            init_im.append(c_im)
            c_re, c_im = (ll_r * c_re - ll_i * c_im + end_re[k:k + 1],
                          ll_r * c_im + ll_i * c_re + end_im[k:k + 1])
        carry_ref[0:1, st] = c_re
        carry_ref[1:2, st] = c_im
        run((jnp.concatenate(init_re, axis=0), jnp.concatenate(init_im, axis=0)), True)

    hlast_ref[0] = carry_ref[...]
    ys = [jnp.dot(buf_ref[:, b * blk_w:(b + 1) * blk_w].astype(BF16), cmat_ref[b],
                  preferred_element_type=F32) for b in range(SSM_BLOCKS)]
    y = _ssm_out(jnp.concatenate(ys, axis=-1), up.astype(F32), d_ref, wglu_ref, bglu_ref)
    y_ref[0] = jnp.dot(permt_ref[...], y.astype(BF16), preferred_element_type=F32).astype(BF16)


def _s5_prompt(u, sp, lc):
    n, l, _ = u.shape
    s = SUBLANES * lc
    rows = jnp.arange(s)
    src = (rows % SUBLANES) * lc + rows // SUBLANES
    perm = jnp.zeros((s, s), BF16).at[rows, src].set(1.0)
    laml = _complex_pow2(sp["lam"], lc)
    kern = functools.partial(_s5_kernel, lc=lc)
    c2 = lambda b, j: (0, 0)
    c3 = lambda b, j: (0, 0, 0)
    return pl.pallas_call(
        kern,
        grid=(n, l // s),
        in_specs=[pl.BlockSpec((1, s, SSM_WIDTH), lambda b, j: (b, j, 0)),
                  pl.BlockSpec((s, s), c2),
                  pl.BlockSpec((s, s), c2),
                  pl.BlockSpec((SSM_BLOCKS, LANES, 2 * BLOCK_STATES), c3),
                  pl.BlockSpec((SSM_BLOCKS, 2 * BLOCK_STATES, LANES), c3),
                  pl.BlockSpec((2, N_STATE), c2),
                  pl.BlockSpec((2, N_STATE), c2),
                  pl.BlockSpec((1, SSM_WIDTH), c2),
                  pl.BlockSpec((SSM_WIDTH, SSM_WIDTH), c2),
                  pl.BlockSpec((1, SSM_WIDTH), c2)],
        out_specs=[pl.BlockSpec((1, s, SSM_WIDTH), lambda b, j: (b, j, 0)),
                   pl.BlockSpec((1, 2, N_STATE), lambda b, j: (b, 0, 0))],
        out_shape=[jax.ShapeDtypeStruct((n, l, SSM_WIDTH), BF16),
                   jax.ShapeDtypeStruct((n, 2, N_STATE), F32)],
        scratch_shapes=[pltpu.VMEM((s, 2 * N_STATE), F32),
                        pltpu.VMEM((2, N_STATE), F32)],
        compiler_params=_cparams(("parallel", "arbitrary")),
        name="s5_scan",
    )(u, perm, perm.T, sp["bmat"], sp["cmat"], sp["lam"], laml, sp["d"], sp["w_glu"], sp["b_glu"])


def _s5_step_kernel(u_ref, h0_ref, bmat_ref, cmat_ref, lam_ref, d_ref, wglu_ref, bglu_ref,
                    y_ref, h_ref):
    u = u_ref[...]
    ys = []
    for b in range(SSM_BLOCKS):
        st = slice(b * BLOCK_STATES, (b + 1) * BLOCK_STATES)
        bu = jnp.dot(u[:, b * LANES:(b + 1) * LANES], bmat_ref[b], preferred_element_type=F32)
        lr = lam_ref[0:1, st]
        li = lam_ref[1:2, st]
        re = h0_ref[0, :, st]
        im = h0_ref[1, :, st]
        n_re = lr * re - li * im + bu[:, :BLOCK_STATES]
        n_im = lr * im + li * re + bu[:, BLOCK_STATES:]
        h_ref[0, :, st] = n_re
        h_ref[1, :, st] = n_im
        ys.append(_mm(jnp.concatenate([n_re, n_im], axis=-1), cmat_ref[b]))
    y = _ssm_out(jnp.concatenate(ys, axis=-1), u.astype(F32), d_ref, wglu_ref, bglu_ref)
    y_ref[...] = y.astype(BF16)


def _s5_step(u, h0, sp):
    rows = u.shape[0]
    return pl.pallas_call(
        _s5_step_kernel,
        out_shape=[jax.ShapeDtypeStruct((rows, SSM_WIDTH), BF16),
                   jax.ShapeDtypeStruct((2, rows, N_STATE), F32)],
        compiler_params=pltpu.CompilerParams(vmem_limit_bytes=VMEM_LIMIT),
        name="s5_step",
    )(u, h0, sp["bmat"], sp["cmat"], sp["lam"], sp["d"], sp["w_glu"], sp["b_glu"])


def _decode_kernel(pt_ref, ql_ref, qr_ref, kvs_ref, krs_ref, lat_hbm, rope_hbm, o_ref,
                   lat_buf, rope_buf, sems, *, n_pages):
    b = pl.program_id(0)
    slot = b % 2

    def page_copies(seq, dst_slot):
        out = []
        for i in range(n_pages):
            page = pt_ref[seq * n_pages + i]
            keys = pl.ds(i * PAGE_SIZE, PAGE_SIZE)
            out.append(pltpu.make_async_copy(lat_hbm.at[page], lat_buf.at[dst_slot, keys, :],
                                             sems.at[dst_slot, 0]))
            out.append(pltpu.make_async_copy(rope_hbm.at[page], rope_buf.at[dst_slot, :, keys],
                                             sems.at[dst_slot, 1]))
        return out

    @pl.when(b == 0)
    def _():
        for cp in page_copies(0, 0):
            cp.start()

    @pl.when(b + 1 < pl.num_programs(0))
    def _():
        for cp in page_copies(b + 1, 1 - slot):
            cp.start()

    for cp in page_copies(b, slot):
        cp.wait()

    ql = ql_ref[0]
    qr = qr_ref[0]
    kvs = kvs_ref[0]
    krs = krs_ref[0]
    lat = lat_buf[slot].astype(BF16)
    s = _mm_nt(ql, lat) + _mm(qr, rope_buf[slot])
    s_self = (jnp.sum(ql.astype(F32) * kvs, axis=-1, keepdims=True)
              + jnp.sum(qr.astype(F32) * krs, axis=-1, keepdims=True))
    m = jnp.maximum(jnp.max(s, axis=-1, keepdims=True), s_self)
    p = jnp.exp2(s - m)
    p_self = jnp.exp2(s_self - m)
    denom = jnp.sum(p, axis=-1, keepdims=True) + p_self
    o_ref[0] = (_mm(p, lat) + p_self * kvs) / denom


def _decode_attn(q_lat, q_rope, kv_self, kr_self, cache_lat, cache_rope_t, page_table):
    b, n_pages = page_table.shape
    kern = functools.partial(_decode_kernel, n_pages=n_pages)
    seq3 = lambda s, pt: (s, 0, 0)
    keys = n_pages * PAGE_SIZE
    grid_spec = pltpu.PrefetchScalarGridSpec(
        num_scalar_prefetch=1,
        grid=(b,),
        in_specs=[pl.BlockSpec((1, N_HEADS, KV_RANK), seq3),
                  pl.BlockSpec((1, N_HEADS, ROPE_DIM), seq3),
                  pl.BlockSpec((1, 1, KV_RANK), seq3),
                  pl.BlockSpec((1, 1, ROPE_DIM), seq3),
                  pl.BlockSpec(memory_space=pl.ANY),
                  pl.BlockSpec(memory_space=pl.ANY)],
        out_specs=pl.BlockSpec((1, N_HEADS, KV_RANK), seq3),
        scratch_shapes=[pltpu.VMEM((2, keys, KV_RANK), F32),
                        pltpu.VMEM((2, ROPE_DIM, keys), F32),
                        pltpu.SemaphoreType.DMA((2, 2))],
    )
    return pl.pallas_call(
        kern,
        grid_spec=grid_spec,
        out_shape=jax.ShapeDtypeStruct((b, N_HEADS, KV_RANK), F32),
        compiler_params=_cparams(("arbitrary",)),
        name="decode_attn",
    )(page_table.reshape(-1), q_lat, q_rope, kv_self, kr_self, cache_lat, cache_rope_t)


def _uv_kernel(o_ref, w_ref, y_ref):
    y_ref[...] = _mm(o_ref[...], w_ref[...]).astype(BF16)


def _uv_proj(o_lat, wuv_bd):
    rows = o_lat.shape[0]
    return pl.pallas_call(
        _uv_kernel,
        out_shape=jax.ShapeDtypeStruct((rows, ATTN_WIDTH), BF16),
        compiler_params=pltpu.CompilerParams(vmem_limit_bytes=VMEM_LIMIT),
        name="uv_proj",
    )(o_lat, wuv_bd)


def _outproj_kernel(oa_ref, ys_ref, x_ref, g1_ref, sc_ref, sh_ref, wout_ref, gpost_ref, gpre_ref,
                    wr_ref, br_ref, *rest, row0):
    x1_ref, h2_ref, route_ref = rest[-3:]
    mix = (jnp.dot(oa_ref[0], wout_ref[:ATTN_WIDTH].astype(BF16), preferred_element_type=F32)
           + jnp.dot(ys_ref[0], wout_ref[ATTN_WIDTH:].astype(BF16), preferred_element_type=F32))
    x1 = x_ref[0] + g1_ref[0] * _rms(mix, gpost_ref[...], D_MODEL)
    x1_ref[0] = x1
    h2 = _rms(x1, gpre_ref[...], D_MODEL) * (1.0 + sc_ref[0]) + sh_ref[0]
    for s in range(D_MODEL // LANES):
        h2_ref[:, s, :] = h2[:, s * LANES:(s + 1) * LANES]
    logits = _mm3(h2, wr_ref[...]) + br_ref[...]
    lane = lax.broadcasted_iota(jnp.int32, logits.shape, 1).astype(F32)
    vals, idxs = [], []
    for _ in range(TOP_K):
        v = jnp.max(logits, axis=-1, keepdims=True)
        idx = jnp.min(jnp.where(logits == v, lane, float(LANES)), axis=-1, keepdims=True)
        vals.append(v)
        idxs.append(idx)
        logits = jnp.where(lane == idx, -jnp.inf, logits)
    exps = [jnp.exp(v - vals[0]) for v in vals]
    denom = exps[0] + exps[1] + exps[2] + exps[3]
    tm = logits.shape[0]
    first = row0 + (pl.program_id(0) * pl.num_programs(1) + pl.program_id(1)) * tm
    token = (first + lax.broadcasted_iota(jnp.int32, (tm, 1), 0)).astype(F32)
    route = jnp.where(lane == float(ROUTE_TOKEN), token, jnp.where(lane == float(ROUTE_VALID), 1.0, 0.0))
    for k in range(TOP_K):
        route = jnp.where(lane == float(k), idxs[k], route)
        route = jnp.where(lane == float(TOP_K + k), exps[k] / denom, route)
    route_ref[...] = route


def _outproj(o_attn, y_ssm, x, g1, sc2, sh2, p, tm, t_all, row0, dest=None):
    n, l, _ = x.shape
    per_row = g1.shape[1] != 1
    mod_blk = (1, tm, D_MODEL) if per_row else (1, 1, D_MODEL)
    mod_map = (lambda b, i: (b, i, 0)) if per_row else (lambda b, i: (b, 0, 0))
    row_map = lambda b, i: (b, i, 0)
    const2 = lambda b, i: (0, 0)
    blk0, per_seq = row0 // tm, l // tm
    n_tile = D_MODEL // LANES
    in_specs = [pl.BlockSpec((1, tm, ATTN_WIDTH), row_map),
                pl.BlockSpec((1, tm, SSM_WIDTH), row_map),
                pl.BlockSpec((1, tm, D_MODEL), row_map),
                pl.BlockSpec(mod_blk, mod_map),
                pl.BlockSpec(mod_blk, mod_map),
                pl.BlockSpec(mod_blk, mod_map),
                pl.BlockSpec((D_MODEL, D_MODEL), const2),
                pl.BlockSpec((1, D_MODEL), const2),
                pl.BlockSpec((1, D_MODEL), const2),
                pl.BlockSpec((D_MODEL, LANES), const2),
                pl.BlockSpec((1, LANES), const2)]
    args = [o_attn, y_ssm, x, g1, sc2, sh2, p["w_out"], p["g_post_mix"], p["g_pre_ffn"],
            p["w_router"], p["b_router"]]
    aliases = {}
    if dest is not None:
        aliases = {len(args): 1, len(args) + 1: 2}
        in_specs += [pl.BlockSpec(memory_space=pl.ANY), pl.BlockSpec(memory_space=pl.ANY)]
        args += list(dest)
    return pl.pallas_call(
        functools.partial(_outproj_kernel, row0=row0),
        grid=(n, per_seq),
        in_specs=in_specs,
        out_specs=[pl.BlockSpec((1, tm, D_MODEL), row_map),
                   pl.BlockSpec((tm, n_tile, LANES), lambda b, i: (blk0 + b * per_seq + i, 0, 0)),
                   pl.BlockSpec((tm, LANES), lambda b, i: (blk0 + b * per_seq + i, 0))],
        out_shape=[jax.ShapeDtypeStruct((n, l, D_MODEL), F32),
                   jax.ShapeDtypeStruct((t_all, n_tile, LANES), F32),
                   jax.ShapeDtypeStruct((t_all, LANES), F32)],
        input_output_aliases=aliases,
        compiler_params=_cparams(("parallel", "parallel")),
        name="outproj",
    )(*args)


def _moe_prep_kernel(w1_ref, w2_ref, perm_ref, w1o_ref, w2o_ref):
    for j in range(w1_ref.shape[2] // GLU_BLOCK):
        cols = slice(j * GLU_BLOCK, (j + 1) * GLU_BLOCK)
        w1o_ref[0, :, cols] = jnp.dot(w1_ref[0, :, cols].astype(BF16), perm_ref[...],
                                      preferred_element_type=F32).astype(BF16)
    w2o_ref[0] = w2_ref[0].astype(BF16)


def _moe_prep(w1, w2):
    n_e, d_in, d_hid = w1.shape
    d_exp = w2.shape[1]
    j = jnp.arange(GLU_BLOCK)
    src = jnp.where(j < LANES, 2 * j, 2 * (j - LANES) + 1)
    perm = (j[:, None] == src[None, :]).astype(BF16)
    e_map = lambda e: (e, 0, 0)
    return pl.pallas_call(
        _moe_prep_kernel,
        grid=(n_e,),
        in_specs=[pl.BlockSpec((1, d_in, d_hid), e_map),
                  pl.BlockSpec((1, d_exp, D_MODEL), e_map),
                  pl.BlockSpec((GLU_BLOCK, GLU_BLOCK), lambda e: (0, 0))],
        out_specs=[pl.BlockSpec((1, d_in, d_hid), e_map),
                   pl.BlockSpec((1, d_exp, D_MODEL), e_map)],
        out_shape=[jax.ShapeDtypeStruct(w1.shape, BF16),
                   jax.ShapeDtypeStruct(w2.shape, BF16)],
        compiler_params=_cparams(("parallel",)),
        name="moe_prep",
    )(w1, w2, perm)


def _sort_rows_kernel(pos_ref, route_ref, rows_ref):
    rows_ref[...] = jnp.zeros(rows_ref.shape, F32)

    def body(t, carry):
        row = route_ref[0, pl.ds(t, 1), :]
        for k in range(TOP_K):
            rows_ref[0, pl.ds(pos_ref[0, 0, TOP_K * t + k], 1), :] = row
        return carry
    lax.fori_loop(0, route_ref.shape[1], body, 0, unroll=4)


def _sort_rows(route, pos, nch, rc):
    t = route.shape[0]
    tc = t // nch
    return pl.pallas_call(
        _sort_rows_kernel,
        grid=(nch,),
        in_specs=[pl.BlockSpec((1, 1, tc * TOP_K), lambda c: (c, 0, 0), memory_space=pltpu.SMEM),
                  pl.BlockSpec((1, tc, LANES), lambda c: (c, 0, 0))],
        out_specs=pl.BlockSpec((1, rc, LANES), lambda c: (c, 0, 0)),
        out_shape=jax.ShapeDtypeStruct((nch, rc, LANES), F32),
        compiler_params=_cparams(("parallel",)),
        name="moe_sort_rows",
    )(pos.reshape(nch, 1, tc * TOP_K), route.reshape(nch, tc, LANES))


def _route_plan(route, nch, tmx):
    idx = route[:, :TOP_K].astype(jnp.int32)
    t = idx.shape[0]
    tc = t // nch
    nt = -(-(tc * TOP_K + N_EXPERTS * (tmx - 1)) // tmx)
    i32 = jnp.int32
    hot = (idx[:, :, None] == jnp.arange(N_EXPERTS, dtype=i32)).astype(i32).sum(1)
    hot = hot.reshape(nch, tc, N_EXPERTS)
    csum = jnp.cumsum(hot, axis=1)
    rank = (csum - hot).reshape(t, N_EXPERTS)
    padded = -(-csum[:, -1, :] // tmx) * tmx
    off_end = jnp.cumsum(padded, axis=1)
    off = off_end - padded
    n_tiles = (off_end[:, -1] // tmx).astype(i32)
    chunk = jnp.arange(t, dtype=i32) // tc
    pos = jnp.take_along_axis(off[chunk] + rank, idx, axis=1).astype(i32)
    rc = nt * tmx
    rows = _sort_rows(route, pos, nch, rc)
    valid = rows[:, :, ROUTE_VALID] > 0.0
    row_tok = rows[:, :, ROUTE_TOKEN].astype(i32) % tc
    row_dst = jnp.where(valid, row_tok, tc + jnp.arange(rc, dtype=i32)[None, :] % SUBLANES)
    tile = jnp.arange(nt, dtype=i32)
    texp_raw = (off_end[:, None, :] <= (tile * tmx)[None, :, None]).sum(-1).astype(i32)
    tile_c = jnp.minimum(tile[None, :], n_tiles[:, None] - 1)
    tile_n = jnp.minimum(tile[None, :] + 1, n_tiles[:, None] - 1)
    texp = jnp.take_along_axis(texp_raw, tile_c, axis=1)
    base = (jnp.arange(nch, dtype=i32) * nt)[:, None]
    return dict(nt=nt, n_tiles=n_tiles, texp=texp.reshape(-1), tblk=(tile_c + base).reshape(-1),
                tnext=(tile_n + base).reshape(-1),
                row_tok=row_tok.reshape(nch * nt, 1, tmx), row_dst=row_dst.reshape(nch * nt, 1, tmx),
                rows=rows.reshape(nch * nt, tmx, LANES))


def _moe_kernel(nt_ref, texp_ref, tblk_ref, tnext_ref, tok_ref, toknext_ref, dst_ref, rows_ref, src_ref,
                w1_ref, b1_ref, w2_ref, b2_ref, out_ref, xs_ref, yt_ref):
    del tblk_ref, tnext_ref
    c = pl.program_id(0)
    i = pl.program_id(1)
    tmx = rows_ref.shape[1]
    slot = i % 2

    @pl.when(i == 0)
    def _():
        out_ref[...] = jnp.zeros(out_ref.shape, F32)

        def gather(r, carry):
            xs_ref[0, pl.ds(pl.multiple_of(r * SUBLANES, SUBLANES), SUBLANES), :] = src_ref[0, tok_ref[0, 0, r]]
            return carry
        lax.fori_loop(0, tmx, gather, 0, unroll=8)

    @pl.when(i < nt_ref[c])
    def _():
        x = jnp.concatenate([xs_ref[slot, pl.ds(s, tmx, stride=SUBLANES), :]
                             for s in range(D_MODEL // LANES)], axis=-1).astype(BF16)
        n_blk = w1_ref.shape[2] // (2 * GLU_BLOCK)
        wide = 2 * GLU_BLOCK

        def hidden(j):
            cols = slice(j * wide, (j + 1) * wide)
            return jnp.dot(x, w1_ref[0, :, cols], preferred_element_type=F32) + b1_ref[0, :, cols]

        hid = hidden(0)
        y = None
        for j in range(n_blk):
            nxt = hidden(j + 1) if j + 1 < n_blk else None
            for r in range(j * tmx // n_blk, (j + 1) * tmx // n_blk):
                xs_ref[1 - slot, r * SUBLANES:(r + 1) * SUBLANES, :] = src_ref[0, toknext_ref[0, 0, r]]
            acts = []
            for b in range(2):
                glu = jnp.minimum(hid[:, b * GLU_BLOCK:b * GLU_BLOCK + LANES], SWIGLU_LIMIT)
                lin = jnp.clip(hid[:, b * GLU_BLOCK + LANES:(b + 1) * GLU_BLOCK], -SWIGLU_LIMIT, SWIGLU_LIMIT)
                acts.append(glu * _sigmoid(SWIGLU_ALPHA * glu) * (lin + 1.0))
            part = _mm(jnp.concatenate(acts, axis=-1), w2_ref[0, j * GLU_BLOCK:(j + 1) * GLU_BLOCK, :])
            y = part if y is None else y + part
            hid = nxt
        meta = rows_ref[0]
        expert = texp_ref[c * pl.num_programs(1) + i].astype(F32)
        lane = lax.broadcasted_iota(jnp.int32, meta.shape, 1)
        hit = jnp.logical_and(lane < TOP_K, meta == expert)
        gates = pltpu.roll(meta, LANES - TOP_K, axis=1)
        gate = jnp.sum(jnp.where(hit, gates, 0.0), axis=-1, keepdims=True)
        y = gate * (y + b2_ref[0])
        for s in range(D_MODEL // LANES):
            yt_ref[pl.ds(s, tmx, stride=SUBLANES), :] = y[:, s * LANES:(s + 1) * LANES]
        for g in range(tmx // SUBLANES):
            toks = [dst_ref[0, 0, g * SUBLANES + u] for u in range(SUBLANES)]
            sums = [out_ref[0, toks[u]]
                    + yt_ref[(g * SUBLANES + u) * SUBLANES:(g * SUBLANES + u + 1) * SUBLANES, :]
                    for u in range(SUBLANES)]
            for u in range(SUBLANES):
                out_ref[0, toks[u]] = sums[u]


def _moe_experts(h2, plan, w1p, b1p, w2p, b2, tmx):
    nch, tc = h2.shape[:2]
    nt = plan["nt"]
    n_e, _, d_hid = w1p.shape
    d_exp = w2p.shape[1]
    flat = lambda c, i: c * nt + i
    cur = lambda c, i, n, te, tb, tn: (tb[flat(c, i)], 0, 0)
    nxt = lambda c, i, n, te, tb, tn: (tn[flat(c, i)], 0, 0)
    exp = lambda c, i, n, te, tb, tn: (te[flat(c, i)], 0, 0)
    grid_spec = pltpu.PrefetchScalarGridSpec(
        num_scalar_prefetch=4,
        grid=(nch, nt),
        in_specs=[pl.BlockSpec((1, 1, tmx), cur, memory_space=pltpu.SMEM),
                  pl.BlockSpec((1, 1, tmx), nxt, memory_space=pltpu.SMEM),
                  pl.BlockSpec((1, 1, tmx), cur, memory_space=pltpu.SMEM),
                  pl.BlockSpec((1, tmx, LANES), cur),
                  pl.BlockSpec((1, tc, D_MODEL // LANES, LANES), lambda c, i, n, te, tb, tn: (c, 0, 0, 0),
                               pipeline_mode=pl.Buffered(1)),
                  pl.BlockSpec((1, D_MODEL, d_hid), exp),
                  pl.BlockSpec((1, 1, d_hid), exp),
                  pl.BlockSpec((1, d_exp, D_MODEL), exp),
                  pl.BlockSpec((1, 1, D_MODEL), exp)],
        out_specs=pl.BlockSpec((1, tc + SUBLANES, D_MODEL // LANES, LANES),
                               lambda c, i, n, te, tb, tn: (c, 0, 0, 0), pipeline_mode=pl.Buffered(1)),
        scratch_shapes=[pltpu.VMEM((2, tmx * (D_MODEL // LANES), LANES), F32),
                        pltpu.VMEM((tmx * (D_MODEL // LANES), LANES), F32)],
    )
    return pl.pallas_call(
        _moe_kernel,
        grid_spec=grid_spec,
        out_shape=jax.ShapeDtypeStruct((nch, tc + SUBLANES, D_MODEL // LANES, LANES), F32),
        compiler_params=_cparams(("parallel", "arbitrary")),
        name="moe_experts",
    )(plan["n_tiles"], plan["texp"], plan["tblk"], plan["tnext"], plan["row_tok"], plan["row_tok"],
      plan["row_dst"], plan["rows"], h2, w1p, b1p, w2p, b2)


def _final_kernel(x1_ref, f_ref, g2_ref, gpost_ref, y_ref):
    tm = x1_ref.shape[1]
    f = jnp.concatenate([f_ref[pl.ds(s, tm, stride=SUBLANES), :] for s in range(D_MODEL // LANES)],
                        axis=-1)
    y_ref[0] = x1_ref[0] + g2_ref[0] * _rms(f, gpost_ref[...], D_MODEL)


def _final(x1, f_all, row0, g2, g_post_ffn, tm):
    n, l, _ = x1.shape
    per_row = g2.shape[1] != 1
    mod_blk = (1, tm, D_MODEL) if per_row else (1, 1, D_MODEL)
    mod_map = (lambda b, i: (b, i, 0)) if per_row else (lambda b, i: (b, 0, 0))
    row_map = lambda b, i: (b, i, 0)
    blk0, per_seq = row0 // tm, l // tm
    return pl.pallas_call(
        _final_kernel,
        grid=(n, l // tm),
        in_specs=[pl.BlockSpec((1, tm, D_MODEL), row_map),
                  pl.BlockSpec((tm * (D_MODEL // LANES), LANES), lambda b, i: (blk0 + b * per_seq + i, 0)),
                  pl.BlockSpec(mod_blk, mod_map),
                  pl.BlockSpec((1, D_MODEL), lambda b, i: (0, 0))],
        out_specs=pl.BlockSpec((1, tm, D_MODEL), row_map),
        out_shape=jax.ShapeDtypeStruct((n, l, D_MODEL), F32),
        compiler_params=_cparams(("parallel", "parallel")),
        name="final",
    )(x1, f_all, g2, g_post_ffn)


def _complex_pow2(lam, n):
    re, im = lam[0], lam[1]
    steps = int(round(math.log2(n)))
    assert 2 ** steps == n
    for _ in range(steps):
        re, im = re * re - im * im, 2.0 * re * im
    return jnp.stack([re, im])


def _rope_tables(pos):
    half = ROPE_DIM // 2
    inv = ROPE_THETA ** (-jnp.arange(half, dtype=F32) / half)
    ang = pos.astype(F32)[:, None] * inv[None, :]
    cos, sin = jnp.cos(ang), jnp.sin(ang)
    cos4 = jnp.tile(jnp.concatenate([cos, cos], axis=-1), (1, LANES // ROPE_DIM))
    sin4 = jnp.tile(jnp.concatenate([-sin, sin], axis=-1), (1, LANES // ROPE_DIM))
    return cos4[None], sin4[None]


def _swap_halves(w):
    half = ROPE_DIM // 2
    return jnp.concatenate([w[..., half:], w[..., :half]], axis=-1)


def _prep_layer(w, layer):
    g = lambda name: w[name][layer]
    p = {}
    row = lambda v: v.reshape(1, -1)
    p["g_pre_mix"] = row(g("g_pre_mix"))
    w_in = g("w_in")
    w_q, w_kv, w_r, w_u = jnp.split(w_in, [Q_RANK, Q_RANK + KV_RANK, Q_RANK + KV_RANK + ROPE_DIM], axis=-1)
    rep = LANES // ROPE_DIM
    p["w_in"] = jnp.concatenate(
        [w_q, jnp.zeros((D_MODEL, Q_PAD - Q_RANK), F32), w_kv,
         jnp.tile(w_r, (1, rep)), jnp.tile(_swap_halves(w_r), (1, rep)), w_u], axis=-1)
    p["g_q"] = row(jnp.pad(g("g_q"), (0, Q_PAD - Q_RANK)))
    w_uq = g("w_uq").reshape(Q_RANK, N_HEADS, NOPE_DIM + ROPE_DIM)
    nope = w_uq[:, :, :NOPE_DIM].reshape(Q_RANK, N_HEADS * NOPE_DIM)
    rope = w_uq[:, :, NOPE_DIM:]
    rope_sw = _swap_halves(rope)
    w_uq_p = jnp.concatenate([nope, rope.reshape(Q_RANK, -1), rope_sw.reshape(Q_RANK, -1)], axis=-1)
    p["w_uq"] = jnp.pad(w_uq_p, ((0, Q_PAD - Q_RANK), (0, 0)))
    w_uk = jnp.transpose(g("w_uk"), (1, 2, 0))
    z = jnp.zeros((NOPE_DIM, KV_RANK), F32)
    p["w_uk"] = jnp.stack([jnp.block([[w_uk[2 * j], z], [z, w_uk[2 * j + 1]]])
                           for j in range(N_HEADS // 2)])
    p["g_kv"] = row(g("g_kv"))
    w_uv = jnp.transpose(g("w_uv"), (1, 0, 2))
    p["w_uv_t"] = jnp.transpose(w_uv, (0, 2, 1))
    eye_h = jnp.eye(N_HEADS, dtype=F32)
    p["w_uv_bd"] = jnp.einsum("hrv,hk->hrkv", w_uv, eye_h).reshape(N_HEADS * KV_RANK, ATTN_WIDTH)
    p["w_out"] = g("w_out")
    p["g_post_mix"] = row(g("g_post_mix"))
    p["g_pre_ffn"] = row(g("g_pre_ffn"))
    p["w_router"] = jnp.pad(g("w_router"), ((0, 0), (0, LANES - N_EXPERTS)))
    p["b_router"] = row(jnp.pad(g("b_router"), (0, LANES - N_EXPERTS), constant_values=NEG_INF))
    p["g_post_ffn"] = row(g("g_post_ffn"))

    lam_re, lam_im = g("ssm_lam_re"), g("ssm_lam_im")
    step = jnp.exp(g("ssm_log_step"))[:, None]
    mag = jnp.exp(lam_re * step)
    bar_re, bar_im = mag * jnp.cos(lam_im * step), mag * jnp.sin(lam_im * step)
    den = lam_re * lam_re + lam_im * lam_im
    co_re = ((bar_re - 1.0) * lam_re + bar_im * lam_im) / den
    co_im = (bar_im * lam_re - (bar_re - 1.0) * lam_im) / den
    b_re, b_im = g("ssm_b_re"), g("ssm_b_im")
    bb_re = co_re[..., None] * b_re - co_im[..., None] * b_im
    bb_im = co_re[..., None] * b_im + co_im[..., None] * b_re
    eye_g = jnp.eye(GROUPS_PER_BLOCK, dtype=F32)

    def b_block(m):
        m = m.reshape(SSM_BLOCKS, GROUPS_PER_BLOCK, SSM_STATE, SSM_GROUP)
        return jnp.einsum("bgpc,gh->bgchp", m, eye_g).reshape(SSM_BLOCKS, LANES, BLOCK_STATES)

    def c_block(m):
        m = m.reshape(SSM_BLOCKS, GROUPS_PER_BLOCK, SSM_GROUP, SSM_STATE)
        return jnp.einsum("bgcp,gh->bgphc", m, eye_g).reshape(SSM_BLOCKS, BLOCK_STATES, LANES)

    sp = {
        "lam": jnp.stack([bar_re.reshape(-1), bar_im.reshape(-1)]),
        "bmat": jnp.concatenate([b_block(bb_re), b_block(bb_im)], axis=-1).astype(BF16),
        "cmat": jnp.concatenate([c_block(g("ssm_c_re")), c_block(-g("ssm_c_im"))], axis=1).astype(BF16),
        "d": row(g("ssm_d")),
        "w_glu": g("ssm_w_glu"),
        "b_glu": row(g("ssm_b_glu")),
    }
    b1 = g("b1")
    n_blk = b1.shape[1] // GLU_BLOCK
    mp = {
        "b1": jnp.transpose(b1.reshape(N_EXPERTS, n_blk, LANES, 2), (0, 1, 3, 2)).reshape(N_EXPERTS, 1, -1),
        "b2": g("b2")[:, None, :],
    }
    return p, sp, mp


def _mods(mod_rows):
    return [mod_rows[:, k * D_MODEL:(k + 1) * D_MODEL] for k in range(N_MOD)]


def _pick_tile(n, target, mult):
    best = mult
    for c in range(mult, min(n, target) + 1, mult):
        if n % c == 0:
            best = c
    return best


def kernel(x_prompt, x_sample, c_prompt, c_sample, cache_kv_latent, cache_k_rope, state_ssm, page_table, w_ada, b_ada, g_pre_mix, w_in, g_q, w_uq, w_uk, g_kv, w_uv, ssm_lam_re, ssm_lam_im, ssm_log_step, ssm_b_re, ssm_b_im, ssm_c_re, ssm_c_im, ssm_d, ssm_w_glu, ssm_b_glu, w_out, g_post_mix, g_pre_ffn, w_router, b_router, w1, b1, w2, b2, g_post_ffn):
    weights = dict(w_ada=w_ada, b_ada=b_ada, g_pre_mix=g_pre_mix, w_in=w_in, g_q=g_q, w_uq=w_uq,
                   w_uk=w_uk, g_kv=g_kv, w_uv=w_uv, ssm_lam_re=ssm_lam_re, ssm_lam_im=ssm_lam_im,
                   ssm_log_step=ssm_log_step, ssm_b_re=ssm_b_re, ssm_b_im=ssm_b_im,
                   ssm_c_re=ssm_c_re, ssm_c_im=ssm_c_im, ssm_d=ssm_d, ssm_w_glu=ssm_w_glu,
                   ssm_b_glu=ssm_b_glu, w_out=w_out, g_post_mix=g_post_mix, g_pre_ffn=g_pre_ffn,
                   w_router=w_router, b_router=b_router, w1=w1, b1=b1, w2=w2, b2=b2,
                   g_post_ffn=g_post_ffn)
    depth = w_ada.shape[0]
    n_p, l_p, _ = x_prompt.shape
    n_s, l_s, _ = x_sample.shape
    assert l_s == 1
    n_pages = page_table.shape[1]
    past_len = n_pages * PAGE_SIZE

    cos_p, sin_p = _rope_tables(jnp.arange(l_p, dtype=jnp.int32))
    cos_s, sin_s = _rope_tables(jnp.full((1,), past_len, jnp.int32))
    cos_s = jnp.broadcast_to(cos_s, (1, n_s, LANES))
    sin_s = jnp.broadcast_to(sin_s, (1, n_s, LANES))

    tm_p = _pick_tile(l_p, 512, 16)
    tq = _pick_tile(l_p, 256, 16)
    tk = _pick_tile(l_p, 1024, 16)
    lc = 64 if l_p % 512 == 0 else l_p // SUBLANES
    c_rows = n_p + n_s
    c_pad = -c_rows % SUBLANES
    c_all = jnp.pad(jnp.concatenate([c_prompt, c_sample], axis=0), ((0, c_pad), (0, 0)))

    y_p = x_prompt
    y_s = x_sample.reshape(1, n_s, D_MODEL)
    outs = [[] for _ in range(6)]
    for layer in range(depth):
        p, sp, mp = _prep_layer(weights, layer)
        mod = _ada(c_all, w_ada[layer], b_ada[layer].reshape(1, -1))
        sh1p, sc1p, g1p, sh2p, sc2p, g2p = [m[:, None, :] for m in _mods(mod[:n_p])]
        sh1s, sc1s, g1s, sh2s, sc2s, g2s = [m[None] for m in _mods(mod[n_p:c_rows])]

        q_t, k_cat, v_t, kv_p, kr_p, u_p = _inproj(y_p, sc1p, sh1p, p, cos_p, sin_p, tm_p)
        o_attn_p = _flash(q_t, k_cat, v_t, p["w_uv_t"], tq, tk)
        y_ssm_p, h_last_p = _s5_prompt(u_p, sp, lc)
        t_p = n_p * l_p
        t_all = t_p + n_s
        blank = (jnp.zeros((t_all, D_MODEL // LANES, LANES), F32), jnp.zeros((t_all, LANES), F32))
        x1_p, h2_all, route = _outproj(o_attn_p, y_ssm_p, y_p, g1p, sc2p, sh2p, p, tm_p, t_all, 0,
                                       dest=blank)

        q_t_s, _, _, kv_s, kr_s, u_s = _inproj(y_s, sc1s, sh1s, p, cos_s, sin_s, n_s)
        q_lat_s = jnp.transpose(q_t_s[0, :, :KV_RANK, :], (2, 0, 1))
        q_rope_s = jnp.stack(
            [q_t_s[0, h, KV_RANK + ROPE_DIM * (h % 4):KV_RANK + ROPE_DIM * (h % 4 + 1), :].T
             for h in range(N_HEADS)], axis=1)
        o_lat_s = _decode_attn(q_lat_s, q_rope_s, kv_s.reshape(n_s, 1, KV_RANK),
                               kr_s.reshape(n_s, 1, ROPE_DIM), cache_kv_latent[layer],
                               jnp.swapaxes(cache_k_rope[layer], 1, 2), page_table)
        o_attn_s = _uv_proj(o_lat_s.reshape(n_s, N_HEADS * KV_RANK), p["w_uv_bd"])
        h0 = jnp.moveaxis(state_ssm[layer].reshape(n_s, N_STATE, 2), -1, 0)
        y_ssm_s, h_s = _s5_step(u_s[0], h0, sp)
        x1_s, h2_all, route = _outproj(o_attn_s[None], y_ssm_s[None], y_s, g1s, sc2s, sh2s, p, n_s,
                                       t_all, t_p, dest=(h2_all, route))

        nch = 4 if t_all % (4 * SUBLANES) == 0 else (2 if t_all % (2 * SUBLANES) == 0 else 1)
        tc = t_all // nch
        plan = _route_plan(route, nch, MOE_TILE)
        w1p, w2p = _moe_prep(w1[layer], w2[layer])
        f_chunks = _moe_experts(h2_all.reshape(nch, tc, D_MODEL // LANES, LANES), plan, w1p, mp["b1"],
                                w2p, mp["b2"], MOE_TILE)
        f_all = f_chunks[:, :tc].reshape(t_all * (D_MODEL // LANES), LANES)
        y_p = _final(x1_p, f_all, 0, g2p, p["g_post_ffn"], tm_p)
        y_s = _final(x1_s, f_all, t_p, g2s, p["g_post_ffn"], n_s)

        outs[0].append(kv_p)
        outs[1].append(kr_p)
        outs[2].append(jnp.moveaxis(h_last_p, 1, -1).reshape(n_p, N_SSM_GROUPS, SSM_STATE, 2))
        outs[3].append(kv_s.reshape(n_s, 1, KV_RANK))
        outs[4].append(kr_s.reshape(n_s, 1, ROPE_DIM))
        outs[5].append(jnp.moveaxis(h_s, 0, -1).reshape(n_s, N_SSM_GROUPS, SSM_STATE, 2))
    return (y_p, y_s.reshape(n_s, 1, D_MODEL)) + tuple(jnp.stack(o) for o in outs)
```

```python
import functools
import math

import jax
import jax.numpy as jnp
from jax import lax
from jax.experimental import pallas as pl
from jax.experimental.pallas import tpu as pltpu

F32 = jnp.float32
BF16 = jnp.bfloat16

D_MODEL = 1024
N_HEADS = 8
NOPE_DIM = 64
ROPE_DIM = 32
V_DIM = 64
Q_RANK = 352
KV_RANK = 128
ROPE_THETA = 10000.0
ATTN_WIDTH = N_HEADS * V_DIM
ATTN_SCALE = 1.0 / math.sqrt(NOPE_DIM + ROPE_DIM)
SSM_WIDTH = D_MODEL - ATTN_WIDTH
SSM_GROUP = 16
N_SSM_GROUPS = SSM_WIDTH // SSM_GROUP
SSM_STATE = 64
N_STATE = N_SSM_GROUPS * SSM_STATE
N_EXPERTS = 32
TOP_K = 4
SWIGLU_LIMIT = 7.0
SWIGLU_ALPHA = 1.702
RMS_EPS = 1e-6
N_MOD = 6
NEG_INF = -1e30
PAGE_SIZE = 128

LANES = 128
SUBLANES = 8
VMEM_LIMIT = 56 * 1024 * 1024

Q_PAD = 384
CAT = 2 * LANES
SSM_BLOCKS = 4
GROUPS_PER_BLOCK = N_SSM_GROUPS // SSM_BLOCKS
BLOCK_STATES = GROUPS_PER_BLOCK * SSM_STATE
IN_PAD = Q_PAD + KV_RANK + 2 * LANES + SSM_WIDTH
UQ_PAD = N_HEADS * NOPE_DIM + 4 * LANES
GLU_BLOCK = 2 * LANES
MOE_TILE = 192
MOE_COL_PARTS = 2
LOG2E = math.log2(math.e)
FLASH_LOOKAHEAD = 3
FLASH_DIAG_SUB = 256
VT_ROWS = KV_RANK + 16
ROUTE_TOKEN = 2 * TOP_K
ROUTE_VALID = 2 * TOP_K + 1


def _cparams(sem):
    return pltpu.CompilerParams(dimension_semantics=sem, vmem_limit_bytes=VMEM_LIMIT)


def _mm(a, b):
    return jnp.dot(a.astype(BF16), b.astype(BF16), preferred_element_type=F32)


def _mm_nt(a, b):
    return lax.dot_general(a.astype(BF16), b.astype(BF16), (((1,), (1,)), ((), ())),
                           preferred_element_type=F32)


def _split_bf16(x):
    hi = x.astype(BF16)
    lo = (x - hi.astype(F32)).astype(BF16)
    return hi, lo


def _mm3(a, b):
    ah, al = _split_bf16(a)
    bh, bl = _split_bf16(b)
    d = functools.partial(jnp.dot, preferred_element_type=F32)
    return d(ah, bh) + (d(ah, bl) + d(al, bh))


def _sigmoid(x):
    return 1.0 / (1.0 + jnp.exp(-x))


def _rms(x, g, width):
    ms = jnp.sum(x * x, axis=-1, keepdims=True) * (1.0 / width)
    return x * lax.rsqrt(ms + RMS_EPS) * g


def _gelu_tanh(x):
    return 0.5 * x * (1.0 + jnp.tanh(math.sqrt(2.0 / math.pi) * (x + 0.044715 * (x * x * x))))


def _ada_kernel(c_ref, w_ref, b_ref, o_ref):
    c = c_ref[...]
    o_ref[...] = _mm3(c * _sigmoid(c), w_ref[...]) + b_ref[...]


def _ada(c_all, w_ada, b_ada):
    rows = c_all.shape[0]
    tn = 512
    n_out = w_ada.shape[1]
    return pl.pallas_call(
        _ada_kernel,
        grid=(n_out // tn,),
        in_specs=[pl.BlockSpec((rows, D_MODEL), lambda j: (0, 0)),
                  pl.BlockSpec((D_MODEL, tn), lambda j: (0, j)),
                  pl.BlockSpec((1, tn), lambda j: (0, j))],
        out_specs=pl.BlockSpec((rows, tn), lambda j: (0, j)),
        out_shape=jax.ShapeDtypeStruct((rows, n_out), F32),
        compiler_params=_cparams(("parallel",)),
        name="adaln",
    )(c_all, w_ada, b_ada)


def _inproj_kernel(x_ref, sc_ref, sh_ref, gpre_ref, win_ref, gq_ref, wuq_ref, wuk_ref, gkv_ref,
                   cos_ref, sin_ref, q_ref, kcat_ref, vt_ref, kv_ref, kr_ref, u_ref):
    x = x_ref[0]
    h = _rms(x, gpre_ref[...], D_MODEL) * (1.0 + sc_ref[0]) + sh_ref[0]
    z = _mm(h, win_ref[...])
    zq = z[:, :Q_PAD]
    zkv = z[:, Q_PAD:Q_PAD + KV_RANK]
    zr = z[:, Q_PAD + KV_RANK:Q_PAD + KV_RANK + LANES]
    zrs = z[:, Q_PAD + KV_RANK + LANES:Q_PAD + KV_RANK + 2 * LANES]
    u = z[:, Q_PAD + KV_RANK + 2 * LANES:]
    cos4 = cos_ref[0]
    sin4 = sin_ref[0]

    kvl = _rms(zkv, gkv_ref[...], KV_RANK)
    kr4 = zr * cos4 + zrs * sin4
    kv_ref[0] = kvl
    kr_ref[0] = kr4[:, :ROPE_DIM]
    kcat_ref[0] = jnp.concatenate([kvl, kr4], axis=-1).astype(BF16)
    extra = (lax.broadcasted_iota(jnp.int32, (VT_ROWS - KV_RANK, kvl.shape[0]), 0) == 0).astype(F32)
    vt_ref[0] = jnp.concatenate([kvl.T, extra], axis=0).astype(BF16)
    u_ref[0] = u.astype(BF16)

    qn = _rms(zq, gq_ref[...], Q_RANK)
    q = _mm(qn, wuq_ref[...])
    nope_w = N_HEADS * NOPE_DIM
    rope_a = q[:, nope_w:nope_w + LANES] * cos4 + q[:, nope_w + 2 * LANES:nope_w + 3 * LANES] * sin4
    rope_b = q[:, nope_w + LANES:nope_w + 2 * LANES] * cos4 + q[:, nope_w + 3 * LANES:] * sin4
    slot = lax.broadcasted_iota(jnp.int32, (1, LANES), 1) // ROPE_DIM
    for j in range(N_HEADS // 2):
        ql = _mm(q[:, j * LANES:(j + 1) * LANES], wuk_ref[j])
        for hh in range(2):
            head = 2 * j + hh
            rope = rope_a if head < 4 else rope_b
            rope_h = jnp.where(slot == (head % 4), rope, 0.0)
            qc = jnp.concatenate([ql[:, hh * KV_RANK:(hh + 1) * KV_RANK], rope_h], axis=-1)
            q_ref[0, head] = (qc * (ATTN_SCALE * LOG2E)).T.astype(BF16)


def _inproj(x, sc, sh, p, cos4, sin4, tm):
    n, l, _ = x.shape
    per_row = sc.shape[1] != 1
    mod_blk = (1, tm, D_MODEL) if per_row else (1, 1, D_MODEL)
    mod_map = (lambda b, i: (b, i, 0)) if per_row else (lambda b, i: (b, 0, 0))
    const2 = lambda b, i: (0, 0)
    const3 = lambda b, i: (0, 0, 0)
    pos_map = lambda b, i: (0, i, 0)
    return pl.pallas_call(
        _inproj_kernel,
        grid=(n, l // tm),
        in_specs=[pl.BlockSpec((1, tm, D_MODEL), lambda b, i: (b, i, 0)),
                  pl.BlockSpec(mod_blk, mod_map),
                  pl.BlockSpec(mod_blk, mod_map),
                  pl.BlockSpec((1, D_MODEL), const2),
                  pl.BlockSpec((D_MODEL, IN_PAD), const2),
                  pl.BlockSpec((1, Q_PAD), const2),
                  pl.BlockSpec((Q_PAD, UQ_PAD), const2),
                  pl.BlockSpec((N_HEADS // 2, LANES, 2 * KV_RANK), const3),
                  pl.BlockSpec((1, KV_RANK), const2),
                  pl.BlockSpec((1, tm, LANES), pos_map),
                  pl.BlockSpec((1, tm, LANES), pos_map)],
        out_specs=[pl.BlockSpec((1, N_HEADS, CAT, tm), lambda b, i: (b, 0, 0, i)),
                   pl.BlockSpec((1, tm, CAT), lambda b, i: (b, i, 0)),
                   pl.BlockSpec((1, VT_ROWS, tm), lambda b, i: (b, 0, i)),
                   pl.BlockSpec((1, tm, KV_RANK), lambda b, i: (b, i, 0)),
                   pl.BlockSpec((1, tm, ROPE_DIM), lambda b, i: (b, i, 0)),
                   pl.BlockSpec((1, tm, SSM_WIDTH), lambda b, i: (b, i, 0))],
        out_shape=[jax.ShapeDtypeStruct((n, N_HEADS, CAT, l), BF16),
                   jax.ShapeDtypeStruct((n, l, CAT), BF16),
                   jax.ShapeDtypeStruct((n, VT_ROWS, l), BF16),
                   jax.ShapeDtypeStruct((n, l, KV_RANK), F32),
                   jax.ShapeDtypeStruct((n, l, ROPE_DIM), F32),
                   jax.ShapeDtypeStruct((n, l, SSM_WIDTH), BF16)],
        compiler_params=_cparams(("parallel", "parallel")),
        name="inproj",
    )(x, sc, sh, p["g_pre_mix"], p["w_in"], p["g_q"], p["w_uq"], p["w_uk"], p["g_kv"], cos4, sin4)


def _flash_kernel(qt_ref, k_ref, vt_ref, wuvt_ref, o_ref, m_ref, acc_ref, *, tq, tk):
    i = pl.program_id(1)
    j = pl.program_id(2)
    j_last = ((i + 1) * tq - 1) // tk

    @pl.when(j == 0)
    def _():
        m_ref[...] = jnp.full(m_ref.shape, NEG_INF, F32)
        acc_ref[...] = jnp.zeros(acc_ref.shape, F32)

    def update(masked, key0, size):
        k = k_ref[0, pl.ds(key0, size), :]
        vt = vt_ref[0, :, pl.ds(key0, size)]
        if masked:
            k_pos = j * tk + key0 + lax.broadcasted_iota(jnp.int32, (size, tq), 0)
            q_pos = i * tq + lax.broadcasted_iota(jnp.int32, (size, tq), 1)
            keep = k_pos <= q_pos
        m_all = m_ref[...]
        m_rows = []
        scores = lambda h: jnp.dot(k, qt_ref[0, h], preferred_element_type=F32)
        ahead = [scores(h) for h in range(min(FLASH_LOOKAHEAD, N_HEADS))]
        pending = None
        for h in range(N_HEADS):
            s = ahead.pop(0)
            if h + FLASH_LOOKAHEAD < N_HEADS:
                ahead.append(scores(h + FLASH_LOOKAHEAD))
            if masked:
                s = jnp.where(keep, s, NEG_INF)
            m_prev = m_all[h:h + 1]
            m_new = jnp.maximum(m_prev, jnp.max(s, axis=0, keepdims=True))
            alpha = jnp.exp2(m_prev - m_new)
            p = jnp.exp2(s - m_new)
            pv = jnp.dot(vt, p.astype(BF16), preferred_element_type=F32)
            if pending is not None:
                acc_ref[pending[0]] = pending[1] * acc_ref[pending[0]] + pending[2]
            pending = (h, alpha, pv)
            m_rows.append(m_new)
        acc_ref[pending[0]] = pending[1] * acc_ref[pending[0]] + pending[2]
        m_ref[...] = jnp.concatenate(m_rows, axis=0)

    crosses = (j + 1) * tk - 1 > i * tq
    sub = min(tk, FLASH_DIAG_SUB)

    @pl.when(jnp.logical_and(j <= j_last, crosses))
    def _():
        n_sub = jnp.minimum(tk // sub, ((i + 1) * tq - 1 - j * tk) // sub + 1)

        def body(sb, carry):
            update(True, pl.multiple_of(sb * sub, sub), sub)
            return carry
        lax.fori_loop(0, n_sub, body, 0)

    @pl.when(jnp.logical_and(j <= j_last, jnp.logical_not(crosses)))
    def _():
        update(False, 0, tk)

    @pl.when(j == j_last)
    def _():
        outs = []
        for h in range(N_HEADS):
            o_t = acc_ref[h, :KV_RANK, :] / acc_ref[h, KV_RANK:KV_RANK + 1, :]
            outs.append(_mm(wuvt_ref[h], o_t))
        o_ref[0] = jnp.concatenate(outs, axis=0).T.astype(BF16)


def _flash(q_t, k_cat, v_t, wuv_t, tq, tk):
    n, _, _, l = q_t.shape
    kern = functools.partial(_flash_kernel, tq=tq, tk=tk)
    last = lambda i: ((i + 1) * tq - 1) // tk
    return pl.pallas_call(
        kern,
        grid=(n, l // tq, l // tk),
        in_specs=[pl.BlockSpec((1, N_HEADS, CAT, tq), lambda b, i, j: (b, 0, 0, i)),
                  pl.BlockSpec((1, tk, CAT), lambda b, i, j: (b, jnp.minimum(j, last(i)), 0)),
                  pl.BlockSpec((1, VT_ROWS, tk), lambda b, i, j: (b, 0, jnp.minimum(j, last(i)))),
                  pl.BlockSpec((N_HEADS, V_DIM, KV_RANK), lambda b, i, j: (0, 0, 0))],
        out_specs=pl.BlockSpec((1, tq, ATTN_WIDTH), lambda b, i, j: (b, i, 0)),
        out_shape=jax.ShapeDtypeStruct((n, l, ATTN_WIDTH), BF16),
        scratch_shapes=[pltpu.VMEM((N_HEADS, tq), F32),
                        pltpu.VMEM((N_HEADS, VT_ROWS, tq), F32)],
        compiler_params=_cparams(("parallel", "parallel", "arbitrary")),
        name="flash",
    )(q_t, k_cat, v_t, wuv_t)


def _ssm_out(y, up, d_ref, wglu_ref, bglu_ref):
    y = _gelu_tanh(y + d_ref[...] * up)
    return y * _sigmoid(_mm(y, wglu_ref[...]) + bglu_ref[...])


def _s5_kernel(u_ref, perm_ref, permt_ref, bmat_ref, cmat_ref, lam_ref, laml_ref, d_ref, wglu_ref,
               bglu_ref, y_ref, hlast_ref, buf_ref, carry_ref, *, lc):
    j = pl.program_id(1)

    @pl.when(j == 0)
    def _():
        carry_ref[...] = jnp.zeros(carry_ref.shape, F32)

    u = u_ref[0]
    up = jnp.dot(perm_ref[...], u, preferred_element_type=F32).astype(BF16)
    blk_w = 2 * BLOCK_STATES
    for b in range(SSM_BLOCKS):
        buf_ref[:, b * blk_w:(b + 1) * blk_w] = jnp.dot(
            up[:, b * LANES:(b + 1) * LANES], bmat_ref[b], preferred_element_type=F32)

    for b in range(SSM_BLOCKS):
        st = slice(b * BLOCK_STATES, (b + 1) * BLOCK_STATES)
        re_cols = slice(b * blk_w, b * blk_w + BLOCK_STATES)
        im_cols = slice(b * blk_w + BLOCK_STATES, (b + 1) * blk_w)
        lr = lam_ref[0:1, st]
        li = lam_ref[1:2, st]

        def run(init, store):
            def body(t, carry):
                re, im = carry
                rows = pl.ds(pl.multiple_of(t * SUBLANES, SUBLANES), SUBLANES)
                n_re = lr * re - li * im + buf_ref[rows, re_cols]
                n_im = lr * im + li * re + buf_ref[rows, im_cols]
                if store:
                    buf_ref[rows, re_cols] = n_re
                    buf_ref[rows, im_cols] = n_im
                return n_re, n_im
            return lax.fori_loop(0, lc, body, init)

        zero = jnp.zeros((SUBLANES, BLOCK_STATES), F32)
        end_re, end_im = run((zero, zero), False)
        ll_r = laml_ref[0:1, st]
        ll_i = laml_ref[1:2, st]
        c_re = carry_ref[0:1, st]
        c_im = carry_ref[1:2, st]
        init_re, init_im = [], []
        for k in range(SUBLANES):
            init_re.append(c_re)
            init_im.append(c_im)
            c_re, c_im = (ll_r * c_re - ll_i * c_im + end_re[k:k + 1],
                          ll_r * c_im + ll_i * c_re + end_im[k:k + 1])
        carry_ref[0:1, st] = c_re
        carry_ref[1:2, st] = c_im
        run((jnp.concatenate(init_re, axis=0), jnp.concatenate(init_im, axis=0)), True)

    hlast_ref[0] = carry_ref[...]
    ys = [jnp.dot(buf_ref[:, b * blk_w:(b + 1) * blk_w].astype(BF16), cmat_ref[b],
                  preferred_element_type=F32) for b in range(SSM_BLOCKS)]
    y = _ssm_out(jnp.concatenate(ys, axis=-1), up.astype(F32), d_ref, wglu_ref, bglu_ref)
    y_ref[0] = jnp.dot(permt_ref[...], y.astype(BF16), preferred_element_type=F32).astype(BF16)


def _s5_prompt(u, sp, lc):
    n, l, _ = u.shape
    s = SUBLANES * lc
    rows = jnp.arange(s)
    src = (rows % SUBLANES) * lc + rows // SUBLANES
    perm = (src[:, None] == rows[None, :]).astype(BF16)
    laml = _complex_pow2(sp["lam"], lc)
    kern = functools.partial(_s5_kernel, lc=lc)
    c2 = lambda b, j: (0, 0)
    c3 = lambda b, j: (0, 0, 0)
    return pl.pallas_call(
        kern,
        grid=(n, l // s),
        in_specs=[pl.BlockSpec((1, s, SSM_WIDTH), lambda b, j: (b, j, 0)),
                  pl.BlockSpec((s, s), c2),
                  pl.BlockSpec((s, s), c2),
                  pl.BlockSpec((SSM_BLOCKS, LANES, 2 * BLOCK_STATES), c3),
                  pl.BlockSpec((SSM_BLOCKS, 2 * BLOCK_STATES, LANES), c3),
                  pl.BlockSpec((2, N_STATE), c2),
                  pl.BlockSpec((2, N_STATE), c2),
                  pl.BlockSpec((1, SSM_WIDTH), c2),
                  pl.BlockSpec((SSM_WIDTH, SSM_WIDTH), c2),
                  pl.BlockSpec((1, SSM_WIDTH), c2)],
        out_specs=[pl.BlockSpec((1, s, SSM_WIDTH), lambda b, j: (b, j, 0)),
                   pl.BlockSpec((1, 2, N_STATE), lambda b, j: (b, 0, 0))],
        out_shape=[jax.ShapeDtypeStruct((n, l, SSM_WIDTH), BF16),
                   jax.ShapeDtypeStruct((n, 2, N_STATE), F32)],
        scratch_shapes=[pltpu.VMEM((s, 2 * N_STATE), F32),
                        pltpu.VMEM((2, N_STATE), F32)],
        compiler_params=_cparams(("parallel", "arbitrary")),
        name="s5_scan",
    )(u, perm, perm.T, sp["bmat"], sp["cmat"], sp["lam"], laml, sp["d"], sp["w_glu"], sp["b_glu"])


def _s5_step_kernel(u_ref, h0_ref, bmat_ref, cmat_ref, lam_ref, d_ref, wglu_ref, bglu_ref,
                    y_ref, h_ref):
    u = u_ref[...]
    ys = []
    for b in range(SSM_BLOCKS):
        st = slice(b * BLOCK_STATES, (b + 1) * BLOCK_STATES)
        bu = jnp.dot(u[:, b * LANES:(b + 1) * LANES], bmat_ref[b], preferred_element_type=F32)
        lr = lam_ref[0:1, st]
        li = lam_ref[1:2, st]
        re = h0_ref[0, :, st]
        im = h0_ref[1, :, st]
        n_re = lr * re - li * im + bu[:, :BLOCK_STATES]
        n_im = lr * im + li * re + bu[:, BLOCK_STATES:]
        h_ref[0, :, st] = n_re
        h_ref[1, :, st] = n_im
        ys.append(_mm(jnp.concatenate([n_re, n_im], axis=-1), cmat_ref[b]))
    y = _ssm_out(jnp.concatenate(ys, axis=-1), u.astype(F32), d_ref, wglu_ref, bglu_ref)
    y_ref[...] = y.astype(BF16)


def _s5_step(u, h0, sp):
    rows = u.shape[0]
    return pl.pallas_call(
        _s5_step_kernel,
        out_shape=[jax.ShapeDtypeStruct((rows, SSM_WIDTH), BF16),
                   jax.ShapeDtypeStruct((2, rows, N_STATE), F32)],
        compiler_params=pltpu.CompilerParams(vmem_limit_bytes=VMEM_LIMIT),
        name="s5_step",
    )(u, h0, sp["bmat"], sp["cmat"], sp["lam"], sp["d"], sp["w_glu"], sp["b_glu"])


def _decode_kernel(pt_ref, ql_ref, qr_ref, kvs_ref, krs_ref, lat_hbm, rope_hbm, o_ref,
                   lat_buf, rope_buf, sems, *, n_pages):
    b = pl.program_id(0)
    slot = b % 2

    def page_copies(seq, dst_slot):
        out = []
        for i in range(n_pages):
            page = pt_ref[seq * n_pages + i]
            keys = pl.ds(i * PAGE_SIZE, PAGE_SIZE)
            out.append(pltpu.make_async_copy(lat_hbm.at[page], lat_buf.at[dst_slot, keys, :],
                                             sems.at[dst_slot, 0]))
            out.append(pltpu.make_async_copy(rope_hbm.at[page], rope_buf.at[dst_slot, :, keys],
                                             sems.at[dst_slot, 1]))
        return out

    @pl.when(b == 0)
    def _():
        for cp in page_copies(0, 0):
            cp.start()

    @pl.when(b + 1 < pl.num_programs(0))
    def _():
        for cp in page_copies(b + 1, 1 - slot):
            cp.start()

    for cp in page_copies(b, slot):
        cp.wait()

    ql = ql_ref[0]
    qr = qr_ref[0]
    kvs = kvs_ref[0]
    krs = krs_ref[0]
    lat = lat_buf[slot].astype(BF16)
    s = _mm_nt(ql, lat) + _mm(qr, rope_buf[slot])
    s_self = (jnp.sum(ql.astype(F32) * kvs, axis=-1, keepdims=True)
              + jnp.sum(qr.astype(F32) * krs, axis=-1, keepdims=True))
    m = jnp.maximum(jnp.max(s, axis=-1, keepdims=True), s_self)
    p = jnp.exp2(s - m)
    p_self = jnp.exp2(s_self - m)
    denom = jnp.sum(p, axis=-1, keepdims=True) + p_self
    o_ref[0] = (_mm(p, lat) + p_self * kvs) / denom


def _decode_attn(q_lat, q_rope, kv_self, kr_self, cache_lat, cache_rope_t, page_table):
    b, n_pages = page_table.shape
    kern = functools.partial(_decode_kernel, n_pages=n_pages)
    seq3 = lambda s, pt: (s, 0, 0)
    keys = n_pages * PAGE_SIZE
    grid_spec = pltpu.PrefetchScalarGridSpec(
        num_scalar_prefetch=1,
        grid=(b,),
        in_specs=[pl.BlockSpec((1, N_HEADS, KV_RANK), seq3),
                  pl.BlockSpec((1, N_HEADS, ROPE_DIM), seq3),
                  pl.BlockSpec((1, 1, KV_RANK), seq3),
                  pl.BlockSpec((1, 1, ROPE_DIM), seq3),
                  pl.BlockSpec(memory_space=pl.ANY),
                  pl.BlockSpec(memory_space=pl.ANY)],
        out_specs=pl.BlockSpec((1, N_HEADS, KV_RANK), seq3),
        scratch_shapes=[pltpu.VMEM((2, keys, KV_RANK), F32),
                        pltpu.VMEM((2, ROPE_DIM, keys), F32),
                        pltpu.SemaphoreType.DMA((2, 2))],
    )
    return pl.pallas_call(
        kern,
        grid_spec=grid_spec,
        out_shape=jax.ShapeDtypeStruct((b, N_HEADS, KV_RANK), F32),
        compiler_params=_cparams(("arbitrary",)),
        name="decode_attn",
    )(page_table.reshape(-1), q_lat, q_rope, kv_self, kr_self, cache_lat, cache_rope_t)


def _uv_kernel(o_ref, w_ref, y_ref):
    y_ref[...] = _mm(o_ref[...], w_ref[...]).astype(BF16)


def _uv_proj(o_lat, wuv_bd):
    rows = o_lat.shape[0]
    return pl.pallas_call(
        _uv_kernel,
        out_shape=jax.ShapeDtypeStruct((rows, ATTN_WIDTH), BF16),
        compiler_params=pltpu.CompilerParams(vmem_limit_bytes=VMEM_LIMIT),
        name="uv_proj",
    )(o_lat, wuv_bd)


def _outproj_kernel(oa_ref, ys_ref, x_ref, g1_ref, sc_ref, sh_ref, wout_ref, gpost_ref, gpre_ref,
                    wr_ref, br_ref, *rest, row0):
    x1_ref, h2_ref, route_ref = rest[-3:]
    mix = (jnp.dot(oa_ref[0], wout_ref[:ATTN_WIDTH].astype(BF16), preferred_element_type=F32)
           + jnp.dot(ys_ref[0], wout_ref[ATTN_WIDTH:].astype(BF16), preferred_element_type=F32))
    x1 = x_ref[0] + g1_ref[0] * _rms(mix, gpost_ref[...], D_MODEL)
    x1_ref[0] = x1
    h2 = _rms(x1, gpre_ref[...], D_MODEL) * (1.0 + sc_ref[0]) + sh_ref[0]
    for s in range(D_MODEL // LANES):
        h2_ref[:, s, :] = h2[:, s * LANES:(s + 1) * LANES]
    logits = _mm3(h2, wr_ref[...]) + br_ref[...]
    lane = lax.broadcasted_iota(jnp.int32, logits.shape, 1).astype(F32)
    vals, idxs = [], []
    for _ in range(TOP_K):
        v = jnp.max(logits, axis=-1, keepdims=True)
        idx = jnp.min(jnp.where(logits == v, lane, float(LANES)), axis=-1, keepdims=True)
        vals.append(v)
        idxs.append(idx)
        logits = jnp.where(lane == idx, -jnp.inf, logits)
    exps = [jnp.exp(v - vals[0]) for v in vals]
    denom = exps[0] + exps[1] + exps[2] + exps[3]
    tm = logits.shape[0]
    first = row0 + (pl.program_id(0) * pl.num_programs(1) + pl.program_id(1)) * tm
    token = (first + lax.broadcasted_iota(jnp.int32, (tm, 1), 0)).astype(F32)
    route = jnp.where(lane == float(ROUTE_TOKEN), token, jnp.where(lane == float(ROUTE_VALID), 1.0, 0.0))
    for k in range(TOP_K):
        route = jnp.where(lane == float(k), idxs[k], route)
        route = jnp.where(lane == float(TOP_K + k), exps[k] / denom, route)
    route_ref[...] = route


def _outproj(o_attn, y_ssm, x, g1, sc2, sh2, p, tm, t_all, row0, dest=None):
    n, l, _ = x.shape
    per_row = g1.shape[1] != 1
    mod_blk = (1, tm, D_MODEL) if per_row else (1, 1, D_MODEL)
    mod_map = (lambda b, i: (b, i, 0)) if per_row else (lambda b, i: (b, 0, 0))
    row_map = lambda b, i: (b, i, 0)
    const2 = lambda b, i: (0, 0)
    blk0, per_seq = row0 // tm, l // tm
    n_tile = D_MODEL // LANES
    in_specs = [pl.BlockSpec((1, tm, ATTN_WIDTH), row_map),
                pl.BlockSpec((1, tm, SSM_WIDTH), row_map),
                pl.BlockSpec((1, tm, D_MODEL), row_map),
                pl.BlockSpec(mod_blk, mod_map),
                pl.BlockSpec(mod_blk, mod_map),
                pl.BlockSpec(mod_blk, mod_map),
                pl.BlockSpec((D_MODEL, D_MODEL), const2),
                pl.BlockSpec((1, D_MODEL), const2),
                pl.BlockSpec((1, D_MODEL), const2),
                pl.BlockSpec((D_MODEL, LANES), const2),
                pl.BlockSpec((1, LANES), const2)]
    args = [o_attn, y_ssm, x, g1, sc2, sh2, p["w_out"], p["g_post_mix"], p["g_pre_ffn"],
            p["w_router"], p["b_router"]]
    aliases = {}
    if dest is not None:
        aliases = {len(args): 1, len(args) + 1: 2}
        in_specs += [pl.BlockSpec(memory_space=pl.ANY), pl.BlockSpec(memory_space=pl.ANY)]
        args += list(dest)
    return pl.pallas_call(
        functools.partial(_outproj_kernel, row0=row0),
        grid=(n, per_seq),
        in_specs=in_specs,
        out_specs=[pl.BlockSpec((1, tm, D_MODEL), row_map),
                   pl.BlockSpec((tm, n_tile, LANES), lambda b, i: (blk0 + b * per_seq + i, 0, 0)),
                   pl.BlockSpec((tm, LANES), lambda b, i: (blk0 + b * per_seq + i, 0))],
        out_shape=[jax.ShapeDtypeStruct((n, l, D_MODEL), F32),
                   jax.ShapeDtypeStruct((t_all, n_tile, LANES), F32),
                   jax.ShapeDtypeStruct((t_all, LANES), F32)],
        input_output_aliases=aliases,
        compiler_params=_cparams(("parallel", "parallel")),
        name="outproj",
    )(*args)


def _moe_prep_kernel(w1_ref, w2_ref, perm_ref, w1o_ref, w2o_ref):
    for j in range(w1_ref.shape[2] // GLU_BLOCK):
        cols = slice(j * GLU_BLOCK, (j + 1) * GLU_BLOCK)
        w1o_ref[0, :, cols] = jnp.dot(w1_ref[0, :, cols].astype(BF16), perm_ref[...],
                                      preferred_element_type=F32).astype(BF16)
    w2o_ref[0] = w2_ref[0].astype(BF16)


def _moe_prep(w1, w2):
    n_e, d_in, d_hid = w1.shape
    d_exp = w2.shape[1]
    j = jnp.arange(GLU_BLOCK)
    src = jnp.where(j < LANES, 2 * j, 2 * (j - LANES) + 1)
    perm = (j[:, None] == src[None, :]).astype(BF16)
    e_map = lambda e: (e, 0, 0)
    return pl.pallas_call(
        _moe_prep_kernel,
        grid=(n_e,),
        in_specs=[pl.BlockSpec((1, d_in, d_hid), e_map),
                  pl.BlockSpec((1, d_exp, D_MODEL), e_map),
                  pl.BlockSpec((GLU_BLOCK, GLU_BLOCK), lambda e: (0, 0))],
        out_specs=[pl.BlockSpec((1, d_in, d_hid), e_map),
                   pl.BlockSpec((1, d_exp, D_MODEL), e_map)],
        out_shape=[jax.ShapeDtypeStruct(w1.shape, BF16),
                   jax.ShapeDtypeStruct(w2.shape, BF16)],
        compiler_params=_cparams(("parallel",)),
        name="moe_prep",
    )(w1, w2, perm)


def _sort_rows_kernel(pos_ref, route_ref, rows_ref):
    rows_ref[...] = jnp.zeros(rows_ref.shape, F32)

    def body(t, carry):
        row = route_ref[0, pl.ds(t, 1), :]
        for k in range(TOP_K):
            rows_ref[0, pl.ds(pos_ref[0, 0, TOP_K * t + k], 1), :] = row
        return carry
    lax.fori_loop(0, route_ref.shape[1], body, 0, unroll=4)


def _sort_rows(route, pos, nch, rc):
    t = route.shape[0]
    tc = t // nch
    return pl.pallas_call(
        _sort_rows_kernel,
        grid=(nch,),
        in_specs=[pl.BlockSpec((1, 1, tc * TOP_K), lambda c: (c, 0, 0), memory_space=pltpu.SMEM),
                  pl.BlockSpec((1, tc, LANES), lambda c: (c, 0, 0))],
        out_specs=pl.BlockSpec((1, rc, LANES), lambda c: (c, 0, 0)),
        out_shape=jax.ShapeDtypeStruct((nch, rc, LANES), F32),
        compiler_params=_cparams(("parallel",)),
        name="moe_sort_rows",
    )(pos.reshape(nch, 1, tc * TOP_K), route.reshape(nch, tc, LANES))


def _route_plan(route, nch, tmx):
    idx = route[:, :TOP_K].astype(jnp.int32)
    t = idx.shape[0]
    tc = t // nch
    nt = -(-(tc * TOP_K + N_EXPERTS * (tmx - 1)) // tmx)
    i32 = jnp.int32
    hot = (idx[:, :, None] == jnp.arange(N_EXPERTS, dtype=i32)).astype(i32).sum(1)
    hot = hot.reshape(nch, tc, N_EXPERTS)
    csum = jnp.cumsum(hot, axis=1)
    rank = (csum - hot).reshape(t, N_EXPERTS)
    padded = -(-csum[:, -1, :] // tmx) * tmx
    off_end = jnp.cumsum(padded, axis=1)
    off = off_end - padded
    n_tiles = (off_end[:, -1] // tmx).astype(i32)
    chunk = jnp.arange(t, dtype=i32) // tc
    pos = jnp.take_along_axis(off[chunk] + rank, idx, axis=1).astype(i32)
    rc = nt * tmx
    rows = _sort_rows(route, pos, nch, rc)
    valid = rows[:, :, ROUTE_VALID] > 0.0
    row_tok = rows[:, :, ROUTE_TOKEN].astype(i32) % tc
    row_dst = jnp.where(valid, row_tok, tc + jnp.arange(rc, dtype=i32)[None, :] % SUBLANES)
    tile = jnp.arange(nt + 1, dtype=i32)
    texp_raw = (off_end[:, None, :] <= (tile * tmx)[None, :, None]).sum(-1).astype(i32)
    last = n_tiles[:, None] - 1
    tile_c = jnp.minimum(tile[None, :], last)
    tile_n = jnp.minimum(tile[None, :] + 1, last)
    tile_p = jnp.clip(tile[None, :] - 1, 0, last)
    texp = jnp.take_along_axis(texp_raw, tile_c, axis=1)
    base = (jnp.arange(nch, dtype=i32) * nt)[:, None]
    return dict(nt=nt, n_tiles=n_tiles, texp=texp.reshape(-1), tblk=(tile_c + base).reshape(-1),
                tnext=(tile_n + base).reshape(-1), tprev=(tile_p + base).reshape(-1),
                row_tok=row_tok.reshape(nch * nt, 1, tmx), row_dst=row_dst.reshape(nch * nt, 1, tmx),
                rows=rows.reshape(nch * nt, tmx, LANES))


def _moe_kernel(nt_ref, texp_ref, tblk_ref, tnext_ref, tprev_ref, tok_ref, toknext_ref, dstprev_ref,
                rows_ref, src_ref, w1_ref, b1_ref, w2_ref, b2_ref, out_ref, xs_ref, yt_ref):
    del tblk_ref, tnext_ref, tprev_ref
    c = pl.program_id(0)
    i = pl.program_id(1)
    tmx = rows_ref.shape[1]
    slot = i % 2
    n_grp = tmx // SUBLANES

    def add_prev_rows(groups):
        for g in groups:
            toks = [dstprev_ref[0, 0, g * SUBLANES + u] for u in range(SUBLANES)]
            sums = [out_ref[0, toks[u]]
                    + yt_ref[1 - slot, (g * SUBLANES + u) * SUBLANES:(g * SUBLANES + u + 1) * SUBLANES, :]
                    for u in range(SUBLANES)]
            for u in range(SUBLANES):
                out_ref[0, toks[u]] = sums[u]

    @pl.when(i == 0)
    def _():
        out_ref[...] = jnp.zeros(out_ref.shape, F32)
        yt_ref[1] = jnp.zeros(yt_ref.shape[1:], F32)

        def gather(r, carry):
            xs_ref[0, pl.ds(pl.multiple_of(r * SUBLANES, SUBLANES), SUBLANES), :] = src_ref[0, tok_ref[0, 0, r]]
            return carry
        lax.fori_loop(0, tmx, gather, 0, unroll=8)

    @pl.when(i < nt_ref[c])
    def _():
        x = jnp.concatenate([xs_ref[slot, pl.ds(s, tmx, stride=SUBLANES), :]
                             for s in range(D_MODEL // LANES)], axis=-1).astype(BF16)
        n_blk = w1_ref.shape[2] // (2 * GLU_BLOCK)
        wide = 2 * GLU_BLOCK

        def hidden(j):
            cols = slice(j * wide, (j + 1) * wide)
            return jnp.dot(x, w1_ref[0, :, cols], preferred_element_type=F32) + b1_ref[0, :, cols]

        hid = hidden(0)
        y = None
        for j in range(n_blk):
            nxt = hidden(j + 1) if j + 1 < n_blk else None
            for r in range(j * tmx // n_blk, (j + 1) * tmx // n_blk):
                xs_ref[1 - slot, r * SUBLANES:(r + 1) * SUBLANES, :] = src_ref[0, toknext_ref[0, 0, r]]
            add_prev_rows(range(j * n_grp // n_blk, (j + 1) * n_grp // n_blk))
            acts = []
            for b in range(2):
                glu = jnp.minimum(hid[:, b * GLU_BLOCK:b * GLU_BLOCK + LANES], SWIGLU_LIMIT)
                lin = jnp.clip(hid[:, b * GLU_BLOCK + LANES:(b + 1) * GLU_BLOCK], -SWIGLU_LIMIT, SWIGLU_LIMIT)
                acts.append(glu * _sigmoid(SWIGLU_ALPHA * glu) * (lin + 1.0))
            part = _mm(jnp.concatenate(acts, axis=-1), w2_ref[0, j * GLU_BLOCK:(j + 1) * GLU_BLOCK, :])
            y = part if y is None else y + part
            hid = nxt
        meta = rows_ref[0]
        expert = texp_ref[c * pl.num_programs(1) + i].astype(F32)
        lane = lax.broadcasted_iota(jnp.int32, meta.shape, 1)
        hit = jnp.logical_and(lane < TOP_K, meta == expert)
        gates = pltpu.roll(meta, LANES - TOP_K, axis=1)
        gate = jnp.sum(jnp.where(hit, gates, 0.0), axis=-1, keepdims=True)
        y = gate * (y + b2_ref[0])
        for s in range(D_MODEL // LANES):
            yt_ref[slot, pl.ds(s, tmx, stride=SUBLANES), :] = y[:, s * LANES:(s + 1) * LANES]

    @pl.when(i == nt_ref[c])
    def _():
        add_prev_rows(range(n_grp))


def _moe_experts(h2, plan, w1p, b1p, w2p, b2, tmx):
    nch, tc = h2.shape[:2]
    nt = plan["nt"]
    n_e, _, d_hid = w1p.shape
    d_exp = w2p.shape[1]
    steps = nt + 1
    flat = lambda c, i: c * steps + i
    cur = lambda c, i, n, te, tb, tn, tp: (tb[flat(c, i)], 0, 0)
    nxt = lambda c, i, n, te, tb, tn, tp: (tn[flat(c, i)], 0, 0)
    prv = lambda c, i, n, te, tb, tn, tp: (tp[flat(c, i)], 0, 0)
    exp = lambda c, i, n, te, tb, tn, tp: (te[flat(c, i)], 0, 0)
    grid_spec = pltpu.PrefetchScalarGridSpec(
        num_scalar_prefetch=5,
        grid=(nch, steps),
        in_specs=[pl.BlockSpec((1, 1, tmx), cur, memory_space=pltpu.SMEM),
                  pl.BlockSpec((1, 1, tmx), nxt, memory_space=pltpu.SMEM),
                  pl.BlockSpec((1, 1, tmx), prv, memory_space=pltpu.SMEM),
                  pl.BlockSpec((1, tmx, LANES), cur),
                  pl.BlockSpec((1, tc, D_MODEL // LANES, LANES), lambda c, i, n, te, tb, tn, tp: (c, 0, 0, 0),
                               pipeline_mode=pl.Buffered(1)),
                  pl.BlockSpec((1, D_MODEL, d_hid), exp),
                  pl.BlockSpec((1, 1, d_hid), exp),
                  pl.BlockSpec((1, d_exp, D_MODEL), exp),
                  pl.BlockSpec((1, 1, D_MODEL), exp)],
        out_specs=pl.BlockSpec((1, tc + SUBLANES, D_MODEL // LANES, LANES),
                               lambda c, i, n, te, tb, tn, tp: (c, 0, 0, 0), pipeline_mode=pl.Buffered(1)),
        scratch_shapes=[pltpu.VMEM((2, tmx * (D_MODEL // LANES), LANES), F32),
                        pltpu.VMEM((2, tmx * (D_MODEL // LANES), LANES), F32)],
    )
    return pl.pallas_call(
        _moe_kernel,
        grid_spec=grid_spec,
        out_shape=jax.ShapeDtypeStruct((nch, tc + SUBLANES, D_MODEL // LANES, LANES), F32),
        compiler_params=_cparams(("parallel", "arbitrary")),
        name="moe_experts",
    )(plan["n_tiles"], plan["texp"], plan["tblk"], plan["tnext"], plan["tprev"], plan["row_tok"],
      plan["row_tok"], plan["row_dst"], plan["rows"], h2, w1p, b1p, w2p, b2)


def _final_kernel(x1_ref, f_ref, g2_ref, gpost_ref, y_ref):
    tm = x1_ref.shape[1]
    f = jnp.concatenate([f_ref[pl.ds(s, tm, stride=SUBLANES), :] for s in range(D_MODEL // LANES)],
                        axis=-1)
    y_ref[0] = x1_ref[0] + g2_ref[0] * _rms(f, gpost_ref[...], D_MODEL)


def _final(x1, f_all, row0, g2, g_post_ffn, tm):
    n, l, _ = x1.shape
    per_row = g2.shape[1] != 1
    mod_blk = (1, tm, D_MODEL) if per_row else (1, 1, D_MODEL)
    mod_map = (lambda b, i: (b, i, 0)) if per_row else (lambda b, i: (b, 0, 0))
    row_map = lambda b, i: (b, i, 0)
    blk0, per_seq = row0 // tm, l // tm
    return pl.pallas_call(
        _final_kernel,
        grid=(n, l // tm),
        in_specs=[pl.BlockSpec((1, tm, D_MODEL), row_map),
                  pl.BlockSpec((tm * (D_MODEL // LANES), LANES), lambda b, i: (blk0 + b * per_seq + i, 0)),
                  pl.BlockSpec(mod_blk, mod_map),
                  pl.BlockSpec((1, D_MODEL), lambda b, i: (0, 0))],
        out_specs=pl.BlockSpec((1, tm, D_MODEL), row_map),
        out_shape=jax.ShapeDtypeStruct((n, l, D_MODEL), F32),
        compiler_params=_cparams(("parallel", "parallel")),
        name="final",
    )(x1, f_all, g2, g_post_ffn)


def _complex_pow2(lam, n):
    re, im = lam[0], lam[1]
    steps = int(round(math.log2(n)))
    assert 2 ** steps == n
    for _ in range(steps):
        re, im = re * re - im * im, 2.0 * re * im
    return jnp.stack([re, im])


def _rope_tables(pos):
    half = ROPE_DIM // 2
    inv = ROPE_THETA ** (-jnp.arange(half, dtype=F32) / half)
    ang = pos.astype(F32)[:, None] * inv[None, :]
    cos, sin = jnp.cos(ang), jnp.sin(ang)
    cos4 = jnp.tile(jnp.concatenate([cos, cos], axis=-1), (1, LANES // ROPE_DIM))
    sin4 = jnp.tile(jnp.concatenate([-sin, sin], axis=-1), (1, LANES // ROPE_DIM))
    return cos4[None], sin4[None]


def _swap_halves(w):
    half = ROPE_DIM // 2
    return jnp.concatenate([w[..., half:], w[..., :half]], axis=-1)


def _prep_layer(w, layer):
    g = lambda name: w[name][layer]
    p = {}
    row = lambda v: v.reshape(1, -1)
    p["g_pre_mix"] = row(g("g_pre_mix"))
    w_in = g("w_in")
    w_q, w_kv, w_r, w_u = jnp.split(w_in, [Q_RANK, Q_RANK + KV_RANK, Q_RANK + KV_RANK + ROPE_DIM], axis=-1)
    rep = LANES // ROPE_DIM
    p["w_in"] = jnp.concatenate(
        [w_q, jnp.zeros((D_MODEL, Q_PAD - Q_RANK), F32), w_kv,
         jnp.tile(w_r, (1, rep)), jnp.tile(_swap_halves(w_r), (1, rep)), w_u], axis=-1)
    p["g_q"] = row(jnp.pad(g("g_q"), (0, Q_PAD - Q_RANK)))
    w_uq = g("w_uq").reshape(Q_RANK, N_HEADS, NOPE_DIM + ROPE_DIM)
    nope = w_uq[:, :, :NOPE_DIM].reshape(Q_RANK, N_HEADS * NOPE_DIM)
    rope = w_uq[:, :, NOPE_DIM:]
    rope_sw = _swap_halves(rope)
    w_uq_p = jnp.concatenate([nope, rope.reshape(Q_RANK, -1), rope_sw.reshape(Q_RANK, -1)], axis=-1)
    p["w_uq"] = jnp.pad(w_uq_p, ((0, Q_PAD - Q_RANK), (0, 0)))
    w_uk = jnp.transpose(g("w_uk"), (1, 2, 0))
    z = jnp.zeros((NOPE_DIM, KV_RANK), F32)
    p["w_uk"] = jnp.stack([jnp.block([[w_uk[2 * j], z], [z, w_uk[2 * j + 1]]])
                           for j in range(N_HEADS // 2)])
    p["g_kv"] = row(g("g_kv"))
    w_uv = jnp.transpose(g("w_uv"), (1, 0, 2))
    p["w_uv_t"] = jnp.transpose(w_uv, (0, 2, 1))
    eye_h = jnp.eye(N_HEADS, dtype=F32)
    p["w_uv_bd"] = jnp.einsum("hrv,hk->hrkv", w_uv, eye_h).reshape(N_HEADS * KV_RANK, ATTN_WIDTH)
    p["w_out"] = g("w_out")
    for name in ("w_in", "w_uq", "w_uk", "w_uv_t", "w_uv_bd", "w_out"):
        p[name] = p[name].astype(BF16)
    p["g_post_mix"] = row(g("g_post_mix"))
    p["g_pre_ffn"] = row(g("g_pre_ffn"))
    p["w_router"] = jnp.pad(g("w_router"), ((0, 0), (0, LANES - N_EXPERTS)))
    p["b_router"] = row(jnp.pad(g("b_router"), (0, LANES - N_EXPERTS), constant_values=NEG_INF))
    p["g_post_ffn"] = row(g("g_post_ffn"))

    lam_re, lam_im = g("ssm_lam_re"), g("ssm_lam_im")
    step = jnp.exp(g("ssm_log_step"))[:, None]
    mag = jnp.exp(lam_re * step)
    bar_re, bar_im = mag * jnp.cos(lam_im * step), mag * jnp.sin(lam_im * step)
    den = lam_re * lam_re + lam_im * lam_im
    co_re = ((bar_re - 1.0) * lam_re + bar_im * lam_im) / den
    co_im = (bar_im * lam_re - (bar_re - 1.0) * lam_im) / den
    b_re, b_im = g("ssm_b_re"), g("ssm_b_im")
    bb_re = co_re[..., None] * b_re - co_im[..., None] * b_im
    bb_im = co_re[..., None] * b_im + co_im[..., None] * b_re
    eye_g = jnp.eye(GROUPS_PER_BLOCK, dtype=F32)

    def b_block(m):
        m = m.reshape(SSM_BLOCKS, GROUPS_PER_BLOCK, SSM_STATE, SSM_GROUP)
        return jnp.einsum("bgpc,gh->bgchp", m, eye_g).reshape(SSM_BLOCKS, LANES, BLOCK_STATES)

    def c_block(m):
        m = m.reshape(SSM_BLOCKS, GROUPS_PER_BLOCK, SSM_GROUP, SSM_STATE)
        return jnp.einsum("bgcp,gh->bgphc", m, eye_g).reshape(SSM_BLOCKS, BLOCK_STATES, LANES)

    sp = {
        "lam": jnp.stack([bar_re.reshape(-1), bar_im.reshape(-1)]),
        "bmat": jnp.concatenate([b_block(bb_re), b_block(bb_im)], axis=-1).astype(BF16),
        "cmat": jnp.concatenate([c_block(g("ssm_c_re")), c_block(-g("ssm_c_im"))], axis=1).astype(BF16),
        "d": row(g("ssm_d")),
        "w_glu": g("ssm_w_glu").astype(BF16),
        "b_glu": row(g("ssm_b_glu")),
    }
    b1 = g("b1")
    n_blk = b1.shape[1] // GLU_BLOCK
    mp = {
        "b1": jnp.transpose(b1.reshape(N_EXPERTS, n_blk, LANES, 2), (0, 1, 3, 2)).reshape(N_EXPERTS, 1, -1),
        "b2": g("b2")[:, None, :],
    }
    return p, sp, mp


def _mods(mod_rows):
    return [mod_rows[:, k * D_MODEL:(k + 1) * D_MODEL] for k in range(N_MOD)]


def _pick_tile(n, target, mult):
    best = mult
    for c in range(mult, min(n, target) + 1, mult):
        if n % c == 0:
            best = c
    return best


def kernel(x_prompt, x_sample, c_prompt, c_sample, cache_kv_latent, cache_k_rope, state_ssm, page_table, w_ada, b_ada, g_pre_mix, w_in, g_q, w_uq, w_uk, g_kv, w_uv, ssm_lam_re, ssm_lam_im, ssm_log_step, ssm_b_re, ssm_b_im, ssm_c_re, ssm_c_im, ssm_d, ssm_w_glu, ssm_b_glu, w_out, g_post_mix, g_pre_ffn, w_router, b_router, w1, b1, w2, b2, g_post_ffn):
    weights = dict(w_ada=w_ada, b_ada=b_ada, g_pre_mix=g_pre_mix, w_in=w_in, g_q=g_q, w_uq=w_uq,
                   w_uk=w_uk, g_kv=g_kv, w_uv=w_uv, ssm_lam_re=ssm_lam_re, ssm_lam_im=ssm_lam_im,
                   ssm_log_step=ssm_log_step, ssm_b_re=ssm_b_re, ssm_b_im=ssm_b_im,
                   ssm_c_re=ssm_c_re, ssm_c_im=ssm_c_im, ssm_d=ssm_d, ssm_w_glu=ssm_w_glu,
                   ssm_b_glu=ssm_b_glu, w_out=w_out, g_post_mix=g_post_mix, g_pre_ffn=g_pre_ffn,
                   w_router=w_router, b_router=b_router, w1=w1, b1=b1, w2=w2, b2=b2,
                   g_post_ffn=g_post_ffn)
    depth = w_ada.shape[0]
    n_p, l_p, _ = x_prompt.shape
    n_s, l_s, _ = x_sample.shape
    assert l_s == 1
    n_pages = page_table.shape[1]
    past_len = n_pages * PAGE_SIZE

    cos_p, sin_p = _rope_tables(jnp.arange(l_p, dtype=jnp.int32))
    cos_s, sin_s = _rope_tables(jnp.full((1,), past_len, jnp.int32))
    cos_s = jnp.broadcast_to(cos_s, (1, n_s, LANES))
    sin_s = jnp.broadcast_to(sin_s, (1, n_s, LANES))

    tm_p = _pick_tile(l_p, 512, 16)
    tq = _pick_tile(l_p, 512, 16)
    tk = _pick_tile(l_p, 1024, 16)
    lc = 64 if l_p % 512 == 0 else l_p // SUBLANES
    c_rows = n_p + n_s
    c_pad = -c_rows % SUBLANES
    c_all = jnp.pad(jnp.concatenate([c_prompt, c_sample], axis=0), ((0, c_pad), (0, 0)))

    y_p = x_prompt
    y_s = x_sample.reshape(1, n_s, D_MODEL)
    outs = [[] for _ in range(6)]
    for layer in range(depth):
        p, sp, mp = _prep_layer(weights, layer)
        mod = _ada(c_all, w_ada[layer], b_ada[layer].reshape(1, -1))
        sh1p, sc1p, g1p, sh2p, sc2p, g2p = [m[:, None, :] for m in _mods(mod[:n_p])]
        sh1s, sc1s, g1s, sh2s, sc2s, g2s = [m[None] for m in _mods(mod[n_p:c_rows])]

        q_t, k_cat, v_t, kv_p, kr_p, u_p = _inproj(y_p, sc1p, sh1p, p, cos_p, sin_p, tm_p)
        o_attn_p = _flash(q_t, k_cat, v_t, p["w_uv_t"], tq, tk)
        y_ssm_p, h_last_p = _s5_prompt(u_p, sp, lc)
        t_p = n_p * l_p
        t_all = t_p + n_s
        blank = (jnp.zeros((t_all, D_MODEL // LANES, LANES), F32), jnp.zeros((t_all, LANES), F32))
        x1_p, h2_all, route = _outproj(o_attn_p, y_ssm_p, y_p, g1p, sc2p, sh2p, p, tm_p, t_all, 0,
                                       dest=blank)

        q_t_s, _, _, kv_s, kr_s, u_s = _inproj(y_s, sc1s, sh1s, p, cos_s, sin_s, n_s)
        q_lat_s = jnp.transpose(q_t_s[0, :, :KV_RANK, :], (2, 0, 1))
        q_rope_s = jnp.stack(
            [q_t_s[0, h, KV_RANK + ROPE_DIM * (h % 4):KV_RANK + ROPE_DIM * (h % 4 + 1), :].T
             for h in range(N_HEADS)], axis=1)
        o_lat_s = _decode_attn(q_lat_s, q_rope_s, kv_s.reshape(n_s, 1, KV_RANK),
                               kr_s.reshape(n_s, 1, ROPE_DIM), cache_kv_latent[layer],
                               jnp.swapaxes(cache_k_rope[layer], 1, 2), page_table)
        o_attn_s = _uv_proj(o_lat_s.reshape(n_s, N_HEADS * KV_RANK), p["w_uv_bd"])
        h0 = jnp.moveaxis(state_ssm[layer].reshape(n_s, N_STATE, 2), -1, 0)
        y_ssm_s, h_s = _s5_step(u_s[0], h0, sp)
        x1_s, h2_all, route = _outproj(o_attn_s[None], y_ssm_s[None], y_s, g1s, sc2s, sh2s, p, n_s,
                                       t_all, t_p, dest=(h2_all, route))

        nch = 4 if t_all % (4 * SUBLANES) == 0 else (2 if t_all % (2 * SUBLANES) == 0 else 1)
        tc = t_all // nch
        plan = _route_plan(route, nch, MOE_TILE)
        w1p, w2p = _moe_prep(w1[layer], w2[layer])
        f_chunks = _moe_experts(h2_all.reshape(nch, tc, D_MODEL // LANES, LANES), plan, w1p, mp["b1"],
                                w2p, mp["b2"], MOE_TILE)
        f_all = f_chunks[:, :tc].reshape(t_all * (D_MODEL // LANES), LANES)
        y_p = _final(x1_p, f_all, 0, g2p, p["g_post_ffn"], tm_p)
        y_s = _final(x1_s, f_all, t_p, g2s, p["g_post_ffn"], n_s)

        outs[0].append(kv_p)
        outs[1].append(kr_p)
        outs[2].append(jnp.moveaxis(h_last_p, 1, -1).reshape(n_p, N_SSM_GROUPS, SSM_STATE, 2))
        outs[3].append(kv_s.reshape(n_s, 1, KV_RANK))
        outs[4].append(kr_s.reshape(n_s, 1, ROPE_DIM))
        outs[5].append(jnp.moveaxis(h_s, 0, -1).reshape(n_s, N_SSM_GROUPS, SSM_STATE, 2))
    return (y_p, y_s.reshape(n_s, 1, D_MODEL)) + tuple(jnp.stack(o) for o in outs)
```

```python
import functools
import math

import jax
import jax.numpy as jnp
from jax import lax
from jax.experimental import pallas as pl
from jax.experimental.pallas import tpu as pltpu

F32 = jnp.float32
BF16 = jnp.bfloat16

D_MODEL = 1024
N_HEADS = 8
NOPE_DIM = 64
ROPE_DIM = 32
V_DIM = 64
Q_RANK = 352
KV_RANK = 128
ROPE_THETA = 10000.0
ATTN_WIDTH = N_HEADS * V_DIM
ATTN_SCALE = 1.0 / math.sqrt(NOPE_DIM + ROPE_DIM)
SSM_WIDTH = D_MODEL - ATTN_WIDTH
SSM_GROUP = 16
N_SSM_GROUPS = SSM_WIDTH // SSM_GROUP
SSM_STATE = 64
N_STATE = N_SSM_GROUPS * SSM_STATE
N_EXPERTS = 32
TOP_K = 4
SWIGLU_LIMIT = 7.0
SWIGLU_ALPHA = 1.702
RMS_EPS = 1e-6
N_MOD = 6
NEG_INF = -1e30
PAGE_SIZE = 128

LANES = 128
SUBLANES = 8
VMEM_LIMIT = 56 * 1024 * 1024

Q_PAD = 384
CAT = 2 * LANES
SSM_BLOCKS = 4
GROUPS_PER_BLOCK = N_SSM_GROUPS // SSM_BLOCKS
BLOCK_STATES = GROUPS_PER_BLOCK * SSM_STATE
IN_PAD = Q_PAD + KV_RANK + 2 * LANES + SSM_WIDTH
UQ_PAD = N_HEADS * NOPE_DIM + 4 * LANES
GLU_BLOCK = 2 * LANES
MOE_TILE = 192
MOE_COL_PARTS = 2
LOG2E = math.log2(math.e)
FLASH_LOOKAHEAD = 3
FLASH_DIAG_SUB = 256
VT_ROWS = KV_RANK + 16
ROUTE_TOKEN = 2 * TOP_K
ROUTE_VALID = 2 * TOP_K + 1


def _cparams(sem):
    return pltpu.CompilerParams(dimension_semantics=sem, vmem_limit_bytes=VMEM_LIMIT)


def _mm(a, b):
    return jnp.dot(a.astype(BF16), b.astype(BF16), preferred_element_type=F32)


def _mm_nt(a, b):
    return lax.dot_general(a.astype(BF16), b.astype(BF16), (((1,), (1,)), ((), ())),
                           preferred_element_type=F32)


def _split_bf16(x):
    hi = x.astype(BF16)
    lo = (x - hi.astype(F32)).astype(BF16)
    return hi, lo


def _mm3(a, b):
    ah, al = _split_bf16(a)
    bh, bl = _split_bf16(b)
    d = functools.partial(jnp.dot, preferred_element_type=F32)
    return d(ah, bh) + (d(ah, bl) + d(al, bh))


def _sigmoid(x):
    return 1.0 / (1.0 + jnp.exp(-x))


def _rms(x, g, width):
    ms = jnp.sum(x * x, axis=-1, keepdims=True) * (1.0 / width)
    return x * lax.rsqrt(ms + RMS_EPS) * g


def _gelu_tanh(x):
    return 0.5 * x * (1.0 + jnp.tanh(math.sqrt(2.0 / math.pi) * (x + 0.044715 * (x * x * x))))


def _ada_kernel(c_ref, w_ref, b_ref, o_ref):
    c = c_ref[...]
    o_ref[...] = _mm3(c * _sigmoid(c), w_ref[...]) + b_ref[...]


def _ada(c_all, w_ada, b_ada):
    rows = c_all.shape[0]
    tn = 512
    n_out = w_ada.shape[1]
    return pl.pallas_call(
        _ada_kernel,
        grid=(n_out // tn,),
        in_specs=[pl.BlockSpec((rows, D_MODEL), lambda j: (0, 0)),
                  pl.BlockSpec((D_MODEL, tn), lambda j: (0, j)),
                  pl.BlockSpec((1, tn), lambda j: (0, j))],
        out_specs=pl.BlockSpec((rows, tn), lambda j: (0, j)),
        out_shape=jax.ShapeDtypeStruct((rows, n_out), F32),
        compiler_params=_cparams(("parallel",)),
        name="adaln",
    )(c_all, w_ada, b_ada)


def _inproj_kernel(x_ref, sc_ref, sh_ref, gpre_ref, win_ref, gq_ref, wuq_ref, wuk_ref, gkv_ref,
                   cos_ref, sin_ref, q_ref, kcat_ref, vt_ref, kv_ref, kr_ref, u_ref):
    x = x_ref[0]
    h = _rms(x, gpre_ref[...], D_MODEL) * (1.0 + sc_ref[0]) + sh_ref[0]
    z = _mm(h, win_ref[...])
    zq = z[:, :Q_PAD]
    zkv = z[:, Q_PAD:Q_PAD + KV_RANK]
    zr = z[:, Q_PAD + KV_RANK:Q_PAD + KV_RANK + LANES]
    zrs = z[:, Q_PAD + KV_RANK + LANES:Q_PAD + KV_RANK + 2 * LANES]
    u = z[:, Q_PAD + KV_RANK + 2 * LANES:]
    cos4 = cos_ref[0]
    sin4 = sin_ref[0]

    kvl = _rms(zkv, gkv_ref[...], KV_RANK)
    kr4 = zr * cos4 + zrs * sin4
    kv_ref[0] = kvl
    kr_ref[0] = kr4[:, :ROPE_DIM]
    kcat_ref[0] = jnp.concatenate([kvl, kr4], axis=-1).astype(BF16)
    extra = (lax.broadcasted_iota(jnp.int32, (VT_ROWS - KV_RANK, kvl.shape[0]), 0) == 0).astype(F32)
    vt_ref[0] = jnp.concatenate([kvl.T, extra], axis=0).astype(BF16)
    u_ref[0] = u.astype(BF16)

    qn = _rms(zq, gq_ref[...], Q_RANK)
    q = _mm(qn, wuq_ref[...])
    nope_w = N_HEADS * NOPE_DIM
    rope_a = q[:, nope_w:nope_w + LANES] * cos4 + q[:, nope_w + 2 * LANES:nope_w + 3 * LANES] * sin4
    rope_b = q[:, nope_w + LANES:nope_w + 2 * LANES] * cos4 + q[:, nope_w + 3 * LANES:] * sin4
    slot = lax.broadcasted_iota(jnp.int32, (1, LANES), 1) // ROPE_DIM
    for j in range(N_HEADS // 2):
        ql = _mm(q[:, j * LANES:(j + 1) * LANES], wuk_ref[j])
        for hh in range(2):
            head = 2 * j + hh
            rope = rope_a if head < 4 else rope_b
            rope_h = jnp.where(slot == (head % 4), rope, 0.0)
            qc = jnp.concatenate([ql[:, hh * KV_RANK:(hh + 1) * KV_RANK], rope_h], axis=-1)
            q_ref[0, head] = (qc * (ATTN_SCALE * LOG2E)).T.astype(BF16)


def _inproj(x, sc, sh, p, cos4, sin4, tm):
    n, l, _ = x.shape
    per_row = sc.shape[1] != 1
    mod_blk = (1, tm, D_MODEL) if per_row else (1, 1, D_MODEL)
    mod_map = (lambda b, i: (b, i, 0)) if per_row else (lambda b, i: (b, 0, 0))
    const2 = lambda b, i: (0, 0)
    const3 = lambda b, i: (0, 0, 0)
    pos_map = lambda b, i: (0, i, 0)
    return pl.pallas_call(
        _inproj_kernel,
        grid=(n, l // tm),
        in_specs=[pl.BlockSpec((1, tm, D_MODEL), lambda b, i: (b, i, 0)),
                  pl.BlockSpec(mod_blk, mod_map),
                  pl.BlockSpec(mod_blk, mod_map),
                  pl.BlockSpec((1, D_MODEL), const2),
                  pl.BlockSpec((D_MODEL, IN_PAD), const2),
                  pl.BlockSpec((1, Q_PAD), const2),
                  pl.BlockSpec((Q_PAD, UQ_PAD), const2),
                  pl.BlockSpec((N_HEADS // 2, LANES, 2 * KV_RANK), const3),
                  pl.BlockSpec((1, KV_RANK), const2),
                  pl.BlockSpec((1, tm, LANES), pos_map),
                  pl.BlockSpec((1, tm, LANES), pos_map)],
        out_specs=[pl.BlockSpec((1, N_HEADS, CAT, tm), lambda b, i: (b, 0, 0, i)),
                   pl.BlockSpec((1, tm, CAT), lambda b, i: (b, i, 0)),
                   pl.BlockSpec((1, VT_ROWS, tm), lambda b, i: (b, 0, i)),
                   pl.BlockSpec((1, tm, KV_RANK), lambda b, i: (b, i, 0)),
                   pl.BlockSpec((1, tm, ROPE_DIM), lambda b, i: (b, i, 0)),
                   pl.BlockSpec((1, tm, SSM_WIDTH), lambda b, i: (b, i, 0))],
        out_shape=[jax.ShapeDtypeStruct((n, N_HEADS, CAT, l), BF16),
                   jax.ShapeDtypeStruct((n, l, CAT), BF16),
                   jax.ShapeDtypeStruct((n, VT_ROWS, l), BF16),
                   jax.ShapeDtypeStruct((n, l, KV_RANK), F32),
                   jax.ShapeDtypeStruct((n, l, ROPE_DIM), F32),
                   jax.ShapeDtypeStruct((n, l, SSM_WIDTH), BF16)],
        compiler_params=_cparams(("parallel", "parallel")),
        name="inproj",
    )(x, sc, sh, p["g_pre_mix"], p["w_in"], p["g_q"], p["w_uq"], p["w_uk"], p["g_kv"], cos4, sin4)


def _flash_kernel(qt_ref, k_ref, vt_ref, wuvt_ref, o_ref, m_ref, acc_ref, *, tq, tk):
    i = pl.program_id(1)
    j = pl.program_id(2)
    j_last = ((i + 1) * tq - 1) // tk

    @pl.when(j == 0)
    def _():
        m_ref[...] = jnp.full(m_ref.shape, NEG_INF, F32)
        acc_ref[...] = jnp.zeros(acc_ref.shape, F32)

    def update(masked, key0, size):
        k = k_ref[0, pl.ds(key0, size), :]
        vt = vt_ref[0, :, pl.ds(key0, size)]
        if masked:
            k_pos = j * tk + key0 + lax.broadcasted_iota(jnp.int32, (size, tq), 0)
            q_pos = i * tq + lax.broadcasted_iota(jnp.int32, (size, tq), 1)
            keep = k_pos <= q_pos
        m_all = m_ref[...]
        m_rows = []
        scores = lambda h: jnp.dot(k, qt_ref[0, h], preferred_element_type=F32)
        ahead = [scores(h) for h in range(min(FLASH_LOOKAHEAD, N_HEADS))]
        pending = None
        for h in range(N_HEADS):
            s = ahead.pop(0)
            if h + FLASH_LOOKAHEAD < N_HEADS:
                ahead.append(scores(h + FLASH_LOOKAHEAD))
            if masked:
                s = jnp.where(keep, s, NEG_INF)
            m_prev = m_all[h:h + 1]
            m_new = jnp.maximum(m_prev, jnp.max(s, axis=0, keepdims=True))
            alpha = jnp.exp2(m_prev - m_new)
            p = jnp.exp2(s - m_new)
            pv = jnp.dot(vt, p.astype(BF16), preferred_element_type=F32)
            if pending is not None:
                acc_ref[pending[0]] = pending[1] * acc_ref[pending[0]] + pending[2]
            pending = (h, alpha, pv)
            m_rows.append(m_new)
        acc_ref[pending[0]] = pending[1] * acc_ref[pending[0]] + pending[2]
        m_ref[...] = jnp.concatenate(m_rows, axis=0)

    crosses = (j + 1) * tk - 1 > i * tq
    sub = min(tk, FLASH_DIAG_SUB)

    @pl.when(jnp.logical_and(j <= j_last, crosses))
    def _():
        n_sub = jnp.minimum(tk // sub, ((i + 1) * tq - 1 - j * tk) // sub + 1)

        def body(sb, carry):
            update(True, pl.multiple_of(sb * sub, sub), sub)
            return carry
        lax.fori_loop(0, n_sub, body, 0)

    @pl.when(jnp.logical_and(j <= j_last, jnp.logical_not(crosses)))
    def _():
        update(False, 0, tk)

    @pl.when(j == j_last)
    def _():
        outs = []
        for h in range(N_HEADS):
            o_t = acc_ref[h, :KV_RANK, :] / acc_ref[h, KV_RANK:KV_RANK + 1, :]
            outs.append(_mm(wuvt_ref[h], o_t))
        o_ref[0] = jnp.concatenate(outs, axis=0).T.astype(BF16)


def _flash(q_t, k_cat, v_t, wuv_t, tq, tk):
    n, _, _, l = q_t.shape
    kern = functools.partial(_flash_kernel, tq=tq, tk=tk)
    last = lambda i: ((i + 1) * tq - 1) // tk
    return pl.pallas_call(
        kern,
        grid=(n, l // tq, l // tk),
        in_specs=[pl.BlockSpec((1, N_HEADS, CAT, tq), lambda b, i, j: (b, 0, 0, i)),
                  pl.BlockSpec((1, tk, CAT), lambda b, i, j: (b, jnp.minimum(j, last(i)), 0)),
                  pl.BlockSpec((1, VT_ROWS, tk), lambda b, i, j: (b, 0, jnp.minimum(j, last(i)))),
                  pl.BlockSpec((N_HEADS, V_DIM, KV_RANK), lambda b, i, j: (0, 0, 0))],
        out_specs=pl.BlockSpec((1, tq, ATTN_WIDTH), lambda b, i, j: (b, i, 0)),
        out_shape=jax.ShapeDtypeStruct((n, l, ATTN_WIDTH), BF16),
        scratch_shapes=[pltpu.VMEM((N_HEADS, tq), F32),
                        pltpu.VMEM((N_HEADS, VT_ROWS, tq), F32)],
        compiler_params=_cparams(("parallel", "parallel", "arbitrary")),
        name="flash",
    )(q_t, k_cat, v_t, wuv_t)


def _ssm_out(y, up, d_ref, wglu_ref, bglu_ref):
    y = _gelu_tanh(y + d_ref[...] * up)
    return y * _sigmoid(_mm(y, wglu_ref[...]) + bglu_ref[...])


def _s5_kernel(u_ref, perm_ref, permt_ref, bmat_ref, cmat_ref, lam_ref, laml_ref, d_ref, wglu_ref,
               bglu_ref, y_ref, hlast_ref, buf_ref, carry_ref, *, lc):
    j = pl.program_id(1)

    @pl.when(j == 0)
    def _():
        carry_ref[...] = jnp.zeros(carry_ref.shape, F32)

    u = u_ref[0]
    up = jnp.dot(perm_ref[...], u, preferred_element_type=F32).astype(BF16)
    blk_w = 2 * BLOCK_STATES

    def input_proj(b):
        buf_ref[:, b * blk_w:(b + 1) * blk_w] = jnp.dot(
            up[:, b * LANES:(b + 1) * LANES], bmat_ref[b], preferred_element_type=F32)

    input_proj(0)
    ys = []
    for b in range(SSM_BLOCKS):
        if b + 1 < SSM_BLOCKS:
            input_proj(b + 1)
        st = slice(b * BLOCK_STATES, (b + 1) * BLOCK_STATES)
        re_cols = slice(b * blk_w, b * blk_w + BLOCK_STATES)
        im_cols = slice(b * blk_w + BLOCK_STATES, (b + 1) * blk_w)
        lr = lam_ref[0, :, st]
        li = lam_ref[1, :, st]

        def run(init, store):
            def body(t, carry):
                re, im = carry
                rows = pl.ds(pl.multiple_of(t * SUBLANES, SUBLANES), SUBLANES)
                n_re = lr * re - li * im + buf_ref[rows, re_cols]
                n_im = lr * im + li * re + buf_ref[rows, im_cols]
                if store:
                    buf_ref[rows, re_cols] = n_re
                    buf_ref[rows, im_cols] = n_im
                return n_re, n_im
            return lax.fori_loop(0, lc, body, init)

        zero = jnp.zeros((SUBLANES, BLOCK_STATES), F32)
        end_re, end_im = run((zero, zero), False)
        ll_r = laml_ref[0:1, st]
        ll_i = laml_ref[1:2, st]
        c_re = carry_ref[0:1, st]
        c_im = carry_ref[1:2, st]
        init_re, init_im = [], []
        for k in range(SUBLANES):
            init_re.append(c_re)
            init_im.append(c_im)
            c_re, c_im = (ll_r * c_re - ll_i * c_im + end_re[k:k + 1],
                          ll_r * c_im + ll_i * c_re + end_im[k:k + 1])
        carry_ref[0:1, st] = c_re
        carry_ref[1:2, st] = c_im
        run((jnp.concatenate(init_re, axis=0), jnp.concatenate(init_im, axis=0)), True)
        ys.append(jnp.dot(buf_ref[:, b * blk_w:(b + 1) * blk_w].astype(BF16), cmat_ref[b],
                          preferred_element_type=F32))

    hlast_ref[0] = carry_ref[...]
    y = _ssm_out(jnp.concatenate(ys, axis=-1), up.astype(F32), d_ref, wglu_ref, bglu_ref)
    y_ref[0] = jnp.dot(permt_ref[...], y.astype(BF16), preferred_element_type=F32).astype(BF16)


def _s5_prompt(u, sp, lc):
    n, l, _ = u.shape
    s = SUBLANES * lc
    rows = jnp.arange(s)
    src = (rows % SUBLANES) * lc + rows // SUBLANES
    perm = (src[:, None] == rows[None, :]).astype(BF16)
    laml = _complex_pow2(sp["lam"], lc)
    lam8 = jnp.broadcast_to(sp["lam"][:, None, :], (2, SUBLANES, N_STATE))
    kern = functools.partial(_s5_kernel, lc=lc)
    c2 = lambda b, j: (0, 0)
    c3 = lambda b, j: (0, 0, 0)
    return pl.pallas_call(
        kern,
        grid=(n, l // s),
        in_specs=[pl.BlockSpec((1, s, SSM_WIDTH), lambda b, j: (b, j, 0)),
                  pl.BlockSpec((s, s), c2),
                  pl.BlockSpec((s, s), c2),
                  pl.BlockSpec((SSM_BLOCKS, LANES, 2 * BLOCK_STATES), c3),
                  pl.BlockSpec((SSM_BLOCKS, 2 * BLOCK_STATES, LANES), c3),
                  pl.BlockSpec((2, SUBLANES, N_STATE), c3),
                  pl.BlockSpec((2, N_STATE), c2),
                  pl.BlockSpec((1, SSM_WIDTH), c2),
                  pl.BlockSpec((SSM_WIDTH, SSM_WIDTH), c2),
                  pl.BlockSpec((1, SSM_WIDTH), c2)],
        out_specs=[pl.BlockSpec((1, s, SSM_WIDTH), lambda b, j: (b, j, 0)),
                   pl.BlockSpec((1, 2, N_STATE), lambda b, j: (b, 0, 0))],
        out_shape=[jax.ShapeDtypeStruct((n, l, SSM_WIDTH), BF16),
                   jax.ShapeDtypeStruct((n, 2, N_STATE), F32)],
        scratch_shapes=[pltpu.VMEM((s, 2 * N_STATE), F32),
                        pltpu.VMEM((2, N_STATE), F32)],
        compiler_params=_cparams(("parallel", "arbitrary")),
        name="s5_scan",
    )(u, perm, perm.T, sp["bmat"], sp["cmat"], lam8, laml, sp["d"], sp["w_glu"], sp["b_glu"])


def _s5_step_kernel(u_ref, h0_ref, bmat_ref, cmat_ref, lam_ref, d_ref, wglu_ref, bglu_ref,
                    y_ref, h_ref):
    u = u_ref[...]
    ys = []
    for b in range(SSM_BLOCKS):
        st = slice(b * BLOCK_STATES, (b + 1) * BLOCK_STATES)
        bu = jnp.dot(u[:, b * LANES:(b + 1) * LANES], bmat_ref[b], preferred_element_type=F32)
        lr = lam_ref[0:1, st]
        li = lam_ref[1:2, st]
        re = h0_ref[0, :, st]
        im = h0_ref[1, :, st]
        n_re = lr * re - li * im + bu[:, :BLOCK_STATES]
        n_im = lr * im + li * re + bu[:, BLOCK_STATES:]
        h_ref[0, :, st] = n_re
        h_ref[1, :, st] = n_im
        ys.append(_mm(jnp.concatenate([n_re, n_im], axis=-1), cmat_ref[b]))
    y = _ssm_out(jnp.concatenate(ys, axis=-1), u.astype(F32), d_ref, wglu_ref, bglu_ref)
    y_ref[...] = y.astype(BF16)


def _s5_step(u, h0, sp):
    rows = u.shape[0]
    return pl.pallas_call(
        _s5_step_kernel,
        out_shape=[jax.ShapeDtypeStruct((rows, SSM_WIDTH), BF16),
                   jax.ShapeDtypeStruct((2, rows, N_STATE), F32)],
        compiler_params=pltpu.CompilerParams(vmem_limit_bytes=VMEM_LIMIT),
        name="s5_step",
    )(u, h0, sp["bmat"], sp["cmat"], sp["lam"], sp["d"], sp["w_glu"], sp["b_glu"])


def _decode_kernel(pt_ref, ql_ref, qr_ref, kvs_ref, krs_ref, lat_hbm, rope_hbm, o_ref,
                   lat_buf, rope_buf, sems, *, n_pages):
    b = pl.program_id(0)
    slot = b % 2

    def page_copies(seq, dst_slot):
        out = []
        for i in range(n_pages):
            page = pt_ref[seq * n_pages + i]
            keys = pl.ds(i * PAGE_SIZE, PAGE_SIZE)
            out.append(pltpu.make_async_copy(lat_hbm.at[page], lat_buf.at[dst_slot, keys, :],
                                             sems.at[dst_slot, 0]))
            out.append(pltpu.make_async_copy(rope_hbm.at[page], rope_buf.at[dst_slot, :, keys],
                                             sems.at[dst_slot, 1]))
        return out

    @pl.when(b == 0)
    def _():
        for cp in page_copies(0, 0):
            cp.start()

    @pl.when(b + 1 < pl.num_programs(0))
    def _():
        for cp in page_copies(b + 1, 1 - slot):
            cp.start()

    for cp in page_copies(b, slot):
        cp.wait()

    ql = ql_ref[0]
    qr = qr_ref[0]
    kvs = kvs_ref[0]
    krs = krs_ref[0]
    lat = lat_buf[slot].astype(BF16)
    s = _mm_nt(ql, lat) + _mm(qr, rope_buf[slot])
    s_self = (jnp.sum(ql.astype(F32) * kvs, axis=-1, keepdims=True)
              + jnp.sum(qr.astype(F32) * krs, axis=-1, keepdims=True))
    m = jnp.maximum(jnp.max(s, axis=-1, keepdims=True), s_self)
    p = jnp.exp2(s - m)
    p_self = jnp.exp2(s_self - m)
    denom = jnp.sum(p, axis=-1, keepdims=True) + p_self
    o_ref[0] = (_mm(p, lat) + p_self * kvs) / denom


def _decode_attn(q_lat, q_rope, kv_self, kr_self, cache_lat, cache_rope_t, page_table):
    b, n_pages = page_table.shape
    kern = functools.partial(_decode_kernel, n_pages=n_pages)
    seq3 = lambda s, pt: (s, 0, 0)
    keys = n_pages * PAGE_SIZE
    grid_spec = pltpu.PrefetchScalarGridSpec(
        num_scalar_prefetch=1,
        grid=(b,),
        in_specs=[pl.BlockSpec((1, N_HEADS, KV_RANK), seq3),
                  pl.BlockSpec((1, N_HEADS, ROPE_DIM), seq3),
                  pl.BlockSpec((1, 1, KV_RANK), seq3),
                  pl.BlockSpec((1, 1, ROPE_DIM), seq3),
                  pl.BlockSpec(memory_space=pl.ANY),
                  pl.BlockSpec(memory_space=pl.ANY)],
        out_specs=pl.BlockSpec((1, N_HEADS, KV_RANK), seq3),
        scratch_shapes=[pltpu.VMEM((2, keys, KV_RANK), F32),
                        pltpu.VMEM((2, ROPE_DIM, keys), F32),
                        pltpu.SemaphoreType.DMA((2, 2))],
    )
    return pl.pallas_call(
        kern,
        grid_spec=grid_spec,
        out_shape=jax.ShapeDtypeStruct((b, N_HEADS, KV_RANK), F32),
        compiler_params=_cparams(("arbitrary",)),
        name="decode_attn",
    )(page_table.reshape(-1), q_lat, q_rope, kv_self, kr_self, cache_lat, cache_rope_t)


def _uv_kernel(o_ref, w_ref, y_ref):
    y_ref[...] = _mm(o_ref[...], w_ref[...]).astype(BF16)


def _uv_proj(o_lat, wuv_bd):
    rows = o_lat.shape[0]
    return pl.pallas_call(
        _uv_kernel,
        out_shape=jax.ShapeDtypeStruct((rows, ATTN_WIDTH), BF16),
        compiler_params=pltpu.CompilerParams(vmem_limit_bytes=VMEM_LIMIT),
        name="uv_proj",
    )(o_lat, wuv_bd)


def _outproj_kernel(oa_ref, ys_ref, x_ref, g1_ref, sc_ref, sh_ref, wout_ref, gpost_ref, gpre_ref,
                    wr_ref, br_ref, *rest, row0):
    x1_ref, h2_ref, route_ref = rest[-3:]
    mix = (jnp.dot(oa_ref[0], wout_ref[:ATTN_WIDTH].astype(BF16), preferred_element_type=F32)
           + jnp.dot(ys_ref[0], wout_ref[ATTN_WIDTH:].astype(BF16), preferred_element_type=F32))
    x1 = x_ref[0] + g1_ref[0] * _rms(mix, gpost_ref[...], D_MODEL)
    x1_ref[0] = x1
    h2 = _rms(x1, gpre_ref[...], D_MODEL) * (1.0 + sc_ref[0]) + sh_ref[0]
    for s in range(D_MODEL // LANES):
        h2_ref[:, s, :] = h2[:, s * LANES:(s + 1) * LANES]
    logits = _mm3(h2, wr_ref[...]) + br_ref[...]
    lane = lax.broadcasted_iota(jnp.int32, logits.shape, 1).astype(F32)
    vals, idxs = [], []
    for _ in range(TOP_K):
        v = jnp.max(logits, axis=-1, keepdims=True)
        idx = jnp.min(jnp.where(logits == v, lane, float(LANES)), axis=-1, keepdims=True)
        vals.append(v)
        idxs.append(idx)
        logits = jnp.where(lane == idx, -jnp.inf, logits)
    exps = [jnp.exp(v - vals[0]) for v in vals]
    denom = exps[0] + exps[1] + exps[2] + exps[3]
    tm = logits.shape[0]
    first = row0 + (pl.program_id(0) * pl.num_programs(1) + pl.program_id(1)) * tm
    token = (first + lax.broadcasted_iota(jnp.int32, (tm, 1), 0)).astype(F32)
    route = jnp.where(lane == float(ROUTE_TOKEN), token, jnp.where(lane == float(ROUTE_VALID), 1.0, 0.0))
    for k in range(TOP_K):
        route = jnp.where(lane == float(k), idxs[k], route)
        route = jnp.where(lane == float(TOP_K + k), exps[k] / denom, route)
    route_ref[...] = route


def _outproj(o_attn, y_ssm, x, g1, sc2, sh2, p, tm, t_all, row0, dest=None):
    n, l, _ = x.shape
    per_row = g1.shape[1] != 1
    mod_blk = (1, tm, D_MODEL) if per_row else (1, 1, D_MODEL)
    mod_map = (lambda b, i: (b, i, 0)) if per_row else (lambda b, i: (b, 0, 0))
    row_map = lambda b, i: (b, i, 0)
    const2 = lambda b, i: (0, 0)
    blk0, per_seq = row0 // tm, l // tm
    n_tile = D_MODEL // LANES
    in_specs = [pl.BlockSpec((1, tm, ATTN_WIDTH), row_map),
                pl.BlockSpec((1, tm, SSM_WIDTH), row_map),
                pl.BlockSpec((1, tm, D_MODEL), row_map),
                pl.BlockSpec(mod_blk, mod_map),
                pl.BlockSpec(mod_blk, mod_map),
                pl.BlockSpec(mod_blk, mod_map),
                pl.BlockSpec((D_MODEL, D_MODEL), const2),
                pl.BlockSpec((1, D_MODEL), const2),
                pl.BlockSpec((1, D_MODEL), const2),
                pl.BlockSpec((D_MODEL, LANES), const2),
                pl.BlockSpec((1, LANES), const2)]
    args = [o_attn, y_ssm, x, g1, sc2, sh2, p["w_out"], p["g_post_mix"], p["g_pre_ffn"],
            p["w_router"], p["b_router"]]
    aliases = {}
    if dest is not None:
        aliases = {len(args): 1, len(args) + 1: 2}
        in_specs += [pl.BlockSpec(memory_space=pl.ANY), pl.BlockSpec(memory_space=pl.ANY)]
        args += list(dest)
    return pl.pallas_call(
        functools.partial(_outproj_kernel, row0=row0),
        grid=(n, per_seq),
        in_specs=in_specs,
        out_specs=[pl.BlockSpec((1, tm, D_MODEL), row_map),
                   pl.BlockSpec((tm, n_tile, LANES), lambda b, i: (blk0 + b * per_seq + i, 0, 0)),
                   pl.BlockSpec((tm, LANES), lambda b, i: (blk0 + b * per_seq + i, 0))],
        out_shape=[jax.ShapeDtypeStruct((n, l, D_MODEL), F32),
                   jax.ShapeDtypeStruct((t_all, n_tile, LANES), F32),
                   jax.ShapeDtypeStruct((t_all, LANES), F32)],
        input_output_aliases=aliases,
        compiler_params=_cparams(("parallel", "parallel")),
        name="outproj",
    )(*args)


def _moe_prep_kernel(w1_ref, w2_ref, perm_ref, w1o_ref, w2o_ref):
    for j in range(w1_ref.shape[2] // GLU_BLOCK):
        cols = slice(j * GLU_BLOCK, (j + 1) * GLU_BLOCK)
        w1o_ref[0, :, cols] = jnp.dot(w1_ref[0, :, cols].astype(BF16), perm_ref[...],
                                      preferred_element_type=F32).astype(BF16)
    w2o_ref[0] = w2_ref[0].astype(BF16)


def _moe_prep(w1, w2):
    n_e, d_in, d_hid = w1.shape
    d_exp = w2.shape[1]
    j = jnp.arange(GLU_BLOCK)
    src = jnp.where(j < LANES, 2 * j, 2 * (j - LANES) + 1)
    perm = (j[:, None] == src[None, :]).astype(BF16)
    e_map = lambda e: (e, 0, 0)
    return pl.pallas_call(
        _moe_prep_kernel,
        grid=(n_e,),
        in_specs=[pl.BlockSpec((1, d_in, d_hid), e_map),
                  pl.BlockSpec((1, d_exp, D_MODEL), e_map),
                  pl.BlockSpec((GLU_BLOCK, GLU_BLOCK), lambda e: (0, 0))],
        out_specs=[pl.BlockSpec((1, d_in, d_hid), e_map),
                   pl.BlockSpec((1, d_exp, D_MODEL), e_map)],
        out_shape=[jax.ShapeDtypeStruct(w1.shape, BF16),
                   jax.ShapeDtypeStruct(w2.shape, BF16)],
        compiler_params=_cparams(("parallel",)),
        name="moe_prep",
    )(w1, w2, perm)


def _sort_rows_kernel(pos_ref, route_ref, rows_ref):
    rows_ref[...] = jnp.zeros(rows_ref.shape, F32)

    def body(t, carry):
        row = route_ref[0, pl.ds(t, 1), :]
        for k in range(TOP_K):
            rows_ref[0, pl.ds(pos_ref[0, 0, TOP_K * t + k], 1), :] = row
        return carry
    lax.fori_loop(0, route_ref.shape[1], body, 0, unroll=4)


def _sort_rows(route, pos, nch, rc):
    t = route.shape[0]
    tc = t // nch
    return pl.pallas_call(
        _sort_rows_kernel,
        grid=(nch,),
        in_specs=[pl.BlockSpec((1, 1, tc * TOP_K), lambda c: (c, 0, 0), memory_space=pltpu.SMEM),
                  pl.BlockSpec((1, tc, LANES), lambda c: (c, 0, 0))],
        out_specs=pl.BlockSpec((1, rc, LANES), lambda c: (c, 0, 0)),
        out_shape=jax.ShapeDtypeStruct((nch, rc, LANES), F32),
        compiler_params=_cparams(("parallel",)),
        name="moe_sort_rows",
    )(pos.reshape(nch, 1, tc * TOP_K), route.reshape(nch, tc, LANES))


def _positions_kernel(route_ref, off_ref, tri_ref, pos_ref, cnt_ref):
    @pl.when(pl.program_id(1) == 0)
    def _():
        cnt_ref[...] = jnp.zeros(cnt_ref.shape, F32)

    r = route_ref[0]
    lane = lax.broadcasted_iota(jnp.int32, r.shape, 1).astype(F32)
    picks = [lane == r[:, k:k + 1] for k in range(TOP_K)]
    hot = jnp.zeros(r.shape, F32)
    for pick in picks:
        hot = hot + jnp.where(pick, 1.0, 0.0)
    rank = jnp.dot(tri_ref[...], hot.astype(BF16), preferred_element_type=F32) + cnt_ref[...]
    cnt_ref[...] = cnt_ref[...] + jnp.sum(hot, axis=0, keepdims=True)
    base = off_ref[0] + rank
    pos = jnp.zeros(r.shape, F32)
    for k, pick in enumerate(picks):
        pos = jnp.where(lane == float(k), jnp.sum(jnp.where(pick, base, 0.0), axis=-1, keepdims=True), pos)
    pos_ref[0] = pos


def _positions(route, off, nch):
    t = route.shape[0]
    tc = t // nch
    tp = _pick_tile(tc, 1024, SUBLANES)
    assert tc % tp == 0
    tri = (jnp.arange(tp)[:, None] > jnp.arange(tp)[None, :]).astype(BF16)
    pos = pl.pallas_call(
        _positions_kernel,
        grid=(nch, tc // tp),
        in_specs=[pl.BlockSpec((1, tp, LANES), lambda c, j: (c, j, 0)),
                  pl.BlockSpec((1, 1, LANES), lambda c, j: (c, 0, 0)),
                  pl.BlockSpec((tp, tp), lambda c, j: (0, 0))],
        out_specs=pl.BlockSpec((1, tp, LANES), lambda c, j: (c, j, 0)),
        out_shape=jax.ShapeDtypeStruct((nch, tc, LANES), F32),
        scratch_shapes=[pltpu.VMEM((1, LANES), F32)],
        compiler_params=_cparams(("parallel", "arbitrary")),
        name="moe_positions",
    )(route.reshape(nch, tc, LANES), off, tri)
    return pos[:, :, :TOP_K].astype(jnp.int32).reshape(t, TOP_K)


def _route_plan(route, nch, tmx):
    idx = route[:, :TOP_K].astype(jnp.int32)
    t = idx.shape[0]
    tc = t // nch
    nt = -(-(tc * TOP_K + N_EXPERTS * (tmx - 1)) // tmx)
    i32 = jnp.int32
    hot = (idx[:, :, None] == jnp.arange(N_EXPERTS, dtype=i32)).astype(i32).sum(1)
    counts = hot.reshape(nch, tc, N_EXPERTS).sum(1)
    padded = -(-counts // tmx) * tmx
    off_end = jnp.cumsum(padded, axis=1)
    off = off_end - padded
    n_tiles = (off_end[:, -1] // tmx).astype(i32)
    off_lanes = jnp.pad(off.astype(F32), ((0, 0), (0, LANES - N_EXPERTS)))[:, None, :]
    pos = _positions(route, off_lanes, nch)
    rc = nt * tmx
    rows = _sort_rows(route, pos, nch, rc)
    valid = rows[:, :, ROUTE_VALID] > 0.0
    row_tok = rows[:, :, ROUTE_TOKEN].astype(i32) % tc
    row_dst = jnp.where(valid, row_tok, tc + jnp.arange(rc, dtype=i32)[None, :] % SUBLANES)
    tile = jnp.arange(nt + 1, dtype=i32)
    texp_raw = (off_end[:, None, :] <= (tile * tmx)[None, :, None]).sum(-1).astype(i32)
    last = n_tiles[:, None] - 1
    tile_c = jnp.minimum(tile[None, :], last)
    tile_n = jnp.minimum(tile[None, :] + 1, last)
    tile_p = jnp.clip(tile[None, :] - 1, 0, last)
    texp = jnp.take_along_axis(texp_raw, tile_c, axis=1)
    base = (jnp.arange(nch, dtype=i32) * nt)[:, None]
    return dict(nt=nt, n_tiles=n_tiles, texp=texp.reshape(-1), tblk=(tile_c + base).reshape(-1),
                tnext=(tile_n + base).reshape(-1), tprev=(tile_p + base).reshape(-1),
                row_tok=row_tok.reshape(nch * nt, 1, tmx), row_dst=row_dst.reshape(nch * nt, 1, tmx),
                rows=rows.reshape(nch * nt, tmx, LANES))


def _moe_kernel(nt_ref, texp_ref, tblk_ref, tnext_ref, tprev_ref, tok_ref, toknext_ref, dstprev_ref,
                rows_ref, src_ref, w1_ref, b1_ref, w2_ref, b2_ref, out_ref, xs_ref, yt_ref):
    del tblk_ref, tnext_ref, tprev_ref
    c = pl.program_id(0)
    i = pl.program_id(1)
    tmx = rows_ref.shape[1]
    slot = i % 2
    n_grp = tmx // SUBLANES

    def add_prev_rows(groups):
        for g in groups:
            toks = [dstprev_ref[0, 0, g * SUBLANES + u] for u in range(SUBLANES)]
            sums = [out_ref[0, toks[u]]
                    + yt_ref[1 - slot, (g * SUBLANES + u) * SUBLANES:(g * SUBLANES + u + 1) * SUBLANES, :]
                    for u in range(SUBLANES)]
            for u in range(SUBLANES):
                out_ref[0, toks[u]] = sums[u]

    @pl.when(i == 0)
    def _():
        out_ref[...] = jnp.zeros(out_ref.shape, F32)
        yt_ref[1] = jnp.zeros(yt_ref.shape[1:], F32)

        def gather(r, carry):
            xs_ref[0, pl.ds(pl.multiple_of(r * SUBLANES, SUBLANES), SUBLANES), :] = src_ref[0, tok_ref[0, 0, r]]
            return carry
        lax.fori_loop(0, tmx, gather, 0, unroll=8)

    @pl.when(i < nt_ref[c])
    def _():
        x = jnp.concatenate([xs_ref[slot, pl.ds(s, tmx, stride=SUBLANES), :]
                             for s in range(D_MODEL // LANES)], axis=-1).astype(BF16)
        n_blk = w1_ref.shape[2] // (2 * GLU_BLOCK)
        wide = 2 * GLU_BLOCK

        def hidden(j):
            cols = slice(j * wide, (j + 1) * wide)
            return jnp.dot(x, w1_ref[0, :, cols], preferred_element_type=F32) + b1_ref[0, :, cols]

        hid = hidden(0)
        y = None
        for j in range(n_blk):
            nxt = hidden(j + 1) if j + 1 < n_blk else None
            for r in range(j * tmx // n_blk, (j + 1) * tmx // n_blk):
                xs_ref[1 - slot, r * SUBLANES:(r + 1) * SUBLANES, :] = src_ref[0, toknext_ref[0, 0, r]]
            add_prev_rows(range(j * n_grp // n_blk, (j + 1) * n_grp // n_blk))
            acts = []
            for b in range(2):
                glu = jnp.minimum(hid[:, b * GLU_BLOCK:b * GLU_BLOCK + LANES], SWIGLU_LIMIT)
                lin = jnp.clip(hid[:, b * GLU_BLOCK + LANES:(b + 1) * GLU_BLOCK], -SWIGLU_LIMIT, SWIGLU_LIMIT)
                acts.append(glu * _sigmoid(SWIGLU_ALPHA * glu) * (lin + 1.0))
            part = _mm(jnp.concatenate(acts, axis=-1), w2_ref[0, j * GLU_BLOCK:(j + 1) * GLU_BLOCK, :])
            y = part if y is None else y + part
            hid = nxt
        meta = rows_ref[0]
        expert = texp_ref[c * pl.num_programs(1) + i].astype(F32)
        lane = lax.broadcasted_iota(jnp.int32, meta.shape, 1)
        hit = jnp.logical_and(lane < TOP_K, meta == expert)
        gates = pltpu.roll(meta, LANES - TOP_K, axis=1)
        gate = jnp.sum(jnp.where(hit, gates, 0.0), axis=-1, keepdims=True)
        y = gate * (y + b2_ref[0])
        for s in range(D_MODEL // LANES):
            yt_ref[slot, pl.ds(s, tmx, stride=SUBLANES), :] = y[:, s * LANES:(s + 1) * LANES]

    @pl.when(i == nt_ref[c])
    def _():
        add_prev_rows(range(n_grp))


def _moe_experts(h2, plan, w1p, b1p, w2p, b2, tmx):
    nch, tc = h2.shape[:2]
    nt = plan["nt"]
    n_e, _, d_hid = w1p.shape
    d_exp = w2p.shape[1]
    steps = nt + 1
    flat = lambda c, i: c * steps + i
    cur = lambda c, i, n, te, tb, tn, tp: (tb[flat(c, i)], 0, 0)
    nxt = lambda c, i, n, te, tb, tn, tp: (tn[flat(c, i)], 0, 0)
    prv = lambda c, i, n, te, tb, tn, tp: (tp[flat(c, i)], 0, 0)
    exp = lambda c, i, n, te, tb, tn, tp: (te[flat(c, i)], 0, 0)
    grid_spec = pltpu.PrefetchScalarGridSpec(
        num_scalar_prefetch=5,
        grid=(nch, steps),
        in_specs=[pl.BlockSpec((1, 1, tmx), cur, memory_space=pltpu.SMEM),
                  pl.BlockSpec((1, 1, tmx), nxt, memory_space=pltpu.SMEM),
                  pl.BlockSpec((1, 1, tmx), prv, memory_space=pltpu.SMEM),
                  pl.BlockSpec((1, tmx, LANES), cur),
                  pl.BlockSpec((1, tc, D_MODEL // LANES, LANES), lambda c, i, n, te, tb, tn, tp: (c, 0, 0, 0),
                               pipeline_mode=pl.Buffered(1)),
                  pl.BlockSpec((1, D_MODEL, d_hid), exp),
                  pl.BlockSpec((1, 1, d_hid), exp),
                  pl.BlockSpec((1, d_exp, D_MODEL), exp),
                  pl.BlockSpec((1, 1, D_MODEL), exp)],
        out_specs=pl.BlockSpec((1, tc + SUBLANES, D_MODEL // LANES, LANES),
                               lambda c, i, n, te, tb, tn, tp: (c, 0, 0, 0), pipeline_mode=pl.Buffered(1)),
        scratch_shapes=[pltpu.VMEM((2, tmx * (D_MODEL // LANES), LANES), F32),
                        pltpu.VMEM((2, tmx * (D_MODEL // LANES), LANES), F32)],
    )
    return pl.pallas_call(
        _moe_kernel,
        grid_spec=grid_spec,
        out_shape=jax.ShapeDtypeStruct((nch, tc + SUBLANES, D_MODEL // LANES, LANES), F32),
        compiler_params=_cparams(("parallel", "arbitrary")),
        name="moe_experts",
    )(plan["n_tiles"], plan["texp"], plan["tblk"], plan["tnext"], plan["tprev"], plan["row_tok"],
      plan["row_tok"], plan["row_dst"], plan["rows"], h2, w1p, b1p, w2p, b2)


def _final_kernel(x1_ref, f_ref, g2_ref, gpost_ref, y_ref):
    tm = x1_ref.shape[1]
    f = jnp.concatenate([f_ref[pl.ds(s, tm, stride=SUBLANES), :] for s in range(D_MODEL // LANES)],
                        axis=-1)
    y_ref[0] = x1_ref[0] + g2_ref[0] * _rms(f, gpost_ref[...], D_MODEL)


def _final(x1, f_all, row0, g2, g_post_ffn, tm):
    n, l, _ = x1.shape
    per_row = g2.shape[1] != 1
    mod_blk = (1, tm, D_MODEL) if per_row else (1, 1, D_MODEL)
    mod_map = (lambda b, i: (b, i, 0)) if per_row else (lambda b, i: (b, 0, 0))
    row_map = lambda b, i: (b, i, 0)
    blk0, per_seq = row0 // tm, l // tm
    return pl.pallas_call(
        _final_kernel,
        grid=(n, l // tm),
        in_specs=[pl.BlockSpec((1, tm, D_MODEL), row_map),
                  pl.BlockSpec((tm * (D_MODEL // LANES), LANES), lambda b, i: (blk0 + b * per_seq + i, 0)),
                  pl.BlockSpec(mod_blk, mod_map),
                  pl.BlockSpec((1, D_MODEL), lambda b, i: (0, 0))],
        out_specs=pl.BlockSpec((1, tm, D_MODEL), row_map),
        out_shape=jax.ShapeDtypeStruct((n, l, D_MODEL), F32),
        compiler_params=_cparams(("parallel", "parallel")),
        name="final",
    )(x1, f_all, g2, g_post_ffn)


def _complex_pow2(lam, n):
    re, im = lam[0], lam[1]
    steps = int(round(math.log2(n)))
    assert 2 ** steps == n
    for _ in range(steps):
        re, im = re * re - im * im, 2.0 * re * im
    return jnp.stack([re, im])


def _rope_tables(pos):
    half = ROPE_DIM // 2
    inv = ROPE_THETA ** (-jnp.arange(half, dtype=F32) / half)
    ang = pos.astype(F32)[:, None] * inv[None, :]
    cos, sin = jnp.cos(ang), jnp.sin(ang)
    cos4 = jnp.tile(jnp.concatenate([cos, cos], axis=-1), (1, LANES // ROPE_DIM))
    sin4 = jnp.tile(jnp.concatenate([-sin, sin], axis=-1), (1, LANES // ROPE_DIM))
    return cos4[None], sin4[None]


def _swap_halves(w):
    half = ROPE_DIM // 2
    return jnp.concatenate([w[..., half:], w[..., :half]], axis=-1)


def _prep_layer(w, layer):
    g = lambda name: w[name][layer]
    p = {}
    row = lambda v: v.reshape(1, -1)
    p["g_pre_mix"] = row(g("g_pre_mix"))
    w_in = g("w_in")
    w_q, w_kv, w_r, w_u = jnp.split(w_in, [Q_RANK, Q_RANK + KV_RANK, Q_RANK + KV_RANK + ROPE_DIM], axis=-1)
    rep = LANES // ROPE_DIM
    p["w_in"] = jnp.concatenate(
        [w_q, jnp.zeros((D_MODEL, Q_PAD - Q_RANK), F32), w_kv,
         jnp.tile(w_r, (1, rep)), jnp.tile(_swap_halves(w_r), (1, rep)), w_u], axis=-1)
    p["g_q"] = row(jnp.pad(g("g_q"), (0, Q_PAD - Q_RANK)))
    w_uq = g("w_uq").reshape(Q_RANK, N_HEADS, NOPE_DIM + ROPE_DIM)
    nope = w_uq[:, :, :NOPE_DIM].reshape(Q_RANK, N_HEADS * NOPE_DIM)
    rope = w_uq[:, :, NOPE_DIM:]
    rope_sw = _swap_halves(rope)
    w_uq_p = jnp.concatenate([nope, rope.reshape(Q_RANK, -1), rope_sw.reshape(Q_RANK, -1)], axis=-1)
    p["w_uq"] = jnp.pad(w_uq_p, ((0, Q_PAD - Q_RANK), (0, 0)))
    w_uk = jnp.transpose(g("w_uk"), (1, 2, 0))
    z = jnp.zeros((NOPE_DIM, KV_RANK), F32)
    p["w_uk"] = jnp.stack([jnp.block([[w_uk[2 * j], z], [z, w_uk[2 * j + 1]]])
                           for j in range(N_HEADS // 2)])
    p["g_kv"] = row(g("g_kv"))
    w_uv = jnp.transpose(g("w_uv"), (1, 0, 2))
    p["w_uv_t"] = jnp.transpose(w_uv, (0, 2, 1))
    eye_h = jnp.eye(N_HEADS, dtype=F32)
    p["w_uv_bd"] = jnp.einsum("hrv,hk->hrkv", w_uv, eye_h).reshape(N_HEADS * KV_RANK, ATTN_WIDTH)
    p["w_out"] = g("w_out")
    for name in ("w_in", "w_uq", "w_uk", "w_uv_t", "w_uv_bd", "w_out"):
        p[name] = p[name].astype(BF16)
    p["g_post_mix"] = row(g("g_post_mix"))
    p["g_pre_ffn"] = row(g("g_pre_ffn"))
    p["w_router"] = jnp.pad(g("w_router"), ((0, 0), (0, LANES - N_EXPERTS)))
    p["b_router"] = row(jnp.pad(g("b_router"), (0, LANES - N_EXPERTS), constant_values=NEG_INF))
    p["g_post_ffn"] = row(g("g_post_ffn"))

    lam_re, lam_im = g("ssm_lam_re"), g("ssm_lam_im")
    step = jnp.exp(g("ssm_log_step"))[:, None]
    mag = jnp.exp(lam_re * step)
    bar_re, bar_im = mag * jnp.cos(lam_im * step), mag * jnp.sin(lam_im * step)
    den = lam_re * lam_re + lam_im * lam_im
    co_re = ((bar_re - 1.0) * lam_re + bar_im * lam_im) / den
    co_im = (bar_im * lam_re - (bar_re - 1.0) * lam_im) / den
    b_re, b_im = g("ssm_b_re"), g("ssm_b_im")
    bb_re = co_re[..., None] * b_re - co_im[..., None] * b_im
    bb_im = co_re[..., None] * b_im + co_im[..., None] * b_re
    eye_g = jnp.eye(GROUPS_PER_BLOCK, dtype=F32)

    def b_block(m):
        m = m.reshape(SSM_BLOCKS, GROUPS_PER_BLOCK, SSM_STATE, SSM_GROUP)
        return jnp.einsum("bgpc,gh->bgchp", m, eye_g).reshape(SSM_BLOCKS, LANES, BLOCK_STATES)

    def c_block(m):
        m = m.reshape(SSM_BLOCKS, GROUPS_PER_BLOCK, SSM_GROUP, SSM_STATE)
        return jnp.einsum("bgcp,gh->bgphc", m, eye_g).reshape(SSM_BLOCKS, BLOCK_STATES, LANES)

    sp = {
        "lam": jnp.stack([bar_re.reshape(-1), bar_im.reshape(-1)]),
        "bmat": jnp.concatenate([b_block(bb_re), b_block(bb_im)], axis=-1).astype(BF16),
        "cmat": jnp.concatenate([c_block(g("ssm_c_re")), c_block(-g("ssm_c_im"))], axis=1).astype(BF16),
        "d": row(g("ssm_d")),
        "w_glu": g("ssm_w_glu").astype(BF16),
        "b_glu": row(g("ssm_b_glu")),
    }
    b1 = g("b1")
    n_blk = b1.shape[1] // GLU_BLOCK
    mp = {
        "b1": jnp.transpose(b1.reshape(N_EXPERTS, n_blk, LANES, 2), (0, 1, 3, 2)).reshape(N_EXPERTS, 1, -1),
        "b2": g("b2")[:, None, :],
    }
    return p, sp, mp


def _mods(mod_rows):
    return [mod_rows[:, k * D_MODEL:(k + 1) * D_MODEL] for k in range(N_MOD)]


def _pick_tile(n, target, mult):
    best = mult
    for c in range(mult, min(n, target) + 1, mult):
        if n % c == 0:
            best = c
    return best


def kernel(x_prompt, x_sample, c_prompt, c_sample, cache_kv_latent, cache_k_rope, state_ssm, page_table, w_ada, b_ada, g_pre_mix, w_in, g_q, w_uq, w_uk, g_kv, w_uv, ssm_lam_re, ssm_lam_im, ssm_log_step, ssm_b_re, ssm_b_im, ssm_c_re, ssm_c_im, ssm_d, ssm_w_glu, ssm_b_glu, w_out, g_post_mix, g_pre_ffn, w_router, b_router, w1, b1, w2, b2, g_post_ffn):
    weights = dict(w_ada=w_ada, b_ada=b_ada, g_pre_mix=g_pre_mix, w_in=w_in, g_q=g_q, w_uq=w_uq,
                   w_uk=w_uk, g_kv=g_kv, w_uv=w_uv, ssm_lam_re=ssm_lam_re, ssm_lam_im=ssm_lam_im,
                   ssm_log_step=ssm_log_step, ssm_b_re=ssm_b_re, ssm_b_im=ssm_b_im,
                   ssm_c_re=ssm_c_re, ssm_c_im=ssm_c_im, ssm_d=ssm_d, ssm_w_glu=ssm_w_glu,
                   ssm_b_glu=ssm_b_glu, w_out=w_out, g_post_mix=g_post_mix, g_pre_ffn=g_pre_ffn,
                   w_router=w_router, b_router=b_router, w1=w1, b1=b1, w2=w2, b2=b2,
                   g_post_ffn=g_post_ffn)
    depth = w_ada.shape[0]
    n_p, l_p, _ = x_prompt.shape
    n_s, l_s, _ = x_sample.shape
    assert l_s == 1
    n_pages = page_table.shape[1]
    past_len = n_pages * PAGE_SIZE

    cos_p, sin_p = _rope_tables(jnp.arange(l_p, dtype=jnp.int32))
    cos_s, sin_s = _rope_tables(jnp.full((1,), past_len, jnp.int32))
    cos_s = jnp.broadcast_to(cos_s, (1, n_s, LANES))
    sin_s = jnp.broadcast_to(sin_s, (1, n_s, LANES))

    tm_p = _pick_tile(l_p, 512, 16)
    tq = _pick_tile(l_p, 512, 16)
    tk = _pick_tile(l_p, 1024, 16)
    lc = 64 if l_p % 512 == 0 else l_p // SUBLANES
    c_rows = n_p + n_s
    c_pad = -c_rows % SUBLANES
    c_all = jnp.pad(jnp.concatenate([c_prompt, c_sample], axis=0), ((0, c_pad), (0, 0)))

    y_p = x_prompt
    y_s = x_sample.reshape(1, n_s, D_MODEL)
    outs = [[] for _ in range(6)]
    for layer in range(depth):
        p, sp, mp = _prep_layer(weights, layer)
        mod = _ada(c_all, w_ada[layer], b_ada[layer].reshape(1, -1))
        sh1p, sc1p, g1p, sh2p, sc2p, g2p = [m[:, None, :] for m in _mods(mod[:n_p])]
        sh1s, sc1s, g1s, sh2s, sc2s, g2s = [m[None] for m in _mods(mod[n_p:c_rows])]

        q_t, k_cat, v_t, kv_p, kr_p, u_p = _inproj(y_p, sc1p, sh1p, p, cos_p, sin_p, tm_p)
        o_attn_p = _flash(q_t, k_cat, v_t, p["w_uv_t"], tq, tk)
        y_ssm_p, h_last_p = _s5_prompt(u_p, sp, lc)
        t_p = n_p * l_p
        t_all = t_p + n_s
        blank = (jnp.zeros((t_all, D_MODEL // LANES, LANES), F32), jnp.zeros((t_all, LANES), F32))
        x1_p, h2_all, route = _outproj(o_attn_p, y_ssm_p, y_p, g1p, sc2p, sh2p, p, tm_p, t_all, 0,
                                       dest=blank)

        q_t_s, _, _, kv_s, kr_s, u_s = _inproj(y_s, sc1s, sh1s, p, cos_s, sin_s, n_s)
        q_lat_s = jnp.transpose(q_t_s[0, :, :KV_RANK, :], (2, 0, 1))
        q_rope_s = jnp.stack(
            [q_t_s[0, h, KV_RANK + ROPE_DIM * (h % 4):KV_RANK + ROPE_DIM * (h % 4 + 1), :].T
             for h in range(N_HEADS)], axis=1)
        o_lat_s = _decode_attn(q_lat_s, q_rope_s, kv_s.reshape(n_s, 1, KV_RANK),
                               kr_s.reshape(n_s, 1, ROPE_DIM), cache_kv_latent[layer],
                               jnp.swapaxes(cache_k_rope[layer], 1, 2), page_table)
        o_attn_s = _uv_proj(o_lat_s.reshape(n_s, N_HEADS * KV_RANK), p["w_uv_bd"])
        h0 = jnp.moveaxis(state_ssm[layer].reshape(n_s, N_STATE, 2), -1, 0)
        y_ssm_s, h_s = _s5_step(u_s[0], h0, sp)
        x1_s, h2_all, route = _outproj(o_attn_s[None], y_ssm_s[None], y_s, g1s, sc2s, sh2s, p, n_s,
                                       t_all, t_p, dest=(h2_all, route))

        nch = 4 if t_all % (4 * SUBLANES) == 0 else (2 if t_all % (2 * SUBLANES) == 0 else 1)
        tc = t_all // nch
        plan = _route_plan(route, nch, MOE_TILE)
        w1p, w2p = _moe_prep(w1[layer], w2[layer])
        f_chunks = _moe_experts(h2_all.reshape(nch, tc, D_MODEL // LANES, LANES), plan, w1p, mp["b1"],
                                w2p, mp["b2"], MOE_TILE)
        f_all = f_chunks[:, :tc].reshape(t_all * (D_MODEL // LANES), LANES)
        y_p = _final(x1_p, f_all, 0, g2p, p["g_post_ffn"], tm_p)
        y_s = _final(x1_s, f_all, t_p, g2s, p["g_post_ffn"], n_s)

        outs[0].append(kv_p)
        outs[1].append(kr_p)
        outs[2].append(jnp.moveaxis(h_last_p, 1, -1).reshape(n_p, N_SSM_GROUPS, SSM_STATE, 2))
        outs[3].append(kv_s.reshape(n_s, 1, KV_RANK))
        outs[4].append(kr_s.reshape(n_s, 1, ROPE_DIM))
        outs[5].append(jnp.moveaxis(h_s, 0, -1).reshape(n_s, N_SSM_GROUPS, SSM_STATE, 2))
    return (y_p, y_s.reshape(n_s, 1, D_MODEL)) + tuple(jnp.stack(o) for o in outs)
```

```python
import functools
import math

import jax
import jax.numpy as jnp
from jax import lax
from jax.experimental import pallas as pl
from jax.experimental.pallas import tpu as pltpu

F32 = jnp.float32
BF16 = jnp.bfloat16

D_MODEL = 1024
N_HEADS = 8
NOPE_DIM = 64
ROPE_DIM = 32
V_DIM = 64
Q_RANK = 352
KV_RANK = 128
ROPE_THETA = 10000.0
ATTN_WIDTH = N_HEADS * V_DIM
ATTN_SCALE = 1.0 / math.sqrt(NOPE_DIM + ROPE_DIM)
SSM_WIDTH = D_MODEL - ATTN_WIDTH
SSM_GROUP = 16
N_SSM_GROUPS = SSM_WIDTH // SSM_GROUP
SSM_STATE = 64
N_STATE = N_SSM_GROUPS * SSM_STATE
N_EXPERTS = 32
TOP_K = 4
SWIGLU_LIMIT = 7.0
SWIGLU_ALPHA = 1.702
RMS_EPS = 1e-6
N_MOD = 6
NEG_INF = -1e30
PAGE_SIZE = 128

LANES = 128
SUBLANES = 8
VMEM_LIMIT = 56 * 1024 * 1024

Q_PAD = 384
CAT = 2 * LANES
SSM_BLOCKS = 4
GROUPS_PER_BLOCK = N_SSM_GROUPS // SSM_BLOCKS
BLOCK_STATES = GROUPS_PER_BLOCK * SSM_STATE
IN_PAD = Q_PAD + KV_RANK + 2 * LANES + SSM_WIDTH
UQ_PAD = N_HEADS * NOPE_DIM + 4 * LANES
GLU_BLOCK = 2 * LANES
MOE_TILE = 192
MOE_COL_PARTS = 2
LOG2E = math.log2(math.e)
FLASH_LOOKAHEAD = 3
FLASH_DIAG_SUB = 256
VT_ROWS = KV_RANK + 16
ROUTE_TOKEN = 2 * TOP_K


def _cparams(sem):
    return pltpu.CompilerParams(dimension_semantics=sem, vmem_limit_bytes=VMEM_LIMIT)


def _mm(a, b):
    return jnp.dot(a.astype(BF16), b.astype(BF16), preferred_element_type=F32)


def _mm_nt(a, b):
    return lax.dot_general(a.astype(BF16), b.astype(BF16), (((1,), (1,)), ((), ())),
                           preferred_element_type=F32)


def _split_bf16(x):
    hi = x.astype(BF16)
    lo = (x - hi.astype(F32)).astype(BF16)
    return hi, lo


def _mm3(a, b):
    ah, al = _split_bf16(a)
    bh, bl = _split_bf16(b)
    d = functools.partial(jnp.dot, preferred_element_type=F32)
    return d(ah, bh) + (d(ah, bl) + d(al, bh))


def _sigmoid(x):
    return 1.0 / (1.0 + jnp.exp(-x))


def _rms(x, g, width):
    ms = jnp.sum(x * x, axis=-1, keepdims=True) * (1.0 / width)
    return x * lax.rsqrt(ms + RMS_EPS) * g


def _gelu_tanh(x):
    return 0.5 * x * (1.0 + jnp.tanh(math.sqrt(2.0 / math.pi) * (x + 0.044715 * (x * x * x))))


def _ada_kernel(c_ref, w_ref, b_ref, o_ref):
    c = c_ref[...]
    o_ref[...] = _mm3(c * _sigmoid(c), w_ref[...]) + b_ref[...]


def _ada(c_all, w_ada, b_ada):
    rows = c_all.shape[0]
    tn = 512
    n_out = w_ada.shape[1]
    return pl.pallas_call(
        _ada_kernel,
        grid=(n_out // tn,),
        in_specs=[pl.BlockSpec((rows, D_MODEL), lambda j: (0, 0)),
                  pl.BlockSpec((D_MODEL, tn), lambda j: (0, j)),
                  pl.BlockSpec((1, tn), lambda j: (0, j))],
        out_specs=pl.BlockSpec((rows, tn), lambda j: (0, j)),
        out_shape=jax.ShapeDtypeStruct((rows, n_out), F32),
        compiler_params=_cparams(("parallel",)),
        name="adaln",
    )(c_all, w_ada, b_ada)


def _inproj_kernel(x_ref, sc_ref, sh_ref, gpre_ref, win_ref, gq_ref, wuq_ref, wuk_ref, gkv_ref,
                   cos_ref, sin_ref, q_ref, kcat_ref, vt_ref, kv_ref, kr_ref, u_ref):
    x = x_ref[0]
    h = _rms(x, gpre_ref[...], D_MODEL) * (1.0 + sc_ref[0]) + sh_ref[0]
    z = _mm(h, win_ref[...])
    zq = z[:, :Q_PAD]
    zkv = z[:, Q_PAD:Q_PAD + KV_RANK]
    zr = z[:, Q_PAD + KV_RANK:Q_PAD + KV_RANK + LANES]
    zrs = z[:, Q_PAD + KV_RANK + LANES:Q_PAD + KV_RANK + 2 * LANES]
    u = z[:, Q_PAD + KV_RANK + 2 * LANES:]
    cos4 = cos_ref[0]
    sin4 = sin_ref[0]

    kvl = _rms(zkv, gkv_ref[...], KV_RANK)
    kr4 = zr * cos4 + zrs * sin4
    kv_ref[0] = kvl
    kr_ref[0] = kr4[:, :ROPE_DIM]
    kcat_ref[0] = jnp.concatenate([kvl, kr4], axis=-1).astype(BF16)
    extra = (lax.broadcasted_iota(jnp.int32, (VT_ROWS - KV_RANK, kvl.shape[0]), 0) == 0).astype(F32)
    vt_ref[0] = jnp.concatenate([kvl.T, extra], axis=0).astype(BF16)
    u_ref[0] = u.astype(BF16)

    qn = _rms(zq, gq_ref[...], Q_RANK)
    q = _mm(qn, wuq_ref[...])
    nope_w = N_HEADS * NOPE_DIM
    rope_a = q[:, nope_w:nope_w + LANES] * cos4 + q[:, nope_w + 2 * LANES:nope_w + 3 * LANES] * sin4
    rope_b = q[:, nope_w + LANES:nope_w + 2 * LANES] * cos4 + q[:, nope_w + 3 * LANES:] * sin4
    slot = lax.broadcasted_iota(jnp.int32, (1, LANES), 1) // ROPE_DIM
    for j in range(N_HEADS // 2):
        ql = _mm(q[:, j * LANES:(j + 1) * LANES], wuk_ref[j])
        for hh in range(2):
            head = 2 * j + hh
            rope = rope_a if head < 4 else rope_b
            rope_h = jnp.where(slot == (head % 4), rope, 0.0)
            qc = jnp.concatenate([ql[:, hh * KV_RANK:(hh + 1) * KV_RANK], rope_h], axis=-1)
            q_ref[0, head] = (qc * (ATTN_SCALE * LOG2E)).T.astype(BF16)


def _inproj(x, sc, sh, p, cos4, sin4, tm):
    n, l, _ = x.shape
    per_row = sc.shape[1] != 1
    mod_blk = (1, tm, D_MODEL) if per_row else (1, 1, D_MODEL)
    mod_map = (lambda b, i: (b, i, 0)) if per_row else (lambda b, i: (b, 0, 0))
    const2 = lambda b, i: (0, 0)
    const3 = lambda b, i: (0, 0, 0)
    pos_map = lambda b, i: (0, i, 0)
    return pl.pallas_call(
        _inproj_kernel,
        grid=(n, l // tm),
        in_specs=[pl.BlockSpec((1, tm, D_MODEL), lambda b, i: (b, i, 0)),
                  pl.BlockSpec(mod_blk, mod_map),
                  pl.BlockSpec(mod_blk, mod_map),
                  pl.BlockSpec((1, D_MODEL), const2),
                  pl.BlockSpec((D_MODEL, IN_PAD), const2),
                  pl.BlockSpec((1, Q_PAD), const2),
                  pl.BlockSpec((Q_PAD, UQ_PAD), const2),
                  pl.BlockSpec((N_HEADS // 2, LANES, 2 * KV_RANK), const3),
                  pl.BlockSpec((1, KV_RANK), const2),
                  pl.BlockSpec((1, tm, LANES), pos_map),
                  pl.BlockSpec((1, tm, LANES), pos_map)],
        out_specs=[pl.BlockSpec((1, N_HEADS, CAT, tm), lambda b, i: (b, 0, 0, i)),
                   pl.BlockSpec((1, tm, CAT), lambda b, i: (b, i, 0)),
                   pl.BlockSpec((1, VT_ROWS, tm), lambda b, i: (b, 0, i)),
                   pl.BlockSpec((1, tm, KV_RANK), lambda b, i: (b, i, 0)),
                   pl.BlockSpec((1, tm, ROPE_DIM), lambda b, i: (b, i, 0)),
                   pl.BlockSpec((1, tm, SSM_WIDTH), lambda b, i: (b, i, 0))],
        out_shape=[jax.ShapeDtypeStruct((n, N_HEADS, CAT, l), BF16),
                   jax.ShapeDtypeStruct((n, l, CAT), BF16),
                   jax.ShapeDtypeStruct((n, VT_ROWS, l), BF16),
                   jax.ShapeDtypeStruct((n, l, KV_RANK), F32),
                   jax.ShapeDtypeStruct((n, l, ROPE_DIM), F32),
                   jax.ShapeDtypeStruct((n, l, SSM_WIDTH), BF16)],
        compiler_params=_cparams(("parallel", "parallel")),
        name="inproj",
    )(x, sc, sh, p["g_pre_mix"], p["w_in"], p["g_q"], p["w_uq"], p["w_uk"], p["g_kv"], cos4, sin4)


def _flash_kernel(qt_ref, k_ref, vt_ref, wuvt_ref, o_ref, m_ref, acc_ref, *, tq, tk):
    i = pl.program_id(1)
    j = pl.program_id(2)
    j_last = ((i + 1) * tq - 1) // tk

    @pl.when(j == 0)
    def _():
        m_ref[...] = jnp.full(m_ref.shape, NEG_INF, F32)
        acc_ref[...] = jnp.zeros(acc_ref.shape, F32)

    def update(masked, key0, size):
        k = k_ref[0, pl.ds(key0, size), :]
        vt = vt_ref[0, :, pl.ds(key0, size)]
        if masked:
            k_pos = j * tk + key0 + lax.broadcasted_iota(jnp.int32, (size, tq), 0)
            q_pos = i * tq + lax.broadcasted_iota(jnp.int32, (size, tq), 1)
            keep = k_pos <= q_pos
        m_all = m_ref[...]
        m_rows = []
        scores = lambda h: jnp.dot(k, qt_ref[0, h], preferred_element_type=F32)
        ahead = [scores(h) for h in range(min(FLASH_LOOKAHEAD, N_HEADS))]
        pending = None
        for h in range(N_HEADS):
            s = ahead.pop(0)
            if h + FLASH_LOOKAHEAD < N_HEADS:
                ahead.append(scores(h + FLASH_LOOKAHEAD))
            if masked:
                s = jnp.where(keep, s, NEG_INF)
            m_prev = m_all[h:h + 1]
            m_new = jnp.maximum(m_prev, jnp.max(s, axis=0, keepdims=True))
            alpha = jnp.exp2(m_prev - m_new)
            p = jnp.exp2(s - m_new)
            pv = jnp.dot(vt, p.astype(BF16), preferred_element_type=F32)
            if pending is not None:
                acc_ref[pending[0]] = pending[1] * acc_ref[pending[0]] + pending[2]
            pending = (h, alpha, pv)
            m_rows.append(m_new)
        acc_ref[pending[0]] = pending[1] * acc_ref[pending[0]] + pending[2]
        m_ref[...] = jnp.concatenate(m_rows, axis=0)

    crosses = (j + 1) * tk - 1 > i * tq
    sub = min(tk, FLASH_DIAG_SUB)

    @pl.when(jnp.logical_and(j <= j_last, crosses))
    def _():
        n_sub = jnp.minimum(tk // sub, ((i + 1) * tq - 1 - j * tk) // sub + 1)

        def body(sb, carry):
            update(True, pl.multiple_of(sb * sub, sub), sub)
            return carry
        lax.fori_loop(0, n_sub, body, 0)

    @pl.when(jnp.logical_and(j <= j_last, jnp.logical_not(crosses)))
    def _():
        update(False, 0, tk)

    @pl.when(j == j_last)
    def _():
        outs = []
        for h in range(N_HEADS):
            o_t = acc_ref[h, :KV_RANK, :] / acc_ref[h, KV_RANK:KV_RANK + 1, :]
            outs.append(_mm(wuvt_ref[h], o_t))
        o_ref[0] = jnp.concatenate(outs, axis=0).T.astype(BF16)


def _flash(q_t, k_cat, v_t, wuv_t, tq, tk):
    n, _, _, l = q_t.shape
    kern = functools.partial(_flash_kernel, tq=tq, tk=tk)
    last = lambda i: ((i + 1) * tq - 1) // tk
    return pl.pallas_call(
        kern,
        grid=(n, l // tq, l // tk),
        in_specs=[pl.BlockSpec((1, N_HEADS, CAT, tq), lambda b, i, j: (b, 0, 0, i)),
                  pl.BlockSpec((1, tk, CAT), lambda b, i, j: (b, jnp.minimum(j, last(i)), 0)),
                  pl.BlockSpec((1, VT_ROWS, tk), lambda b, i, j: (b, 0, jnp.minimum(j, last(i)))),
                  pl.BlockSpec((N_HEADS, V_DIM, KV_RANK), lambda b, i, j: (0, 0, 0))],
        out_specs=pl.BlockSpec((1, tq, ATTN_WIDTH), lambda b, i, j: (b, i, 0)),
        out_shape=jax.ShapeDtypeStruct((n, l, ATTN_WIDTH), BF16),
        scratch_shapes=[pltpu.VMEM((N_HEADS, tq), F32),
                        pltpu.VMEM((N_HEADS, VT_ROWS, tq), F32)],
        compiler_params=_cparams(("parallel", "parallel", "arbitrary")),
        name="flash",
    )(q_t, k_cat, v_t, wuv_t)


def _ssm_out(y, up, d_ref, wglu_ref, bglu_ref):
    y = _gelu_tanh(y + d_ref[...] * up)
    return y * _sigmoid(_mm(y, wglu_ref[...]) + bglu_ref[...])


def _s5_kernel(u_ref, perm_ref, permt_ref, bmat_ref, cmat_ref, lam_ref, laml_ref, d_ref, wglu_ref,
               bglu_ref, y_ref, hlast_ref, buf_ref, carry_ref, *, lc):
    j = pl.program_id(1)

    @pl.when(j == 0)
    def _():
        carry_ref[...] = jnp.zeros(carry_ref.shape, F32)

    u = u_ref[0]
    up = jnp.dot(perm_ref[...], u, preferred_element_type=F32).astype(BF16)
    blk_w = 2 * BLOCK_STATES

    def input_proj(b):
        buf_ref[:, b * blk_w:(b + 1) * blk_w] = jnp.dot(
            up[:, b * LANES:(b + 1) * LANES], bmat_ref[b], preferred_element_type=F32)

    input_proj(0)
    ys = []
    for b in range(SSM_BLOCKS):
        if b + 1 < SSM_BLOCKS:
            input_proj(b + 1)
        st = slice(b * BLOCK_STATES, (b + 1) * BLOCK_STATES)
        re_cols = slice(b * blk_w, b * blk_w + BLOCK_STATES)
        im_cols = slice(b * blk_w + BLOCK_STATES, (b + 1) * blk_w)
        lr = lam_ref[0, :, st]
        li = lam_ref[1, :, st]

        def run(init, store):
            re, im = init
            for t in range(lc):
                rows = slice(t * SUBLANES, (t + 1) * SUBLANES)
                re, im = (lr * re - li * im + buf_ref[rows, re_cols],
                          lr * im + li * re + buf_ref[rows, im_cols])
                if store:
                    buf_ref[rows, re_cols] = re
                    buf_ref[rows, im_cols] = im
            return re, im

        zero = jnp.zeros((SUBLANES, BLOCK_STATES), F32)
        end_re, end_im = run((zero, zero), False)
        ll_r = laml_ref[0:1, st]
        ll_i = laml_ref[1:2, st]
        c_re = carry_ref[0:1, st]
        c_im = carry_ref[1:2, st]
        init_re, init_im = [], []
        for k in range(SUBLANES):
            init_re.append(c_re)
            init_im.append(c_im)
            c_re, c_im = (ll_r * c_re - ll_i * c_im + end_re[k:k + 1],
                          ll_r * c_im + ll_i * c_re + end_im[k:k + 1])
        carry_ref[0:1, st] = c_re
        carry_ref[1:2, st] = c_im
        run((jnp.concatenate(init_re, axis=0), jnp.concatenate(init_im, axis=0)), True)
        ys.append(jnp.dot(buf_ref[:, b * blk_w:(b + 1) * blk_w].astype(BF16), cmat_ref[b],
                          preferred_element_type=F32))

    hlast_ref[0] = carry_ref[...]
    y = _ssm_out(jnp.concatenate(ys, axis=-1), up.astype(F32), d_ref, wglu_ref, bglu_ref)
    y_ref[0] = jnp.dot(permt_ref[...], y.astype(BF16), preferred_element_type=F32).astype(BF16)


def _s5_prompt(u, sp, lc):
    n, l, _ = u.shape
    s = SUBLANES * lc
    rows = jnp.arange(s)
    src = (rows % SUBLANES) * lc + rows // SUBLANES
    perm = (src[:, None] == rows[None, :]).astype(BF16)
    laml = _complex_pow2(sp["lam"], lc)
    lam8 = jnp.broadcast_to(sp["lam"][:, None, :], (2, SUBLANES, N_STATE))
    kern = functools.partial(_s5_kernel, lc=lc)
    c2 = lambda b, j: (0, 0)
    c3 = lambda b, j: (0, 0, 0)
    return pl.pallas_call(
        kern,
        grid=(n, l // s),
        in_specs=[pl.BlockSpec((1, s, SSM_WIDTH), lambda b, j: (b, j, 0)),
                  pl.BlockSpec((s, s), c2),
                  pl.BlockSpec((s, s), c2),
                  pl.BlockSpec((SSM_BLOCKS, LANES, 2 * BLOCK_STATES), c3),
                  pl.BlockSpec((SSM_BLOCKS, 2 * BLOCK_STATES, LANES), c3),
                  pl.BlockSpec((2, SUBLANES, N_STATE), c3),
                  pl.BlockSpec((2, N_STATE), c2),
                  pl.BlockSpec((1, SSM_WIDTH), c2),
                  pl.BlockSpec((SSM_WIDTH, SSM_WIDTH), c2),
                  pl.BlockSpec((1, SSM_WIDTH), c2)],
        out_specs=[pl.BlockSpec((1, s, SSM_WIDTH), lambda b, j: (b, j, 0)),
                   pl.BlockSpec((1, 2, N_STATE), lambda b, j: (b, 0, 0))],
        out_shape=[jax.ShapeDtypeStruct((n, l, SSM_WIDTH), BF16),
                   jax.ShapeDtypeStruct((n, 2, N_STATE), F32)],
        scratch_shapes=[pltpu.VMEM((s, 2 * N_STATE), F32),
                        pltpu.VMEM((2, N_STATE), F32)],
        compiler_params=_cparams(("parallel", "arbitrary")),
        name="s5_scan",
    )(u, perm, perm.T, sp["bmat"], sp["cmat"], lam8, laml, sp["d"], sp["w_glu"], sp["b_glu"])


def _s5_step_kernel(u_ref, h0_ref, bmat_ref, cmat_ref, lam_ref, d_ref, wglu_ref, bglu_ref,
                    y_ref, h_ref):
    u = u_ref[...]
    ys = []
    for b in range(SSM_BLOCKS):
        st = slice(b * BLOCK_STATES, (b + 1) * BLOCK_STATES)
        bu = jnp.dot(u[:, b * LANES:(b + 1) * LANES], bmat_ref[b], preferred_element_type=F32)
        lr = lam_ref[0:1, st]
        li = lam_ref[1:2, st]
        re = h0_ref[0, :, st]
        im = h0_ref[1, :, st]
        n_re = lr * re - li * im + bu[:, :BLOCK_STATES]
        n_im = lr * im + li * re + bu[:, BLOCK_STATES:]
        h_ref[0, :, st] = n_re
        h_ref[1, :, st] = n_im
        ys.append(_mm(jnp.concatenate([n_re, n_im], axis=-1), cmat_ref[b]))
    y = _ssm_out(jnp.concatenate(ys, axis=-1), u.astype(F32), d_ref, wglu_ref, bglu_ref)
    y_ref[...] = y.astype(BF16)


def _s5_step(u, h0, sp):
    rows = u.shape[0]
    return pl.pallas_call(
        _s5_step_kernel,
        out_shape=[jax.ShapeDtypeStruct((rows, SSM_WIDTH), BF16),
                   jax.ShapeDtypeStruct((2, rows, N_STATE), F32)],
        compiler_params=pltpu.CompilerParams(vmem_limit_bytes=VMEM_LIMIT),
        name="s5_step",
    )(u, h0, sp["bmat"], sp["cmat"], sp["lam"], sp["d"], sp["w_glu"], sp["b_glu"])


def _decode_kernel(pt_ref, ql_ref, qr_ref, kvs_ref, krs_ref, lat_hbm, rope_hbm, o_ref,
                   lat_buf, rope_buf, sems, *, n_pages):
    b = pl.program_id(0)
    slot = b % 2

    def page_copies(seq, dst_slot):
        out = []
        for i in range(n_pages):
            page = pt_ref[seq * n_pages + i]
            keys = pl.ds(i * PAGE_SIZE, PAGE_SIZE)
            out.append(pltpu.make_async_copy(lat_hbm.at[page], lat_buf.at[dst_slot, keys, :],
                                             sems.at[dst_slot, 0]))
            out.append(pltpu.make_async_copy(rope_hbm.at[page], rope_buf.at[dst_slot, :, keys],
                                             sems.at[dst_slot, 1]))
        return out

    @pl.when(b == 0)
    def _():
        for cp in page_copies(0, 0):
            cp.start()

    @pl.when(b + 1 < pl.num_programs(0))
    def _():
        for cp in page_copies(b + 1, 1 - slot):
            cp.start()

    for cp in page_copies(b, slot):
        cp.wait()

    ql = ql_ref[0]
    qr = qr_ref[0]
    kvs = kvs_ref[0]
    krs = krs_ref[0]
    lat = lat_buf[slot].astype(BF16)
    s = _mm_nt(ql, lat) + _mm(qr, rope_buf[slot])
    s_self = (jnp.sum(ql.astype(F32) * kvs, axis=-1, keepdims=True)
              + jnp.sum(qr.astype(F32) * krs, axis=-1, keepdims=True))
    m = jnp.maximum(jnp.max(s, axis=-1, keepdims=True), s_self)
    p = jnp.exp2(s - m)
    p_self = jnp.exp2(s_self - m)
    denom = jnp.sum(p, axis=-1, keepdims=True) + p_self
    o_ref[0] = (_mm(p, lat) + p_self * kvs) / denom


def _decode_attn(q_lat, q_rope, kv_self, kr_self, cache_lat, cache_rope_t, page_table):
    b, n_pages = page_table.shape
    kern = functools.partial(_decode_kernel, n_pages=n_pages)
    seq3 = lambda s, pt: (s, 0, 0)
    keys = n_pages * PAGE_SIZE
    grid_spec = pltpu.PrefetchScalarGridSpec(
        num_scalar_prefetch=1,
        grid=(b,),
        in_specs=[pl.BlockSpec((1, N_HEADS, KV_RANK), seq3),
                  pl.BlockSpec((1, N_HEADS, ROPE_DIM), seq3),
                  pl.BlockSpec((1, 1, KV_RANK), seq3),
                  pl.BlockSpec((1, 1, ROPE_DIM), seq3),
                  pl.BlockSpec(memory_space=pl.ANY),
                  pl.BlockSpec(memory_space=pl.ANY)],
        out_specs=pl.BlockSpec((1, N_HEADS, KV_RANK), seq3),
        scratch_shapes=[pltpu.VMEM((2, keys, KV_RANK), F32),
                        pltpu.VMEM((2, ROPE_DIM, keys), F32),
                        pltpu.SemaphoreType.DMA((2, 2))],
    )
    return pl.pallas_call(
        kern,
        grid_spec=grid_spec,
        out_shape=jax.ShapeDtypeStruct((b, N_HEADS, KV_RANK), F32),
        compiler_params=_cparams(("arbitrary",)),
        name="decode_attn",
    )(page_table.reshape(-1), q_lat, q_rope, kv_self, kr_self, cache_lat, cache_rope_t)


def _uv_kernel(o_ref, w_ref, y_ref):
    y_ref[...] = _mm(o_ref[...], w_ref[...]).astype(BF16)


def _uv_proj(o_lat, wuv_bd):
    rows = o_lat.shape[0]
    return pl.pallas_call(
        _uv_kernel,
        out_shape=jax.ShapeDtypeStruct((rows, ATTN_WIDTH), BF16),
        compiler_params=pltpu.CompilerParams(vmem_limit_bytes=VMEM_LIMIT),
        name="uv_proj",
    )(o_lat, wuv_bd)


def _outproj_kernel(oa_ref, ys_ref, x_ref, g1_ref, sc_ref, sh_ref, wout_ref, gpost_ref, gpre_ref,
                    wr_ref, br_ref, *rest, row0):
    x1_ref, h2_ref, route_ref = rest[-3:]
    mix = (jnp.dot(oa_ref[0], wout_ref[:ATTN_WIDTH].astype(BF16), preferred_element_type=F32)
           + jnp.dot(ys_ref[0], wout_ref[ATTN_WIDTH:].astype(BF16), preferred_element_type=F32))
    x1 = x_ref[0] + g1_ref[0] * _rms(mix, gpost_ref[...], D_MODEL)
    x1_ref[0] = x1
    h2 = _rms(x1, gpre_ref[...], D_MODEL) * (1.0 + sc_ref[0]) + sh_ref[0]
    for s in range(D_MODEL // LANES):
        h2_ref[:, s, :] = h2[:, s * LANES:(s + 1) * LANES]
    logits = _mm3(h2, wr_ref[...]) + br_ref[...]
    lane = lax.broadcasted_iota(jnp.int32, logits.shape, 1).astype(F32)
    vals, idxs = [], []
    for _ in range(TOP_K):
        v = jnp.max(logits, axis=-1, keepdims=True)
        idx = jnp.min(jnp.where(logits == v, lane, float(LANES)), axis=-1, keepdims=True)
        vals.append(v)
        idxs.append(idx)
        logits = jnp.where(lane == idx, -jnp.inf, logits)
    exps = [jnp.exp(v - vals[0]) for v in vals]
    denom = exps[0] + exps[1] + exps[2] + exps[3]
    tm = logits.shape[0]
    first = 1 + row0 + (pl.program_id(0) * pl.num_programs(1) + pl.program_id(1)) * tm
    token = (first + lax.broadcasted_iota(jnp.int32, (tm, 1), 0)).astype(F32)
    route = jnp.where(lane == float(ROUTE_TOKEN), token, 0.0)
    for k in range(TOP_K):
        route = jnp.where(lane == float(k), idxs[k], route)
        route = jnp.where(lane == float(TOP_K + k), exps[k] / denom, route)
    route_ref[...] = route


def _outproj(o_attn, y_ssm, x, g1, sc2, sh2, p, tm, t_all, row0, dest=None):
    n, l, _ = x.shape
    per_row = g1.shape[1] != 1
    mod_blk = (1, tm, D_MODEL) if per_row else (1, 1, D_MODEL)
    mod_map = (lambda b, i: (b, i, 0)) if per_row else (lambda b, i: (b, 0, 0))
    row_map = lambda b, i: (b, i, 0)
    const2 = lambda b, i: (0, 0)
    blk0, per_seq = row0 // tm, l // tm
    n_tile = D_MODEL // LANES
    in_specs = [pl.BlockSpec((1, tm, ATTN_WIDTH), row_map),
                pl.BlockSpec((1, tm, SSM_WIDTH), row_map),
                pl.BlockSpec((1, tm, D_MODEL), row_map),
                pl.BlockSpec(mod_blk, mod_map),
                pl.BlockSpec(mod_blk, mod_map),
                pl.BlockSpec(mod_blk, mod_map),
                pl.BlockSpec((D_MODEL, D_MODEL), const2),
                pl.BlockSpec((1, D_MODEL), const2),
                pl.BlockSpec((1, D_MODEL), const2),
                pl.BlockSpec((D_MODEL, LANES), const2),
                pl.BlockSpec((1, LANES), const2)]
    args = [o_attn, y_ssm, x, g1, sc2, sh2, p["w_out"], p["g_post_mix"], p["g_pre_ffn"],
            p["w_router"], p["b_router"]]
    aliases = {}
    if dest is not None:
        aliases = {len(args): 1, len(args) + 1: 2}
        in_specs += [pl.BlockSpec(memory_space=pl.ANY), pl.BlockSpec(memory_space=pl.ANY)]
        args += list(dest)
    return pl.pallas_call(
        functools.partial(_outproj_kernel, row0=row0),
        grid=(n, per_seq),
        in_specs=in_specs,
        out_specs=[pl.BlockSpec((1, tm, D_MODEL), row_map),
                   pl.BlockSpec((tm, n_tile, LANES), lambda b, i: (blk0 + b * per_seq + i, 0, 0)),
                   pl.BlockSpec((tm, LANES), lambda b, i: (blk0 + b * per_seq + i, 0))],
        out_shape=[jax.ShapeDtypeStruct((n, l, D_MODEL), F32),
                   jax.ShapeDtypeStruct((t_all, n_tile, LANES), F32),
                   jax.ShapeDtypeStruct((t_all, LANES), F32)],
        input_output_aliases=aliases,
        compiler_params=_cparams(("parallel", "parallel")),
        name="outproj",
    )(*args)


def _moe_prep_kernel(w1_ref, w2_ref, perm_ref, w1o_ref, w2o_ref):
    for j in range(w1_ref.shape[2] // GLU_BLOCK):
        cols = slice(j * GLU_BLOCK, (j + 1) * GLU_BLOCK)
        w1o_ref[0, :, cols] = jnp.dot(w1_ref[0, :, cols].astype(BF16), perm_ref[...],
                                      preferred_element_type=F32).astype(BF16)
    w2o_ref[0] = w2_ref[0].astype(BF16)


def _moe_prep(w1, w2):
    n_e, d_in, d_hid = w1.shape
    d_exp = w2.shape[1]
    j = jnp.arange(GLU_BLOCK)
    src = jnp.where(j < LANES, 2 * j, 2 * (j - LANES) + 1)
    perm = (j[:, None] == src[None, :]).astype(BF16)
    e_map = lambda e: (e, 0, 0)
    return pl.pallas_call(
        _moe_prep_kernel,
        grid=(n_e,),
        in_specs=[pl.BlockSpec((1, d_in, d_hid), e_map),
                  pl.BlockSpec((1, d_exp, D_MODEL), e_map),
                  pl.BlockSpec((GLU_BLOCK, GLU_BLOCK), lambda e: (0, 0))],
        out_specs=[pl.BlockSpec((1, d_in, d_hid), e_map),
                   pl.BlockSpec((1, d_exp, D_MODEL), e_map)],
        out_shape=[jax.ShapeDtypeStruct(w1.shape, BF16),
                   jax.ShapeDtypeStruct(w2.shape, BF16)],
        compiler_params=_cparams(("parallel",)),
        name="moe_prep",
    )(w1, w2, perm)


def _sort_rows_kernel(pos_ref, route_ref, rows_ref):
    rows_ref[...] = jnp.zeros(rows_ref.shape, F32)

    def body(t, carry):
        row = route_ref[0, pl.ds(t, 1), :]
        for k in range(TOP_K):
            rows_ref[0, pl.ds(pos_ref[0, 0, TOP_K * t + k], 1), :] = row
        return carry
    lax.fori_loop(0, route_ref.shape[1], body, 0, unroll=4)


def _sort_rows(route, pos, nch, rc):
    t = route.shape[0]
    tc = t // nch
    return pl.pallas_call(
        _sort_rows_kernel,
        grid=(nch,),
        in_specs=[pl.BlockSpec((1, 1, tc * TOP_K), lambda c: (c, 0, 0), memory_space=pltpu.SMEM),
                  pl.BlockSpec((1, tc, LANES), lambda c: (c, 0, 0))],
        out_specs=pl.BlockSpec((1, rc, LANES), lambda c: (c, 0, 0)),
        out_shape=jax.ShapeDtypeStruct((nch, rc, LANES), F32),
        compiler_params=_cparams(("parallel",)),
        name="moe_sort_rows",
    )(pos.reshape(nch, 1, tc * TOP_K), route.reshape(nch, tc, LANES))


def _positions_kernel(route_ref, off_ref, tri_ref, pos_ref, cnt_ref):
    @pl.when(pl.program_id(1) == 0)
    def _():
        cnt_ref[...] = jnp.zeros(cnt_ref.shape, F32)

    r = route_ref[0]
    lane = lax.broadcasted_iota(jnp.int32, r.shape, 1).astype(F32)
    picks = [lane == r[:, k:k + 1] for k in range(TOP_K)]
    hot = jnp.zeros(r.shape, F32)
    for pick in picks:
        hot = hot + jnp.where(pick, 1.0, 0.0)
    rank = jnp.dot(tri_ref[...], hot.astype(BF16), preferred_element_type=F32) + cnt_ref[...]
    cnt_ref[...] = cnt_ref[...] + jnp.sum(hot, axis=0, keepdims=True)
    base = off_ref[0] + rank
    pos = jnp.zeros(r.shape, F32)
    for k, pick in enumerate(picks):
        pos = jnp.where(lane == float(k), jnp.sum(jnp.where(pick, base, 0.0), axis=-1, keepdims=True), pos)
    pos_ref[0] = pos


def _positions(route, off, nch):
    t = route.shape[0]
    tc = t // nch
    tp = _pick_tile(tc, 1024, SUBLANES)
    assert tc % tp == 0
    tri = (jnp.arange(tp)[:, None] > jnp.arange(tp)[None, :]).astype(BF16)
    pos = pl.pallas_call(
        _positions_kernel,
        grid=(nch, tc // tp),
        in_specs=[pl.BlockSpec((1, tp, LANES), lambda c, j: (c, j, 0)),
                  pl.BlockSpec((1, 1, LANES), lambda c, j: (c, 0, 0)),
                  pl.BlockSpec((tp, tp), lambda c, j: (0, 0))],
        out_specs=pl.BlockSpec((1, tp, LANES), lambda c, j: (c, j, 0)),
        out_shape=jax.ShapeDtypeStruct((nch, tc, LANES), F32),
        scratch_shapes=[pltpu.VMEM((1, LANES), F32)],
        compiler_params=_cparams(("parallel", "arbitrary")),
        name="moe_positions",
    )(route.reshape(nch, tc, LANES), off, tri)
    return pos[:, :, :TOP_K].astype(jnp.int32).reshape(t, TOP_K)


def _route_plan(route, nch, tmx):
    idx = route[:, :TOP_K].astype(jnp.int32)
    t = idx.shape[0]
    tc = t // nch
    nt = -(-(tc * TOP_K + N_EXPERTS * (tmx - 1)) // tmx)
    i32 = jnp.int32
    hot = (idx[:, :, None] == jnp.arange(N_EXPERTS, dtype=i32)).astype(i32).sum(1)
    counts = hot.reshape(nch, tc, N_EXPERTS).sum(1)
    padded = -(-counts // tmx) * tmx
    off_end = jnp.cumsum(padded, axis=1)
    off = off_end - padded
    n_tiles = (off_end[:, -1] // tmx).astype(i32)
    off_lanes = jnp.pad(off.astype(F32), ((0, 0), (0, LANES - N_EXPERTS)))[:, None, :]
    pos = _positions(route, off_lanes, nch)
    rc = nt * tmx
    rows = _sort_rows(route, pos, nch, rc)
    token = rows[:, :, ROUTE_TOKEN].astype(i32) - 1
    row_tok = jnp.maximum(token, 0) % tc
    row_dst = jnp.where(token >= 0, row_tok, tc + jnp.arange(rc, dtype=i32)[None, :] % SUBLANES)
    tile = jnp.arange(nt + 1, dtype=i32)
    texp_raw = (off_end[:, None, :] <= (tile * tmx)[None, :, None]).sum(-1).astype(i32)
    last = n_tiles[:, None] - 1
    tile_c = jnp.minimum(tile[None, :], last)
    tile_n = jnp.minimum(tile[None, :] + 1, last)
    tile_p = jnp.clip(tile[None, :] - 1, 0, last)
    texp = jnp.take_along_axis(texp_raw, tile_c, axis=1)
    base = (jnp.arange(nch, dtype=i32) * nt)[:, None]
    return dict(nt=nt, n_tiles=n_tiles, texp=texp.reshape(-1), tblk=(tile_c + base).reshape(-1),
                tnext=(tile_n + base).reshape(-1), tprev=(tile_p + base).reshape(-1),
                row_tok=row_tok.reshape(nch * nt, 1, tmx), row_dst=row_dst.reshape(nch * nt, 1, tmx),
                rows=rows.reshape(nch * nt, tmx, LANES))


def _moe_kernel(nt_ref, texp_ref, tblk_ref, tnext_ref, tprev_ref, tok_ref, toknext_ref, dstprev_ref,
                rows_ref, src_ref, w1_ref, b1_ref, w2_ref, b2_ref, out_ref, xs_ref, yt_ref):
    del tblk_ref, tnext_ref, tprev_ref
    c = pl.program_id(0)
    i = pl.program_id(1)
    tmx = rows_ref.shape[1]
    slot = i % 2
    n_grp = tmx // SUBLANES

    def add_prev_rows(groups):
        for g in groups:
            toks = [dstprev_ref[0, 0, g * SUBLANES + u] for u in range(SUBLANES)]
            sums = [out_ref[0, toks[u]]
                    + yt_ref[1 - slot, (g * SUBLANES + u) * SUBLANES:(g * SUBLANES + u + 1) * SUBLANES, :]
                    for u in range(SUBLANES)]
            for u in range(SUBLANES):
                out_ref[0, toks[u]] = sums[u]

    @pl.when(i == 0)
    def _():
        out_ref[...] = jnp.zeros(out_ref.shape, F32)
        yt_ref[1] = jnp.zeros(yt_ref.shape[1:], F32)

        def gather(r, carry):
            xs_ref[0, pl.ds(pl.multiple_of(r * SUBLANES, SUBLANES), SUBLANES), :] = src_ref[0, tok_ref[0, 0, r]]
            return carry
        lax.fori_loop(0, tmx, gather, 0, unroll=8)

    @pl.when(i < nt_ref[c])
    def _():
        x = jnp.concatenate([xs_ref[slot, pl.ds(s, tmx, stride=SUBLANES), :]
                             for s in range(D_MODEL // LANES)], axis=-1).astype(BF16)
        n_blk = w1_ref.shape[2] // (2 * GLU_BLOCK)
        wide = 2 * GLU_BLOCK

        def hidden(j):
            cols = slice(j * wide, (j + 1) * wide)
            return jnp.dot(x, w1_ref[0, :, cols], preferred_element_type=F32) + b1_ref[0, :, cols]

        hid = hidden(0)
        y = None
        for j in range(n_blk):
            nxt = hidden(j + 1) if j + 1 < n_blk else None
            for r in range(j * tmx // n_blk, (j + 1) * tmx // n_blk):
                xs_ref[1 - slot, r * SUBLANES:(r + 1) * SUBLANES, :] = src_ref[0, toknext_ref[0, 0, r]]
            add_prev_rows(range(j * n_grp // n_blk, (j + 1) * n_grp // n_blk))
            acts = []
            for b in range(2):
                glu = jnp.minimum(hid[:, b * GLU_BLOCK:b * GLU_BLOCK + LANES], SWIGLU_LIMIT)
                lin = jnp.clip(hid[:, b * GLU_BLOCK + LANES:(b + 1) * GLU_BLOCK], -SWIGLU_LIMIT, SWIGLU_LIMIT)
                acts.append(glu * _sigmoid(SWIGLU_ALPHA * glu) * (lin + 1.0))
            part = _mm(jnp.concatenate(acts, axis=-1), w2_ref[0, j * GLU_BLOCK:(j + 1) * GLU_BLOCK, :])
            y = part if y is None else y + part
            hid = nxt
        meta = rows_ref[0]
        expert = texp_ref[c * pl.num_programs(1) + i].astype(F32)
        lane = lax.broadcasted_iota(jnp.int32, meta.shape, 1)
        hit = jnp.logical_and(lane < TOP_K, meta == expert)
        gates = pltpu.roll(meta, LANES - TOP_K, axis=1)
        gate = jnp.sum(jnp.where(hit, gates, 0.0), axis=-1, keepdims=True)
        y = gate * (y + b2_ref[0])
        for s in range(D_MODEL // LANES):
            yt_ref[slot, pl.ds(s, tmx, stride=SUBLANES), :] = y[:, s * LANES:(s + 1) * LANES]

    @pl.when(i == nt_ref[c])
    def _():
        add_prev_rows(range(n_grp))


def _moe_experts(h2, plan, w1p, b1p, w2p, b2, tmx):
    nch, tc = h2.shape[:2]
    nt = plan["nt"]
    n_e, _, d_hid = w1p.shape
    d_exp = w2p.shape[1]
    steps = nt + 1
    flat = lambda c, i: c * steps + i
    cur = lambda c, i, n, te, tb, tn, tp: (tb[flat(c, i)], 0, 0)
    nxt = lambda c, i, n, te, tb, tn, tp: (tn[flat(c, i)], 0, 0)
    prv = lambda c, i, n, te, tb, tn, tp: (tp[flat(c, i)], 0, 0)
    exp = lambda c, i, n, te, tb, tn, tp: (te[flat(c, i)], 0, 0)
    grid_spec = pltpu.PrefetchScalarGridSpec(
        num_scalar_prefetch=5,
        grid=(nch, steps),
        in_specs=[pl.BlockSpec((1, 1, tmx), cur, memory_space=pltpu.SMEM),
                  pl.BlockSpec((1, 1, tmx), nxt, memory_space=pltpu.SMEM),
                  pl.BlockSpec((1, 1, tmx), prv, memory_space=pltpu.SMEM),
                  pl.BlockSpec((1, tmx, LANES), cur),
                  pl.BlockSpec((1, tc, D_MODEL // LANES, LANES), lambda c, i, n, te, tb, tn, tp: (c, 0, 0, 0),
                               pipeline_mode=pl.Buffered(1)),
                  pl.BlockSpec((1, D_MODEL, d_hid), exp),
                  pl.BlockSpec((1, 1, d_hid), exp),
                  pl.BlockSpec((1, d_exp, D_MODEL), exp),
                  pl.BlockSpec((1, 1, D_MODEL), exp)],
        out_specs=pl.BlockSpec((1, tc + SUBLANES, D_MODEL // LANES, LANES),
                               lambda c, i, n, te, tb, tn, tp: (c, 0, 0, 0), pipeline_mode=pl.Buffered(1)),
        scratch_shapes=[pltpu.VMEM((2, tmx * (D_MODEL // LANES), LANES), F32),
                        pltpu.VMEM((2, tmx * (D_MODEL // LANES), LANES), F32)],
    )
    return pl.pallas_call(
        _moe_kernel,
        grid_spec=grid_spec,
        out_shape=jax.ShapeDtypeStruct((nch, tc + SUBLANES, D_MODEL // LANES, LANES), F32),
        compiler_params=_cparams(("parallel", "arbitrary")),
        name="moe_experts",
    )(plan["n_tiles"], plan["texp"], plan["tblk"], plan["tnext"], plan["tprev"], plan["row_tok"],
      plan["row_tok"], plan["row_dst"], plan["rows"], h2, w1p, b1p, w2p, b2)


def _final_kernel(x1_ref, f_ref, g2_ref, gpost_ref, y_ref):
    tm = x1_ref.shape[1]
    f = jnp.concatenate([f_ref[pl.ds(s, tm, stride=SUBLANES), :] for s in range(D_MODEL // LANES)],
                        axis=-1)
    y_ref[0] = x1_ref[0] + g2_ref[0] * _rms(f, gpost_ref[...], D_MODEL)


def _final(x1, f_all, row0, g2, g_post_ffn, tm):
    n, l, _ = x1.shape
    per_row = g2.shape[1] != 1
    mod_blk = (1, tm, D_MODEL) if per_row else (1, 1, D_MODEL)
    mod_map = (lambda b, i: (b, i, 0)) if per_row else (lambda b, i: (b, 0, 0))
    row_map = lambda b, i: (b, i, 0)
    blk0, per_seq = row0 // tm, l // tm
    return pl.pallas_call(
        _final_kernel,
        grid=(n, l // tm),
        in_specs=[pl.BlockSpec((1, tm, D_MODEL), row_map),
                  pl.BlockSpec((tm * (D_MODEL // LANES), LANES), lambda b, i: (blk0 + b * per_seq + i, 0)),
                  pl.BlockSpec(mod_blk, mod_map),
                  pl.BlockSpec((1, D_MODEL), lambda b, i: (0, 0))],
        out_specs=pl.BlockSpec((1, tm, D_MODEL), row_map),
        out_shape=jax.ShapeDtypeStruct((n, l, D_MODEL), F32),
        compiler_params=_cparams(("parallel", "parallel")),
        name="final",
    )(x1, f_all, g2, g_post_ffn)


def _complex_pow2(lam, n):
    re, im = lam[0], lam[1]
    steps = int(round(math.log2(n)))
    assert 2 ** steps == n
    for _ in range(steps):
        re, im = re * re - im * im, 2.0 * re * im
    return jnp.stack([re, im])


def _rope_tables(pos):
    half = ROPE_DIM // 2
    inv = ROPE_THETA ** (-jnp.arange(half, dtype=F32) / half)
    ang = pos.astype(F32)[:, None] * inv[None, :]
    cos, sin = jnp.cos(ang), jnp.sin(ang)
    cos4 = jnp.tile(jnp.concatenate([cos, cos], axis=-1), (1, LANES // ROPE_DIM))
    sin4 = jnp.tile(jnp.concatenate([-sin, sin], axis=-1), (1, LANES // ROPE_DIM))
    return cos4[None], sin4[None]


def _swap_halves(w):
    half = ROPE_DIM // 2
    return jnp.concatenate([w[..., half:], w[..., :half]], axis=-1)


def _prep_layer(w, layer):
    g = lambda name: w[name][layer]
    p = {}
    row = lambda v: v.reshape(1, -1)
    p["g_pre_mix"] = row(g("g_pre_mix"))
    w_in = g("w_in")
    w_q, w_kv, w_r, w_u = jnp.split(w_in, [Q_RANK, Q_RANK + KV_RANK, Q_RANK + KV_RANK + ROPE_DIM], axis=-1)
    rep = LANES // ROPE_DIM
    p["w_in"] = jnp.concatenate(
        [w_q, jnp.zeros((D_MODEL, Q_PAD - Q_RANK), F32), w_kv,
         jnp.tile(w_r, (1, rep)), jnp.tile(_swap_halves(w_r), (1, rep)), w_u], axis=-1)
    p["g_q"] = row(jnp.pad(g("g_q"), (0, Q_PAD - Q_RANK)))
    w_uq = g("w_uq").reshape(Q_RANK, N_HEADS, NOPE_DIM + ROPE_DIM)
    nope = w_uq[:, :, :NOPE_DIM].reshape(Q_RANK, N_HEADS * NOPE_DIM)
    rope = w_uq[:, :, NOPE_DIM:]
    rope_sw = _swap_halves(rope)
    w_uq_p = jnp.concatenate([nope, rope.reshape(Q_RANK, -1), rope_sw.reshape(Q_RANK, -1)], axis=-1)
    p["w_uq"] = jnp.pad(w_uq_p, ((0, Q_PAD - Q_RANK), (0, 0)))
    w_uk = jnp.transpose(g("w_uk"), (1, 2, 0))
    z = jnp.zeros((NOPE_DIM, KV_RANK), F32)
    p["w_uk"] = jnp.stack([jnp.block([[w_uk[2 * j], z], [z, w_uk[2 * j + 1]]])
                           for j in range(N_HEADS // 2)])
    p["g_kv"] = row(g("g_kv"))
    w_uv = jnp.transpose(g("w_uv"), (1, 0, 2))
    p["w_uv_t"] = jnp.transpose(w_uv, (0, 2, 1))
    eye_h = jnp.eye(N_HEADS, dtype=F32)
    p["w_uv_bd"] = jnp.einsum("hrv,hk->hrkv", w_uv, eye_h).reshape(N_HEADS * KV_RANK, ATTN_WIDTH)
    p["w_out"] = g("w_out")
    for name in ("w_in", "w_uq", "w_uk", "w_uv_t", "w_uv_bd", "w_out"):
        p[name] = p[name].astype(BF16)
    p["g_post_mix"] = row(g("g_post_mix"))
    p["g_pre_ffn"] = row(g("g_pre_ffn"))
    p["w_router"] = jnp.pad(g("w_router"), ((0, 0), (0, LANES - N_EXPERTS)))
    p["b_router"] = row(jnp.pad(g("b_router"), (0, LANES - N_EXPERTS), constant_values=NEG_INF))
    p["g_post_ffn"] = row(g("g_post_ffn"))

    lam_re, lam_im = g("ssm_lam_re"), g("ssm_lam_im")
    step = jnp.exp(g("ssm_log_step"))[:, None]
    mag = jnp.exp(lam_re * step)
    bar_re, bar_im = mag * jnp.cos(lam_im * step), mag * jnp.sin(lam_im * step)
    den = lam_re * lam_re + lam_im * lam_im
    co_re = ((bar_re - 1.0) * lam_re + bar_im * lam_im) / den
    co_im = (bar_im * lam_re - (bar_re - 1.0) * lam_im) / den
    b_re, b_im = g("ssm_b_re"), g("ssm_b_im")
    bb_re = co_re[..., None] * b_re - co_im[..., None] * b_im
    bb_im = co_re[..., None] * b_im + co_im[..., None] * b_re
    eye_g = jnp.eye(GROUPS_PER_BLOCK, dtype=F32)

    def b_block(m):
        m = m.reshape(SSM_BLOCKS, GROUPS_PER_BLOCK, SSM_STATE, SSM_GROUP)
        return jnp.einsum("bgpc,gh->bgchp", m, eye_g).reshape(SSM_BLOCKS, LANES, BLOCK_STATES)

    def c_block(m):
        m = m.reshape(SSM_BLOCKS, GROUPS_PER_BLOCK, SSM_GROUP, SSM_STATE)
        return jnp.einsum("bgcp,gh->bgphc", m, eye_g).reshape(SSM_BLOCKS, BLOCK_STATES, LANES)

    sp = {
        "lam": jnp.stack([bar_re.reshape(-1), bar_im.reshape(-1)]),
        "bmat": jnp.concatenate([b_block(bb_re), b_block(bb_im)], axis=-1).astype(BF16),
        "cmat": jnp.concatenate([c_block(g("ssm_c_re")), c_block(-g("ssm_c_im"))], axis=1).astype(BF16),
        "d": row(g("ssm_d")),
        "w_glu": g("ssm_w_glu").astype(BF16),
        "b_glu": row(g("ssm_b_glu")),
    }
    b1 = g("b1")
    n_blk = b1.shape[1] // GLU_BLOCK
    mp = {
        "b1": jnp.transpose(b1.reshape(N_EXPERTS, n_blk, LANES, 2), (0, 1, 3, 2)).reshape(N_EXPERTS, 1, -1),
        "b2": g("b2")[:, None, :],
    }
    return p, sp, mp


def _mods(mod_rows):
    return [mod_rows[:, k * D_MODEL:(k + 1) * D_MODEL] for k in range(N_MOD)]


def _pick_tile(n, target, mult):
    best = mult
    for c in range(mult, min(n, target) + 1, mult):
        if n % c == 0:
            best = c
    return best


def kernel(x_prompt, x_sample, c_prompt, c_sample, cache_kv_latent, cache_k_rope, state_ssm, page_table, w_ada, b_ada, g_pre_mix, w_in, g_q, w_uq, w_uk, g_kv, w_uv, ssm_lam_re, ssm_lam_im, ssm_log_step, ssm_b_re, ssm_b_im, ssm_c_re, ssm_c_im, ssm_d, ssm_w_glu, ssm_b_glu, w_out, g_post_mix, g_pre_ffn, w_router, b_router, w1, b1, w2, b2, g_post_ffn):
    weights = dict(w_ada=w_ada, b_ada=b_ada, g_pre_mix=g_pre_mix, w_in=w_in, g_q=g_q, w_uq=w_uq,
                   w_uk=w_uk, g_kv=g_kv, w_uv=w_uv, ssm_lam_re=ssm_lam_re, ssm_lam_im=ssm_lam_im,
                   ssm_log_step=ssm_log_step, ssm_b_re=ssm_b_re, ssm_b_im=ssm_b_im,
                   ssm_c_re=ssm_c_re, ssm_c_im=ssm_c_im, ssm_d=ssm_d, ssm_w_glu=ssm_w_glu,
                   ssm_b_glu=ssm_b_glu, w_out=w_out, g_post_mix=g_post_mix, g_pre_ffn=g_pre_ffn,
                   w_router=w_router, b_router=b_router, w1=w1, b1=b1, w2=w2, b2=b2,
                   g_post_ffn=g_post_ffn)
    depth = w_ada.shape[0]
    n_p, l_p, _ = x_prompt.shape
    n_s, l_s, _ = x_sample.shape
    assert l_s == 1
    n_pages = page_table.shape[1]
    past_len = n_pages * PAGE_SIZE

    cos_p, sin_p = _rope_tables(jnp.arange(l_p, dtype=jnp.int32))
    cos_s, sin_s = _rope_tables(jnp.full((1,), past_len, jnp.int32))
    cos_s = jnp.broadcast_to(cos_s, (1, n_s, LANES))
    sin_s = jnp.broadcast_to(sin_s, (1, n_s, LANES))

    tm_p = _pick_tile(l_p, 512, 16)
    tq = _pick_tile(l_p, 512, 16)
    tk = _pick_tile(l_p, 1024, 16)
    lc = 64 if l_p % 512 == 0 else l_p // SUBLANES
    c_rows = n_p + n_s
    c_pad = -c_rows % SUBLANES
    c_all = jnp.pad(jnp.concatenate([c_prompt, c_sample], axis=0), ((0, c_pad), (0, 0)))

    y_p = x_prompt
    y_s = x_sample.reshape(1, n_s, D_MODEL)
    outs = [[] for _ in range(6)]
    for layer in range(depth):
        p, sp, mp = _prep_layer(weights, layer)
        mod = _ada(c_all, w_ada[layer], b_ada[layer].reshape(1, -1))
        sh1p, sc1p, g1p, sh2p, sc2p, g2p = [m[:, None, :] for m in _mods(mod[:n_p])]
        sh1s, sc1s, g1s, sh2s, sc2s, g2s = [m[None] for m in _mods(mod[n_p:c_rows])]

        q_t, k_cat, v_t, kv_p, kr_p, u_p = _inproj(y_p, sc1p, sh1p, p, cos_p, sin_p, tm_p)
        o_attn_p = _flash(q_t, k_cat, v_t, p["w_uv_t"], tq, tk)
        y_ssm_p, h_last_p = _s5_prompt(u_p, sp, lc)
        t_p = n_p * l_p
        t_all = t_p + n_s
        blank = (jnp.zeros((t_all, D_MODEL // LANES, LANES), F32), jnp.zeros((t_all, LANES), F32))
        x1_p, h2_all, route = _outproj(o_attn_p, y_ssm_p, y_p, g1p, sc2p, sh2p, p, tm_p, t_all, 0,
                                       dest=blank)

        q_t_s, _, _, kv_s, kr_s, u_s = _inproj(y_s, sc1s, sh1s, p, cos_s, sin_s, n_s)
        q_lat_s = jnp.transpose(q_t_s[0, :, :KV_RANK, :], (2, 0, 1))
        q_rope_s = jnp.stack(
            [q_t_s[0, h, KV_RANK + ROPE_DIM * (h % 4):KV_RANK + ROPE_DIM * (h % 4 + 1), :].T
             for h in range(N_HEADS)], axis=1)
        o_lat_s = _decode_attn(q_lat_s, q_rope_s, kv_s.reshape(n_s, 1, KV_RANK),
                               kr_s.reshape(n_s, 1, ROPE_DIM), cache_kv_latent[layer],
                               jnp.swapaxes(cache_k_rope[layer], 1, 2), page_table)
        o_attn_s = _uv_proj(o_lat_s.reshape(n_s, N_HEADS * KV_RANK), p["w_uv_bd"])
        h0 = jnp.moveaxis(state_ssm[layer].reshape(n_s, N_STATE, 2), -1, 0)
        y_ssm_s, h_s = _s5_step(u_s[0], h0, sp)
        x1_s, h2_all, route = _outproj(o_attn_s[None], y_ssm_s[None], y_s, g1s, sc2s, sh2s, p, n_s,
                                       t_all, t_p, dest=(h2_all, route))

        nch = 4 if t_all % (4 * SUBLANES) == 0 else (2 if t_all % (2 * SUBLANES) == 0 else 1)
        tc = t_all // nch
        plan = _route_plan(route, nch, MOE_TILE)
        w1p, w2p = _moe_prep(w1[layer], w2[layer])
        f_chunks = _moe_experts(h2_all.reshape(nch, tc, D_MODEL // LANES, LANES), plan, w1p, mp["b1"],
                                w2p, mp["b2"], MOE_TILE)
        f_all = f_chunks[:, :tc].reshape(t_all * (D_MODEL // LANES), LANES)
        y_p = _final(x1_p, f_all, 0, g2p, p["g_post_ffn"], tm_p)
        y_s = _final(x1_s, f_all, t_p, g2s, p["g_post_ffn"], n_s)

        outs[0].append(kv_p)
        outs[1].append(kr_p)
        outs[2].append(jnp.moveaxis(h_last_p, 1, -1).reshape(n_p, N_SSM_GROUPS, SSM_STATE, 2))
        outs[3].append(kv_s.reshape(n_s, 1, KV_RANK))
        outs[4].append(kr_s.reshape(n_s, 1, ROPE_DIM))
        outs[5].append(jnp.moveaxis(h_s, 0, -1).reshape(n_s, N_SSM_GROUPS, SSM_STATE, 2))
    return (y_p, y_s.reshape(n_s, 1, D_MODEL)) + tuple(jnp.stack(o) for o in outs)
```

```python
import functools
import math

import jax
import jax.numpy as jnp
from jax import lax
from jax.experimental import pallas as pl
from jax.experimental.pallas import tpu as pltpu

F32 = jnp.float32
BF16 = jnp.bfloat16

D_MODEL = 1024
N_HEADS = 8
NOPE_DIM = 64
ROPE_DIM = 32
V_DIM = 64
Q_RANK = 352
KV_RANK = 128
ROPE_THETA = 10000.0
ATTN_WIDTH = N_HEADS * V_DIM
ATTN_SCALE = 1.0 / math.sqrt(NOPE_DIM + ROPE_DIM)
SSM_WIDTH = D_MODEL - ATTN_WIDTH
SSM_GROUP = 16
N_SSM_GROUPS = SSM_WIDTH // SSM_GROUP
SSM_STATE = 64
N_STATE = N_SSM_GROUPS * SSM_STATE
N_EXPERTS = 32
TOP_K = 4
SWIGLU_LIMIT = 7.0
SWIGLU_ALPHA = 1.702
RMS_EPS = 1e-6
N_MOD = 6
NEG_INF = -1e30
PAGE_SIZE = 128

LANES = 128
SUBLANES = 8
VMEM_LIMIT = 56 * 1024 * 1024

Q_PAD = 384
CAT = 2 * LANES
SSM_BLOCKS = 4
GROUPS_PER_BLOCK = N_SSM_GROUPS // SSM_BLOCKS
BLOCK_STATES = GROUPS_PER_BLOCK * SSM_STATE
IN_PAD = Q_PAD + KV_RANK + 2 * LANES + SSM_WIDTH
UQ_PAD = N_HEADS * NOPE_DIM + 4 * LANES
GLU_BLOCK = 2 * LANES
MOE_TILE = 192
MOE_COL_PARTS = 2
LOG2E = math.log2(math.e)
FLASH_LOOKAHEAD = 3
FLASH_DIAG_SUB = 256
VT_ROWS = KV_RANK + 16
ROUTE_TOKEN = 2 * TOP_K


def _cparams(sem):
    return pltpu.CompilerParams(dimension_semantics=sem, vmem_limit_bytes=VMEM_LIMIT)


def _mm(a, b):
    return jnp.dot(a.astype(BF16), b.astype(BF16), preferred_element_type=F32)


def _mm_nt(a, b):
    return lax.dot_general(a.astype(BF16), b.astype(BF16), (((1,), (1,)), ((), ())),
                           preferred_element_type=F32)


def _split_bf16(x):
    hi = x.astype(BF16)
    lo = (x - hi.astype(F32)).astype(BF16)
    return hi, lo


def _mm3(a, b):
    ah, al = _split_bf16(a)
    bh, bl = _split_bf16(b)
    d = functools.partial(jnp.dot, preferred_element_type=F32)
    return d(ah, bh) + (d(ah, bl) + d(al, bh))


def _sigmoid(x):
    return 1.0 / (1.0 + jnp.exp(-x))


def _rms(x, g, width):
    ms = jnp.sum(x * x, axis=-1, keepdims=True) * (1.0 / width)
    return x * lax.rsqrt(ms + RMS_EPS) * g


def _gelu_tanh(x):
    return 0.5 * x * (1.0 + jnp.tanh(math.sqrt(2.0 / math.pi) * (x + 0.044715 * (x * x * x))))


def _ada_kernel(c_ref, w_ref, b_ref, o_ref):
    c = c_ref[...]
    o_ref[...] = _mm3(c * _sigmoid(c), w_ref[...]) + b_ref[...]


def _ada(c_all, w_ada, b_ada):
    rows = c_all.shape[0]
    tn = 512
    n_out = w_ada.shape[1]
    return pl.pallas_call(
        _ada_kernel,
        grid=(n_out // tn,),
        in_specs=[pl.BlockSpec((rows, D_MODEL), lambda j: (0, 0)),
                  pl.BlockSpec((D_MODEL, tn), lambda j: (0, j)),
                  pl.BlockSpec((1, tn), lambda j: (0, j))],
        out_specs=pl.BlockSpec((rows, tn), lambda j: (0, j)),
        out_shape=jax.ShapeDtypeStruct((rows, n_out), F32),
        compiler_params=_cparams(("parallel",)),
        name="adaln",
    )(c_all, w_ada, b_ada)


def _inproj_kernel(x_ref, sc_ref, sh_ref, gpre_ref, win_ref, gq_ref, wuq_ref, wuk_ref, gkv_ref,
                   cos_ref, sin_ref, q_ref, kcat_ref, vt_ref, kv_ref, kr_ref, u_ref):
    x = x_ref[0]
    h = _rms(x, gpre_ref[...], D_MODEL) * (1.0 + sc_ref[0]) + sh_ref[0]
    z = _mm(h, win_ref[...])
    zq = z[:, :Q_PAD]
    zkv = z[:, Q_PAD:Q_PAD + KV_RANK]
    zr = z[:, Q_PAD + KV_RANK:Q_PAD + KV_RANK + LANES]
    zrs = z[:, Q_PAD + KV_RANK + LANES:Q_PAD + KV_RANK + 2 * LANES]
    u = z[:, Q_PAD + KV_RANK + 2 * LANES:]
    cos4 = cos_ref[0]
    sin4 = sin_ref[0]

    kvl = _rms(zkv, gkv_ref[...], KV_RANK)
    kr4 = zr * cos4 + zrs * sin4
    kv_ref[0] = kvl
    kr_ref[0] = kr4[:, :ROPE_DIM]
    kcat_ref[0] = jnp.concatenate([kvl, kr4], axis=-1).astype(BF16)
    extra = (lax.broadcasted_iota(jnp.int32, (VT_ROWS - KV_RANK, kvl.shape[0]), 0) == 0).astype(F32)
    vt_ref[0] = jnp.concatenate([kvl.T, extra], axis=0).astype(BF16)
    u_ref[0] = u.astype(BF16)

    qn = _rms(zq, gq_ref[...], Q_RANK)
    q = _mm(qn, wuq_ref[...])
    nope_w = N_HEADS * NOPE_DIM
    rope_a = q[:, nope_w:nope_w + LANES] * cos4 + q[:, nope_w + 2 * LANES:nope_w + 3 * LANES] * sin4
    rope_b = q[:, nope_w + LANES:nope_w + 2 * LANES] * cos4 + q[:, nope_w + 3 * LANES:] * sin4
    slot = lax.broadcasted_iota(jnp.int32, (1, LANES), 1) // ROPE_DIM
    for j in range(N_HEADS // 2):
        ql = _mm(q[:, j * LANES:(j + 1) * LANES], wuk_ref[j])
        for hh in range(2):
            head = 2 * j + hh
            rope = rope_a if head < 4 else rope_b
            rope_h = jnp.where(slot == (head % 4), rope, 0.0)
            qc = jnp.concatenate([ql[:, hh * KV_RANK:(hh + 1) * KV_RANK], rope_h], axis=-1)
            q_ref[0, head] = (qc * (ATTN_SCALE * LOG2E)).T.astype(BF16)


def _inproj(x, sc, sh, p, cos4, sin4, tm):
    n, l, _ = x.shape
    per_row = sc.shape[1] != 1
    mod_blk = (1, tm, D_MODEL) if per_row else (1, 1, D_MODEL)
    mod_map = (lambda b, i: (b, i, 0)) if per_row else (lambda b, i: (b, 0, 0))
    const2 = lambda b, i: (0, 0)
    const3 = lambda b, i: (0, 0, 0)
    pos_map = lambda b, i: (0, i, 0)
    return pl.pallas_call(
        _inproj_kernel,
        grid=(n, l // tm),
        in_specs=[pl.BlockSpec((1, tm, D_MODEL), lambda b, i: (b, i, 0)),
                  pl.BlockSpec(mod_blk, mod_map),
                  pl.BlockSpec(mod_blk, mod_map),
                  pl.BlockSpec((1, D_MODEL), const2),
                  pl.BlockSpec((D_MODEL, IN_PAD), const2),
                  pl.BlockSpec((1, Q_PAD), const2),
                  pl.BlockSpec((Q_PAD, UQ_PAD), const2),
                  pl.BlockSpec((N_HEADS // 2, LANES, 2 * KV_RANK), const3),
                  pl.BlockSpec((1, KV_RANK), const2),
                  pl.BlockSpec((1, tm, LANES), pos_map),
                  pl.BlockSpec((1, tm, LANES), pos_map)],
        out_specs=[pl.BlockSpec((1, N_HEADS, CAT, tm), lambda b, i: (b, 0, 0, i)),
                   pl.BlockSpec((1, tm, CAT), lambda b, i: (b, i, 0)),
                   pl.BlockSpec((1, VT_ROWS, tm), lambda b, i: (b, 0, i)),
                   pl.BlockSpec((1, tm, KV_RANK), lambda b, i: (b, i, 0)),
                   pl.BlockSpec((1, tm, ROPE_DIM), lambda b, i: (b, i, 0)),
                   pl.BlockSpec((1, tm, SSM_WIDTH), lambda b, i: (b, i, 0))],
        out_shape=[jax.ShapeDtypeStruct((n, N_HEADS, CAT, l), BF16),
                   jax.ShapeDtypeStruct((n, l, CAT), BF16),
                   jax.ShapeDtypeStruct((n, VT_ROWS, l), BF16),
                   jax.ShapeDtypeStruct((n, l, KV_RANK), F32),
                   jax.ShapeDtypeStruct((n, l, ROPE_DIM), F32),
                   jax.ShapeDtypeStruct((n, l, SSM_WIDTH), BF16)],
        compiler_params=_cparams(("parallel", "parallel")),
        name="inproj",
    )(x, sc, sh, p["g_pre_mix"], p["w_in"], p["g_q"], p["w_uq"], p["w_uk"], p["g_kv"], cos4, sin4)


def _flash_kernel(qt_ref, k_ref, vt_ref, wuvt_ref, o_ref, m_ref, acc_ref, *, tq, tk):
    i = pl.program_id(1)
    j = pl.program_id(2)
    j_last = ((i + 1) * tq - 1) // tk

    @pl.when(j == 0)
    def _():
        m_ref[...] = jnp.full(m_ref.shape, NEG_INF, F32)
        acc_ref[...] = jnp.zeros(acc_ref.shape, F32)

    def update(masked, key0, size):
        k = k_ref[0, pl.ds(key0, size), :]
        vt = vt_ref[0, :, pl.ds(key0, size)]
        if masked:
            k_pos = j * tk + key0 + lax.broadcasted_iota(jnp.int32, (size, tq), 0)
            q_pos = i * tq + lax.broadcasted_iota(jnp.int32, (size, tq), 1)
            keep = k_pos <= q_pos
        m_all = m_ref[...]
        m_rows = []
        scores = lambda h: jnp.dot(k, qt_ref[0, h], preferred_element_type=F32)
        ahead = [scores(h) for h in range(min(FLASH_LOOKAHEAD, N_HEADS))]
        pending = None
        for h in range(N_HEADS):
            s = ahead.pop(0)
            if h + FLASH_LOOKAHEAD < N_HEADS:
                ahead.append(scores(h + FLASH_LOOKAHEAD))
            if masked:
                s = jnp.where(keep, s, NEG_INF)
            m_prev = m_all[h:h + 1]
            m_new = jnp.maximum(m_prev, jnp.max(s, axis=0, keepdims=True))
            alpha = jnp.exp2(m_prev - m_new)
            p = jnp.exp2(s - m_new)
            pv = jnp.dot(vt, p.astype(BF16), preferred_element_type=F32)
            if pending is not None:
                acc_ref[pending[0]] = pending[1] * acc_ref[pending[0]] + pending[2]
            pending = (h, alpha, pv)
            m_rows.append(m_new)
        acc_ref[pending[0]] = pending[1] * acc_ref[pending[0]] + pending[2]
        m_ref[...] = jnp.concatenate(m_rows, axis=0)

    crosses = (j + 1) * tk - 1 > i * tq
    sub = min(tk, FLASH_DIAG_SUB)

    @pl.when(jnp.logical_and(j <= j_last, crosses))
    def _():
        n_sub = jnp.minimum(tk // sub, ((i + 1) * tq - 1 - j * tk) // sub + 1)

        def body(sb, carry):
            update(True, pl.multiple_of(sb * sub, sub), sub)
            return carry
        lax.fori_loop(0, n_sub, body, 0)

    @pl.when(jnp.logical_and(j <= j_last, jnp.logical_not(crosses)))
    def _():
        update(False, 0, tk)

    @pl.when(j == j_last)
    def _():
        outs = []
        for h in range(N_HEADS):
            o_t = acc_ref[h, :KV_RANK, :] / acc_ref[h, KV_RANK:KV_RANK + 1, :]
            outs.append(_mm(wuvt_ref[h], o_t))
        o_ref[0] = jnp.concatenate(outs, axis=0).T.astype(BF16)


def _flash(q_t, k_cat, v_t, wuv_t, tq, tk):
    n, _, _, l = q_t.shape
    kern = functools.partial(_flash_kernel, tq=tq, tk=tk)
    last = lambda i: ((i + 1) * tq - 1) // tk
    return pl.pallas_call(
        kern,
        grid=(n, l // tq, l // tk),
        in_specs=[pl.BlockSpec((1, N_HEADS, CAT, tq), lambda b, i, j: (b, 0, 0, i)),
                  pl.BlockSpec((1, tk, CAT), lambda b, i, j: (b, jnp.minimum(j, last(i)), 0)),
                  pl.BlockSpec((1, VT_ROWS, tk), lambda b, i, j: (b, 0, jnp.minimum(j, last(i)))),
                  pl.BlockSpec((N_HEADS, V_DIM, KV_RANK), lambda b, i, j: (0, 0, 0))],
        out_specs=pl.BlockSpec((1, tq, ATTN_WIDTH), lambda b, i, j: (b, i, 0)),
        out_shape=jax.ShapeDtypeStruct((n, l, ATTN_WIDTH), BF16),
        scratch_shapes=[pltpu.VMEM((N_HEADS, tq), F32),
                        pltpu.VMEM((N_HEADS, VT_ROWS, tq), F32)],
        compiler_params=_cparams(("parallel", "parallel", "arbitrary")),
        name="flash",
    )(q_t, k_cat, v_t, wuv_t)


def _ssm_out(y, up, d_ref, wglu_ref, bglu_ref):
    y = _gelu_tanh(y + d_ref[...] * up)
    return y * _sigmoid(_mm(y, wglu_ref[...]) + bglu_ref[...])


def _s5_kernel(u_ref, perm_ref, permt_ref, bmat_ref, cmat_ref, lam_ref, laml_ref, d_ref, wglu_ref,
               bglu_ref, y_ref, hlast_ref, buf_ref, carry_ref, *, lc):
    j = pl.program_id(1)

    @pl.when(j == 0)
    def _():
        carry_ref[...] = jnp.zeros(carry_ref.shape, F32)

    u = u_ref[0]
    up = jnp.dot(perm_ref[...], u, preferred_element_type=F32).astype(BF16)
    blk_w = 2 * BLOCK_STATES

    def input_proj(b):
        buf_ref[:, b * blk_w:(b + 1) * blk_w] = jnp.dot(
            up[:, b * LANES:(b + 1) * LANES], bmat_ref[b], preferred_element_type=F32)

    input_proj(0)
    ys = []
    for b in range(SSM_BLOCKS):
        if b + 1 < SSM_BLOCKS:
            input_proj(b + 1)
        st = slice(b * BLOCK_STATES, (b + 1) * BLOCK_STATES)
        re_cols = slice(b * blk_w, b * blk_w + BLOCK_STATES)
        im_cols = slice(b * blk_w + BLOCK_STATES, (b + 1) * blk_w)
        lr = lam_ref[0, :, st]
        li = lam_ref[1, :, st]

        def run(init, store):
            re, im = init
            for t in range(lc):
                rows = slice(t * SUBLANES, (t + 1) * SUBLANES)
                re, im = (lr * re - li * im + buf_ref[rows, re_cols],
                          lr * im + li * re + buf_ref[rows, im_cols])
                if store:
                    buf_ref[rows, re_cols] = re
                    buf_ref[rows, im_cols] = im
            return re, im

        zero = jnp.zeros((SUBLANES, BLOCK_STATES), F32)
        end_re, end_im = run((zero, zero), False)
        ll_r = laml_ref[0:1, st]
        ll_i = laml_ref[1:2, st]
        c_re = carry_ref[0:1, st]
        c_im = carry_ref[1:2, st]
        init_re, init_im = [], []
        for k in range(SUBLANES):
            init_re.append(c_re)
            init_im.append(c_im)
            c_re, c_im = (ll_r * c_re - ll_i * c_im + end_re[k:k + 1],
                          ll_r * c_im + ll_i * c_re + end_im[k:k + 1])
        carry_ref[0:1, st] = c_re
        carry_ref[1:2, st] = c_im
        run((jnp.concatenate(init_re, axis=0), jnp.concatenate(init_im, axis=0)), True)
        ys.append(jnp.dot(buf_ref[:, b * blk_w:(b + 1) * blk_w].astype(BF16), cmat_ref[b],
                          preferred_element_type=F32))

    hlast_ref[0] = carry_ref[...]
    y = _ssm_out(jnp.concatenate(ys, axis=-1), up.astype(F32), d_ref, wglu_ref, bglu_ref)
    y_ref[0] = jnp.dot(permt_ref[...], y.astype(BF16), preferred_element_type=F32).astype(BF16)


def _s5_prompt(u, sp, lc):
    n, l, _ = u.shape
    s = SUBLANES * lc
    rows = jnp.arange(s)
    src = (rows % SUBLANES) * lc + rows // SUBLANES
    perm = (src[:, None] == rows[None, :]).astype(BF16)
    laml = _complex_pow2(sp["lam"], lc)
    lam8 = jnp.broadcast_to(sp["lam"][:, None, :], (2, SUBLANES, N_STATE))
    kern = functools.partial(_s5_kernel, lc=lc)
    c2 = lambda b, j: (0, 0)
    c3 = lambda b, j: (0, 0, 0)
    return pl.pallas_call(
        kern,
        grid=(n, l // s),
        in_specs=[pl.BlockSpec((1, s, SSM_WIDTH), lambda b, j: (b, j, 0)),
                  pl.BlockSpec((s, s), c2),
                  pl.BlockSpec((s, s), c2),
                  pl.BlockSpec((SSM_BLOCKS, LANES, 2 * BLOCK_STATES), c3),
                  pl.BlockSpec((SSM_BLOCKS, 2 * BLOCK_STATES, LANES), c3),
                  pl.BlockSpec((2, SUBLANES, N_STATE), c3),
                  pl.BlockSpec((2, N_STATE), c2),
                  pl.BlockSpec((1, SSM_WIDTH), c2),
                  pl.BlockSpec((SSM_WIDTH, SSM_WIDTH), c2),
                  pl.BlockSpec((1, SSM_WIDTH), c2)],
        out_specs=[pl.BlockSpec((1, s, SSM_WIDTH), lambda b, j: (b, j, 0)),
                   pl.BlockSpec((1, 2, N_STATE), lambda b, j: (b, 0, 0))],
        out_shape=[jax.ShapeDtypeStruct((n, l, SSM_WIDTH), BF16),
                   jax.ShapeDtypeStruct((n, 2, N_STATE), F32)],
        scratch_shapes=[pltpu.VMEM((s, 2 * N_STATE), F32),
                        pltpu.VMEM((2, N_STATE), F32)],
        compiler_params=_cparams(("parallel", "arbitrary")),
        name="s5_scan",
    )(u, perm, perm.T, sp["bmat"], sp["cmat"], lam8, laml, sp["d"], sp["w_glu"], sp["b_glu"])


def _s5_step_kernel(u_ref, h0_ref, bmat_ref, cmat_ref, lam_ref, d_ref, wglu_ref, bglu_ref,
                    y_ref, h_ref):
    u = u_ref[...]
    ys = []
    for b in range(SSM_BLOCKS):
        st = slice(b * BLOCK_STATES, (b + 1) * BLOCK_STATES)
        bu = jnp.dot(u[:, b * LANES:(b + 1) * LANES], bmat_ref[b], preferred_element_type=F32)
        lr = lam_ref[0:1, st]
        li = lam_ref[1:2, st]
        re = h0_ref[0, :, st]
        im = h0_ref[1, :, st]
        n_re = lr * re - li * im + bu[:, :BLOCK_STATES]
        n_im = lr * im + li * re + bu[:, BLOCK_STATES:]
        h_ref[0, :, st] = n_re
        h_ref[1, :, st] = n_im
        ys.append(_mm(jnp.concatenate([n_re, n_im], axis=-1), cmat_ref[b]))
    y = _ssm_out(jnp.concatenate(ys, axis=-1), u.astype(F32), d_ref, wglu_ref, bglu_ref)
    y_ref[...] = y.astype(BF16)


def _s5_step(u, h0, sp):
    rows = u.shape[0]
    return pl.pallas_call(
        _s5_step_kernel,
        out_shape=[jax.ShapeDtypeStruct((rows, SSM_WIDTH), BF16),
                   jax.ShapeDtypeStruct((2, rows, N_STATE), F32)],
        compiler_params=pltpu.CompilerParams(vmem_limit_bytes=VMEM_LIMIT),
        name="s5_step",
    )(u, h0, sp["bmat"], sp["cmat"], sp["lam"], sp["d"], sp["w_glu"], sp["b_glu"])


def _decode_kernel(pt_ref, ql_ref, qr_ref, kvs_ref, krs_ref, lat_hbm, rope_hbm, o_ref,
                   lat_buf, rope_buf, sems, *, n_pages):
    b = pl.program_id(0)
    slot = b % 2

    def page_copies(seq, dst_slot):
        out = []
        for i in range(n_pages):
            page = pt_ref[seq * n_pages + i]
            keys = pl.ds(i * PAGE_SIZE, PAGE_SIZE)
            out.append(pltpu.make_async_copy(lat_hbm.at[page], lat_buf.at[dst_slot, keys, :],
                                             sems.at[dst_slot, 0]))
            out.append(pltpu.make_async_copy(rope_hbm.at[page], rope_buf.at[dst_slot, :, keys],
                                             sems.at[dst_slot, 1]))
        return out

    @pl.when(b == 0)
    def _():
        for cp in page_copies(0, 0):
            cp.start()

    @pl.when(b + 1 < pl.num_programs(0))
    def _():
        for cp in page_copies(b + 1, 1 - slot):
            cp.start()

    for cp in page_copies(b, slot):
        cp.wait()

    ql = ql_ref[0]
    qr = qr_ref[0]
    kvs = kvs_ref[0]
    krs = krs_ref[0]
    lat = lat_buf[slot].astype(BF16)
    s = _mm_nt(ql, lat) + _mm(qr, rope_buf[slot])
    s_self = (jnp.sum(ql.astype(F32) * kvs, axis=-1, keepdims=True)
              + jnp.sum(qr.astype(F32) * krs, axis=-1, keepdims=True))
    m = jnp.maximum(jnp.max(s, axis=-1, keepdims=True), s_self)
    p = jnp.exp2(s - m)
    p_self = jnp.exp2(s_self - m)
    denom = jnp.sum(p, axis=-1, keepdims=True) + p_self
    o_ref[0] = (_mm(p, lat) + p_self * kvs) / denom


def _decode_attn(q_lat, q_rope, kv_self, kr_self, cache_lat, cache_rope_t, page_table):
    b, n_pages = page_table.shape
    kern = functools.partial(_decode_kernel, n_pages=n_pages)
    seq3 = lambda s, pt: (s, 0, 0)
    keys = n_pages * PAGE_SIZE
    grid_spec = pltpu.PrefetchScalarGridSpec(
        num_scalar_prefetch=1,
        grid=(b,),
        in_specs=[pl.BlockSpec((1, N_HEADS, KV_RANK), seq3),
                  pl.BlockSpec((1, N_HEADS, ROPE_DIM), seq3),
                  pl.BlockSpec((1, 1, KV_RANK), seq3),
                  pl.BlockSpec((1, 1, ROPE_DIM), seq3),
                  pl.BlockSpec(memory_space=pl.ANY),
                  pl.BlockSpec(memory_space=pl.ANY)],
        out_specs=pl.BlockSpec((1, N_HEADS, KV_RANK), seq3),
        scratch_shapes=[pltpu.VMEM((2, keys, KV_RANK), F32),
                        pltpu.VMEM((2, ROPE_DIM, keys), F32),
                        pltpu.SemaphoreType.DMA((2, 2))],
    )
    return pl.pallas_call(
        kern,
        grid_spec=grid_spec,
        out_shape=jax.ShapeDtypeStruct((b, N_HEADS, KV_RANK), F32),
        compiler_params=_cparams(("arbitrary",)),
        name="decode_attn",
    )(page_table.reshape(-1), q_lat, q_rope, kv_self, kr_self, cache_lat, cache_rope_t)


def _uv_kernel(o_ref, w_ref, y_ref):
    y_ref[...] = _mm(o_ref[...], w_ref[...]).astype(BF16)


def _uv_proj(o_lat, wuv_bd):
    rows = o_lat.shape[0]
    return pl.pallas_call(
        _uv_kernel,
        out_shape=jax.ShapeDtypeStruct((rows, ATTN_WIDTH), BF16),
        compiler_params=pltpu.CompilerParams(vmem_limit_bytes=VMEM_LIMIT),
        name="uv_proj",
    )(o_lat, wuv_bd)


def _outproj_kernel(oa_ref, ys_ref, x_ref, g1_ref, sc_ref, sh_ref, wout_ref, gpost_ref, gpre_ref,
                    wr_ref, br_ref, *rest, row0):
    x1_ref, h2_ref, route_ref = rest[-3:]
    mix = (jnp.dot(oa_ref[0], wout_ref[:ATTN_WIDTH].astype(BF16), preferred_element_type=F32)
           + jnp.dot(ys_ref[0], wout_ref[ATTN_WIDTH:].astype(BF16), preferred_element_type=F32))
    x1 = x_ref[0] + g1_ref[0] * _rms(mix, gpost_ref[...], D_MODEL)
    x1_ref[0] = x1
    h2 = _rms(x1, gpre_ref[...], D_MODEL) * (1.0 + sc_ref[0]) + sh_ref[0]
    for s in range(D_MODEL // LANES):
        h2_ref[pl.ds(s, h2.shape[0], stride=SUBLANES), :] = h2[:, s * LANES:(s + 1) * LANES]
    logits = _mm3(h2, wr_ref[...]) + br_ref[...]
    lane = lax.broadcasted_iota(jnp.int32, logits.shape, 1).astype(F32)
    vals, idxs = [], []
    for _ in range(TOP_K):
        v = jnp.max(logits, axis=-1, keepdims=True)
        idx = jnp.min(jnp.where(logits == v, lane, float(LANES)), axis=-1, keepdims=True)
        vals.append(v)
        idxs.append(idx)
        logits = jnp.where(lane == idx, -jnp.inf, logits)
    exps = [jnp.exp(v - vals[0]) for v in vals]
    denom = exps[0] + exps[1] + exps[2] + exps[3]
    tm = logits.shape[0]
    first = 1 + row0 + (pl.program_id(0) * pl.num_programs(1) + pl.program_id(1)) * tm
    token = (first + lax.broadcasted_iota(jnp.int32, (tm, 1), 0)).astype(F32)
    route = jnp.where(lane == float(ROUTE_TOKEN), token, 0.0)
    for k in range(TOP_K):
        route = jnp.where(lane == float(k), idxs[k], route)
        route = jnp.where(lane == float(TOP_K + k), exps[k] / denom, route)
    route_ref[...] = route


def _outproj(o_attn, y_ssm, x, g1, sc2, sh2, p, tm, t_all, row0, dest=None):
    n, l, _ = x.shape
    per_row = g1.shape[1] != 1
    mod_blk = (1, tm, D_MODEL) if per_row else (1, 1, D_MODEL)
    mod_map = (lambda b, i: (b, i, 0)) if per_row else (lambda b, i: (b, 0, 0))
    row_map = lambda b, i: (b, i, 0)
    const2 = lambda b, i: (0, 0)
    blk0, per_seq = row0 // tm, l // tm
    n_tile = D_MODEL // LANES
    in_specs = [pl.BlockSpec((1, tm, ATTN_WIDTH), row_map),
                pl.BlockSpec((1, tm, SSM_WIDTH), row_map),
                pl.BlockSpec((1, tm, D_MODEL), row_map),
                pl.BlockSpec(mod_blk, mod_map),
                pl.BlockSpec(mod_blk, mod_map),
                pl.BlockSpec(mod_blk, mod_map),
                pl.BlockSpec((D_MODEL, D_MODEL), const2),
                pl.BlockSpec((1, D_MODEL), const2),
                pl.BlockSpec((1, D_MODEL), const2),
                pl.BlockSpec((D_MODEL, LANES), const2),
                pl.BlockSpec((1, LANES), const2)]
    args = [o_attn, y_ssm, x, g1, sc2, sh2, p["w_out"], p["g_post_mix"], p["g_pre_ffn"],
            p["w_router"], p["b_router"]]
    aliases = {}
    if dest is not None:
        aliases = {len(args): 1, len(args) + 1: 2}
        in_specs += [pl.BlockSpec(memory_space=pl.ANY), pl.BlockSpec(memory_space=pl.ANY)]
        args += list(dest)
    return pl.pallas_call(
        functools.partial(_outproj_kernel, row0=row0),
        grid=(n, per_seq),
        in_specs=in_specs,
        out_specs=[pl.BlockSpec((1, tm, D_MODEL), row_map),
                   pl.BlockSpec((tm * n_tile, LANES), lambda b, i: (blk0 + b * per_seq + i, 0)),
                   pl.BlockSpec((tm, LANES), lambda b, i: (blk0 + b * per_seq + i, 0))],
        out_shape=[jax.ShapeDtypeStruct((n, l, D_MODEL), F32),
                   jax.ShapeDtypeStruct((t_all * n_tile, LANES), F32),
                   jax.ShapeDtypeStruct((t_all, LANES), F32)],
        input_output_aliases=aliases,
        compiler_params=_cparams(("parallel", "parallel")),
        name="outproj",
    )(*args)


def _moe_prep_kernel(w1_ref, w2_ref, perm_ref, w1o_ref, w2o_ref):
    for j in range(w1_ref.shape[2] // GLU_BLOCK):
        cols = slice(j * GLU_BLOCK, (j + 1) * GLU_BLOCK)
        w1o_ref[0, :, cols] = jnp.dot(w1_ref[0, :, cols].astype(BF16), perm_ref[...],
                                      preferred_element_type=F32).astype(BF16)
    w2o_ref[0] = w2_ref[0].astype(BF16)


def _moe_prep(w1, w2):
    n_e, d_in, d_hid = w1.shape
    d_exp = w2.shape[1]
    j = jnp.arange(GLU_BLOCK)
    src = jnp.where(j < LANES, 2 * j, 2 * (j - LANES) + 1)
    perm = (j[:, None] == src[None, :]).astype(BF16)
    e_map = lambda e: (e, 0, 0)
    return pl.pallas_call(
        _moe_prep_kernel,
        grid=(n_e,),
        in_specs=[pl.BlockSpec((1, d_in, d_hid), e_map),
                  pl.BlockSpec((1, d_exp, D_MODEL), e_map),
                  pl.BlockSpec((GLU_BLOCK, GLU_BLOCK), lambda e: (0, 0))],
        out_specs=[pl.BlockSpec((1, d_in, d_hid), e_map),
                   pl.BlockSpec((1, d_exp, D_MODEL), e_map)],
        out_shape=[jax.ShapeDtypeStruct(w1.shape, BF16),
                   jax.ShapeDtypeStruct(w2.shape, BF16)],
        compiler_params=_cparams(("parallel",)),
        name="moe_prep",
    )(w1, w2, perm)


def _sort_rows_kernel(pos_ref, route_ref, rows_ref):
    rows_ref[...] = jnp.zeros(rows_ref.shape, F32)

    def body(t, carry):
        row = route_ref[0, pl.ds(t, 1), :]
        for k in range(TOP_K):
            rows_ref[0, pl.ds(pos_ref[0, 0, TOP_K * t + k], 1), :] = row
        return carry
    lax.fori_loop(0, route_ref.shape[1], body, 0, unroll=4)


def _sort_rows(route, pos, nch, rc):
    t = route.shape[0]
    tc = t // nch
    return pl.pallas_call(
        _sort_rows_kernel,
        grid=(nch,),
        in_specs=[pl.BlockSpec((1, 1, tc * TOP_K), lambda c: (c, 0, 0), memory_space=pltpu.SMEM),
                  pl.BlockSpec((1, tc, LANES), lambda c: (c, 0, 0))],
        out_specs=pl.BlockSpec((1, rc, LANES), lambda c: (c, 0, 0)),
        out_shape=jax.ShapeDtypeStruct((nch, rc, LANES), F32),
        compiler_params=_cparams(("parallel",)),
        name="moe_sort_rows",
    )(pos.reshape(nch, 1, tc * TOP_K), route.reshape(nch, tc, LANES))


def _positions_kernel(route_ref, off_ref, tri_ref, pos_ref, cnt_ref):
    @pl.when(pl.program_id(1) == 0)
    def _():
        cnt_ref[...] = jnp.zeros(cnt_ref.shape, F32)

    r = route_ref[0]
    lane = lax.broadcasted_iota(jnp.int32, r.shape, 1).astype(F32)
    picks = [lane == r[:, k:k + 1] for k in range(TOP_K)]
    hot = jnp.zeros(r.shape, F32)
    for pick in picks:
        hot = hot + jnp.where(pick, 1.0, 0.0)
    rank = jnp.dot(tri_ref[...], hot.astype(BF16), preferred_element_type=F32) + cnt_ref[...]
    cnt_ref[...] = cnt_ref[...] + jnp.sum(hot, axis=0, keepdims=True)
    base = off_ref[0] + rank
    pos = jnp.zeros(r.shape, F32)
    for k, pick in enumerate(picks):
        pos = jnp.where(lane == float(k), jnp.sum(jnp.where(pick, base, 0.0), axis=-1, keepdims=True), pos)
    pos_ref[0] = pos


def _positions(route, off, nch):
    t = route.shape[0]
    tc = t // nch
    tp = _pick_tile(tc, 1024, SUBLANES)
    assert tc % tp == 0
    tri = (jnp.arange(tp)[:, None] > jnp.arange(tp)[None, :]).astype(BF16)
    pos = pl.pallas_call(
        _positions_kernel,
        grid=(nch, tc // tp),
        in_specs=[pl.BlockSpec((1, tp, LANES), lambda c, j: (c, j, 0)),
                  pl.BlockSpec((1, 1, LANES), lambda c, j: (c, 0, 0)),
                  pl.BlockSpec((tp, tp), lambda c, j: (0, 0))],
        out_specs=pl.BlockSpec((1, tp, LANES), lambda c, j: (c, j, 0)),
        out_shape=jax.ShapeDtypeStruct((nch, tc, LANES), F32),
        scratch_shapes=[pltpu.VMEM((1, LANES), F32)],
        compiler_params=_cparams(("parallel", "arbitrary")),
        name="moe_positions",
    )(route.reshape(nch, tc, LANES), off, tri)
    return pos[:, :, :TOP_K].astype(jnp.int32).reshape(t, TOP_K)


def _route_plan(route, nch, tmx):
    idx = route[:, :TOP_K].astype(jnp.int32)
    t = idx.shape[0]
    tc = t // nch
    nt = -(-(tc * TOP_K + N_EXPERTS * (tmx - 1)) // tmx)
    i32 = jnp.int32
    hot = (idx[:, :, None] == jnp.arange(N_EXPERTS, dtype=i32)).astype(i32).sum(1)
    counts = hot.reshape(nch, tc, N_EXPERTS).sum(1)
    padded = -(-counts // tmx) * tmx
    off_end = jnp.cumsum(padded, axis=1)
    off = off_end - padded
    n_tiles = (off_end[:, -1] // tmx).astype(i32)
    off_lanes = jnp.pad(off.astype(F32), ((0, 0), (0, LANES - N_EXPERTS)))[:, None, :]
    pos = _positions(route, off_lanes, nch)
    rc = nt * tmx
    rows = _sort_rows(route, pos, nch, rc)
    token = lax.optimization_barrier(rows[:, :, ROUTE_TOKEN].astype(i32) - 1)
    row_tok = jnp.maximum(token, 0) % tc
    row_dst = jnp.where(token >= 0, row_tok, tc + jnp.arange(rc, dtype=i32)[None, :] % SUBLANES)
    tile = jnp.arange(nt + 1, dtype=i32)
    texp_raw = (off_end[:, None, :] <= (tile * tmx)[None, :, None]).sum(-1).astype(i32)
    last = n_tiles[:, None] - 1
    tile_c = jnp.minimum(tile[None, :], last)
    tile_n = jnp.minimum(tile[None, :] + 1, last)
    tile_p = jnp.clip(tile[None, :] - 1, 0, last)
    texp = jnp.take_along_axis(texp_raw, tile_c, axis=1)
    base = (jnp.arange(nch, dtype=i32) * nt)[:, None]
    return dict(nt=nt, n_tiles=n_tiles, texp=texp.reshape(-1), tblk=(tile_c + base).reshape(-1),
                tnext=(tile_n + base).reshape(-1), tprev=(tile_p + base).reshape(-1),
                row_tok=row_tok.reshape(nch * nt, 1, tmx), row_dst=row_dst.reshape(nch * nt, 1, tmx),
                rows=rows.reshape(nch * nt, tmx, LANES))


def _moe_kernel(nt_ref, texp_ref, tblk_ref, tnext_ref, tprev_ref, tok_ref, toknext_ref, dstprev_ref,
                rows_ref, src_ref, w1_ref, b1_ref, w2_ref, b2_ref, out_ref, xs_ref, yt_ref):
    del tblk_ref, tnext_ref, tprev_ref
    c = pl.program_id(0)
    i = pl.program_id(1)
    tmx = rows_ref.shape[1]
    slot = i % 2
    n_grp = tmx // SUBLANES

    def add_prev_rows(groups):
        for g in groups:
            toks = [dstprev_ref[0, 0, g * SUBLANES + u] for u in range(SUBLANES)]
            sums = [out_ref[0, toks[u]]
                    + yt_ref[1 - slot, (g * SUBLANES + u) * SUBLANES:(g * SUBLANES + u + 1) * SUBLANES, :]
                    for u in range(SUBLANES)]
            for u in range(SUBLANES):
                out_ref[0, toks[u]] = sums[u]

    @pl.when(i == 0)
    def _():
        out_ref[...] = jnp.zeros(out_ref.shape, F32)
        yt_ref[1] = jnp.zeros(yt_ref.shape[1:], F32)

        def gather(r, carry):
            xs_ref[0, pl.ds(pl.multiple_of(r * SUBLANES, SUBLANES), SUBLANES), :] = src_ref[0, tok_ref[0, 0, r]]
            return carry
        lax.fori_loop(0, tmx, gather, 0, unroll=8)

    @pl.when(i < nt_ref[c])
    def _():
        x = jnp.concatenate([xs_ref[slot, pl.ds(s, tmx, stride=SUBLANES), :]
                             for s in range(D_MODEL // LANES)], axis=-1).astype(BF16)
        n_blk = w1_ref.shape[2] // (2 * GLU_BLOCK)
        wide = 2 * GLU_BLOCK

        def hidden(j):
            cols = slice(j * wide, (j + 1) * wide)
            return jnp.dot(x, w1_ref[0, :, cols], preferred_element_type=F32) + b1_ref[0, :, cols]

        hid = hidden(0)
        y = None
        for j in range(n_blk):
            nxt = hidden(j + 1) if j + 1 < n_blk else None
            for r in range(j * tmx // n_blk, (j + 1) * tmx // n_blk):
                xs_ref[1 - slot, r * SUBLANES:(r + 1) * SUBLANES, :] = src_ref[0, toknext_ref[0, 0, r]]
            add_prev_rows(range(j * n_grp // n_blk, (j + 1) * n_grp // n_blk))
            acts = []
            for b in range(2):
                glu = jnp.minimum(hid[:, b * GLU_BLOCK:b * GLU_BLOCK + LANES], SWIGLU_LIMIT)
                lin = jnp.clip(hid[:, b * GLU_BLOCK + LANES:(b + 1) * GLU_BLOCK], -SWIGLU_LIMIT, SWIGLU_LIMIT)
                acts.append(glu * _sigmoid(SWIGLU_ALPHA * glu) * (lin + 1.0))
            part = _mm(jnp.concatenate(acts, axis=-1), w2_ref[0, j * GLU_BLOCK:(j + 1) * GLU_BLOCK, :])
            y = part if y is None else y + part
            hid = nxt
        meta = rows_ref[0]
        expert = texp_ref[c * pl.num_programs(1) + i].astype(F32)
        lane = lax.broadcasted_iota(jnp.int32, meta.shape, 1)
        hit = jnp.logical_and(lane < TOP_K, meta == expert)
        gates = pltpu.roll(meta, LANES - TOP_K, axis=1)
        gate = jnp.sum(jnp.where(hit, gates, 0.0), axis=-1, keepdims=True)
        y = gate * (y + b2_ref[0])
        for s in range(D_MODEL // LANES):
            yt_ref[slot, pl.ds(s, tmx, stride=SUBLANES), :] = y[:, s * LANES:(s + 1) * LANES]

    @pl.when(i == nt_ref[c])
    def _():
        add_prev_rows(range(n_grp))


def _moe_experts(h2, plan, w1p, b1p, w2p, b2, tmx):
    nch, tc = h2.shape[:2]
    nt = plan["nt"]
    n_e, _, d_hid = w1p.shape
    d_exp = w2p.shape[1]
    steps = nt + 1
    flat = lambda c, i: c * steps + i
    cur = lambda c, i, n, te, tb, tn, tp: (tb[flat(c, i)], 0, 0)
    nxt = lambda c, i, n, te, tb, tn, tp: (tn[flat(c, i)], 0, 0)
    prv = lambda c, i, n, te, tb, tn, tp: (tp[flat(c, i)], 0, 0)
    exp = lambda c, i, n, te, tb, tn, tp: (te[flat(c, i)], 0, 0)
    grid_spec = pltpu.PrefetchScalarGridSpec(
        num_scalar_prefetch=5,
        grid=(nch, steps),
        in_specs=[pl.BlockSpec((1, 1, tmx), cur, memory_space=pltpu.SMEM),
                  pl.BlockSpec((1, 1, tmx), nxt, memory_space=pltpu.SMEM),
                  pl.BlockSpec((1, 1, tmx), prv, memory_space=pltpu.SMEM),
                  pl.BlockSpec((1, tmx, LANES), cur),
                  pl.BlockSpec((1, tc, D_MODEL // LANES, LANES), lambda c, i, n, te, tb, tn, tp: (c, 0, 0, 0),
                               pipeline_mode=pl.Buffered(1)),
                  pl.BlockSpec((1, D_MODEL, d_hid), exp),
                  pl.BlockSpec((1, 1, d_hid), exp),
                  pl.BlockSpec((1, d_exp, D_MODEL), exp),
                  pl.BlockSpec((1, 1, D_MODEL), exp)],
        out_specs=pl.BlockSpec((1, tc + SUBLANES, D_MODEL // LANES, LANES),
                               lambda c, i, n, te, tb, tn, tp: (c, 0, 0, 0), pipeline_mode=pl.Buffered(1)),
        scratch_shapes=[pltpu.VMEM((2, tmx * (D_MODEL // LANES), LANES), F32),
                        pltpu.VMEM((2, tmx * (D_MODEL // LANES), LANES), F32)],
    )
    return pl.pallas_call(
        _moe_kernel,
        grid_spec=grid_spec,
        out_shape=jax.ShapeDtypeStruct((nch, tc + SUBLANES, D_MODEL // LANES, LANES), F32),
        compiler_params=_cparams(("parallel", "arbitrary")),
        name="moe_experts",
    )(plan["n_tiles"], plan["texp"], plan["tblk"], plan["tnext"], plan["tprev"], plan["row_tok"],
      plan["row_tok"], plan["row_dst"], plan["rows"], h2, w1p, b1p, w2p, b2)


def _final_kernel(x1_ref, f_ref, g2_ref, gpost_ref, y_ref):
    tm = x1_ref.shape[1]
    f = jnp.concatenate([f_ref[pl.ds(s, tm, stride=SUBLANES), :] for s in range(D_MODEL // LANES)],
                        axis=-1)
    y_ref[0] = x1_ref[0] + g2_ref[0] * _rms(f, gpost_ref[...], D_MODEL)


def _final(x1, f_all, row0, g2, g_post_ffn, tm):
    n, l, _ = x1.shape
    per_row = g2.shape[1] != 1
    mod_blk = (1, tm, D_MODEL) if per_row else (1, 1, D_MODEL)
    mod_map = (lambda b, i: (b, i, 0)) if per_row else (lambda b, i: (b, 0, 0))
    row_map = lambda b, i: (b, i, 0)
    blk0, per_seq = row0 // tm, l // tm
    return pl.pallas_call(
        _final_kernel,
        grid=(n, l // tm),
        in_specs=[pl.BlockSpec((1, tm, D_MODEL), row_map),
                  pl.BlockSpec((tm * (D_MODEL // LANES), LANES), lambda b, i: (blk0 + b * per_seq + i, 0)),
                  pl.BlockSpec(mod_blk, mod_map),
                  pl.BlockSpec((1, D_MODEL), lambda b, i: (0, 0))],
        out_specs=pl.BlockSpec((1, tm, D_MODEL), row_map),
        out_shape=jax.ShapeDtypeStruct((n, l, D_MODEL), F32),
        compiler_params=_cparams(("parallel", "parallel")),
        name="final",
    )(x1, f_all, g2, g_post_ffn)


def _complex_pow2(lam, n):
    re, im = lam[0], lam[1]
    steps = int(round(math.log2(n)))
    assert 2 ** steps == n
    for _ in range(steps):
        re, im = re * re - im * im, 2.0 * re * im
    return jnp.stack([re, im])


def _rope_tables(pos):
    half = ROPE_DIM // 2
    inv = ROPE_THETA ** (-jnp.arange(half, dtype=F32) / half)
    ang = pos.astype(F32)[:, None] * inv[None, :]
    cos, sin = jnp.cos(ang), jnp.sin(ang)
    cos4 = jnp.tile(jnp.concatenate([cos, cos], axis=-1), (1, LANES // ROPE_DIM))
    sin4 = jnp.tile(jnp.concatenate([-sin, sin], axis=-1), (1, LANES // ROPE_DIM))
    return cos4[None], sin4[None]


def _swap_halves(w):
    half = ROPE_DIM // 2
    return jnp.concatenate([w[..., half:], w[..., :half]], axis=-1)


def _prep_layer(w, layer):
    g = lambda name: w[name][layer]
    p = {}
    row = lambda v: v.reshape(1, -1)
    p["g_pre_mix"] = row(g("g_pre_mix"))
    w_in = g("w_in")
    w_q, w_kv, w_r, w_u = jnp.split(w_in, [Q_RANK, Q_RANK + KV_RANK, Q_RANK + KV_RANK + ROPE_DIM], axis=-1)
    rep = LANES // ROPE_DIM
    p["w_in"] = jnp.concatenate(
        [w_q, jnp.zeros((D_MODEL, Q_PAD - Q_RANK), F32), w_kv,
         jnp.tile(w_r, (1, rep)), jnp.tile(_swap_halves(w_r), (1, rep)), w_u], axis=-1)
    p["g_q"] = row(jnp.pad(g("g_q"), (0, Q_PAD - Q_RANK)))
    w_uq = g("w_uq").reshape(Q_RANK, N_HEADS, NOPE_DIM + ROPE_DIM)
    nope = w_uq[:, :, :NOPE_DIM].reshape(Q_RANK, N_HEADS * NOPE_DIM)
    rope = w_uq[:, :, NOPE_DIM:]
    rope_sw = _swap_halves(rope)
    w_uq_p = jnp.concatenate([nope, rope.reshape(Q_RANK, -1), rope_sw.reshape(Q_RANK, -1)], axis=-1)
    p["w_uq"] = jnp.pad(w_uq_p, ((0, Q_PAD - Q_RANK), (0, 0)))
    w_uk = jnp.transpose(g("w_uk"), (1, 2, 0))
    z = jnp.zeros((NOPE_DIM, KV_RANK), F32)
    p["w_uk"] = jnp.stack([jnp.block([[w_uk[2 * j], z], [z, w_uk[2 * j + 1]]])
                           for j in range(N_HEADS // 2)])
    p["g_kv"] = row(g("g_kv"))
    w_uv = jnp.transpose(g("w_uv"), (1, 0, 2))
    p["w_uv_t"] = jnp.transpose(w_uv, (0, 2, 1))
    eye_h = jnp.eye(N_HEADS, dtype=F32)
    p["w_uv_bd"] = jnp.einsum("hrv,hk->hrkv", w_uv, eye_h).reshape(N_HEADS * KV_RANK, ATTN_WIDTH)
    p["w_out"] = g("w_out")
    for name in ("w_in", "w_uq", "w_uk", "w_uv_t", "w_uv_bd", "w_out"):
        p[name] = p[name].astype(BF16)
    p["g_post_mix"] = row(g("g_post_mix"))
    p["g_pre_ffn"] = row(g("g_pre_ffn"))
    p["w_router"] = jnp.pad(g("w_router"), ((0, 0), (0, LANES - N_EXPERTS)))
    p["b_router"] = row(jnp.pad(g("b_router"), (0, LANES - N_EXPERTS), constant_values=NEG_INF))
    p["g_post_ffn"] = row(g("g_post_ffn"))

    lam_re, lam_im = g("ssm_lam_re"), g("ssm_lam_im")
    step = jnp.exp(g("ssm_log_step"))[:, None]
    mag = jnp.exp(lam_re * step)
    bar_re, bar_im = mag * jnp.cos(lam_im * step), mag * jnp.sin(lam_im * step)
    den = lam_re * lam_re + lam_im * lam_im
    co_re = ((bar_re - 1.0) * lam_re + bar_im * lam_im) / den
    co_im = (bar_im * lam_re - (bar_re - 1.0) * lam_im) / den
    b_re, b_im = g("ssm_b_re"), g("ssm_b_im")
    bb_re = co_re[..., None] * b_re - co_im[..., None] * b_im
    bb_im = co_re[..., None] * b_im + co_im[..., None] * b_re
    eye_g = jnp.eye(GROUPS_PER_BLOCK, dtype=F32)

    def b_block(m):
        m = m.reshape(SSM_BLOCKS, GROUPS_PER_BLOCK, SSM_STATE, SSM_GROUP)
        return jnp.einsum("bgpc,gh->bgchp", m, eye_g).reshape(SSM_BLOCKS, LANES, BLOCK_STATES)

    def c_block(m):
        m = m.reshape(SSM_BLOCKS, GROUPS_PER_BLOCK, SSM_GROUP, SSM_STATE)
        return jnp.einsum("bgcp,gh->bgphc", m, eye_g).reshape(SSM_BLOCKS, BLOCK_STATES, LANES)

    sp = {
        "lam": jnp.stack([bar_re.reshape(-1), bar_im.reshape(-1)]),
        "bmat": jnp.concatenate([b_block(bb_re), b_block(bb_im)], axis=-1).astype(BF16),
        "cmat": jnp.concatenate([c_block(g("ssm_c_re")), c_block(-g("ssm_c_im"))], axis=1).astype(BF16),
        "d": row(g("ssm_d")),
        "w_glu": g("ssm_w_glu").astype(BF16),
        "b_glu": row(g("ssm_b_glu")),
    }
    b1 = g("b1")
    n_blk = b1.shape[1] // GLU_BLOCK
    mp = {
        "b1": jnp.transpose(b1.reshape(N_EXPERTS, n_blk, LANES, 2), (0, 1, 3, 2)).reshape(N_EXPERTS, 1, -1),
        "b2": g("b2")[:, None, :],
    }
    return p, sp, mp


def _mods(mod_rows):
    return [mod_rows[:, k * D_MODEL:(k + 1) * D_MODEL] for k in range(N_MOD)]


def _pick_tile(n, target, mult):
    best = mult
    for c in range(mult, min(n, target) + 1, mult):
        if n % c == 0:
            best = c
    return best


def kernel(x_prompt, x_sample, c_prompt, c_sample, cache_kv_latent, cache_k_rope, state_ssm, page_table, w_ada, b_ada, g_pre_mix, w_in, g_q, w_uq, w_uk, g_kv, w_uv, ssm_lam_re, ssm_lam_im, ssm_log_step, ssm_b_re, ssm_b_im, ssm_c_re, ssm_c_im, ssm_d, ssm_w_glu, ssm_b_glu, w_out, g_post_mix, g_pre_ffn, w_router, b_router, w1, b1, w2, b2, g_post_ffn):
    weights = dict(w_ada=w_ada, b_ada=b_ada, g_pre_mix=g_pre_mix, w_in=w_in, g_q=g_q, w_uq=w_uq,
                   w_uk=w_uk, g_kv=g_kv, w_uv=w_uv, ssm_lam_re=ssm_lam_re, ssm_lam_im=ssm_lam_im,
                   ssm_log_step=ssm_log_step, ssm_b_re=ssm_b_re, ssm_b_im=ssm_b_im,
                   ssm_c_re=ssm_c_re, ssm_c_im=ssm_c_im, ssm_d=ssm_d, ssm_w_glu=ssm_w_glu,
                   ssm_b_glu=ssm_b_glu, w_out=w_out, g_post_mix=g_post_mix, g_pre_ffn=g_pre_ffn,
                   w_router=w_router, b_router=b_router, w1=w1, b1=b1, w2=w2, b2=b2,
                   g_post_ffn=g_post_ffn)
    depth = w_ada.shape[0]
    n_p, l_p, _ = x_prompt.shape
    n_s, l_s, _ = x_sample.shape
    assert l_s == 1
    n_pages = page_table.shape[1]
    past_len = n_pages * PAGE_SIZE

    cos_p, sin_p = _rope_tables(jnp.arange(l_p, dtype=jnp.int32))
    cos_s, sin_s = _rope_tables(jnp.full((1,), past_len, jnp.int32))
    cos_s = jnp.broadcast_to(cos_s, (1, n_s, LANES))
    sin_s = jnp.broadcast_to(sin_s, (1, n_s, LANES))

    tm_p = _pick_tile(l_p, 512, 16)
    tq = _pick_tile(l_p, 512, 16)
    tk = _pick_tile(l_p, 1024, 16)
    lc = 64 if l_p % 512 == 0 else l_p // SUBLANES
    c_rows = n_p + n_s
    c_pad = -c_rows % SUBLANES
    c_all = jnp.pad(jnp.concatenate([c_prompt, c_sample], axis=0), ((0, c_pad), (0, 0)))

    y_p = x_prompt
    y_s = x_sample.reshape(1, n_s, D_MODEL)
    outs = [[] for _ in range(6)]
    for layer in range(depth):
        p, sp, mp = _prep_layer(weights, layer)
        mod = _ada(c_all, w_ada[layer], b_ada[layer].reshape(1, -1))
        sh1p, sc1p, g1p, sh2p, sc2p, g2p = [m[:, None, :] for m in _mods(mod[:n_p])]
        sh1s, sc1s, g1s, sh2s, sc2s, g2s = [m[None] for m in _mods(mod[n_p:c_rows])]

        q_t, k_cat, v_t, kv_p, kr_p, u_p = _inproj(y_p, sc1p, sh1p, p, cos_p, sin_p, tm_p)
        o_attn_p = _flash(q_t, k_cat, v_t, p["w_uv_t"], tq, tk)
        y_ssm_p, h_last_p = _s5_prompt(u_p, sp, lc)
        t_p = n_p * l_p
        t_all = t_p + n_s
        blank = (jnp.zeros((t_all * (D_MODEL // LANES), LANES), F32), jnp.zeros((t_all, LANES), F32))
        x1_p, h2_all, route = _outproj(o_attn_p, y_ssm_p, y_p, g1p, sc2p, sh2p, p, tm_p, t_all, 0,
                                       dest=blank)

        q_t_s, _, _, kv_s, kr_s, u_s = _inproj(y_s, sc1s, sh1s, p, cos_s, sin_s, n_s)
        q_lat_s = jnp.transpose(q_t_s[0, :, :KV_RANK, :], (2, 0, 1))
        q_rope_s = jnp.stack(
            [q_t_s[0, h, KV_RANK + ROPE_DIM * (h % 4):KV_RANK + ROPE_DIM * (h % 4 + 1), :].T
             for h in range(N_HEADS)], axis=1)
        o_lat_s = _decode_attn(q_lat_s, q_rope_s, kv_s.reshape(n_s, 1, KV_RANK),
                               kr_s.reshape(n_s, 1, ROPE_DIM), cache_kv_latent[layer],
                               jnp.swapaxes(cache_k_rope[layer], 1, 2), page_table)
        o_attn_s = _uv_proj(o_lat_s.reshape(n_s, N_HEADS * KV_RANK), p["w_uv_bd"])
        h0 = jnp.moveaxis(state_ssm[layer].reshape(n_s, N_STATE, 2), -1, 0)
        y_ssm_s, h_s = _s5_step(u_s[0], h0, sp)
        x1_s, h2_all, route = _outproj(o_attn_s[None], y_ssm_s[None], y_s, g1s, sc2s, sh2s, p, n_s,
                                       t_all, t_p, dest=(h2_all, route))

        nch = 4 if t_all % (4 * SUBLANES) == 0 else (2 if t_all % (2 * SUBLANES) == 0 else 1)
        tc = t_all // nch
        plan = _route_plan(route, nch, MOE_TILE)
        w1p, w2p = _moe_prep(w1[layer], w2[layer])
        f_chunks = _moe_experts(h2_all.reshape(nch, tc, D_MODEL // LANES, LANES), plan, w1p, mp["b1"],
                                w2p, mp["b2"], MOE_TILE)
        f_all = f_chunks[:, :tc].reshape(t_all * (D_MODEL // LANES), LANES)
        y_p = _final(x1_p, f_all, 0, g2p, p["g_post_ffn"], tm_p)
        y_s = _final(x1_s, f_all, t_p, g2s, p["g_post_ffn"], n_s)

        outs[0].append(kv_p)
        outs[1].append(kr_p)
        outs[2].append(jnp.moveaxis(h_last_p, 1, -1).reshape(n_p, N_SSM_GROUPS, SSM_STATE, 2))
        outs[3].append(kv_s.reshape(n_s, 1, KV_RANK))
        outs[4].append(kr_s.reshape(n_s, 1, ROPE_DIM))
        outs[5].append(jnp.moveaxis(h_s, 0, -1).reshape(n_s, N_SSM_GROUPS, SSM_STATE, 2))
    return (y_p, y_s.reshape(n_s, 1, D_MODEL)) + tuple(jnp.stack(o) for o in outs)
```

```python
import functools
import math

import jax
import jax.numpy as jnp
from jax import lax
from jax.experimental import pallas as pl
from jax.experimental.pallas import tpu as pltpu

F32 = jnp.float32
BF16 = jnp.bfloat16

D_MODEL = 1024
N_HEADS = 8
NOPE_DIM = 64
ROPE_DIM = 32
V_DIM = 64
Q_RANK = 352
KV_RANK = 128
ROPE_THETA = 10000.0
ATTN_WIDTH = N_HEADS * V_DIM
ATTN_SCALE = 1.0 / math.sqrt(NOPE_DIM + ROPE_DIM)
SSM_WIDTH = D_MODEL - ATTN_WIDTH
SSM_GROUP = 16
N_SSM_GROUPS = SSM_WIDTH // SSM_GROUP
SSM_STATE = 64
N_STATE = N_SSM_GROUPS * SSM_STATE
N_EXPERTS = 32
TOP_K = 4
SWIGLU_LIMIT = 7.0
SWIGLU_ALPHA = 1.702
RMS_EPS = 1e-6
N_MOD = 6
NEG_INF = -1e30
PAGE_SIZE = 128

LANES = 128
SUBLANES = 8
VMEM_LIMIT = 56 * 1024 * 1024

Q_PAD = 384
CAT = 2 * LANES
SSM_BLOCKS = 4
GROUPS_PER_BLOCK = N_SSM_GROUPS // SSM_BLOCKS
BLOCK_STATES = GROUPS_PER_BLOCK * SSM_STATE
IN_PAD = Q_PAD + KV_RANK + 2 * LANES + SSM_WIDTH
UQ_PAD = N_HEADS * NOPE_DIM + 4 * LANES
GLU_BLOCK = 2 * LANES
MOE_TILE = 192
LOG2E = math.log2(math.e)
FLASH_LOOKAHEAD = 6
FLASH_DIAG_SUB = 256
VT_ROWS = KV_RANK + 16
ROUTE_TOKEN = 2 * TOP_K


def _cparams(sem):
    return pltpu.CompilerParams(dimension_semantics=sem, vmem_limit_bytes=VMEM_LIMIT)


def _mm(a, b):
    return jnp.dot(a.astype(BF16), b.astype(BF16), preferred_element_type=F32)


def _mm_nt(a, b):
    return lax.dot_general(a.astype(BF16), b.astype(BF16), (((1,), (1,)), ((), ())),
                           preferred_element_type=F32)


def _split_bf16(x):
    hi = x.astype(BF16)
    lo = (x - hi.astype(F32)).astype(BF16)
    return hi, lo


def _mm3(a, b):
    ah, al = _split_bf16(a)
    bh, bl = _split_bf16(b)
    d = functools.partial(jnp.dot, preferred_element_type=F32)
    return d(ah, bh) + (d(ah, bl) + d(al, bh))


def _sigmoid(x):
    return 1.0 / (1.0 + jnp.exp(-x))


def _rms(x, g, width):
    ms = jnp.sum(x * x, axis=-1, keepdims=True) * (1.0 / width)
    return x * lax.rsqrt(ms + RMS_EPS) * g


def _gelu_tanh(x):
    return 0.5 * x * (1.0 + jnp.tanh(math.sqrt(2.0 / math.pi) * (x + 0.044715 * (x * x * x))))


def _ada_kernel(c_ref, w_ref, b_ref, o_ref):
    c = c_ref[...]
    o_ref[...] = _mm3(c * _sigmoid(c), w_ref[...]) + b_ref[...]


def _ada(c_all, w_ada, b_ada):
    rows = c_all.shape[0]
    tn = 512
    n_out = w_ada.shape[1]
    return pl.pallas_call(
        _ada_kernel,
        grid=(n_out // tn,),
        in_specs=[pl.BlockSpec((rows, D_MODEL), lambda j: (0, 0)),
                  pl.BlockSpec((D_MODEL, tn), lambda j: (0, j)),
                  pl.BlockSpec((1, tn), lambda j: (0, j))],
        out_specs=pl.BlockSpec((rows, tn), lambda j: (0, j)),
        out_shape=jax.ShapeDtypeStruct((rows, n_out), F32),
        compiler_params=_cparams(("parallel",)),
        name="adaln",
    )(c_all, w_ada, b_ada)


def _inproj_kernel(x_ref, sc_ref, sh_ref, gpre_ref, win_ref, gq_ref, wuq_ref, wuk_ref, gkv_ref,
                   cos_ref, sin_ref, q_ref, kcat_ref, vt_ref, kv_ref, kr_ref, u_ref):
    x = x_ref[0]
    h = _rms(x, gpre_ref[...], D_MODEL) * (1.0 + sc_ref[0]) + sh_ref[0]
    z = _mm(h, win_ref[...])
    zq = z[:, :Q_PAD]
    zkv = z[:, Q_PAD:Q_PAD + KV_RANK]
    zr = z[:, Q_PAD + KV_RANK:Q_PAD + KV_RANK + LANES]
    zrs = z[:, Q_PAD + KV_RANK + LANES:Q_PAD + KV_RANK + 2 * LANES]
    u = z[:, Q_PAD + KV_RANK + 2 * LANES:]
    cos4 = cos_ref[0]
    sin4 = sin_ref[0]

    kvl = _rms(zkv, gkv_ref[...], KV_RANK)
    kr4 = zr * cos4 + zrs * sin4
    kv_ref[0] = kvl
    kr_ref[0] = kr4[:, :ROPE_DIM]
    kcat_ref[0] = jnp.concatenate([kvl, kr4], axis=-1).astype(BF16)
    extra = (lax.broadcasted_iota(jnp.int32, (VT_ROWS - KV_RANK, kvl.shape[0]), 0) == 0).astype(F32)
    vt_ref[0] = jnp.concatenate([kvl.T, extra], axis=0).astype(BF16)
    u_ref[0] = u.astype(BF16)

    qn = _rms(zq, gq_ref[...], Q_RANK)
    q = _mm(qn, wuq_ref[...])
    nope_w = N_HEADS * NOPE_DIM
    rope_a = q[:, nope_w:nope_w + LANES] * cos4 + q[:, nope_w + 2 * LANES:nope_w + 3 * LANES] * sin4
    rope_b = q[:, nope_w + LANES:nope_w + 2 * LANES] * cos4 + q[:, nope_w + 3 * LANES:] * sin4
    slot = lax.broadcasted_iota(jnp.int32, (1, LANES), 1) // ROPE_DIM
    for j in range(N_HEADS // 2):
        ql = _mm(q[:, j * LANES:(j + 1) * LANES], wuk_ref[j])
        for hh in range(2):
            head = 2 * j + hh
            rope = rope_a if head < 4 else rope_b
            rope_h = jnp.where(slot == (head % 4), rope, 0.0)
            qc = jnp.concatenate([ql[:, hh * KV_RANK:(hh + 1) * KV_RANK], rope_h], axis=-1)
            q_ref[0, head] = (qc * (ATTN_SCALE * LOG2E)).T.astype(BF16)


def _inproj(x, sc, sh, p, cos4, sin4, tm):
    n, l, _ = x.shape
    per_row = sc.shape[1] != 1
    mod_blk = (1, tm, D_MODEL) if per_row else (1, 1, D_MODEL)
    mod_map = (lambda b, i: (b, i, 0)) if per_row else (lambda b, i: (b, 0, 0))
    const2 = lambda b, i: (0, 0)
    const3 = lambda b, i: (0, 0, 0)
    pos_map = lambda b, i: (0, i, 0)
    return pl.pallas_call(
        _inproj_kernel,
        grid=(n, l // tm),
        in_specs=[pl.BlockSpec((1, tm, D_MODEL), lambda b, i: (b, i, 0)),
                  pl.BlockSpec(mod_blk, mod_map),
                  pl.BlockSpec(mod_blk, mod_map),
                  pl.BlockSpec((1, D_MODEL), const2),
                  pl.BlockSpec((D_MODEL, IN_PAD), const2),
                  pl.BlockSpec((1, Q_PAD), const2),
                  pl.BlockSpec((Q_PAD, UQ_PAD), const2),
                  pl.BlockSpec((N_HEADS // 2, LANES, 2 * KV_RANK), const3),
                  pl.BlockSpec((1, KV_RANK), const2),
                  pl.BlockSpec((1, tm, LANES), pos_map),
                  pl.BlockSpec((1, tm, LANES), pos_map)],
        out_specs=[pl.BlockSpec((1, N_HEADS, CAT, tm), lambda b, i: (b, 0, 0, i)),
                   pl.BlockSpec((1, tm, CAT), lambda b, i: (b, i, 0)),
                   pl.BlockSpec((1, VT_ROWS, tm), lambda b, i: (b, 0, i)),
                   pl.BlockSpec((1, tm, KV_RANK), lambda b, i: (b, i, 0)),
                   pl.BlockSpec((1, tm, ROPE_DIM), lambda b, i: (b, i, 0)),
                   pl.BlockSpec((1, tm, SSM_WIDTH), lambda b, i: (b, i, 0))],
        out_shape=[jax.ShapeDtypeStruct((n, N_HEADS, CAT, l), BF16),
                   jax.ShapeDtypeStruct((n, l, CAT), BF16),
                   jax.ShapeDtypeStruct((n, VT_ROWS, l), BF16),
                   jax.ShapeDtypeStruct((n, l, KV_RANK), F32),
                   jax.ShapeDtypeStruct((n, l, ROPE_DIM), F32),
                   jax.ShapeDtypeStruct((n, l, SSM_WIDTH), BF16)],
        compiler_params=_cparams(("parallel", "parallel")),
        name="inproj",
    )(x, sc, sh, p["g_pre_mix"], p["w_in"], p["g_q"], p["w_uq"], p["w_uk"], p["g_kv"], cos4, sin4)


def _flash_kernel(qt_ref, k_ref, vt_ref, wuvt_ref, o_ref, m_ref, acc_ref, *, tq, tk):
    i = pl.program_id(1)
    j = pl.program_id(2)
    j_last = ((i + 1) * tq - 1) // tk

    @pl.when(j == 0)
    def _():
        m_ref[...] = jnp.full(m_ref.shape, NEG_INF, F32)
        acc_ref[...] = jnp.zeros(acc_ref.shape, F32)

    def update(masked, key0, size):
        k = k_ref[0, pl.ds(key0, size), :]
        vt = vt_ref[0, :, pl.ds(key0, size)]
        if masked:
            k_pos = j * tk + key0 + lax.broadcasted_iota(jnp.int32, (size, tq), 0)
            q_pos = i * tq + lax.broadcasted_iota(jnp.int32, (size, tq), 1)
            keep = k_pos <= q_pos
        m_all = m_ref[...]
        m_rows = []
        scores = lambda h: jnp.dot(k, qt_ref[0, h], preferred_element_type=F32)
        ahead = [scores(h) for h in range(min(FLASH_LOOKAHEAD, N_HEADS))]
        pending = None
        for h in range(N_HEADS):
            s = ahead.pop(0)
            if h + FLASH_LOOKAHEAD < N_HEADS:
                ahead.append(scores(h + FLASH_LOOKAHEAD))
            if masked:
                s = jnp.where(keep, s, NEG_INF)
            m_prev = m_all[h:h + 1]
            m_new = jnp.maximum(m_prev, jnp.max(s, axis=0, keepdims=True))
            alpha = jnp.exp2(m_prev - m_new)
            p = jnp.exp2(s - m_new)
            pv = jnp.dot(vt, p.astype(BF16), preferred_element_type=F32)
            if pending is not None:
                acc_ref[pending[0]] = pending[1] * acc_ref[pending[0]] + pending[2]
            pending = (h, alpha, pv)
            m_rows.append(m_new)
        acc_ref[pending[0]] = pending[1] * acc_ref[pending[0]] + pending[2]
        m_ref[...] = jnp.concatenate(m_rows, axis=0)

    crosses = (j + 1) * tk - 1 > i * tq
    sub = min(tk, FLASH_DIAG_SUB)

    @pl.when(jnp.logical_and(j <= j_last, crosses))
    def _():
        n_sub = jnp.minimum(tk // sub, ((i + 1) * tq - 1 - j * tk) // sub + 1)

        def body(sb, carry):
            update(True, pl.multiple_of(sb * sub, sub), sub)
            return carry
        lax.fori_loop(0, n_sub, body, 0)

    @pl.when(jnp.logical_and(j <= j_last, jnp.logical_not(crosses)))
    def _():
        update(False, 0, tk)

    @pl.when(j == j_last)
    def _():
        outs = []
        for h in range(N_HEADS):
            o_t = acc_ref[h, :KV_RANK, :] / acc_ref[h, KV_RANK:KV_RANK + 1, :]
            outs.append(_mm(wuvt_ref[h], o_t))
        o_ref[0] = jnp.concatenate(outs, axis=0).T.astype(BF16)


def _flash(q_t, k_cat, v_t, wuv_t, tq, tk):
    n, _, _, l = q_t.shape
    kern = functools.partial(_flash_kernel, tq=tq, tk=tk)
    last = lambda i: ((i + 1) * tq - 1) // tk
    return pl.pallas_call(
        kern,
        grid=(n, l // tq, l // tk),
        in_specs=[pl.BlockSpec((1, N_HEADS, CAT, tq), lambda b, i, j: (b, 0, 0, i)),
                  pl.BlockSpec((1, tk, CAT), lambda b, i, j: (b, jnp.minimum(j, last(i)), 0)),
                  pl.BlockSpec((1, VT_ROWS, tk), lambda b, i, j: (b, 0, jnp.minimum(j, last(i)))),
                  pl.BlockSpec((N_HEADS, V_DIM, KV_RANK), lambda b, i, j: (0, 0, 0))],
        out_specs=pl.BlockSpec((1, tq, ATTN_WIDTH), lambda b, i, j: (b, i, 0)),
        out_shape=jax.ShapeDtypeStruct((n, l, ATTN_WIDTH), BF16),
        scratch_shapes=[pltpu.VMEM((N_HEADS, tq), F32),
                        pltpu.VMEM((N_HEADS, VT_ROWS, tq), F32)],
        compiler_params=_cparams(("parallel", "parallel", "arbitrary")),
        name="flash",
    )(q_t, k_cat, v_t, wuv_t)


def _ssm_out(y, up, d_ref, wglu_ref, bglu_ref):
    y = _gelu_tanh(y + d_ref[...] * up)
    return y * _sigmoid(_mm(y, wglu_ref[...]) + bglu_ref[...])


def _s5_kernel(u_ref, perm_ref, permt_ref, bmat_ref, cmat_ref, lam_ref, laml_ref, d_ref, wglu_ref,
               bglu_ref, y_ref, hlast_ref, buf_ref, carry_ref, *, lc):
    j = pl.program_id(1)

    @pl.when(j == 0)
    def _():
        carry_ref[...] = jnp.zeros(carry_ref.shape, F32)

    u = u_ref[0]
    up = jnp.dot(perm_ref[...], u, preferred_element_type=F32).astype(BF16)
    blk_w = 2 * BLOCK_STATES

    def input_proj(b):
        buf_ref[:, b * blk_w:(b + 1) * blk_w] = jnp.dot(
            up[:, b * LANES:(b + 1) * LANES], bmat_ref[b], preferred_element_type=F32)

    input_proj(0)
    ys = []
    for b in range(SSM_BLOCKS):
        if b + 1 < SSM_BLOCKS:
            input_proj(b + 1)
        st = slice(b * BLOCK_STATES, (b + 1) * BLOCK_STATES)
        re_cols = slice(b * blk_w, b * blk_w + BLOCK_STATES)
        im_cols = slice(b * blk_w + BLOCK_STATES, (b + 1) * blk_w)
        lr = lam_ref[0, :, st]
        li = lam_ref[1, :, st]

        def run(init, store):
            re, im = init
            for t in range(lc):
                rows = slice(t * SUBLANES, (t + 1) * SUBLANES)
                re, im = (lr * re - li * im + buf_ref[rows, re_cols],
                          lr * im + li * re + buf_ref[rows, im_cols])
                if store:
                    buf_ref[rows, re_cols] = re
                    buf_ref[rows, im_cols] = im
            return re, im

        zero = jnp.zeros((SUBLANES, BLOCK_STATES), F32)
        end_re, end_im = run((zero, zero), False)
        ll_r = laml_ref[0:1, st]
        ll_i = laml_ref[1:2, st]
        c_re = carry_ref[0:1, st]
        c_im = carry_ref[1:2, st]
        init_re, init_im = [], []
        for k in range(SUBLANES):
            init_re.append(c_re)
            init_im.append(c_im)
            c_re, c_im = (ll_r * c_re - ll_i * c_im + end_re[k:k + 1],
                          ll_r * c_im + ll_i * c_re + end_im[k:k + 1])
        carry_ref[0:1, st] = c_re
        carry_ref[1:2, st] = c_im
        run((jnp.concatenate(init_re, axis=0), jnp.concatenate(init_im, axis=0)), True)
        ys.append(jnp.dot(buf_ref[:, b * blk_w:(b + 1) * blk_w].astype(BF16), cmat_ref[b],
                          preferred_element_type=F32))

    hlast_ref[0] = carry_ref[...]
    y = _ssm_out(jnp.concatenate(ys, axis=-1), up.astype(F32), d_ref, wglu_ref, bglu_ref)
    y_ref[0] = jnp.dot(permt_ref[...], y.astype(BF16), preferred_element_type=F32).astype(BF16)


def _s5_prompt(u, sp, lc):
    n, l, _ = u.shape
    s = SUBLANES * lc
    rows = jnp.arange(s)
    src = (rows % SUBLANES) * lc + rows // SUBLANES
    perm = (src[:, None] == rows[None, :]).astype(BF16)
    laml = _complex_pow2(sp["lam"], lc)
    lam8 = jnp.broadcast_to(sp["lam"][:, None, :], (2, SUBLANES, N_STATE))
    kern = functools.partial(_s5_kernel, lc=lc)
    c2 = lambda b, j: (0, 0)
    c3 = lambda b, j: (0, 0, 0)
    return pl.pallas_call(
        kern,
        grid=(n, l // s),
        in_specs=[pl.BlockSpec((1, s, SSM_WIDTH), lambda b, j: (b, j, 0)),
                  pl.BlockSpec((s, s), c2),
                  pl.BlockSpec((s, s), c2),
                  pl.BlockSpec((SSM_BLOCKS, LANES, 2 * BLOCK_STATES), c3),
                  pl.BlockSpec((SSM_BLOCKS, 2 * BLOCK_STATES, LANES), c3),
                  pl.BlockSpec((2, SUBLANES, N_STATE), c3),
                  pl.BlockSpec((2, N_STATE), c2),
                  pl.BlockSpec((1, SSM_WIDTH), c2),
                  pl.BlockSpec((SSM_WIDTH, SSM_WIDTH), c2),
                  pl.BlockSpec((1, SSM_WIDTH), c2)],
        out_specs=[pl.BlockSpec((1, s, SSM_WIDTH), lambda b, j: (b, j, 0)),
                   pl.BlockSpec((1, 2, N_STATE), lambda b, j: (b, 0, 0))],
        out_shape=[jax.ShapeDtypeStruct((n, l, SSM_WIDTH), BF16),
                   jax.ShapeDtypeStruct((n, 2, N_STATE), F32)],
        scratch_shapes=[pltpu.VMEM((s, 2 * N_STATE), F32),
                        pltpu.VMEM((2, N_STATE), F32)],
        compiler_params=_cparams(("parallel", "arbitrary")),
        name="s5_scan",
    )(u, perm, perm.T, sp["bmat"], sp["cmat"], lam8, laml, sp["d"], sp["w_glu"], sp["b_glu"])


def _s5_step_kernel(u_ref, h0_ref, bmat_ref, cmat_ref, lam_ref, d_ref, wglu_ref, bglu_ref,
                    y_ref, h_ref):
    u = u_ref[...]
    ys = []
    for b in range(SSM_BLOCKS):
        st = slice(b * BLOCK_STATES, (b + 1) * BLOCK_STATES)
        bu = jnp.dot(u[:, b * LANES:(b + 1) * LANES], bmat_ref[b], preferred_element_type=F32)
        lr = lam_ref[0:1, st]
        li = lam_ref[1:2, st]
        re = h0_ref[0, :, st]
        im = h0_ref[1, :, st]
        n_re = lr * re - li * im + bu[:, :BLOCK_STATES]
        n_im = lr * im + li * re + bu[:, BLOCK_STATES:]
        h_ref[0, :, st] = n_re
        h_ref[1, :, st] = n_im
        ys.append(_mm(jnp.concatenate([n_re, n_im], axis=-1), cmat_ref[b]))
    y = _ssm_out(jnp.concatenate(ys, axis=-1), u.astype(F32), d_ref, wglu_ref, bglu_ref)
    y_ref[...] = y.astype(BF16)


def _s5_step(u, h0, sp):
    rows = u.shape[0]
    return pl.pallas_call(
        _s5_step_kernel,
        out_shape=[jax.ShapeDtypeStruct((rows, SSM_WIDTH), BF16),
                   jax.ShapeDtypeStruct((2, rows, N_STATE), F32)],
        compiler_params=pltpu.CompilerParams(vmem_limit_bytes=VMEM_LIMIT),
        name="s5_step",
    )(u, h0, sp["bmat"], sp["cmat"], sp["lam"], sp["d"], sp["w_glu"], sp["b_glu"])


def _decode_kernel(pt_ref, ql_ref, qr_ref, kvs_ref, krs_ref, lat_hbm, rope_hbm, o_ref,
                   lat_buf, rope_buf, sems, *, n_pages):
    b = pl.program_id(0)
    slot = b % 2

    def page_copies(seq, dst_slot):
        out = []
        for i in range(n_pages):
            page = pt_ref[seq * n_pages + i]
            keys = pl.ds(i * PAGE_SIZE, PAGE_SIZE)
            out.append(pltpu.make_async_copy(lat_hbm.at[page], lat_buf.at[dst_slot, keys, :],
                                             sems.at[dst_slot, 0]))
            out.append(pltpu.make_async_copy(rope_hbm.at[page], rope_buf.at[dst_slot, :, keys],
                                             sems.at[dst_slot, 1]))
        return out

    @pl.when(b == 0)
    def _():
        for cp in page_copies(0, 0):
            cp.start()

    @pl.when(b + 1 < pl.num_programs(0))
    def _():
        for cp in page_copies(b + 1, 1 - slot):
            cp.start()

    for cp in page_copies(b, slot):
        cp.wait()

    ql = ql_ref[0]
    qr = qr_ref[0]
    kvs = kvs_ref[0]
    krs = krs_ref[0]
    lat = lat_buf[slot].astype(BF16)
    s = _mm_nt(ql, lat) + _mm(qr, rope_buf[slot])
    s_self = (jnp.sum(ql.astype(F32) * kvs, axis=-1, keepdims=True)
              + jnp.sum(qr.astype(F32) * krs, axis=-1, keepdims=True))
    m = jnp.maximum(jnp.max(s, axis=-1, keepdims=True), s_self)
    p = jnp.exp2(s - m)
    p_self = jnp.exp2(s_self - m)
    denom = jnp.sum(p, axis=-1, keepdims=True) + p_self
    o_ref[0] = (_mm(p, lat) + p_self * kvs) / denom


def _decode_attn(q_lat, q_rope, kv_self, kr_self, cache_lat, cache_rope_t, page_table):
    b, n_pages = page_table.shape
    kern = functools.partial(_decode_kernel, n_pages=n_pages)
    seq3 = lambda s, pt: (s, 0, 0)
    keys = n_pages * PAGE_SIZE
    grid_spec = pltpu.PrefetchScalarGridSpec(
        num_scalar_prefetch=1,
        grid=(b,),
        in_specs=[pl.BlockSpec((1, N_HEADS, KV_RANK), seq3),
                  pl.BlockSpec((1, N_HEADS, ROPE_DIM), seq3),
                  pl.BlockSpec((1, 1, KV_RANK), seq3),
                  pl.BlockSpec((1, 1, ROPE_DIM), seq3),
                  pl.BlockSpec(memory_space=pl.ANY),
                  pl.BlockSpec(memory_space=pl.ANY)],
        out_specs=pl.BlockSpec((1, N_HEADS, KV_RANK), seq3),
        scratch_shapes=[pltpu.VMEM((2, keys, KV_RANK), F32),
                        pltpu.VMEM((2, ROPE_DIM, keys), F32),
                        pltpu.SemaphoreType.DMA((2, 2))],
    )
    return pl.pallas_call(
        kern,
        grid_spec=grid_spec,
        out_shape=jax.ShapeDtypeStruct((b, N_HEADS, KV_RANK), F32),
        compiler_params=_cparams(("arbitrary",)),
        name="decode_attn",
    )(page_table.reshape(-1), q_lat, q_rope, kv_self, kr_self, cache_lat, cache_rope_t)


def _uv_kernel(o_ref, w_ref, y_ref):
    y_ref[...] = _mm(o_ref[...], w_ref[...]).astype(BF16)


def _uv_proj(o_lat, wuv_bd):
    rows = o_lat.shape[0]
    return pl.pallas_call(
        _uv_kernel,
        out_shape=jax.ShapeDtypeStruct((rows, ATTN_WIDTH), BF16),
        compiler_params=pltpu.CompilerParams(vmem_limit_bytes=VMEM_LIMIT),
        name="uv_proj",
    )(o_lat, wuv_bd)


def _outproj_kernel(oa_ref, ys_ref, x_ref, g1_ref, sc_ref, sh_ref, wout_ref, gpost_ref, gpre_ref,
                    wr_ref, br_ref, *rest, row0):
    x1_ref, h2_ref, route_ref = rest[-3:]
    mix = (jnp.dot(oa_ref[0], wout_ref[:ATTN_WIDTH].astype(BF16), preferred_element_type=F32)
           + jnp.dot(ys_ref[0], wout_ref[ATTN_WIDTH:].astype(BF16), preferred_element_type=F32))
    x1 = x_ref[0] + g1_ref[0] * _rms(mix, gpost_ref[...], D_MODEL)
    x1_ref[0] = x1
    h2 = _rms(x1, gpre_ref[...], D_MODEL) * (1.0 + sc_ref[0]) + sh_ref[0]
    for s in range(D_MODEL // LANES):
        h2_ref[pl.ds(s, h2.shape[0], stride=SUBLANES), :] = h2[:, s * LANES:(s + 1) * LANES]
    logits = _mm3(h2, wr_ref[...]) + br_ref[...]
    lane = lax.broadcasted_iota(jnp.int32, logits.shape, 1).astype(F32)
    vals, idxs = [], []
    for _ in range(TOP_K):
        v = jnp.max(logits, axis=-1, keepdims=True)
        idx = jnp.min(jnp.where(logits == v, lane, float(LANES)), axis=-1, keepdims=True)
        vals.append(v)
        idxs.append(idx)
        logits = jnp.where(lane == idx, -jnp.inf, logits)
    exps = [jnp.exp(v - vals[0]) for v in vals]
    denom = exps[0] + exps[1] + exps[2] + exps[3]
    tm = logits.shape[0]
    first = 1 + row0 + (pl.program_id(0) * pl.num_programs(1) + pl.program_id(1)) * tm
    token = (first + lax.broadcasted_iota(jnp.int32, (tm, 1), 0)).astype(F32)
    route = jnp.where(lane == float(ROUTE_TOKEN), token, 0.0)
    for k in range(TOP_K):
        route = jnp.where(lane == float(k), idxs[k], route)
        route = jnp.where(lane == float(TOP_K + k), exps[k] / denom, route)
    route_ref[...] = route


def _outproj(o_attn, y_ssm, x, g1, sc2, sh2, p, tm, t_all, row0, dest=None):
    n, l, _ = x.shape
    per_row = g1.shape[1] != 1
    mod_blk = (1, tm, D_MODEL) if per_row else (1, 1, D_MODEL)
    mod_map = (lambda b, i: (b, i, 0)) if per_row else (lambda b, i: (b, 0, 0))
    row_map = lambda b, i: (b, i, 0)
    const2 = lambda b, i: (0, 0)
    blk0, per_seq = row0 // tm, l // tm
    n_tile = D_MODEL // LANES
    in_specs = [pl.BlockSpec((1, tm, ATTN_WIDTH), row_map),
                pl.BlockSpec((1, tm, SSM_WIDTH), row_map),
                pl.BlockSpec((1, tm, D_MODEL), row_map),
                pl.BlockSpec(mod_blk, mod_map),
                pl.BlockSpec(mod_blk, mod_map),
                pl.BlockSpec(mod_blk, mod_map),
                pl.BlockSpec((D_MODEL, D_MODEL), const2),
                pl.BlockSpec((1, D_MODEL), const2),
                pl.BlockSpec((1, D_MODEL), const2),
                pl.BlockSpec((D_MODEL, LANES), const2),
                pl.BlockSpec((1, LANES), const2)]
    args = [o_attn, y_ssm, x, g1, sc2, sh2, p["w_out"], p["g_post_mix"], p["g_pre_ffn"],
            p["w_router"], p["b_router"]]
    aliases = {}
    if dest is not None:
        aliases = {len(args): 1, len(args) + 1: 2}
        in_specs += [pl.BlockSpec(memory_space=pl.ANY), pl.BlockSpec(memory_space=pl.ANY)]
        args += list(dest)
    return pl.pallas_call(
        functools.partial(_outproj_kernel, row0=row0),
        grid=(n, per_seq),
        in_specs=in_specs,
        out_specs=[pl.BlockSpec((1, tm, D_MODEL), row_map),
                   pl.BlockSpec((tm * n_tile, LANES), lambda b, i: (blk0 + b * per_seq + i, 0)),
                   pl.BlockSpec((tm, LANES), lambda b, i: (blk0 + b * per_seq + i, 0))],
        out_shape=[jax.ShapeDtypeStruct((n, l, D_MODEL), F32),
                   jax.ShapeDtypeStruct((t_all * n_tile, LANES), F32),
                   jax.ShapeDtypeStruct((t_all, LANES), F32)],
        input_output_aliases=aliases,
        compiler_params=_cparams(("parallel", "parallel")),
        name="outproj",
    )(*args)


def _moe_prep_kernel(w1_ref, w2_ref, perm_ref, w1o_ref, w2o_ref):
    for j in range(w1_ref.shape[2] // GLU_BLOCK):
        cols = slice(j * GLU_BLOCK, (j + 1) * GLU_BLOCK)
        w1o_ref[0, :, cols] = jnp.dot(w1_ref[0, :, cols].astype(BF16), perm_ref[...],
                                      preferred_element_type=F32).astype(BF16)
    w2o_ref[0] = w2_ref[0].astype(BF16)


def _moe_prep(w1, w2):
    n_e, d_in, d_hid = w1.shape
    d_exp = w2.shape[1]
    j = jnp.arange(GLU_BLOCK)
    src = jnp.where(j < LANES, 2 * j, 2 * (j - LANES) + 1)
    perm = (j[:, None] == src[None, :]).astype(BF16)
    e_map = lambda e: (e, 0, 0)
    return pl.pallas_call(
        _moe_prep_kernel,
        grid=(n_e,),
        in_specs=[pl.BlockSpec((1, d_in, d_hid), e_map),
                  pl.BlockSpec((1, d_exp, D_MODEL), e_map),
                  pl.BlockSpec((GLU_BLOCK, GLU_BLOCK), lambda e: (0, 0))],
        out_specs=[pl.BlockSpec((1, d_in, d_hid), e_map),
                   pl.BlockSpec((1, d_exp, D_MODEL), e_map)],
        out_shape=[jax.ShapeDtypeStruct(w1.shape, BF16),
                   jax.ShapeDtypeStruct(w2.shape, BF16)],
        compiler_params=_cparams(("parallel",)),
        name="moe_prep",
    )(w1, w2, perm)


def _sort_rows_kernel(pos_ref, route_ref, rows_ref):
    rows_ref[...] = jnp.zeros(rows_ref.shape, F32)

    def body(t, carry):
        row = route_ref[0, pl.ds(t, 1), :]
        for k in range(TOP_K):
            rows_ref[0, pl.ds(pos_ref[0, 0, TOP_K * t + k], 1), :] = row
        return carry
    lax.fori_loop(0, route_ref.shape[1], body, 0, unroll=4)


def _sort_rows(route, pos, nch, rc):
    t = route.shape[0]
    tc = t // nch
    return pl.pallas_call(
        _sort_rows_kernel,
        grid=(nch,),
        in_specs=[pl.BlockSpec((1, 1, tc * TOP_K), lambda c: (c, 0, 0), memory_space=pltpu.SMEM),
                  pl.BlockSpec((1, tc, LANES), lambda c: (c, 0, 0))],
        out_specs=pl.BlockSpec((1, rc, LANES), lambda c: (c, 0, 0)),
        out_shape=jax.ShapeDtypeStruct((nch, rc, LANES), F32),
        compiler_params=_cparams(("parallel",)),
        name="moe_sort_rows",
    )(pos.reshape(nch, 1, tc * TOP_K), route.reshape(nch, tc, LANES))


def _positions_kernel(route_ref, off_ref, tri_ref, pos_ref, cnt_ref):
    @pl.when(pl.program_id(1) == 0)
    def _():
        cnt_ref[...] = jnp.zeros(cnt_ref.shape, F32)

    r = route_ref[0]
    lane = lax.broadcasted_iota(jnp.int32, r.shape, 1).astype(F32)
    picks = [lane == r[:, k:k + 1] for k in range(TOP_K)]
    hot = jnp.zeros(r.shape, F32)
    for pick in picks:
        hot = hot + jnp.where(pick, 1.0, 0.0)
    rank = jnp.dot(tri_ref[...], hot.astype(BF16), preferred_element_type=F32) + cnt_ref[...]
    cnt_ref[...] = cnt_ref[...] + jnp.sum(hot, axis=0, keepdims=True)
    base = off_ref[0] + rank
    pos = jnp.zeros(r.shape, F32)
    for k, pick in enumerate(picks):
        pos = jnp.where(lane == float(k), jnp.sum(jnp.where(pick, base, 0.0), axis=-1, keepdims=True), pos)
    pos_ref[0] = pos


def _positions(route, off, nch):
    t = route.shape[0]
    tc = t // nch
    tp = _pick_tile(tc, 1024, SUBLANES)
    assert tc % tp == 0
    tri = (jnp.arange(tp)[:, None] > jnp.arange(tp)[None, :]).astype(BF16)
    pos = pl.pallas_call(
        _positions_kernel,
        grid=(nch, tc // tp),
        in_specs=[pl.BlockSpec((1, tp, LANES), lambda c, j: (c, j, 0)),
                  pl.BlockSpec((1, 1, LANES), lambda c, j: (c, 0, 0)),
                  pl.BlockSpec((tp, tp), lambda c, j: (0, 0))],
        out_specs=pl.BlockSpec((1, tp, LANES), lambda c, j: (c, j, 0)),
        out_shape=jax.ShapeDtypeStruct((nch, tc, LANES), F32),
        scratch_shapes=[pltpu.VMEM((1, LANES), F32)],
        compiler_params=_cparams(("parallel", "arbitrary")),
        name="moe_positions",
    )(route.reshape(nch, tc, LANES), off, tri)
    return pos[:, :, :TOP_K].astype(jnp.int32).reshape(t, TOP_K)


def _route_plan(route, nch, tmx):
    idx = route[:, :TOP_K].astype(jnp.int32)
    t = idx.shape[0]
    tc = t // nch
    nt = -(-(tc * TOP_K + N_EXPERTS * (tmx - 1)) // tmx)
    i32 = jnp.int32
    hot = (idx[:, :, None] == jnp.arange(N_EXPERTS, dtype=i32)).astype(i32).sum(1)
    counts = hot.reshape(nch, tc, N_EXPERTS).sum(1)
    padded = -(-counts // tmx) * tmx
    off_end = jnp.cumsum(padded, axis=1)
    off = off_end - padded
    n_tiles = (off_end[:, -1] // tmx).astype(i32)
    off_lanes = jnp.pad(off.astype(F32), ((0, 0), (0, LANES - N_EXPERTS)))[:, None, :]
    pos = _positions(route, off_lanes, nch)
    rc = nt * tmx
    rows = _sort_rows(route, pos, nch, rc)
    token = lax.optimization_barrier(rows[:, :, ROUTE_TOKEN].astype(i32) - 1)
    row_tok = jnp.maximum(token, 0) % tc
    row_dst = jnp.where(token >= 0, row_tok, tc + jnp.arange(rc, dtype=i32)[None, :] % SUBLANES)
    tile = jnp.arange(nt + 1, dtype=i32)
    texp_raw = (off_end[:, None, :] <= (tile * tmx)[None, :, None]).sum(-1).astype(i32)
    last = n_tiles[:, None] - 1
    tile_c = jnp.minimum(tile[None, :], last)
    tile_n = jnp.minimum(tile[None, :] + 1, last)
    tile_p = jnp.clip(tile[None, :] - 1, 0, last)
    texp = jnp.take_along_axis(texp_raw, tile_c, axis=1)
    base = (jnp.arange(nch, dtype=i32) * nt)[:, None]
    return dict(nt=nt, n_tiles=n_tiles, texp=texp.reshape(-1), tblk=(tile_c + base).reshape(-1),
                tnext=(tile_n + base).reshape(-1), tprev=(tile_p + base).reshape(-1),
                row_tok=row_tok.reshape(nch * nt, 1, tmx), row_dst=row_dst.reshape(nch * nt, 1, tmx),
                rows=rows.reshape(nch * nt, tmx, LANES))


def _moe_kernel(nt_ref, texp_ref, tblk_ref, tnext_ref, tprev_ref, tok_ref, toknext_ref, dstprev_ref,
                rows_ref, src_ref, w1_ref, b1_ref, w2_ref, b2_ref, out_ref, xs_ref, yt_ref):
    del tblk_ref, tnext_ref, tprev_ref
    c = pl.program_id(0)
    i = pl.program_id(1)
    tmx = rows_ref.shape[1]
    slot = i % 2
    n_grp = tmx // SUBLANES

    def add_prev_rows(groups):
        for g in groups:
            toks = [dstprev_ref[0, 0, g * SUBLANES + u] for u in range(SUBLANES)]
            sums = [out_ref[0, toks[u]]
                    + yt_ref[1 - slot, (g * SUBLANES + u) * SUBLANES:(g * SUBLANES + u + 1) * SUBLANES, :]
                    for u in range(SUBLANES)]
            for u in range(SUBLANES):
                out_ref[0, toks[u]] = sums[u]

    @pl.when(i == 0)
    def _():
        out_ref[...] = jnp.zeros(out_ref.shape, F32)
        yt_ref[1] = jnp.zeros(yt_ref.shape[1:], F32)

        def gather(r, carry):
            xs_ref[0, pl.ds(pl.multiple_of(r * SUBLANES, SUBLANES), SUBLANES), :] = src_ref[0, tok_ref[0, 0, r]]
            return carry
        lax.fori_loop(0, tmx, gather, 0, unroll=8)

    @pl.when(i < nt_ref[c])
    def _():
        x = jnp.concatenate([xs_ref[slot, pl.ds(s, tmx, stride=SUBLANES), :]
                             for s in range(D_MODEL // LANES)], axis=-1).astype(BF16)
        n_blk = w1_ref.shape[2] // (2 * GLU_BLOCK)
        wide = 2 * GLU_BLOCK

        def hidden(j):
            cols = slice(j * wide, (j + 1) * wide)
            return jnp.dot(x, w1_ref[0, :, cols], preferred_element_type=F32) + b1_ref[0, :, cols]

        hid = hidden(0)
        y = None
        for j in range(n_blk):
            nxt = hidden(j + 1) if j + 1 < n_blk else None
            for r in range(j * tmx // n_blk, (j + 1) * tmx // n_blk):
                xs_ref[1 - slot, r * SUBLANES:(r + 1) * SUBLANES, :] = src_ref[0, toknext_ref[0, 0, r]]
            add_prev_rows(range(j * n_grp // n_blk, (j + 1) * n_grp // n_blk))
            acts = []
            for b in range(2):
                glu = jnp.minimum(hid[:, b * GLU_BLOCK:b * GLU_BLOCK + LANES], SWIGLU_LIMIT)
                lin = jnp.clip(hid[:, b * GLU_BLOCK + LANES:(b + 1) * GLU_BLOCK], -SWIGLU_LIMIT, SWIGLU_LIMIT)
                acts.append(glu * _sigmoid(SWIGLU_ALPHA * glu) * (lin + 1.0))
            part = _mm(jnp.concatenate(acts, axis=-1), w2_ref[0, j * GLU_BLOCK:(j + 1) * GLU_BLOCK, :])
            y = part if y is None else y + part
            hid = nxt
        meta = rows_ref[0]
        expert = texp_ref[c * pl.num_programs(1) + i].astype(F32)
        lane = lax.broadcasted_iota(jnp.int32, meta.shape, 1)
        hit = jnp.logical_and(lane < TOP_K, meta == expert)
        gates = pltpu.roll(meta, LANES - TOP_K, axis=1)
        gate = jnp.sum(jnp.where(hit, gates, 0.0), axis=-1, keepdims=True)
        y = gate * (y + b2_ref[0])
        for s in range(D_MODEL // LANES):
            yt_ref[slot, pl.ds(s, tmx, stride=SUBLANES), :] = y[:, s * LANES:(s + 1) * LANES]

    @pl.when(i == nt_ref[c])
    def _():
        add_prev_rows(range(n_grp))


def _moe_experts(h2, plan, w1p, b1p, w2p, b2, tmx):
    nch, tc = h2.shape[:2]
    nt = plan["nt"]
    n_e, _, d_hid = w1p.shape
    d_exp = w2p.shape[1]
    steps = nt + 1
    flat = lambda c, i: c * steps + i
    cur = lambda c, i, n, te, tb, tn, tp: (tb[flat(c, i)], 0, 0)
    nxt = lambda c, i, n, te, tb, tn, tp: (tn[flat(c, i)], 0, 0)
    prv = lambda c, i, n, te, tb, tn, tp: (tp[flat(c, i)], 0, 0)
    exp = lambda c, i, n, te, tb, tn, tp: (te[flat(c, i)], 0, 0)
    grid_spec = pltpu.PrefetchScalarGridSpec(
        num_scalar_prefetch=5,
        grid=(nch, steps),
        in_specs=[pl.BlockSpec((1, 1, tmx), cur, memory_space=pltpu.SMEM),
                  pl.BlockSpec((1, 1, tmx), nxt, memory_space=pltpu.SMEM),
                  pl.BlockSpec((1, 1, tmx), prv, memory_space=pltpu.SMEM),
                  pl.BlockSpec((1, tmx, LANES), cur),
                  pl.BlockSpec((1, tc, D_MODEL // LANES, LANES), lambda c, i, n, te, tb, tn, tp: (c, 0, 0, 0),
                               pipeline_mode=pl.Buffered(1)),
                  pl.BlockSpec((1, D_MODEL, d_hid), exp),
                  pl.BlockSpec((1, 1, d_hid), exp),
                  pl.BlockSpec((1, d_exp, D_MODEL), exp),
                  pl.BlockSpec((1, 1, D_MODEL), exp)],
        out_specs=pl.BlockSpec((1, tc + SUBLANES, D_MODEL // LANES, LANES),
                               lambda c, i, n, te, tb, tn, tp: (c, 0, 0, 0), pipeline_mode=pl.Buffered(1)),
        scratch_shapes=[pltpu.VMEM((2, tmx * (D_MODEL // LANES), LANES), F32),
                        pltpu.VMEM((2, tmx * (D_MODEL // LANES), LANES), F32)],
    )
    return pl.pallas_call(
        _moe_kernel,
        grid_spec=grid_spec,
        out_shape=jax.ShapeDtypeStruct((nch, tc + SUBLANES, D_MODEL // LANES, LANES), F32),
        compiler_params=_cparams(("parallel", "arbitrary")),
        name="moe_experts",
    )(plan["n_tiles"], plan["texp"], plan["tblk"], plan["tnext"], plan["tprev"], plan["row_tok"],
      plan["row_tok"], plan["row_dst"], plan["rows"], h2, w1p, b1p, w2p, b2)


def _final_kernel(x1_ref, f_ref, g2_ref, gpost_ref, y_ref):
    tm = x1_ref.shape[1]
    f = jnp.concatenate([f_ref[pl.ds(s, tm, stride=SUBLANES), :] for s in range(D_MODEL // LANES)],
                        axis=-1)
    y_ref[0] = x1_ref[0] + g2_ref[0] * _rms(f, gpost_ref[...], D_MODEL)


def _final(x1, f_all, row0, g2, g_post_ffn, tm):
    n, l, _ = x1.shape
    per_row = g2.shape[1] != 1
    mod_blk = (1, tm, D_MODEL) if per_row else (1, 1, D_MODEL)
    mod_map = (lambda b, i: (b, i, 0)) if per_row else (lambda b, i: (b, 0, 0))
    row_map = lambda b, i: (b, i, 0)
    blk0, per_seq = row0 // tm, l // tm
    return pl.pallas_call(
        _final_kernel,
        grid=(n, l // tm),
        in_specs=[pl.BlockSpec((1, tm, D_MODEL), row_map),
                  pl.BlockSpec((tm * (D_MODEL // LANES), LANES), lambda b, i: (blk0 + b * per_seq + i, 0)),
                  pl.BlockSpec(mod_blk, mod_map),
                  pl.BlockSpec((1, D_MODEL), lambda b, i: (0, 0))],
        out_specs=pl.BlockSpec((1, tm, D_MODEL), row_map),
        out_shape=jax.ShapeDtypeStruct((n, l, D_MODEL), F32),
        compiler_params=_cparams(("parallel", "parallel")),
        name="final",
    )(x1, f_all, g2, g_post_ffn)


def _complex_pow2(lam, n):
    re, im = lam[0], lam[1]
    steps = int(round(math.log2(n)))
    assert 2 ** steps == n
    for _ in range(steps):
        re, im = re * re - im * im, 2.0 * re * im
    return jnp.stack([re, im])


def _rope_tables(pos):
    half = ROPE_DIM // 2
    inv = ROPE_THETA ** (-jnp.arange(half, dtype=F32) / half)
    ang = pos.astype(F32)[:, None] * inv[None, :]
    cos, sin = jnp.cos(ang), jnp.sin(ang)
    cos4 = jnp.tile(jnp.concatenate([cos, cos], axis=-1), (1, LANES // ROPE_DIM))
    sin4 = jnp.tile(jnp.concatenate([-sin, sin], axis=-1), (1, LANES // ROPE_DIM))
    return cos4[None], sin4[None]


def _swap_halves(w):
    half = ROPE_DIM // 2
    return jnp.concatenate([w[..., half:], w[..., :half]], axis=-1)


def _prep_layer(w, layer):
    g = lambda name: w[name][layer]
    p = {}
    row = lambda v: v.reshape(1, -1)
    p["g_pre_mix"] = row(g("g_pre_mix"))
    w_in = g("w_in")
    w_q, w_kv, w_r, w_u = jnp.split(w_in, [Q_RANK, Q_RANK + KV_RANK, Q_RANK + KV_RANK + ROPE_DIM], axis=-1)
    rep = LANES // ROPE_DIM
    p["w_in"] = jnp.concatenate(
        [w_q, jnp.zeros((D_MODEL, Q_PAD - Q_RANK), F32), w_kv,
         jnp.tile(w_r, (1, rep)), jnp.tile(_swap_halves(w_r), (1, rep)), w_u], axis=-1)
    p["g_q"] = row(jnp.pad(g("g_q"), (0, Q_PAD - Q_RANK)))
    w_uq = g("w_uq").reshape(Q_RANK, N_HEADS, NOPE_DIM + ROPE_DIM)
    nope = w_uq[:, :, :NOPE_DIM].reshape(Q_RANK, N_HEADS * NOPE_DIM)
    rope = w_uq[:, :, NOPE_DIM:]
    rope_sw = _swap_halves(rope)
    w_uq_p = jnp.concatenate([nope, rope.reshape(Q_RANK, -1), rope_sw.reshape(Q_RANK, -1)], axis=-1)
    p["w_uq"] = jnp.pad(w_uq_p, ((0, Q_PAD - Q_RANK), (0, 0)))
    w_uk = jnp.transpose(g("w_uk"), (1, 2, 0))
    z = jnp.zeros((NOPE_DIM, KV_RANK), F32)
    p["w_uk"] = jnp.stack([jnp.block([[w_uk[2 * j], z], [z, w_uk[2 * j + 1]]])
                           for j in range(N_HEADS // 2)])
    p["g_kv"] = row(g("g_kv"))
    w_uv = jnp.transpose(g("w_uv"), (1, 0, 2))
    p["w_uv_t"] = jnp.transpose(w_uv, (0, 2, 1))
    eye_h = jnp.eye(N_HEADS, dtype=F32)
    p["w_uv_bd"] = jnp.einsum("hrv,hk->hrkv", w_uv, eye_h).reshape(N_HEADS * KV_RANK, ATTN_WIDTH)
    p["w_out"] = g("w_out")
    for name in ("w_in", "w_uq", "w_uk", "w_uv_t", "w_uv_bd", "w_out"):
        p[name] = p[name].astype(BF16)
    p["g_post_mix"] = row(g("g_post_mix"))
    p["g_pre_ffn"] = row(g("g_pre_ffn"))
    p["w_router"] = jnp.pad(g("w_router"), ((0, 0), (0, LANES - N_EXPERTS)))
    p["b_router"] = row(jnp.pad(g("b_router"), (0, LANES - N_EXPERTS), constant_values=NEG_INF))
    p["g_post_ffn"] = row(g("g_post_ffn"))

    lam_re, lam_im = g("ssm_lam_re"), g("ssm_lam_im")
    step = jnp.exp(g("ssm_log_step"))[:, None]
    mag = jnp.exp(lam_re * step)
    bar_re, bar_im = mag * jnp.cos(lam_im * step), mag * jnp.sin(lam_im * step)
    den = lam_re * lam_re + lam_im * lam_im
    co_re = ((bar_re - 1.0) * lam_re + bar_im * lam_im) / den
    co_im = (bar_im * lam_re - (bar_re - 1.0) * lam_im) / den
    b_re, b_im = g("ssm_b_re"), g("ssm_b_im")
    bb_re = co_re[..., None] * b_re - co_im[..., None] * b_im
    bb_im = co_re[..., None] * b_im + co_im[..., None] * b_re
    eye_g = jnp.eye(GROUPS_PER_BLOCK, dtype=F32)

    def b_block(m):
        m = m.reshape(SSM_BLOCKS, GROUPS_PER_BLOCK, SSM_STATE, SSM_GROUP)
        return jnp.einsum("bgpc,gh->bgchp", m, eye_g).reshape(SSM_BLOCKS, LANES, BLOCK_STATES)

    def c_block(m):
        m = m.reshape(SSM_BLOCKS, GROUPS_PER_BLOCK, SSM_GROUP, SSM_STATE)
        return jnp.einsum("bgcp,gh->bgphc", m, eye_g).reshape(SSM_BLOCKS, BLOCK_STATES, LANES)

    sp = {
        "lam": jnp.stack([bar_re.reshape(-1), bar_im.reshape(-1)]),
        "bmat": jnp.concatenate([b_block(bb_re), b_block(bb_im)], axis=-1).astype(BF16),
        "cmat": jnp.concatenate([c_block(g("ssm_c_re")), c_block(-g("ssm_c_im"))], axis=1).astype(BF16),
        "d": row(g("ssm_d")),
        "w_glu": g("ssm_w_glu").astype(BF16),
        "b_glu": row(g("ssm_b_glu")),
    }
    b1 = g("b1")
    n_blk = b1.shape[1] // GLU_BLOCK
    mp = {
        "b1": jnp.transpose(b1.reshape(N_EXPERTS, n_blk, LANES, 2), (0, 1, 3, 2)).reshape(N_EXPERTS, 1, -1),
        "b2": g("b2")[:, None, :],
    }
    return p, sp, mp


def _mods(mod_rows):
    return [mod_rows[:, k * D_MODEL:(k + 1) * D_MODEL] for k in range(N_MOD)]


def _pick_tile(n, target, mult):
    best = mult
    for c in range(mult, min(n, target) + 1, mult):
        if n % c == 0:
            best = c
    return best


def kernel(x_prompt, x_sample, c_prompt, c_sample, cache_kv_latent, cache_k_rope, state_ssm, page_table, w_ada, b_ada, g_pre_mix, w_in, g_q, w_uq, w_uk, g_kv, w_uv, ssm_lam_re, ssm_lam_im, ssm_log_step, ssm_b_re, ssm_b_im, ssm_c_re, ssm_c_im, ssm_d, ssm_w_glu, ssm_b_glu, w_out, g_post_mix, g_pre_ffn, w_router, b_router, w1, b1, w2, b2, g_post_ffn):
    weights = dict(w_ada=w_ada, b_ada=b_ada, g_pre_mix=g_pre_mix, w_in=w_in, g_q=g_q, w_uq=w_uq,
                   w_uk=w_uk, g_kv=g_kv, w_uv=w_uv, ssm_lam_re=ssm_lam_re, ssm_lam_im=ssm_lam_im,
                   ssm_log_step=ssm_log_step, ssm_b_re=ssm_b_re, ssm_b_im=ssm_b_im,
                   ssm_c_re=ssm_c_re, ssm_c_im=ssm_c_im, ssm_d=ssm_d, ssm_w_glu=ssm_w_glu,
                   ssm_b_glu=ssm_b_glu, w_out=w_out, g_post_mix=g_post_mix, g_pre_ffn=g_pre_ffn,
                   w_router=w_router, b_router=b_router, w1=w1, b1=b1, w2=w2, b2=b2,
                   g_post_ffn=g_post_ffn)
    depth = w_ada.shape[0]
    n_p, l_p, _ = x_prompt.shape
    n_s, l_s, _ = x_sample.shape
    assert l_s == 1
    n_pages = page_table.shape[1]
    past_len = n_pages * PAGE_SIZE

    cos_p, sin_p = _rope_tables(jnp.arange(l_p, dtype=jnp.int32))
    cos_s, sin_s = _rope_tables(jnp.full((1,), past_len, jnp.int32))
    cos_s = jnp.broadcast_to(cos_s, (1, n_s, LANES))
    sin_s = jnp.broadcast_to(sin_s, (1, n_s, LANES))

    tm_p = _pick_tile(l_p, 512, 16)
    tq = _pick_tile(l_p, 512, 16)
    tk = _pick_tile(l_p, 1024, 16)
    lc = 64 if l_p % 512 == 0 else l_p // SUBLANES
    c_rows = n_p + n_s
    c_pad = -c_rows % SUBLANES
    c_all = jnp.pad(jnp.concatenate([c_prompt, c_sample], axis=0), ((0, c_pad), (0, 0)))

    y_p = x_prompt
    y_s = x_sample.reshape(1, n_s, D_MODEL)
    outs = [[] for _ in range(6)]
    for layer in range(depth):
        p, sp, mp = _prep_layer(weights, layer)
        mod = _ada(c_all, w_ada[layer], b_ada[layer].reshape(1, -1))
        sh1p, sc1p, g1p, sh2p, sc2p, g2p = [m[:, None, :] for m in _mods(mod[:n_p])]
        sh1s, sc1s, g1s, sh2s, sc2s, g2s = [m[None] for m in _mods(mod[n_p:c_rows])]

        q_t, k_cat, v_t, kv_p, kr_p, u_p = _inproj(y_p, sc1p, sh1p, p, cos_p, sin_p, tm_p)
        o_attn_p = _flash(q_t, k_cat, v_t, p["w_uv_t"], tq, tk)
        y_ssm_p, h_last_p = _s5_prompt(u_p, sp, lc)
        t_p = n_p * l_p
        t_all = t_p + n_s
        blank = (jnp.zeros((t_all * (D_MODEL // LANES), LANES), F32), jnp.zeros((t_all, LANES), F32))
        x1_p, h2_all, route = _outproj(o_attn_p, y_ssm_p, y_p, g1p, sc2p, sh2p, p, tm_p, t_all, 0,
                                       dest=blank)

        q_t_s, _, _, kv_s, kr_s, u_s = _inproj(y_s, sc1s, sh1s, p, cos_s, sin_s, n_s)
        q_lat_s = jnp.transpose(q_t_s[0, :, :KV_RANK, :], (2, 0, 1))
        q_rope_s = jnp.stack(
            [q_t_s[0, h, KV_RANK + ROPE_DIM * (h % 4):KV_RANK + ROPE_DIM * (h % 4 + 1), :].T
             for h in range(N_HEADS)], axis=1)
        o_lat_s = _decode_attn(q_lat_s, q_rope_s, kv_s.reshape(n_s, 1, KV_RANK),
                               kr_s.reshape(n_s, 1, ROPE_DIM), cache_kv_latent[layer],
                               jnp.swapaxes(cache_k_rope[layer], 1, 2), page_table)
        o_attn_s = _uv_proj(o_lat_s.reshape(n_s, N_HEADS * KV_RANK), p["w_uv_bd"])
        h0 = jnp.moveaxis(state_ssm[layer].reshape(n_s, N_STATE, 2), -1, 0)
        y_ssm_s, h_s = _s5_step(u_s[0], h0, sp)
        x1_s, h2_all, route = _outproj(o_attn_s[None], y_ssm_s[None], y_s, g1s, sc2s, sh2s, p, n_s,
                                       t_all, t_p, dest=(h2_all, route))

        nch = 4 if t_all % (4 * SUBLANES) == 0 else (2 if t_all % (2 * SUBLANES) == 0 else 1)
        tc = t_all // nch
        plan = _route_plan(route, nch, MOE_TILE)
        w1p, w2p = _moe_prep(w1[layer], w2[layer])
        f_chunks = _moe_experts(h2_all.reshape(nch, tc, D_MODEL // LANES, LANES), plan, w1p, mp["b1"],
                                w2p, mp["b2"], MOE_TILE)
        f_all = f_chunks[:, :tc].reshape(t_all * (D_MODEL // LANES), LANES)
        y_p = _final(x1_p, f_all, 0, g2p, p["g_post_ffn"], tm_p)
        y_s = _final(x1_s, f_all, t_p, g2s, p["g_post_ffn"], n_s)

        outs[0].append(kv_p)
        outs[1].append(kr_p)
        outs[2].append(jnp.moveaxis(h_last_p, 1, -1).reshape(n_p, N_SSM_GROUPS, SSM_STATE, 2))
        outs[3].append(kv_s.reshape(n_s, 1, KV_RANK))
        outs[4].append(kr_s.reshape(n_s, 1, ROPE_DIM))
        outs[5].append(jnp.moveaxis(h_s, 0, -1).reshape(n_s, N_SSM_GROUPS, SSM_STATE, 2))
    return (y_p, y_s.reshape(n_s, 1, D_MODEL)) + tuple(jnp.stack(o) for o in outs)
```

```python
import functools
import math

import jax
import jax.numpy as jnp
from jax import lax
from jax.experimental import pallas as pl
from jax.experimental.pallas import tpu as pltpu

F32 = jnp.float32
BF16 = jnp.bfloat16

D_MODEL = 1024
N_HEADS = 8
NOPE_DIM = 64
ROPE_DIM = 32
V_DIM = 64
Q_RANK = 352
KV_RANK = 128
ROPE_THETA = 10000.0
ATTN_WIDTH = N_HEADS * V_DIM
ATTN_SCALE = 1.0 / math.sqrt(NOPE_DIM + ROPE_DIM)
SSM_WIDTH = D_MODEL - ATTN_WIDTH
SSM_GROUP = 16
N_SSM_GROUPS = SSM_WIDTH // SSM_GROUP
SSM_STATE = 64
N_STATE = N_SSM_GROUPS * SSM_STATE
N_EXPERTS = 32
TOP_K = 4
SWIGLU_LIMIT = 7.0
SWIGLU_ALPHA = 1.702
RMS_EPS = 1e-6
N_MOD = 6
NEG_INF = -1e30
PAGE_SIZE = 128

LANES = 128
SUBLANES = 8
VMEM_LIMIT = 56 * 1024 * 1024

Q_PAD = 384
CAT = 2 * LANES
SSM_BLOCKS = 4
GROUPS_PER_BLOCK = N_SSM_GROUPS // SSM_BLOCKS
BLOCK_STATES = GROUPS_PER_BLOCK * SSM_STATE
IN_PAD = Q_PAD + KV_RANK + 2 * LANES + SSM_WIDTH
UQ_PAD = N_HEADS * NOPE_DIM + 4 * LANES
GLU_BLOCK = 2 * LANES
MOE_TILE = 192
LOG2E = math.log2(math.e)
FLASH_LOOKAHEAD = 6
FLASH_DIAG_SUB = 256
VT_ROWS = KV_RANK + 16
ROUTE_TOKEN = 2 * TOP_K


def _cparams(sem):
    return pltpu.CompilerParams(dimension_semantics=sem, vmem_limit_bytes=VMEM_LIMIT)


def _mm(a, b):
    return jnp.dot(a.astype(BF16), b.astype(BF16), preferred_element_type=F32)


def _mm_nt(a, b):
    return lax.dot_general(a.astype(BF16), b.astype(BF16), (((1,), (1,)), ((), ())),
                           preferred_element_type=F32)


def _split_bf16(x):
    hi = x.astype(BF16)
    lo = (x - hi.astype(F32)).astype(BF16)
    return hi, lo


def _mm3(a, b):
    ah, al = _split_bf16(a)
    bh, bl = _split_bf16(b)
    d = functools.partial(jnp.dot, preferred_element_type=F32)
    return d(ah, bh) + (d(ah, bl) + d(al, bh))


def _sigmoid(x):
    return 1.0 / (1.0 + jnp.exp(-x))


def _rms(x, g, width):
    ms = jnp.sum(x * x, axis=-1, keepdims=True) * (1.0 / width)
    return x * lax.rsqrt(ms + RMS_EPS) * g


def _gelu_tanh(x):
    return 0.5 * x * (1.0 + jnp.tanh(math.sqrt(2.0 / math.pi) * (x + 0.044715 * (x * x * x))))


def _ada_kernel(c_ref, w_ref, b_ref, o_ref):
    c = c_ref[...]
    o_ref[...] = _mm3(c * _sigmoid(c), w_ref[...]) + b_ref[...]


def _ada(c_all, w_ada, b_ada):
    rows = c_all.shape[0]
    tn = 512
    n_out = w_ada.shape[1]
    return pl.pallas_call(
        _ada_kernel,
        grid=(n_out // tn,),
        in_specs=[pl.BlockSpec((rows, D_MODEL), lambda j: (0, 0)),
                  pl.BlockSpec((D_MODEL, tn), lambda j: (0, j)),
                  pl.BlockSpec((1, tn), lambda j: (0, j))],
        out_specs=pl.BlockSpec((rows, tn), lambda j: (0, j)),
        out_shape=jax.ShapeDtypeStruct((rows, n_out), F32),
        compiler_params=_cparams(("parallel",)),
        name="adaln",
    )(c_all, w_ada, b_ada)


def _inproj_kernel(x_ref, sc_ref, sh_ref, gpre_ref, win_ref, gq_ref, wuq_ref, wuk_ref, gkv_ref,
                   cos_ref, sin_ref, q_ref, kcat_ref, vt_ref, kv_ref, kr_ref, u_ref):
    x = x_ref[0]
    h = _rms(x, gpre_ref[...], D_MODEL) * (1.0 + sc_ref[0]) + sh_ref[0]
    z = _mm(h, win_ref[...])
    zq = z[:, :Q_PAD]
    zkv = z[:, Q_PAD:Q_PAD + KV_RANK]
    zr = z[:, Q_PAD + KV_RANK:Q_PAD + KV_RANK + LANES]
    zrs = z[:, Q_PAD + KV_RANK + LANES:Q_PAD + KV_RANK + 2 * LANES]
    u = z[:, Q_PAD + KV_RANK + 2 * LANES:]
    cos4 = cos_ref[0]
    sin4 = sin_ref[0]

    kvl = _rms(zkv, gkv_ref[...], KV_RANK)
    kr4 = zr * cos4 + zrs * sin4
    kv_ref[0] = kvl
    kr_ref[0] = kr4[:, :ROPE_DIM]
    kcat_ref[0] = jnp.concatenate([kvl, kr4], axis=-1).astype(BF16)
    extra = (lax.broadcasted_iota(jnp.int32, (VT_ROWS - KV_RANK, kvl.shape[0]), 0) == 0).astype(F32)
    vt_ref[0] = jnp.concatenate([kvl.T, extra], axis=0).astype(BF16)
    u_ref[0] = u.astype(BF16)

    qn = _rms(zq, gq_ref[...], Q_RANK)
    q = _mm(qn, wuq_ref[...])
    nope_w = N_HEADS * NOPE_DIM
    rope_a = q[:, nope_w:nope_w + LANES] * cos4 + q[:, nope_w + 2 * LANES:nope_w + 3 * LANES] * sin4
    rope_b = q[:, nope_w + LANES:nope_w + 2 * LANES] * cos4 + q[:, nope_w + 3 * LANES:] * sin4
    slot = lax.broadcasted_iota(jnp.int32, (1, LANES), 1) // ROPE_DIM
    for j in range(N_HEADS // 2):
        ql = _mm(q[:, j * LANES:(j + 1) * LANES], wuk_ref[j])
        for hh in range(2):
            head = 2 * j + hh
            rope = rope_a if head < 4 else rope_b
            rope_h = jnp.where(slot == (head % 4), rope, 0.0)
            qc = jnp.concatenate([ql[:, hh * KV_RANK:(hh + 1) * KV_RANK], rope_h], axis=-1)
            q_ref[0, head] = (qc * (ATTN_SCALE * LOG2E)).T.astype(BF16)


def _inproj(x, sc, sh, p, cos4, sin4, tm):
    n, l, _ = x.shape
    per_row = sc.shape[1] != 1
    mod_blk = (1, tm, D_MODEL) if per_row else (1, 1, D_MODEL)
    mod_map = (lambda b, i: (b, i, 0)) if per_row else (lambda b, i: (b, 0, 0))
    const2 = lambda b, i: (0, 0)
    const3 = lambda b, i: (0, 0, 0)
    pos_map = lambda b, i: (0, i, 0)
    return pl.pallas_call(
        _inproj_kernel,
        grid=(n, l // tm),
        in_specs=[pl.BlockSpec((1, tm, D_MODEL), lambda b, i: (b, i, 0)),
                  pl.BlockSpec(mod_blk, mod_map),
                  pl.BlockSpec(mod_blk, mod_map),
                  pl.BlockSpec((1, D_MODEL), const2),
                  pl.BlockSpec((D_MODEL, IN_PAD), const2),
                  pl.BlockSpec((1, Q_PAD), const2),
                  pl.BlockSpec((Q_PAD, UQ_PAD), const2),
                  pl.BlockSpec((N_HEADS // 2, LANES, 2 * KV_RANK), const3),
                  pl.BlockSpec((1, KV_RANK), const2),
                  pl.BlockSpec((1, tm, LANES), pos_map),
                  pl.BlockSpec((1, tm, LANES), pos_map)],
        out_specs=[pl.BlockSpec((1, N_HEADS, CAT, tm), lambda b, i: (b, 0, 0, i)),
                   pl.BlockSpec((1, tm, CAT), lambda b, i: (b, i, 0)),
                   pl.BlockSpec((1, VT_ROWS, tm), lambda b, i: (b, 0, i)),
                   pl.BlockSpec((1, tm, KV_RANK), lambda b, i: (b, i, 0)),
                   pl.BlockSpec((1, tm, ROPE_DIM), lambda b, i: (b, i, 0)),
                   pl.BlockSpec((1, tm, SSM_WIDTH), lambda b, i: (b, i, 0))],
        out_shape=[jax.ShapeDtypeStruct((n, N_HEADS, CAT, l), BF16),
                   jax.ShapeDtypeStruct((n, l, CAT), BF16),
                   jax.ShapeDtypeStruct((n, VT_ROWS, l), BF16),
                   jax.ShapeDtypeStruct((n, l, KV_RANK), F32),
                   jax.ShapeDtypeStruct((n, l, ROPE_DIM), F32),
                   jax.ShapeDtypeStruct((n, l, SSM_WIDTH), BF16)],
        compiler_params=_cparams(("parallel", "parallel")),
        name="inproj",
    )(x, sc, sh, p["g_pre_mix"], p["w_in"], p["g_q"], p["w_uq"], p["w_uk"], p["g_kv"], cos4, sin4)


def _flash_kernel(qt_ref, k_ref, vt_ref, wuvt_ref, o_ref, m_ref, acc_ref, *, tq, tk):
    i = pl.program_id(1)
    j = pl.program_id(2)
    j_last = ((i + 1) * tq - 1) // tk

    @pl.when(j == 0)
    def _():
        m_ref[...] = jnp.full(m_ref.shape, NEG_INF, F32)
        acc_ref[...] = jnp.zeros(acc_ref.shape, F32)

    def update(masked, key0, size):
        k = k_ref[0, pl.ds(key0, size), :]
        vt = vt_ref[0, :, pl.ds(key0, size)]
        if masked:
            k_pos = j * tk + key0 + lax.broadcasted_iota(jnp.int32, (size, tq), 0)
            q_pos = i * tq + lax.broadcasted_iota(jnp.int32, (size, tq), 1)
            keep = k_pos <= q_pos
        m_all = m_ref[...]
        m_rows = []
        scores = lambda h: jnp.dot(k, qt_ref[0, h], preferred_element_type=F32)
        ahead = [scores(h) for h in range(min(FLASH_LOOKAHEAD, N_HEADS))]
        pending = None
        for h in range(N_HEADS):
            s = ahead.pop(0)
            if h + FLASH_LOOKAHEAD < N_HEADS:
                ahead.append(scores(h + FLASH_LOOKAHEAD))
            if masked:
                s = jnp.where(keep, s, NEG_INF)
            m_prev = m_all[h:h + 1]
            m_new = jnp.maximum(m_prev, jnp.max(s, axis=0, keepdims=True))
            alpha = jnp.exp2(m_prev - m_new)
            p = jnp.exp2(s - m_new)
            pv = jnp.dot(vt, p.astype(BF16), preferred_element_type=F32)
            if pending is not None:
                acc_ref[pending[0]] = pending[1] * acc_ref[pending[0]] + pending[2]
            pending = (h, alpha, pv)
            m_rows.append(m_new)
        acc_ref[pending[0]] = pending[1] * acc_ref[pending[0]] + pending[2]
        m_ref[...] = jnp.concatenate(m_rows, axis=0)

    crosses = (j + 1) * tk - 1 > i * tq
    sub = min(tk, FLASH_DIAG_SUB)

    @pl.when(jnp.logical_and(j <= j_last, crosses))
    def _():
        n_sub = jnp.minimum(tk // sub, ((i + 1) * tq - 1 - j * tk) // sub + 1)

        def body(sb, carry):
            update(True, pl.multiple_of(sb * sub, sub), sub)
            return carry
        lax.fori_loop(0, n_sub, body, 0)

    @pl.when(jnp.logical_and(j <= j_last, jnp.logical_not(crosses)))
    def _():
        update(False, 0, tk)

    @pl.when(j == j_last)
    def _():
        outs = []
        for h in range(N_HEADS):
            o_t = acc_ref[h, :KV_RANK, :] / acc_ref[h, KV_RANK:KV_RANK + 1, :]
            outs.append(_mm(wuvt_ref[h], o_t))
        o_ref[0] = jnp.concatenate(outs, axis=0).T.astype(BF16)


def _flash(q_t, k_cat, v_t, wuv_t, tq, tk):
    n, _, _, l = q_t.shape
    kern = functools.partial(_flash_kernel, tq=tq, tk=tk)
    last = lambda i: ((i + 1) * tq - 1) // tk
    return pl.pallas_call(
        kern,
        grid=(n, l // tq, l // tk),
        in_specs=[pl.BlockSpec((1, N_HEADS, CAT, tq), lambda b, i, j: (b, 0, 0, i)),
                  pl.BlockSpec((1, tk, CAT), lambda b, i, j: (b, jnp.minimum(j, last(i)), 0)),
                  pl.BlockSpec((1, VT_ROWS, tk), lambda b, i, j: (b, 0, jnp.minimum(j, last(i)))),
                  pl.BlockSpec((N_HEADS, V_DIM, KV_RANK), lambda b, i, j: (0, 0, 0))],
        out_specs=pl.BlockSpec((1, tq, ATTN_WIDTH), lambda b, i, j: (b, i, 0)),
        out_shape=jax.ShapeDtypeStruct((n, l, ATTN_WIDTH), BF16),
        scratch_shapes=[pltpu.VMEM((N_HEADS, tq), F32),
                        pltpu.VMEM((N_HEADS, VT_ROWS, tq), F32)],
        compiler_params=_cparams(("parallel", "parallel", "arbitrary")),
        name="flash",
    )(q_t, k_cat, v_t, wuv_t)


def _ssm_out(y, up, d_ref, wglu_ref, bglu_ref):
    y = _gelu_tanh(y + d_ref[...] * up)
    return y * _sigmoid(_mm(y, wglu_ref[...]) + bglu_ref[...])


def _s5_kernel(u_ref, perm_ref, permt_ref, bmat_ref, cmat_ref, lam_ref, laml_ref, d_ref, wglu_ref,
               bglu_ref, y_ref, hlast_ref, buf_ref, carry_ref, *, lc):
    j = pl.program_id(1)

    @pl.when(j == 0)
    def _():
        carry_ref[...] = jnp.zeros(carry_ref.shape, F32)

    u = u_ref[0]
    up = jnp.dot(perm_ref[...], u, preferred_element_type=F32).astype(BF16)
    blk_w = 2 * BLOCK_STATES

    def input_proj(b):
        buf_ref[:, b * blk_w:(b + 1) * blk_w] = jnp.dot(
            up[:, b * LANES:(b + 1) * LANES], bmat_ref[b], preferred_element_type=F32)

    input_proj(0)
    ys = []
    for b in range(SSM_BLOCKS):
        if b + 1 < SSM_BLOCKS:
            input_proj(b + 1)
        st = slice(b * BLOCK_STATES, (b + 1) * BLOCK_STATES)
        re_cols = slice(b * blk_w, b * blk_w + BLOCK_STATES)
        im_cols = slice(b * blk_w + BLOCK_STATES, (b + 1) * blk_w)
        lr = lam_ref[0, :, st]
        li = lam_ref[1, :, st]

        def run(init, store):
            re, im = init
            for t in range(lc):
                rows = slice(t * SUBLANES, (t + 1) * SUBLANES)
                re, im = (lr * re - li * im + buf_ref[rows, re_cols],
                          lr * im + li * re + buf_ref[rows, im_cols])
                if store:
                    buf_ref[rows, re_cols] = re
                    buf_ref[rows, im_cols] = im
            return re, im

        zero = jnp.zeros((SUBLANES, BLOCK_STATES), F32)
        end_re, end_im = run((zero, zero), False)
        ll_r = laml_ref[0:1, st]
        ll_i = laml_ref[1:2, st]
        c_re = carry_ref[0:1, st]
        c_im = carry_ref[1:2, st]
        init_re, init_im = [], []
        for k in range(SUBLANES):
            init_re.append(c_re)
            init_im.append(c_im)
            c_re, c_im = (ll_r * c_re - ll_i * c_im + end_re[k:k + 1],
                          ll_r * c_im + ll_i * c_re + end_im[k:k + 1])
        carry_ref[0:1, st] = c_re
        carry_ref[1:2, st] = c_im
        run((jnp.concatenate(init_re, axis=0), jnp.concatenate(init_im, axis=0)), True)
        ys.append(jnp.dot(buf_ref[:, b * blk_w:(b + 1) * blk_w].astype(BF16), cmat_ref[b],
                          preferred_element_type=F32))

    hlast_ref[0] = carry_ref[...]
    y = _ssm_out(jnp.concatenate(ys, axis=-1), up.astype(F32), d_ref, wglu_ref, bglu_ref)
    y_ref[0] = jnp.dot(permt_ref[...], y.astype(BF16), preferred_element_type=F32).astype(BF16)


def _s5_prompt(u, sp, lc):
    n, l, _ = u.shape
    s = SUBLANES * lc
    rows = jnp.arange(s)
    src = (rows % SUBLANES) * lc + rows // SUBLANES
    perm = (src[:, None] == rows[None, :]).astype(BF16)
    laml = _complex_pow2(sp["lam"], lc)
    lam8 = jnp.broadcast_to(sp["lam"][:, None, :], (2, SUBLANES, N_STATE))
    kern = functools.partial(_s5_kernel, lc=lc)
    c2 = lambda b, j: (0, 0)
    c3 = lambda b, j: (0, 0, 0)
    return pl.pallas_call(
        kern,
        grid=(n, l // s),
        in_specs=[pl.BlockSpec((1, s, SSM_WIDTH), lambda b, j: (b, j, 0)),
                  pl.BlockSpec((s, s), c2),
                  pl.BlockSpec((s, s), c2),
                  pl.BlockSpec((SSM_BLOCKS, LANES, 2 * BLOCK_STATES), c3),
                  pl.BlockSpec((SSM_BLOCKS, 2 * BLOCK_STATES, LANES), c3),
                  pl.BlockSpec((2, SUBLANES, N_STATE), c3),
                  pl.BlockSpec((2, N_STATE), c2),
                  pl.BlockSpec((1, SSM_WIDTH), c2),
                  pl.BlockSpec((SSM_WIDTH, SSM_WIDTH), c2),
                  pl.BlockSpec((1, SSM_WIDTH), c2)],
        out_specs=[pl.BlockSpec((1, s, SSM_WIDTH), lambda b, j: (b, j, 0)),
                   pl.BlockSpec((1, 2, N_STATE), lambda b, j: (b, 0, 0))],
        out_shape=[jax.ShapeDtypeStruct((n, l, SSM_WIDTH), BF16),
                   jax.ShapeDtypeStruct((n, 2, N_STATE), F32)],
        scratch_shapes=[pltpu.VMEM((s, 2 * N_STATE), F32),
                        pltpu.VMEM((2, N_STATE), F32)],
        compiler_params=_cparams(("parallel", "arbitrary")),
        name="s5_scan",
    )(u, perm, perm.T, sp["bmat"], sp["cmat"], lam8, laml, sp["d"], sp["w_glu"], sp["b_glu"])


def _s5_step_kernel(u_ref, h0_ref, bmat_ref, cmat_ref, lam_ref, d_ref, wglu_ref, bglu_ref,
                    y_ref, h_ref):
    u = u_ref[...]
    ys = []
    for b in range(SSM_BLOCKS):
        st = slice(b * BLOCK_STATES, (b + 1) * BLOCK_STATES)
        bu = jnp.dot(u[:, b * LANES:(b + 1) * LANES], bmat_ref[b], preferred_element_type=F32)
        lr = lam_ref[0:1, st]
        li = lam_ref[1:2, st]
        re = h0_ref[0, :, st]
        im = h0_ref[1, :, st]
        n_re = lr * re - li * im + bu[:, :BLOCK_STATES]
        n_im = lr * im + li * re + bu[:, BLOCK_STATES:]
        h_ref[0, :, st] = n_re
        h_ref[1, :, st] = n_im
        ys.append(_mm(jnp.concatenate([n_re, n_im], axis=-1), cmat_ref[b]))
    y = _ssm_out(jnp.concatenate(ys, axis=-1), u.astype(F32), d_ref, wglu_ref, bglu_ref)
    y_ref[...] = y.astype(BF16)


def _s5_step(u, h0, sp):
    rows = u.shape[0]
    return pl.pallas_call(
        _s5_step_kernel,
        out_shape=[jax.ShapeDtypeStruct((rows, SSM_WIDTH), BF16),
                   jax.ShapeDtypeStruct((2, rows, N_STATE), F32)],
        compiler_params=pltpu.CompilerParams(vmem_limit_bytes=VMEM_LIMIT),
        name="s5_step",
    )(u, h0, sp["bmat"], sp["cmat"], sp["lam"], sp["d"], sp["w_glu"], sp["b_glu"])


def _decode_kernel(pt_ref, ql_ref, qr_ref, kvs_ref, krs_ref, lat_hbm, rope_hbm, o_ref,
                   lat_buf, rope_buf, sems, *, n_pages):
    b = pl.program_id(0)
    slot = b % 2

    def page_copies(seq, dst_slot):
        out = []
        for i in range(n_pages):
            page = pt_ref[seq * n_pages + i]
            keys = pl.ds(i * PAGE_SIZE, PAGE_SIZE)
            out.append(pltpu.make_async_copy(lat_hbm.at[page], lat_buf.at[dst_slot, keys, :],
                                             sems.at[dst_slot, 0]))
            out.append(pltpu.make_async_copy(rope_hbm.at[page], rope_buf.at[dst_slot, :, keys],
                                             sems.at[dst_slot, 1]))
        return out

    @pl.when(b == 0)
    def _():
        for n, cp in enumerate(page_copies(0, 0)):
            cp.start(priority=n % 2)

    @pl.when(b + 1 < pl.num_programs(0))
    def _():
        for n, cp in enumerate(page_copies(b + 1, 1 - slot)):
            cp.start(priority=n % 2)

    for cp in page_copies(b, slot):
        cp.wait()

    ql = ql_ref[0]
    qr = qr_ref[0]
    kvs = kvs_ref[0]
    krs = krs_ref[0]
    lat = lat_buf[slot].astype(BF16)
    s = _mm_nt(ql, lat) + _mm(qr, rope_buf[slot])
    s_self = (jnp.sum(ql.astype(F32) * kvs, axis=-1, keepdims=True)
              + jnp.sum(qr.astype(F32) * krs, axis=-1, keepdims=True))
    m = jnp.maximum(jnp.max(s, axis=-1, keepdims=True), s_self)
    p = jnp.exp2(s - m)
    p_self = jnp.exp2(s_self - m)
    denom = jnp.sum(p, axis=-1, keepdims=True) + p_self
    o_ref[0] = (_mm(p, lat) + p_self * kvs) / denom


def _decode_attn(q_lat, q_rope, kv_self, kr_self, cache_lat, cache_rope_t, page_table):
    b, n_pages = page_table.shape
    kern = functools.partial(_decode_kernel, n_pages=n_pages)
    seq3 = lambda s, pt: (s, 0, 0)
    keys = n_pages * PAGE_SIZE
    grid_spec = pltpu.PrefetchScalarGridSpec(
        num_scalar_prefetch=1,
        grid=(b,),
        in_specs=[pl.BlockSpec((1, N_HEADS, KV_RANK), seq3),
                  pl.BlockSpec((1, N_HEADS, ROPE_DIM), seq3),
                  pl.BlockSpec((1, 1, KV_RANK), seq3),
                  pl.BlockSpec((1, 1, ROPE_DIM), seq3),
                  pl.BlockSpec(memory_space=pl.ANY),
                  pl.BlockSpec(memory_space=pl.ANY)],
        out_specs=pl.BlockSpec((1, N_HEADS, KV_RANK), seq3),
        scratch_shapes=[pltpu.VMEM((2, keys, KV_RANK), F32),
                        pltpu.VMEM((2, ROPE_DIM, keys), F32),
                        pltpu.SemaphoreType.DMA((2, 2))],
    )
    return pl.pallas_call(
        kern,
        grid_spec=grid_spec,
        out_shape=jax.ShapeDtypeStruct((b, N_HEADS, KV_RANK), F32),
        compiler_params=_cparams(("arbitrary",)),
        name="decode_attn",
    )(page_table.reshape(-1), q_lat, q_rope, kv_self, kr_self, cache_lat, cache_rope_t)


def _uv_kernel(o_ref, w_ref, y_ref):
    y_ref[...] = _mm(o_ref[...], w_ref[...]).astype(BF16)


def _uv_proj(o_lat, wuv_bd):
    rows = o_lat.shape[0]
    return pl.pallas_call(
        _uv_kernel,
        out_shape=jax.ShapeDtypeStruct((rows, ATTN_WIDTH), BF16),
        compiler_params=pltpu.CompilerParams(vmem_limit_bytes=VMEM_LIMIT),
        name="uv_proj",
    )(o_lat, wuv_bd)


def _outproj_kernel(oa_ref, ys_ref, x_ref, g1_ref, sc_ref, sh_ref, wout_ref, gpost_ref, gpre_ref,
                    wr_ref, br_ref, *rest, row0):
    x1_ref, h2_ref, route_ref = rest[-3:]
    mix = (jnp.dot(oa_ref[0], wout_ref[:ATTN_WIDTH].astype(BF16), preferred_element_type=F32)
           + jnp.dot(ys_ref[0], wout_ref[ATTN_WIDTH:].astype(BF16), preferred_element_type=F32))
    x1 = x_ref[0] + g1_ref[0] * _rms(mix, gpost_ref[...], D_MODEL)
    x1_ref[0] = x1
    h2 = _rms(x1, gpre_ref[...], D_MODEL) * (1.0 + sc_ref[0]) + sh_ref[0]
    for s in range(D_MODEL // LANES):
        h2_ref[pl.ds(s, h2.shape[0], stride=SUBLANES), :] = h2[:, s * LANES:(s + 1) * LANES]
    logits = _mm3(h2, wr_ref[...]) + br_ref[...]
    lane = lax.broadcasted_iota(jnp.int32, logits.shape, 1).astype(F32)
    vals, idxs = [], []
    for _ in range(TOP_K):
        v = jnp.max(logits, axis=-1, keepdims=True)
        idx = jnp.min(jnp.where(logits == v, lane, float(LANES)), axis=-1, keepdims=True)
        vals.append(v)
        idxs.append(idx)
        logits = jnp.where(lane == idx, -jnp.inf, logits)
    exps = [jnp.exp(v - vals[0]) for v in vals]
    denom = exps[0] + exps[1] + exps[2] + exps[3]
    tm = logits.shape[0]
    first = 1 + row0 + (pl.program_id(0) * pl.num_programs(1) + pl.program_id(1)) * tm
    token = (first + lax.broadcasted_iota(jnp.int32, (tm, 1), 0)).astype(F32)
    route = jnp.where(lane == float(ROUTE_TOKEN), token, 0.0)
    for k in range(TOP_K):
        route = jnp.where(lane == float(k), idxs[k], route)
        route = jnp.where(lane == float(TOP_K + k), exps[k] / denom, route)
    route_ref[...] = route


def _outproj(o_attn, y_ssm, x, g1, sc2, sh2, p, tm, t_all, row0, dest=None):
    n, l, _ = x.shape
    per_row = g1.shape[1] != 1
    mod_blk = (1, tm, D_MODEL) if per_row else (1, 1, D_MODEL)
    mod_map = (lambda b, i: (b, i, 0)) if per_row else (lambda b, i: (b, 0, 0))
    row_map = lambda b, i: (b, i, 0)
    const2 = lambda b, i: (0, 0)
    blk0, per_seq = row0 // tm, l // tm
    n_tile = D_MODEL // LANES
    in_specs = [pl.BlockSpec((1, tm, ATTN_WIDTH), row_map),
                pl.BlockSpec((1, tm, SSM_WIDTH), row_map),
                pl.BlockSpec((1, tm, D_MODEL), row_map),
                pl.BlockSpec(mod_blk, mod_map),
                pl.BlockSpec(mod_blk, mod_map),
                pl.BlockSpec(mod_blk, mod_map),
                pl.BlockSpec((D_MODEL, D_MODEL), const2),
                pl.BlockSpec((1, D_MODEL), const2),
                pl.BlockSpec((1, D_MODEL), const2),
                pl.BlockSpec((D_MODEL, LANES), const2),
                pl.BlockSpec((1, LANES), const2)]
    args = [o_attn, y_ssm, x, g1, sc2, sh2, p["w_out"], p["g_post_mix"], p["g_pre_ffn"],
            p["w_router"], p["b_router"]]
    aliases = {}
    if dest is not None:
        aliases = {len(args): 1, len(args) + 1: 2}
        in_specs += [pl.BlockSpec(memory_space=pl.ANY), pl.BlockSpec(memory_space=pl.ANY)]
        args += list(dest)
    return pl.pallas_call(
        functools.partial(_outproj_kernel, row0=row0),
        grid=(n, per_seq),
        in_specs=in_specs,
        out_specs=[pl.BlockSpec((1, tm, D_MODEL), row_map),
                   pl.BlockSpec((tm * n_tile, LANES), lambda b, i: (blk0 + b * per_seq + i, 0)),
                   pl.BlockSpec((tm, LANES), lambda b, i: (blk0 + b * per_seq + i, 0))],
        out_shape=[jax.ShapeDtypeStruct((n, l, D_MODEL), F32),
                   jax.ShapeDtypeStruct((t_all * n_tile, LANES), F32),
                   jax.ShapeDtypeStruct((t_all, LANES), F32)],
        input_output_aliases=aliases,
        compiler_params=_cparams(("parallel", "parallel")),
        name="outproj",
    )(*args)


def _moe_prep_kernel(w1_ref, w2_ref, perm_ref, w1o_ref, w2o_ref):
    for j in range(w1_ref.shape[2] // GLU_BLOCK):
        cols = slice(j * GLU_BLOCK, (j + 1) * GLU_BLOCK)
        w1o_ref[0, :, cols] = jnp.dot(w1_ref[0, :, cols].astype(BF16), perm_ref[...],
                                      preferred_element_type=F32).astype(BF16)
    w2o_ref[0] = w2_ref[0].astype(BF16)


def _moe_prep(w1, w2):
    n_e, d_in, d_hid = w1.shape
    d_exp = w2.shape[1]
    j = jnp.arange(GLU_BLOCK)
    src = jnp.where(j < LANES, 2 * j, 2 * (j - LANES) + 1)
    perm = (j[:, None] == src[None, :]).astype(BF16)
    e_map = lambda e: (e, 0, 0)
    return pl.pallas_call(
        _moe_prep_kernel,
        grid=(n_e,),
        in_specs=[pl.BlockSpec((1, d_in, d_hid), e_map),
                  pl.BlockSpec((1, d_exp, D_MODEL), e_map),
                  pl.BlockSpec((GLU_BLOCK, GLU_BLOCK), lambda e: (0, 0))],
        out_specs=[pl.BlockSpec((1, d_in, d_hid), e_map),
                   pl.BlockSpec((1, d_exp, D_MODEL), e_map)],
        out_shape=[jax.ShapeDtypeStruct(w1.shape, BF16),
                   jax.ShapeDtypeStruct(w2.shape, BF16)],
        compiler_params=_cparams(("parallel",)),
        name="moe_prep",
    )(w1, w2, perm)


def _sort_rows_kernel(pos_ref, route_ref, rows_ref):
    rows_ref[...] = jnp.zeros(rows_ref.shape, F32)

    def body(t, carry):
        row = route_ref[0, pl.ds(t, 1), :]
        for k in range(TOP_K):
            rows_ref[0, pl.ds(pos_ref[0, 0, TOP_K * t + k], 1), :] = row
        return carry
    lax.fori_loop(0, route_ref.shape[1], body, 0, unroll=4)


def _sort_rows(route, pos, nch, rc):
    t = route.shape[0]
    tc = t // nch
    return pl.pallas_call(
        _sort_rows_kernel,
        grid=(nch,),
        in_specs=[pl.BlockSpec((1, 1, tc * TOP_K), lambda c: (c, 0, 0), memory_space=pltpu.SMEM),
                  pl.BlockSpec((1, tc, LANES), lambda c: (c, 0, 0))],
        out_specs=pl.BlockSpec((1, rc, LANES), lambda c: (c, 0, 0)),
        out_shape=jax.ShapeDtypeStruct((nch, rc, LANES), F32),
        compiler_params=_cparams(("parallel",)),
        name="moe_sort_rows",
    )(pos.reshape(nch, 1, tc * TOP_K), route.reshape(nch, tc, LANES))


def _positions_kernel(route_ref, off_ref, tri_ref, pos_ref, cnt_ref):
    @pl.when(pl.program_id(1) == 0)
    def _():
        cnt_ref[...] = jnp.zeros(cnt_ref.shape, F32)

    r = route_ref[0]
    lane = lax.broadcasted_iota(jnp.int32, r.shape, 1).astype(F32)
    picks = [lane == r[:, k:k + 1] for k in range(TOP_K)]
    hot = jnp.zeros(r.shape, F32)
    for pick in picks:
        hot = hot + jnp.where(pick, 1.0, 0.0)
    rank = jnp.dot(tri_ref[...], hot.astype(BF16), preferred_element_type=F32) + cnt_ref[...]
    cnt_ref[...] = cnt_ref[...] + jnp.sum(hot, axis=0, keepdims=True)
    base = off_ref[0] + rank
    pos = jnp.zeros(r.shape, F32)
    for k, pick in enumerate(picks):
        pos = jnp.where(lane == float(k), jnp.sum(jnp.where(pick, base, 0.0), axis=-1, keepdims=True), pos)
    pos_ref[0] = pos


def _positions(route, off, nch):
    t = route.shape[0]
    tc = t // nch
    tp = _pick_tile(tc, 1024, SUBLANES)
    assert tc % tp == 0
    tri = (jnp.arange(tp)[:, None] > jnp.arange(tp)[None, :]).astype(BF16)
    pos = pl.pallas_call(
        _positions_kernel,
        grid=(nch, tc // tp),
        in_specs=[pl.BlockSpec((1, tp, LANES), lambda c, j: (c, j, 0)),
                  pl.BlockSpec((1, 1, LANES), lambda c, j: (c, 0, 0)),
                  pl.BlockSpec((tp, tp), lambda c, j: (0, 0))],
        out_specs=pl.BlockSpec((1, tp, LANES), lambda c, j: (c, j, 0)),
        out_shape=jax.ShapeDtypeStruct((nch, tc, LANES), F32),
        scratch_shapes=[pltpu.VMEM((1, LANES), F32)],
        compiler_params=_cparams(("parallel", "arbitrary")),
        name="moe_positions",
    )(route.reshape(nch, tc, LANES), off, tri)
    return pos[:, :, :TOP_K].astype(jnp.int32).reshape(t, TOP_K)


def _route_plan(route, nch, tmx):
    idx = route[:, :TOP_K].astype(jnp.int32)
    t = idx.shape[0]
    tc = t // nch
    nt = -(-(tc * TOP_K + N_EXPERTS * (tmx - 1)) // tmx)
    i32 = jnp.int32
    hot = (idx[:, :, None] == jnp.arange(N_EXPERTS, dtype=i32)).astype(i32).sum(1)
    counts = hot.reshape(nch, tc, N_EXPERTS).sum(1)
    padded = -(-counts // tmx) * tmx
    off_end = jnp.cumsum(padded, axis=1)
    off = off_end - padded
    n_tiles = (off_end[:, -1] // tmx).astype(i32)
    off_lanes = jnp.pad(off.astype(F32), ((0, 0), (0, LANES - N_EXPERTS)))[:, None, :]
    pos = _positions(route, off_lanes, nch)
    rc = nt * tmx
    rows = _sort_rows(route, pos, nch, rc)
    token = lax.optimization_barrier(rows[:, :, ROUTE_TOKEN].astype(i32) - 1)
    row_tok = jnp.maximum(token, 0) % tc
    row_dst = jnp.where(token >= 0, row_tok, tc + jnp.arange(rc, dtype=i32)[None, :] % SUBLANES)
    tile = jnp.arange(nt + 1, dtype=i32)
    texp_raw = (off_end[:, None, :] <= (tile * tmx)[None, :, None]).sum(-1).astype(i32)
    last = n_tiles[:, None] - 1
    tile_c = jnp.minimum(tile[None, :], last)
    tile_n = jnp.minimum(tile[None, :] + 1, last)
    tile_p = jnp.clip(tile[None, :] - 1, 0, last)
    texp = jnp.take_along_axis(texp_raw, tile_c, axis=1)
    base = (jnp.arange(nch, dtype=i32) * nt)[:, None]
    return dict(nt=nt, n_tiles=n_tiles, texp=texp.reshape(-1), tblk=(tile_c + base).reshape(-1),
                tnext=(tile_n + base).reshape(-1), tprev=(tile_p + base).reshape(-1),
                row_tok=row_tok.reshape(nch * nt, 1, tmx), row_dst=row_dst.reshape(nch * nt, 1, tmx),
                rows=rows.reshape(nch * nt, tmx, LANES))


def _moe_kernel(nt_ref, texp_ref, tblk_ref, tnext_ref, tprev_ref, tok_ref, toknext_ref, dstprev_ref,
                rows_ref, src_ref, w1_ref, b1_ref, w2_ref, b2_ref, out_ref, xs_ref, yt_ref):
    del tblk_ref, tnext_ref, tprev_ref
    c = pl.program_id(0)
    i = pl.program_id(1)
    tmx = rows_ref.shape[1]
    slot = i % 2
    n_grp = tmx // SUBLANES

    def add_prev_rows(groups):
        for g in groups:
            toks = [dstprev_ref[0, 0, g * SUBLANES + u] for u in range(SUBLANES)]
            sums = [out_ref[0, toks[u]]
                    + yt_ref[1 - slot, (g * SUBLANES + u) * SUBLANES:(g * SUBLANES + u + 1) * SUBLANES, :]
                    for u in range(SUBLANES)]
            for u in range(SUBLANES):
                out_ref[0, toks[u]] = sums[u]

    @pl.when(i == 0)
    def _():
        out_ref[...] = jnp.zeros(out_ref.shape, F32)
        yt_ref[1] = jnp.zeros(yt_ref.shape[1:], F32)

        def gather(r, carry):
            xs_ref[0, pl.ds(pl.multiple_of(r * SUBLANES, SUBLANES), SUBLANES), :] = src_ref[0, tok_ref[0, 0, r]]
            return carry
        lax.fori_loop(0, tmx, gather, 0, unroll=8)

    @pl.when(i < nt_ref[c])
    def _():
        x = jnp.concatenate([xs_ref[slot, pl.ds(s, tmx, stride=SUBLANES), :]
                             for s in range(D_MODEL // LANES)], axis=-1).astype(BF16)
        n_blk = w1_ref.shape[2] // (2 * GLU_BLOCK)
        wide = 2 * GLU_BLOCK

        def hidden(j):
            cols = slice(j * wide, (j + 1) * wide)
            return jnp.dot(x, w1_ref[0, :, cols], preferred_element_type=F32) + b1_ref[0, :, cols]

        hid = hidden(0)
        y = None
        for j in range(n_blk):
            nxt = hidden(j + 1) if j + 1 < n_blk else None
            for r in range(j * tmx // n_blk, (j + 1) * tmx // n_blk):
                xs_ref[1 - slot, r * SUBLANES:(r + 1) * SUBLANES, :] = src_ref[0, toknext_ref[0, 0, r]]
            add_prev_rows(range(j * n_grp // n_blk, (j + 1) * n_grp // n_blk))
            acts = []
            for b in range(2):
                glu = jnp.minimum(hid[:, b * GLU_BLOCK:b * GLU_BLOCK + LANES], SWIGLU_LIMIT)
                lin = jnp.clip(hid[:, b * GLU_BLOCK + LANES:(b + 1) * GLU_BLOCK], -SWIGLU_LIMIT, SWIGLU_LIMIT)
                acts.append(glu * _sigmoid(SWIGLU_ALPHA * glu) * (lin + 1.0))
            part = _mm(jnp.concatenate(acts, axis=-1), w2_ref[0, j * GLU_BLOCK:(j + 1) * GLU_BLOCK, :])
            y = part if y is None else y + part
            hid = nxt
        meta = rows_ref[0]
        expert = texp_ref[c * pl.num_programs(1) + i].astype(F32)
        lane = lax.broadcasted_iota(jnp.int32, meta.shape, 1)
        hit = jnp.logical_and(lane < TOP_K, meta == expert)
        gates = pltpu.roll(meta, LANES - TOP_K, axis=1)
        gate = jnp.sum(jnp.where(hit, gates, 0.0), axis=-1, keepdims=True)
        y = gate * (y + b2_ref[0])
        for s in range(D_MODEL // LANES):
            yt_ref[slot, pl.ds(s, tmx, stride=SUBLANES), :] = y[:, s * LANES:(s + 1) * LANES]

    @pl.when(i == nt_ref[c])
    def _():
        add_prev_rows(range(n_grp))


def _moe_experts(h2, plan, w1p, b1p, w2p, b2, tmx):
    nch, tc = h2.shape[:2]
    nt = plan["nt"]
    n_e, _, d_hid = w1p.shape
    d_exp = w2p.shape[1]
    steps = nt + 1
    flat = lambda c, i: c * steps + i
    cur = lambda c, i, n, te, tb, tn, tp: (tb[flat(c, i)], 0, 0)
    nxt = lambda c, i, n, te, tb, tn, tp: (tn[flat(c, i)], 0, 0)
    prv = lambda c, i, n, te, tb, tn, tp: (tp[flat(c, i)], 0, 0)
    exp = lambda c, i, n, te, tb, tn, tp: (te[flat(c, i)], 0, 0)
    grid_spec = pltpu.PrefetchScalarGridSpec(
        num_scalar_prefetch=5,
        grid=(nch, steps),
        in_specs=[pl.BlockSpec((1, 1, tmx), cur, memory_space=pltpu.SMEM),
                  pl.BlockSpec((1, 1, tmx), nxt, memory_space=pltpu.SMEM),
                  pl.BlockSpec((1, 1, tmx), prv, memory_space=pltpu.SMEM),
                  pl.BlockSpec((1, tmx, LANES), cur),
                  pl.BlockSpec((1, tc, D_MODEL // LANES, LANES), lambda c, i, n, te, tb, tn, tp: (c, 0, 0, 0),
                               pipeline_mode=pl.Buffered(1)),
                  pl.BlockSpec((1, D_MODEL, d_hid), exp),
                  pl.BlockSpec((1, 1, d_hid), exp),
                  pl.BlockSpec((1, d_exp, D_MODEL), exp),
                  pl.BlockSpec((1, 1, D_MODEL), exp)],
        out_specs=pl.BlockSpec((1, tc + SUBLANES, D_MODEL // LANES, LANES),
                               lambda c, i, n, te, tb, tn, tp: (c, 0, 0, 0), pipeline_mode=pl.Buffered(1)),
        scratch_shapes=[pltpu.VMEM((2, tmx * (D_MODEL // LANES), LANES), F32),
                        pltpu.VMEM((2, tmx * (D_MODEL // LANES), LANES), F32)],
    )
    return pl.pallas_call(
        _moe_kernel,
        grid_spec=grid_spec,
        out_shape=jax.ShapeDtypeStruct((nch, tc + SUBLANES, D_MODEL // LANES, LANES), F32),
        compiler_params=_cparams(("parallel", "arbitrary")),
        name="moe_experts",
    )(plan["n_tiles"], plan["texp"], plan["tblk"], plan["tnext"], plan["tprev"], plan["row_tok"],
      plan["row_tok"], plan["row_dst"], plan["rows"], h2, w1p, b1p, w2p, b2)


def _final_kernel(x1_ref, f_ref, g2_ref, gpost_ref, y_ref):
    tm = x1_ref.shape[1]
    f = jnp.concatenate([f_ref[pl.ds(s, tm, stride=SUBLANES), :] for s in range(D_MODEL // LANES)],
                        axis=-1)
    y_ref[0] = x1_ref[0] + g2_ref[0] * _rms(f, gpost_ref[...], D_MODEL)


def _final(x1, f_all, row0, g2, g_post_ffn, tm):
    n, l, _ = x1.shape
    per_row = g2.shape[1] != 1
    mod_blk = (1, tm, D_MODEL) if per_row else (1, 1, D_MODEL)
    mod_map = (lambda b, i: (b, i, 0)) if per_row else (lambda b, i: (b, 0, 0))
    row_map = lambda b, i: (b, i, 0)
    blk0, per_seq = row0 // tm, l // tm
    return pl.pallas_call(
        _final_kernel,
        grid=(n, l // tm),
        in_specs=[pl.BlockSpec((1, tm, D_MODEL), row_map),
                  pl.BlockSpec((tm * (D_MODEL // LANES), LANES), lambda b, i: (blk0 + b * per_seq + i, 0)),
                  pl.BlockSpec(mod_blk, mod_map),
                  pl.BlockSpec((1, D_MODEL), lambda b, i: (0, 0))],
        out_specs=pl.BlockSpec((1, tm, D_MODEL), row_map),
        out_shape=jax.ShapeDtypeStruct((n, l, D_MODEL), F32),
        compiler_params=_cparams(("parallel", "parallel")),
        name="final",
    )(x1, f_all, g2, g_post_ffn)


def _complex_pow2(lam, n):
    re, im = lam[0], lam[1]
    steps = int(round(math.log2(n)))
    assert 2 ** steps == n
    for _ in range(steps):
        re, im = re * re - im * im, 2.0 * re * im
    return jnp.stack([re, im])


def _rope_tables(pos):
    half = ROPE_DIM // 2
    inv = ROPE_THETA ** (-jnp.arange(half, dtype=F32) / half)
    ang = pos.astype(F32)[:, None] * inv[None, :]
    cos, sin = jnp.cos(ang), jnp.sin(ang)
    cos4 = jnp.tile(jnp.concatenate([cos, cos], axis=-1), (1, LANES // ROPE_DIM))
    sin4 = jnp.tile(jnp.concatenate([-sin, sin], axis=-1), (1, LANES // ROPE_DIM))
    return cos4[None], sin4[None]


def _swap_halves(w):
    half = ROPE_DIM // 2
    return jnp.concatenate([w[..., half:], w[..., :half]], axis=-1)


def _prep_layer(w, layer):
    g = lambda name: w[name][layer]
    p = {}
    row = lambda v: v.reshape(1, -1)
    p["g_pre_mix"] = row(g("g_pre_mix"))
    w_in = g("w_in")
    w_q, w_kv, w_r, w_u = jnp.split(w_in, [Q_RANK, Q_RANK + KV_RANK, Q_RANK + KV_RANK + ROPE_DIM], axis=-1)
    rep = LANES // ROPE_DIM
    p["w_in"] = jnp.concatenate(
        [w_q, jnp.zeros((D_MODEL, Q_PAD - Q_RANK), F32), w_kv,
         jnp.tile(w_r, (1, rep)), jnp.tile(_swap_halves(w_r), (1, rep)), w_u], axis=-1)
    p["g_q"] = row(jnp.pad(g("g_q"), (0, Q_PAD - Q_RANK)))
    w_uq = g("w_uq").reshape(Q_RANK, N_HEADS, NOPE_DIM + ROPE_DIM)
    nope = w_uq[:, :, :NOPE_DIM].reshape(Q_RANK, N_HEADS * NOPE_DIM)
    rope = w_uq[:, :, NOPE_DIM:]
    rope_sw = _swap_halves(rope)
    w_uq_p = jnp.concatenate([nope, rope.reshape(Q_RANK, -1), rope_sw.reshape(Q_RANK, -1)], axis=-1)
    p["w_uq"] = jnp.pad(w_uq_p, ((0, Q_PAD - Q_RANK), (0, 0)))
    w_uk = jnp.transpose(g("w_uk"), (1, 2, 0))
    z = jnp.zeros((NOPE_DIM, KV_RANK), F32)
    p["w_uk"] = jnp.stack([jnp.block([[w_uk[2 * j], z], [z, w_uk[2 * j + 1]]])
                           for j in range(N_HEADS // 2)])
    p["g_kv"] = row(g("g_kv"))
    w_uv = jnp.transpose(g("w_uv"), (1, 0, 2))
    p["w_uv_t"] = jnp.transpose(w_uv, (0, 2, 1))
    eye_h = jnp.eye(N_HEADS, dtype=F32)
    p["w_uv_bd"] = jnp.einsum("hrv,hk->hrkv", w_uv, eye_h).reshape(N_HEADS * KV_RANK, ATTN_WIDTH)
    p["w_out"] = g("w_out")
    for name in ("w_in", "w_uq", "w_uk", "w_uv_t", "w_uv_bd", "w_out"):
        p[name] = p[name].astype(BF16)
    p["g_post_mix"] = row(g("g_post_mix"))
    p["g_pre_ffn"] = row(g("g_pre_ffn"))
    p["w_router"] = jnp.pad(g("w_router"), ((0, 0), (0, LANES - N_EXPERTS)))
    p["b_router"] = row(jnp.pad(g("b_router"), (0, LANES - N_EXPERTS), constant_values=NEG_INF))
    p["g_post_ffn"] = row(g("g_post_ffn"))

    lam_re, lam_im = g("ssm_lam_re"), g("ssm_lam_im")
    step = jnp.exp(g("ssm_log_step"))[:, None]
    mag = jnp.exp(lam_re * step)
    bar_re, bar_im = mag * jnp.cos(lam_im * step), mag * jnp.sin(lam_im * step)
    den = lam_re * lam_re + lam_im * lam_im
    co_re = ((bar_re - 1.0) * lam_re + bar_im * lam_im) / den
    co_im = (bar_im * lam_re - (bar_re - 1.0) * lam_im) / den
    b_re, b_im = g("ssm_b_re"), g("ssm_b_im")
    bb_re = co_re[..., None] * b_re - co_im[..., None] * b_im
    bb_im = co_re[..., None] * b_im + co_im[..., None] * b_re
    eye_g = jnp.eye(GROUPS_PER_BLOCK, dtype=F32)

    def b_block(m):
        m = m.reshape(SSM_BLOCKS, GROUPS_PER_BLOCK, SSM_STATE, SSM_GROUP)
        return jnp.einsum("bgpc,gh->bgchp", m, eye_g).reshape(SSM_BLOCKS, LANES, BLOCK_STATES)

    def c_block(m):
        m = m.reshape(SSM_BLOCKS, GROUPS_PER_BLOCK, SSM_GROUP, SSM_STATE)
        return jnp.einsum("bgcp,gh->bgphc", m, eye_g).reshape(SSM_BLOCKS, BLOCK_STATES, LANES)

    sp = {
        "lam": jnp.stack([bar_re.reshape(-1), bar_im.reshape(-1)]),
        "bmat": jnp.concatenate([b_block(bb_re), b_block(bb_im)], axis=-1).astype(BF16),
        "cmat": jnp.concatenate([c_block(g("ssm_c_re")), c_block(-g("ssm_c_im"))], axis=1).astype(BF16),
        "d": row(g("ssm_d")),
        "w_glu": g("ssm_w_glu").astype(BF16),
        "b_glu": row(g("ssm_b_glu")),
    }
    b1 = g("b1")
    n_blk = b1.shape[1] // GLU_BLOCK
    mp = {
        "b1": jnp.transpose(b1.reshape(N_EXPERTS, n_blk, LANES, 2), (0, 1, 3, 2)).reshape(N_EXPERTS, 1, -1),
        "b2": g("b2")[:, None, :],
    }
    return p, sp, mp


def _mods(mod_rows):
    return [mod_rows[:, k * D_MODEL:(k + 1) * D_MODEL] for k in range(N_MOD)]


def _pick_tile(n, target, mult):
    best = mult
    for c in range(mult, min(n, target) + 1, mult):
        if n % c == 0:
            best = c
    return best


def kernel(x_prompt, x_sample, c_prompt, c_sample, cache_kv_latent, cache_k_rope, state_ssm, page_table, w_ada, b_ada, g_pre_mix, w_in, g_q, w_uq, w_uk, g_kv, w_uv, ssm_lam_re, ssm_lam_im, ssm_log_step, ssm_b_re, ssm_b_im, ssm_c_re, ssm_c_im, ssm_d, ssm_w_glu, ssm_b_glu, w_out, g_post_mix, g_pre_ffn, w_router, b_router, w1, b1, w2, b2, g_post_ffn):
    weights = dict(w_ada=w_ada, b_ada=b_ada, g_pre_mix=g_pre_mix, w_in=w_in, g_q=g_q, w_uq=w_uq,
                   w_uk=w_uk, g_kv=g_kv, w_uv=w_uv, ssm_lam_re=ssm_lam_re, ssm_lam_im=ssm_lam_im,
                   ssm_log_step=ssm_log_step, ssm_b_re=ssm_b_re, ssm_b_im=ssm_b_im,
                   ssm_c_re=ssm_c_re, ssm_c_im=ssm_c_im, ssm_d=ssm_d, ssm_w_glu=ssm_w_glu,
                   ssm_b_glu=ssm_b_glu, w_out=w_out, g_post_mix=g_post_mix, g_pre_ffn=g_pre_ffn,
                   w_router=w_router, b_router=b_router, w1=w1, b1=b1, w2=w2, b2=b2,
                   g_post_ffn=g_post_ffn)
    depth = w_ada.shape[0]
    n_p, l_p, _ = x_prompt.shape
    n_s, l_s, _ = x_sample.shape
    assert l_s == 1
    n_pages = page_table.shape[1]
    past_len = n_pages * PAGE_SIZE

    cos_p, sin_p = _rope_tables(jnp.arange(l_p, dtype=jnp.int32))
    cos_s, sin_s = _rope_tables(jnp.full((1,), past_len, jnp.int32))
    cos_s = jnp.broadcast_to(cos_s, (1, n_s, LANES))
    sin_s = jnp.broadcast_to(sin_s, (1, n_s, LANES))

    tm_p = _pick_tile(l_p, 512, 16)
    tq = _pick_tile(l_p, 512, 16)
    tk = _pick_tile(l_p, 1024, 16)
    lc = 64 if l_p % 512 == 0 else l_p // SUBLANES
    c_rows = n_p + n_s
    c_pad = -c_rows % SUBLANES
    c_all = jnp.pad(jnp.concatenate([c_prompt, c_sample], axis=0), ((0, c_pad), (0, 0)))

    y_p = x_prompt
    y_s = x_sample.reshape(1, n_s, D_MODEL)
    outs = [[] for _ in range(6)]
    for layer in range(depth):
        p, sp, mp = _prep_layer(weights, layer)
        mod = _ada(c_all, w_ada[layer], b_ada[layer].reshape(1, -1))
        sh1p, sc1p, g1p, sh2p, sc2p, g2p = [m[:, None, :] for m in _mods(mod[:n_p])]
        sh1s, sc1s, g1s, sh2s, sc2s, g2s = [m[None] for m in _mods(mod[n_p:c_rows])]

        q_t, k_cat, v_t, kv_p, kr_p, u_p = _inproj(y_p, sc1p, sh1p, p, cos_p, sin_p, tm_p)
        o_attn_p = _flash(q_t, k_cat, v_t, p["w_uv_t"], tq, tk)
        y_ssm_p, h_last_p = _s5_prompt(u_p, sp, lc)
        t_p = n_p * l_p
        t_all = t_p + n_s
        blank = (jnp.zeros((t_all * (D_MODEL // LANES), LANES), F32), jnp.zeros((t_all, LANES), F32))
        x1_p, h2_all, route = _outproj(o_attn_p, y_ssm_p, y_p, g1p, sc2p, sh2p, p, tm_p, t_all, 0,
                                       dest=blank)

        q_t_s, _, _, kv_s, kr_s, u_s = _inproj(y_s, sc1s, sh1s, p, cos_s, sin_s, n_s)
        q_lat_s = jnp.transpose(q_t_s[0, :, :KV_RANK, :], (2, 0, 1))
        q_rope_s = jnp.stack(
            [q_t_s[0, h, KV_RANK + ROPE_DIM * (h % 4):KV_RANK + ROPE_DIM * (h % 4 + 1), :].T
             for h in range(N_HEADS)], axis=1)
        o_lat_s = _decode_attn(q_lat_s, q_rope_s, kv_s.reshape(n_s, 1, KV_RANK),
                               kr_s.reshape(n_s, 1, ROPE_DIM), cache_kv_latent[layer],
                               jnp.swapaxes(cache_k_rope[layer], 1, 2), page_table)
        o_attn_s = _uv_proj(o_lat_s.reshape(n_s, N_HEADS * KV_RANK), p["w_uv_bd"])
        h0 = jnp.moveaxis(state_ssm[layer].reshape(n_s, N_STATE, 2), -1, 0)
        y_ssm_s, h_s = _s5_step(u_s[0], h0, sp)
        x1_s, h2_all, route = _outproj(o_attn_s[None], y_ssm_s[None], y_s, g1s, sc2s, sh2s, p, n_s,
                                       t_all, t_p, dest=(h2_all, route))

        nch = 4 if t_all % (4 * SUBLANES) == 0 else (2 if t_all % (2 * SUBLANES) == 0 else 1)
        tc = t_all // nch
        plan = _route_plan(route, nch, MOE_TILE)
        w1p, w2p = _moe_prep(w1[layer], w2[layer])
        f_chunks = _moe_experts(h2_all.reshape(nch, tc, D_MODEL // LANES, LANES), plan, w1p, mp["b1"],
                                w2p, mp["b2"], MOE_TILE)
        f_all = f_chunks[:, :tc].reshape(t_all * (D_MODEL // LANES), LANES)
        y_p = _final(x1_p, f_all, 0, g2p, p["g_post_ffn"], tm_p)
        y_s = _final(x1_s, f_all, t_p, g2s, p["g_post_ffn"], n_s)

        outs[0].append(kv_p)
        outs[1].append(kr_p)
        outs[2].append(jnp.moveaxis(h_last_p, 1, -1).reshape(n_p, N_SSM_GROUPS, SSM_STATE, 2))
        outs[3].append(kv_s.reshape(n_s, 1, KV_RANK))
        outs[4].append(kr_s.reshape(n_s, 1, ROPE_DIM))
        outs[5].append(jnp.moveaxis(h_s, 0, -1).reshape(n_s, N_SSM_GROUPS, SSM_STATE, 2))
    return (y_p, y_s.reshape(n_s, 1, D_MODEL)) + tuple(jnp.stack(o) for o in outs)
```
